```python
import jax, jax.numpy as jnp
from jax import lax
import numpy as np

D_MODEL = 1024
BATCH = 8
SEQ = 8192
DEPTH = 1

CHUNK = 64
Q_BLOCK = 128
PLE_DIM = 256
CONV_DIM = D_MODEL // 2
CONV_WIDTH = 3
SB_HEADS = 8
SB_HEAD_DIM = 64
SB_DIM = SB_HEADS * SB_HEAD_DIM
MIX_WIDTH = CONV_DIM + SB_DIM
IN_PROJ_DIM = 3 * CONV_DIM + 3 * SB_DIM
D_FF = 2816
FFN_RES = 0.5
EPS = 1e-6

kernel_name = "hybrid_shortconv_stickbreaking_macaron_block"


def _rmsnorm(x, g):
    xf = x.astype(jnp.float32)
    y = xf * lax.rsqrt(jnp.mean(xf * xf, axis=-1, keepdims=True) + EPS)
    return (y * g.astype(jnp.float32)).astype(x.dtype)


def _swiglu(h, w_gate, w_up, w_down):
    return (jax.nn.silu(h @ w_gate) * (h @ w_up)) @ w_down


def _short_gated_conv(b, c, u, conv_w, conv_b):
    z = c * u
    rhs = conv_w[:, None, :].astype(z.dtype)
    y = lax.conv_general_dilated(
        z, rhs, window_strides=(1,), padding=[(CONV_WIDTH - 1, 0)],
        dimension_numbers=("NWC", "WIO", "NWC"), feature_group_count=CONV_DIM)
    return b * (y + conv_b.astype(z.dtype))


def _stick_breaking(q, k, v):
    S = q.shape[2]
    scale = SB_HEAD_DIM ** -0.5
    outs = []
    for t0 in range(0, S, Q_BLOCK):
        kl = t0 + Q_BLOCK
        qs = q[:, :, t0:t0 + Q_BLOCK]
        ks = k[:, :, :kl]
        vs = v[:, :, :kl]
        z = jnp.einsum("bhqd,bhkd->bhqk", qs, ks) * scale
        t_idx = t0 + jnp.arange(Q_BLOCK)[:, None]
        s_idx = jnp.arange(kl)[None, :]
        causal = s_idx < t_idx
        log_keep = jnp.where(causal, jax.nn.log_sigmoid(-z), 0.0)
        later = lax.cumsum(log_keep, axis=3, reverse=True) - log_keep
        a = jnp.where(causal, jnp.exp(jax.nn.log_sigmoid(z) + later), 0.0)
        outs.append(jnp.einsum("bhqk,bhkd->bhqd", a, vs))
    return jnp.concatenate(outs, axis=2)


def _fwd_setup_inputs(seed: int = 0) -> dict:
    key = jax.random.key(seed)
    ks = jax.random.split(key, 24)
    f32 = jnp.float32

    def w(k, shape, fan_in):
        return jax.random.normal(k, shape, f32) * (fan_in ** -0.5)

    def gain(k, shape):
        return 1.0 + 0.05 * jax.random.normal(k, shape, f32)

    return {
        "x": jax.random.normal(ks[0], (BATCH, SEQ, D_MODEL), f32),
        "p": jax.random.normal(ks[1], (DEPTH, BATCH, SEQ, PLE_DIM), f32),
        "ffn1_norm": gain(ks[2], (DEPTH, D_MODEL)),
        "ffn1_w_gate": w(ks[3], (DEPTH, D_MODEL, D_FF), D_MODEL),
        "ffn1_w_up": w(ks[4], (DEPTH, D_MODEL, D_FF), D_MODEL),
        "ffn1_w_down": w(ks[5], (DEPTH, D_FF, D_MODEL), D_FF),
        "mix_norm": gain(ks[6], (DEPTH, D_MODEL)),
        "w_in": w(ks[7], (DEPTH, D_MODEL, IN_PROJ_DIM), D_MODEL),
        "conv_w": w(ks[8], (DEPTH, CONV_WIDTH, CONV_DIM), CONV_WIDTH),
        "conv_b": 0.02 * jax.random.normal(ks[9], (DEPTH, CONV_DIM), f32),
        "q_norm": gain(ks[10], (DEPTH, SB_HEAD_DIM)),
        "k_norm": gain(ks[11], (DEPTH, SB_HEAD_DIM)),
        "w_out": w(ks[12], (DEPTH, MIX_WIDTH, D_MODEL), MIX_WIDTH),
        "ffn2_norm": gain(ks[13], (DEPTH, D_MODEL)),
        "ffn2_w_gate": w(ks[14], (DEPTH, D_MODEL, D_FF), D_MODEL),
        "ffn2_w_up": w(ks[15], (DEPTH, D_MODEL, D_FF), D_MODEL),
        "ffn2_w_down": w(ks[16], (DEPTH, D_FF, D_MODEL), D_FF),
        "ple_norm": gain(ks[17], (DEPTH, D_MODEL)),
        "ple_w_gate": w(ks[18], (DEPTH, D_MODEL, D_MODEL), D_MODEL),
        "ple_w_proj": w(ks[19], (DEPTH, PLE_DIM, D_MODEL), PLE_DIM),
    }


def _fwd_reference(x, p, ffn1_norm, ffn1_w_gate, ffn1_w_up, ffn1_w_down, mix_norm, w_in,
              conv_w, conv_b, q_norm, k_norm, w_out, ffn2_norm, ffn2_w_gate, ffn2_w_up,
              ffn2_w_down, ple_norm, ple_w_gate, ple_w_proj):
    B, S, _ = x.shape
    for i in range(DEPTH):
        x = x + FFN_RES * _swiglu(_rmsnorm(x, ffn1_norm[i]), ffn1_w_gate[i], ffn1_w_up[i], ffn1_w_down[i])

        h = _rmsnorm(x, mix_norm[i])
        proj = h @ w_in[i]
        b_g, c_g, u, q, k, v = jnp.split(
            proj, np.cumsum([CONV_DIM, CONV_DIM, CONV_DIM, SB_DIM, SB_DIM]).tolist(), axis=-1)

        y_conv = _short_gated_conv(b_g, c_g, u, conv_w[i], conv_b[i])

        def heads(t):
            return t.reshape(B, S, SB_HEADS, SB_HEAD_DIM).transpose(0, 2, 1, 3).astype(jnp.float32)
        qh = _rmsnorm(heads(q), q_norm[i])
        kh = _rmsnorm(heads(k), k_norm[i])
        y_sb = _stick_breaking(qh, kh, heads(v))
        y_sb = y_sb.transpose(0, 2, 1, 3).reshape(B, S, SB_DIM).astype(x.dtype)

        x = x + jnp.concatenate([y_conv, y_sb], axis=-1) @ w_out[i]

        x = x + FFN_RES * _swiglu(_rmsnorm(x, ffn2_norm[i]), ffn2_w_gate[i], ffn2_w_up[i], ffn2_w_down[i])

        gate = jax.nn.sigmoid(_rmsnorm(x, ple_norm[i]) @ ple_w_gate[i])
        x = x + gate * (p[i].astype(x.dtype) @ ple_w_proj[i])
    return x


import jax as _jax
import jax.numpy as _jnp

TWIN_FORMAT = 'train_step'
FWD_PARAMS = ['x', 'p', 'ffn1_norm', 'ffn1_w_gate', 'ffn1_w_up', 'ffn1_w_down', 'mix_norm', 'w_in', 'conv_w', 'conv_b', 'q_norm', 'k_norm', 'w_out', 'ffn2_norm', 'ffn2_w_gate', 'ffn2_w_up', 'ffn2_w_down', 'ple_norm', 'ple_w_gate', 'ple_w_proj']
TWIN_WEIGHTS = ['ffn1_norm', 'ffn1_w_gate', 'ffn1_w_up', 'ffn1_w_down', 'mix_norm', 'w_in', 'conv_w', 'conv_b', 'q_norm', 'k_norm', 'w_out', 'ffn2_norm', 'ffn2_w_gate', 'ffn2_w_up', 'ffn2_w_down', 'ple_norm', 'ple_w_gate', 'ple_w_proj']
TWIN_DIFF_INPUT = 'x'
TWIN_INPUTS = ['x', 'p', 'ffn1_norm', 'ffn1_w_gate', 'ffn1_w_up', 'ffn1_w_down', 'mix_norm', 'w_in', 'conv_w', 'conv_b', 'q_norm', 'k_norm', 'w_out', 'ffn2_norm', 'ffn2_w_gate', 'ffn2_w_up', 'ffn2_w_down', 'ple_norm', 'ple_w_gate', 'ple_w_proj', 'loss_target', 'm_ffn1_norm', 'm_ffn1_w_gate', 'm_ffn1_w_up', 'm_ffn1_w_down', 'm_mix_norm', 'm_w_in', 'm_conv_w', 'm_conv_b', 'm_q_norm', 'm_k_norm', 'm_w_out', 'm_ffn2_norm', 'm_ffn2_w_gate', 'm_ffn2_w_up', 'm_ffn2_w_down', 'm_ple_norm', 'm_ple_w_gate', 'm_ple_w_proj', 'v_ffn1_norm', 'v_ffn1_w_gate', 'v_ffn1_w_up', 'v_ffn1_w_down', 'v_mix_norm', 'v_w_in', 'v_conv_w', 'v_conv_b', 'v_q_norm', 'v_k_norm', 'v_w_out', 'v_ffn2_norm', 'v_ffn2_w_gate', 'v_ffn2_w_up', 'v_ffn2_w_down', 'v_ple_norm', 'v_ple_w_gate', 'v_ple_w_proj']
TWIN_OUTPUTS = ['loss', 'grad_x', 'grad_ffn1_norm', 'grad_ffn1_w_gate', 'grad_ffn1_w_up', 'grad_ffn1_w_down', 'grad_mix_norm', 'grad_w_in', 'grad_conv_w', 'grad_conv_b', 'grad_q_norm', 'grad_k_norm', 'grad_w_out', 'grad_ffn2_norm', 'grad_ffn2_w_gate', 'grad_ffn2_w_up', 'grad_ffn2_w_down', 'grad_ple_norm', 'grad_ple_w_gate', 'grad_ple_w_proj', 'delta_ffn1_norm', 'delta_ffn1_w_gate', 'delta_ffn1_w_up', 'delta_ffn1_w_down', 'delta_mix_norm', 'delta_w_in', 'delta_conv_w', 'delta_conv_b', 'delta_q_norm', 'delta_k_norm', 'delta_w_out', 'delta_ffn2_norm', 'delta_ffn2_w_gate', 'delta_ffn2_w_up', 'delta_ffn2_w_down', 'delta_ple_norm', 'delta_ple_w_gate', 'delta_ple_w_proj', 'new_m_ffn1_norm', 'new_m_ffn1_w_gate', 'new_m_ffn1_w_up', 'new_m_ffn1_w_down', 'new_m_mix_norm', 'new_m_w_in', 'new_m_conv_w', 'new_m_conv_b', 'new_m_q_norm', 'new_m_k_norm', 'new_m_w_out', 'new_m_ffn2_norm', 'new_m_ffn2_w_gate', 'new_m_ffn2_w_up', 'new_m_ffn2_w_down', 'new_m_ple_norm', 'new_m_ple_w_gate', 'new_m_ple_w_proj', 'new_v_ffn1_norm', 'new_v_ffn1_w_gate', 'new_v_ffn1_w_up', 'new_v_ffn1_w_down', 'new_v_mix_norm', 'new_v_w_in', 'new_v_conv_w', 'new_v_conv_b', 'new_v_q_norm', 'new_v_k_norm', 'new_v_w_out', 'new_v_ffn2_norm', 'new_v_ffn2_w_gate', 'new_v_ffn2_w_up', 'new_v_ffn2_w_down', 'new_v_ple_norm', 'new_v_ple_w_gate', 'new_v_ple_w_proj']
TWIN_LEAF_KINDS = {'loss': 'loss', 'grad_x': 'grad_x', 'grad_ffn1_norm': 'grad_w', 'grad_ffn1_w_gate': 'grad_w', 'grad_ffn1_w_up': 'grad_w', 'grad_ffn1_w_down': 'grad_w', 'grad_mix_norm': 'grad_w', 'grad_w_in': 'grad_w', 'grad_conv_w': 'grad_w', 'grad_conv_b': 'grad_w', 'grad_q_norm': 'grad_w', 'grad_k_norm': 'grad_w', 'grad_w_out': 'grad_w', 'grad_ffn2_norm': 'grad_w', 'grad_ffn2_w_gate': 'grad_w', 'grad_ffn2_w_up': 'grad_w', 'grad_ffn2_w_down': 'grad_w', 'grad_ple_norm': 'grad_w', 'grad_ple_w_gate': 'grad_w', 'grad_ple_w_proj': 'grad_w', 'delta_ffn1_norm': 'delta_w', 'delta_ffn1_w_gate': 'delta_w', 'delta_ffn1_w_up': 'delta_w', 'delta_ffn1_w_down': 'delta_w', 'delta_mix_norm': 'delta_w', 'delta_w_in': 'delta_w', 'delta_conv_w': 'delta_w', 'delta_conv_b': 'delta_w', 'delta_q_norm': 'delta_w', 'delta_k_norm': 'delta_w', 'delta_w_out': 'delta_w', 'delta_ffn2_norm': 'delta_w', 'delta_ffn2_w_gate': 'delta_w', 'delta_ffn2_w_up': 'delta_w', 'delta_ffn2_w_down': 'delta_w', 'delta_ple_norm': 'delta_w', 'delta_ple_w_gate': 'delta_w', 'delta_ple_w_proj': 'delta_w', 'new_m_ffn1_norm': 'new_m', 'new_m_ffn1_w_gate': 'new_m', 'new_m_ffn1_w_up': 'new_m', 'new_m_ffn1_w_down': 'new_m', 'new_m_mix_norm': 'new_m', 'new_m_w_in': 'new_m', 'new_m_conv_w': 'new_m', 'new_m_conv_b': 'new_m', 'new_m_q_norm': 'new_m', 'new_m_k_norm': 'new_m', 'new_m_w_out': 'new_m', 'new_m_ffn2_norm': 'new_m', 'new_m_ffn2_w_gate': 'new_m', 'new_m_ffn2_w_up': 'new_m', 'new_m_ffn2_w_down': 'new_m', 'new_m_ple_norm': 'new_m', 'new_m_ple_w_gate': 'new_m', 'new_m_ple_w_proj': 'new_m', 'new_v_ffn1_norm': 'new_v', 'new_v_ffn1_w_gate': 'new_v', 'new_v_ffn1_w_up': 'new_v', 'new_v_ffn1_w_down': 'new_v', 'new_v_mix_norm': 'new_v', 'new_v_w_in': 'new_v', 'new_v_conv_w': 'new_v', 'new_v_conv_b': 'new_v', 'new_v_q_norm': 'new_v', 'new_v_k_norm': 'new_v', 'new_v_w_out': 'new_v', 'new_v_ffn2_norm': 'new_v', 'new_v_ffn2_w_gate': 'new_v', 'new_v_ffn2_w_up': 'new_v', 'new_v_ffn2_w_down': 'new_v', 'new_v_ple_norm': 'new_v', 'new_v_ple_w_gate': 'new_v', 'new_v_ple_w_proj': 'new_v'}


def _forward(args):
    return _fwd_reference(*[args[k] for k in FWD_PARAMS])


def _output_shape():
    def fwd():
        inp = _fwd_setup_inputs(0)
        return _fwd_reference(*[inp[k] for k in FWD_PARAMS])
    out = _jax.eval_shape(fwd)
    return out.shape, out.dtype

N_MICROBATCH = 1
ADAM_LR = 0.001
ADAM_B1 = 0.9
ADAM_B2 = 0.999
ADAM_EPS = 1e-08
ADAM_WD = 0.01
ADAM_STEP = 10
PER_EXAMPLE_BATCH_AXIS = {'x': 0, 'p': 1, 'loss_target': 0}
SHARED_INPUTS = []
_WEIGHT_DTYPES = {'ffn1_norm': _jnp.float32, 'ffn1_w_gate': _jnp.float32, 'ffn1_w_up': _jnp.float32, 'ffn1_w_down': _jnp.float32, 'mix_norm': _jnp.float32, 'w_in': _jnp.float32, 'conv_w': _jnp.float32, 'conv_b': _jnp.float32, 'q_norm': _jnp.float32, 'k_norm': _jnp.float32, 'w_out': _jnp.float32, 'ffn2_norm': _jnp.float32, 'ffn2_w_gate': _jnp.float32, 'ffn2_w_up': _jnp.float32, 'ffn2_w_down': _jnp.float32, 'ple_norm': _jnp.float32, 'ple_w_gate': _jnp.float32, 'ple_w_proj': _jnp.float32}
MOMENT_SCALE = {'ffn1_norm': 1.212493e+01, 'ffn1_w_gate': 2.408635e-01, 'ffn1_w_up': 2.588664e-01, 'ffn1_w_down': 4.370776e-01, 'mix_norm': 1.046387e+02, 'w_in': 1.307102e+00, 'conv_w': 3.605450e+01, 'conv_b': 2.564366e+00, 'q_norm': 3.192793e+01, 'k_norm': 3.204586e+01, 'w_out': 1.422190e+00, 'ffn2_norm': 1.241274e+01, 'ffn2_w_gate': 1.009082e-01, 'ffn2_w_up': 1.492515e-01, 'ffn2_w_down': 2.427554e-01, 'ple_norm': 1.939035e+00, 'ple_w_gate': 1.141263e-01, 'ple_w_proj': 8.894482e-01}


def _to_microbatches(a, axis):
    t = _jnp.moveaxis(a, axis, 0)
    t = t.reshape((N_MICROBATCH, t.shape[0] // N_MICROBATCH) + t.shape[1:])
    return _jnp.moveaxis(t, 1, axis + 1)


def setup_inputs(seed: int = 0) -> dict:
    inp = _fwd_setup_inputs(seed)
    key = _jax.random.fold_in(_jax.random.key(seed), 7919)
    shape, _ = _output_shape()
    out = dict(inp)
    out["loss_target"] = _jax.random.normal(_jax.random.fold_in(key, 0), shape, _jnp.float32)
    for i, name in enumerate(TWIN_WEIGHTS):
        w = inp[name].astype(_jnp.float32)
        if MOMENT_SCALE is None:
            s = _jnp.sqrt(_jnp.mean(_jnp.square(w)) + 1e-30)
        else:
            s = MOMENT_SCALE[name]
        km, kv = _jax.random.split(_jax.random.fold_in(key, i + 1))
        out[name] = w
        out["m_" + name] = s * _jax.random.normal(km, w.shape, _jnp.float32)
        out["v_" + name] = (s * s) * _jax.random.uniform(kv, w.shape, _jnp.float32, 0.5, 1.5)
    if N_MICROBATCH > 1:
        for name, axis in PER_EXAMPLE_BATCH_AXIS.items():
            out[name] = _to_microbatches(out[name], axis)
    return {'x': out['x'], 'p': out['p'], 'ffn1_norm': out['ffn1_norm'], 'ffn1_w_gate': out['ffn1_w_gate'], 'ffn1_w_up': out['ffn1_w_up'], 'ffn1_w_down': out['ffn1_w_down'], 'mix_norm': out['mix_norm'], 'w_in': out['w_in'], 'conv_w': out['conv_w'], 'conv_b': out['conv_b'], 'q_norm': out['q_norm'], 'k_norm': out['k_norm'], 'w_out': out['w_out'], 'ffn2_norm': out['ffn2_norm'], 'ffn2_w_gate': out['ffn2_w_gate'], 'ffn2_w_up': out['ffn2_w_up'], 'ffn2_w_down': out['ffn2_w_down'], 'ple_norm': out['ple_norm'], 'ple_w_gate': out['ple_w_gate'], 'ple_w_proj': out['ple_w_proj'], 'loss_target': out['loss_target'], 'm_ffn1_norm': out['m_ffn1_norm'], 'm_ffn1_w_gate': out['m_ffn1_w_gate'], 'm_ffn1_w_up': out['m_ffn1_w_up'], 'm_ffn1_w_down': out['m_ffn1_w_down'], 'm_mix_norm': out['m_mix_norm'], 'm_w_in': out['m_w_in'], 'm_conv_w': out['m_conv_w'], 'm_conv_b': out['m_conv_b'], 'm_q_norm': out['m_q_norm'], 'm_k_norm': out['m_k_norm'], 'm_w_out': out['m_w_out'], 'm_ffn2_norm': out['m_ffn2_norm'], 'm_ffn2_w_gate': out['m_ffn2_w_gate'], 'm_ffn2_w_up': out['m_ffn2_w_up'], 'm_ffn2_w_down': out['m_ffn2_w_down'], 'm_ple_norm': out['m_ple_norm'], 'm_ple_w_gate': out['m_ple_w_gate'], 'm_ple_w_proj': out['m_ple_w_proj'], 'v_ffn1_norm': out['v_ffn1_norm'], 'v_ffn1_w_gate': out['v_ffn1_w_gate'], 'v_ffn1_w_up': out['v_ffn1_w_up'], 'v_ffn1_w_down': out['v_ffn1_w_down'], 'v_mix_norm': out['v_mix_norm'], 'v_w_in': out['v_w_in'], 'v_conv_w': out['v_conv_w'], 'v_conv_b': out['v_conv_b'], 'v_q_norm': out['v_q_norm'], 'v_k_norm': out['v_k_norm'], 'v_w_out': out['v_w_out'], 'v_ffn2_norm': out['v_ffn2_norm'], 'v_ffn2_w_gate': out['v_ffn2_w_gate'], 'v_ffn2_w_up': out['v_ffn2_w_up'], 'v_ffn2_w_down': out['v_ffn2_w_down'], 'v_ple_norm': out['v_ple_norm'], 'v_ple_w_gate': out['v_ple_w_gate'], 'v_ple_w_proj': out['v_ple_w_proj']}


def _loss(weights, diff, rest, loss_target):
    with _jax.named_scope("forward"):
        args = {**rest, TWIN_DIFF_INPUT: diff, **{k: w.astype(_WEIGHT_DTYPES[k]) for k, w in weights.items()}}
        y = _forward(args)
    with _jax.named_scope("loss_head"):
        err = _jnp.square(y.astype(_jnp.float32) - loss_target)
        return 0.5 * _jnp.sum(_jnp.mean(err, axis=-1)) if err.ndim else 0.5 * err


def _adamw(w, g, m, v):
    m = ADAM_B1 * m + (1.0 - ADAM_B1) * g
    v = ADAM_B2 * v + (1.0 - ADAM_B2) * _jnp.square(g)
    m_hat = m / (1.0 - ADAM_B1 ** ADAM_STEP)
    v_hat = v / (1.0 - ADAM_B2 ** ADAM_STEP)
    delta = -ADAM_LR * (m_hat / (_jnp.sqrt(v_hat) + ADAM_EPS) + ADAM_WD * w)
    return delta, m, v


def reference(x, p, ffn1_norm, ffn1_w_gate, ffn1_w_up, ffn1_w_down, mix_norm, w_in, conv_w, conv_b, q_norm, k_norm, w_out, ffn2_norm, ffn2_w_gate, ffn2_w_up, ffn2_w_down, ple_norm, ple_w_gate, ple_w_proj, loss_target, m_ffn1_norm, m_ffn1_w_gate, m_ffn1_w_up, m_ffn1_w_down, m_mix_norm, m_w_in, m_conv_w, m_conv_b, m_q_norm, m_k_norm, m_w_out, m_ffn2_norm, m_ffn2_w_gate, m_ffn2_w_up, m_ffn2_w_down, m_ple_norm, m_ple_w_gate, m_ple_w_proj, v_ffn1_norm, v_ffn1_w_gate, v_ffn1_w_up, v_ffn1_w_down, v_mix_norm, v_w_in, v_conv_w, v_conv_b, v_q_norm, v_k_norm, v_w_out, v_ffn2_norm, v_ffn2_w_gate, v_ffn2_w_up, v_ffn2_w_down, v_ple_norm, v_ple_w_gate, v_ple_w_proj):
    given = dict(x=x, p=p, ffn1_norm=ffn1_norm, ffn1_w_gate=ffn1_w_gate, ffn1_w_up=ffn1_w_up, ffn1_w_down=ffn1_w_down, mix_norm=mix_norm, w_in=w_in, conv_w=conv_w, conv_b=conv_b, q_norm=q_norm, k_norm=k_norm, w_out=w_out, ffn2_norm=ffn2_norm, ffn2_w_gate=ffn2_w_gate, ffn2_w_up=ffn2_w_up, ffn2_w_down=ffn2_w_down, ple_norm=ple_norm, ple_w_gate=ple_w_gate, ple_w_proj=ple_w_proj, loss_target=loss_target, m_ffn1_norm=m_ffn1_norm, m_ffn1_w_gate=m_ffn1_w_gate, m_ffn1_w_up=m_ffn1_w_up, m_ffn1_w_down=m_ffn1_w_down, m_mix_norm=m_mix_norm, m_w_in=m_w_in, m_conv_w=m_conv_w, m_conv_b=m_conv_b, m_q_norm=m_q_norm, m_k_norm=m_k_norm, m_w_out=m_w_out, m_ffn2_norm=m_ffn2_norm, m_ffn2_w_gate=m_ffn2_w_gate, m_ffn2_w_up=m_ffn2_w_up, m_ffn2_w_down=m_ffn2_w_down, m_ple_norm=m_ple_norm, m_ple_w_gate=m_ple_w_gate, m_ple_w_proj=m_ple_w_proj, v_ffn1_norm=v_ffn1_norm, v_ffn1_w_gate=v_ffn1_w_gate, v_ffn1_w_up=v_ffn1_w_up, v_ffn1_w_down=v_ffn1_w_down, v_mix_norm=v_mix_norm, v_w_in=v_w_in, v_conv_w=v_conv_w, v_conv_b=v_conv_b, v_q_norm=v_q_norm, v_k_norm=v_k_norm, v_w_out=v_w_out, v_ffn2_norm=v_ffn2_norm, v_ffn2_w_gate=v_ffn2_w_gate, v_ffn2_w_up=v_ffn2_w_up, v_ffn2_w_down=v_ffn2_w_down, v_ple_norm=v_ple_norm, v_ple_w_gate=v_ple_w_gate, v_ple_w_proj=v_ple_w_proj)
    weights = {n: given[n] for n in TWIN_WEIGHTS}
    shared = {n: given[n] for n in SHARED_INPUTS}
    per_example = {n: given[n] for n in ['x', 'p']}
    grad_fn = _jax.value_and_grad(_loss, argnums=(0, 1))

    def one_microbatch(ex, loss_target):
        ex = dict(ex)
        diff = ex.pop(TWIN_DIFF_INPUT)
        return grad_fn(weights, diff, {**shared, **ex}, loss_target)

    if N_MICROBATCH == 1:
        loss, (grad_w, grad_x) = one_microbatch(per_example, given["loss_target"])
    else:
        def body(carry, xs):
            loss_sum, grad_sum = carry
            l_k, (gw_k, gx_k) = one_microbatch(xs[0], xs[1])
            with _jax.named_scope("update"):
                return (loss_sum + l_k, _jax.tree.map(_jnp.add, grad_sum, gw_k)), gx_k

        init = (_jnp.zeros((), _jnp.float32), _jax.tree.map(_jnp.zeros_like, weights))
        (loss, grad_w), grad_x = _jax.lax.scan(body, init, (per_example, given["loss_target"]))
    with _jax.named_scope("update"):
        delta_w, new_m, new_v = {}, {}, {}
        for n in TWIN_WEIGHTS:
            delta_w[n], new_m[n], new_v[n] = _adamw(weights[n], grad_w[n], given["m_" + n], given["v_" + n])
    return (loss, grad_x, *[grad_w[n] for n in TWIN_WEIGHTS], *[delta_w[n] for n in TWIN_WEIGHTS],
            *[new_m[n] for n in TWIN_WEIGHTS], *[new_v[n] for n in TWIN_WEIGHTS])
```

```python
import jax
import jax.numpy as jnp
from jax import lax
from jax.experimental import pallas as pl
from jax.experimental.pallas import tpu as pltpu

F32 = jnp.float32
BF16 = jnp.bfloat16
MESH = pl.DeviceIdType.MESH

EPS = 1e-6
HEAD_DIM = 64
LANES = 128
FFN_RES = 0.5
ADAM_LR = 0.001
ADAM_B1 = 0.9
ADAM_B2 = 0.999
ADAM_EPS = 1e-08
ADAM_WD = 0.01
ADAM_STEP = 10
N_SHARDS = 4
N_DEV = 8
ATT_TILE = 256
VMEM_LIMIT = 52 * 1024 * 1024
SMALL_ROWS = 16
HBM = pl.BlockSpec(memory_space=pltpu.HBM)
VMEM_WHOLE = pl.BlockSpec(memory_space=pltpu.VMEM)


def _cparams(**kw):
    return pltpu.CompilerParams(vmem_limit_bytes=VMEM_LIMIT, **kw)


def _dot(a, b):
    return jnp.dot(a, b, preferred_element_type=F32)


def _dot_nt(a, b):
    return lax.dot_general(a, b, (((1,), (1,)), ((), ())), preferred_element_type=F32)


def _dot_tn(a, b):
    return lax.dot_general(a, b, (((0,), (0,)), ((), ())), preferred_element_type=F32)


def _split_dot(x, m):
    hi = x.astype(BF16)
    lo = (x - hi.astype(F32)).astype(BF16)
    return _dot(hi, m) + _dot(lo, m)


def _rms(x, g):
    r = lax.rsqrt(jnp.mean(x * x, axis=-1, keepdims=True) + EPS)
    xn = x * r
    return xn * g, xn, r


def _rms_bwd(dh, xn, r, g):
    dxn = dh * g
    dx = r * (dxn - xn * jnp.mean(dxn * xn, axis=-1, keepdims=True))
    return dx, jnp.sum(dh * xn, axis=0, keepdims=True)


def _rows(n, cap=512):
    for t in (512, 448, 384, 352, 256, 192, 176, 128, 96, 88, 64, 48, 32, 16, 8):
        if t <= cap and n % t == 0:
            return t
    raise ValueError(f"no row tile for {n}")


def _ffn_fwd(x, gain, wg, wu, wd, name):
    S, D = x.shape
    ns, _, fs = wg.shape
    tm = _rows(S, 512)

    def body(x_ref, g_ref, wg_ref, wu_ref, wd_ref, xo_ref, h_ref, a_ref, b_ref, hs, acc):
        j = pl.program_id(1)

        @pl.when(j == 0)
        def _():
            h, _, _ = _rms(x_ref[...], g_ref[...])
            hb = h.astype(BF16)
            hs[...] = hb
            h_ref[...] = hb
            acc[...] = jnp.zeros_like(acc)

        hb = hs[...]
        a = _dot(hb, wg_ref[...])
        b = _dot(hb, wu_ref[...])
        a_ref[...] = a
        b_ref[...] = b
        s = (a * jax.nn.sigmoid(a)) * b
        acc[...] += _dot(s.astype(BF16), wd_ref[...])

        @pl.when(j == ns - 1)
        def _():
            xo_ref[...] = x_ref[...] + FFN_RES * acc[...]

    return pl.pallas_call(
        body, name=name, grid=(S // tm, ns),
        in_specs=[
            pl.BlockSpec((tm, D), lambda i, j: (i, 0)),
            pl.BlockSpec((1, D), lambda i, j: (0, 0)),
            pl.BlockSpec((None, D, fs), lambda i, j: (j, 0, 0)),
            pl.BlockSpec((None, D, fs), lambda i, j: (j, 0, 0)),
            pl.BlockSpec((None, fs, D), lambda i, j: (j, 0, 0)),
        ],
        out_specs=[
            pl.BlockSpec((tm, D), lambda i, j: (i, 0)),
            pl.BlockSpec((tm, D), lambda i, j: (i, 0)),
            pl.BlockSpec((None, tm, fs), lambda i, j: (j, i, 0)),
            pl.BlockSpec((None, tm, fs), lambda i, j: (j, i, 0)),
        ],
        out_shape=[
            jax.ShapeDtypeStruct((S, D), F32),
            jax.ShapeDtypeStruct((S, D), BF16),
            jax.ShapeDtypeStruct((ns, S, fs), F32),
            jax.ShapeDtypeStruct((ns, S, fs), F32),
        ],
        scratch_shapes=[pltpu.VMEM((tm, D), BF16), pltpu.VMEM((tm, D), F32)],
        compiler_params=_cparams(),
    )(x, gain, wg, wu, wd)


def _ffn_bwd(dxo, x, gain, a, b, wg, wu, wd, name):
    S, D = x.shape
    ns, _, fs = wg.shape
    tm = _rows(S, 256)

    def body(dxo_ref, x_ref, g_ref, a_ref, b_ref, wg_ref, wu_ref, wd_ref,
             dx_ref, s_ref, da_ref, db_ref, dy_ref, dg_ref, dys, acc):
        i = pl.program_id(0)
        j = pl.program_id(1)

        @pl.when((i == 0) & (j == 0))
        def _():
            dg_ref[...] = jnp.zeros_like(dg_ref)

        @pl.when(j == 0)
        def _():
            dy = (FFN_RES * dxo_ref[...]).astype(BF16)
            dys[...] = dy
            dy_ref[...] = dy
            acc[...] = jnp.zeros_like(acc)

        av = a_ref[...]
        bv = b_ref[...]
        ds = _dot_nt(dys[...], wd_ref[...])
        sig = jax.nn.sigmoid(av)
        sl = av * sig
        s_ref[...] = (sl * bv).astype(BF16)
        da = (ds * bv * (sig * (1.0 + av * (1.0 - sig)))).astype(BF16)
        db = (ds * sl).astype(BF16)
        da_ref[...] = da
        db_ref[...] = db
        acc[...] += _dot_nt(da, wg_ref[...]) + _dot_nt(db, wu_ref[...])

        @pl.when(j == ns - 1)
        def _():
            g = g_ref[...]
            _, xn, r = _rms(x_ref[...], g)
            dx, dg = _rms_bwd(acc[...], xn, r, g)
            dx_ref[...] = dxo_ref[...] + dx
            dg_ref[...] += jnp.broadcast_to(dg, dg_ref.shape)

    return pl.pallas_call(
        body, name=name, grid=(S // tm, ns),
        in_specs=[
            pl.BlockSpec((tm, D), lambda i, j: (i, 0)),
            pl.BlockSpec((tm, D), lambda i, j: (i, 0)),
            pl.BlockSpec((1, D), lambda i, j: (0, 0)),
            pl.BlockSpec((None, tm, fs), lambda i, j: (j, i, 0)),
            pl.BlockSpec((None, tm, fs), lambda i, j: (j, i, 0)),
            pl.BlockSpec((None, D, fs), lambda i, j: (j, 0, 0)),
            pl.BlockSpec((None, D, fs), lambda i, j: (j, 0, 0)),
            pl.BlockSpec((None, fs, D), lambda i, j: (j, 0, 0)),
        ],
        out_specs=[
            pl.BlockSpec((tm, D), lambda i, j: (i, 0)),
            pl.BlockSpec((None, tm, fs), lambda i, j: (j, i, 0)),
            pl.BlockSpec((None, tm, fs), lambda i, j: (j, i, 0)),
            pl.BlockSpec((None, tm, fs), lambda i, j: (j, i, 0)),
            pl.BlockSpec((tm, D), lambda i, j: (i, 0)),
            pl.BlockSpec((8, D), lambda i, j: (0, 0)),
        ],
        out_shape=[
            jax.ShapeDtypeStruct((S, D), F32),
            jax.ShapeDtypeStruct((ns, S, fs), BF16),
            jax.ShapeDtypeStruct((ns, S, fs), BF16),
            jax.ShapeDtypeStruct((ns, S, fs), BF16),
            jax.ShapeDtypeStruct((S, D), BF16),
            jax.ShapeDtypeStruct((8, D), F32),
        ],
        scratch_shapes=[pltpu.VMEM((tm, D), BF16), pltpu.VMEM((tm, D), F32)],
        compiler_params=_cparams(),
    )(dxo, x, gain, a, b, wg, wu, wd)


def _tn_matmul(a, b, a_spec, b_spec, o_shape, o_spec, grid, name):
    kaxis = len(grid) - 1

    def body(a_ref, b_ref, o_ref):
        @pl.when(pl.program_id(kaxis) == 0)
        def _():
            o_ref[...] = jnp.zeros_like(o_ref)

        o_ref[...] += _dot_tn(a_ref[...].astype(BF16), b_ref[...].astype(BF16))

    return pl.pallas_call(
        body, name=name, grid=grid, in_specs=[a_spec, b_spec], out_specs=o_spec,
        out_shape=jax.ShapeDtypeStruct(o_shape, F32), compiler_params=_cparams(),
    )(a, b)


def _norm_proj(x, gain, w, name):
    S, D = x.shape
    ns, _, n = w.shape
    tm = _rows(S, 512)

    def body(x_ref, g_ref, w_ref, o_ref, h_ref, hs):
        @pl.when(pl.program_id(1) == 0)
        def _():
            h, _, _ = _rms(x_ref[...], g_ref[...])
            hb = h.astype(BF16)
            hs[...] = hb
            h_ref[...] = hb

        o_ref[...] = _dot(hs[...], w_ref[...])

    return pl.pallas_call(
        body, name=name, grid=(S // tm, ns),
        in_specs=[
            pl.BlockSpec((tm, D), lambda i, j: (i, 0)),
            pl.BlockSpec((1, D), lambda i, j: (0, 0)),
            pl.BlockSpec((None, D, n), lambda i, j: (j, 0, 0)),
        ],
        out_specs=[
            pl.BlockSpec((tm, n), lambda i, j: (i, j)),
            pl.BlockSpec((tm, D), lambda i, j: (i, 0)),
        ],
        out_shape=[jax.ShapeDtypeStruct((S, ns * n), F32), jax.ShapeDtypeStruct((S, D), BF16)],
        scratch_shapes=[pltpu.VMEM((tm, D), BF16)],
        compiler_params=_cparams(),
    )(x, gain, w)


def _norm_proj_bwd(dproj, w, dres, x, gain, name):
    S, D = x.shape
    ns, _, n = w.shape
    tm = _rows(S, 256)

    def body(dp_ref, w_ref, dres_ref, x_ref, g_ref, dx_ref, dg_ref, acc):
        i = pl.program_id(0)
        j = pl.program_id(1)

        @pl.when((i == 0) & (j == 0))
        def _():
            dg_ref[...] = jnp.zeros_like(dg_ref)

        @pl.when(j == 0)
        def _():
            acc[...] = jnp.zeros_like(acc)

        acc[...] += _dot_nt(dp_ref[...], w_ref[...])

        @pl.when(j == ns - 1)
        def _():
            g = g_ref[...]
            _, xn, r = _rms(x_ref[...], g)
            dx, dg = _rms_bwd(acc[...], xn, r, g)
            dx_ref[...] = dres_ref[...] + dx
            dg_ref[...] += jnp.broadcast_to(dg, dg_ref.shape)

    return pl.pallas_call(
        body, name=name, grid=(S // tm, ns),
        in_specs=[
            pl.BlockSpec((tm, n), lambda i, j: (i, j)),
            pl.BlockSpec((None, D, n), lambda i, j: (j, 0, 0)),
            pl.BlockSpec((tm, D), lambda i, j: (i, 0)),
            pl.BlockSpec((tm, D), lambda i, j: (i, 0)),
            pl.BlockSpec((1, D), lambda i, j: (0, 0)),
        ],
        out_specs=[
            pl.BlockSpec((tm, D), lambda i, j: (i, 0)),
            pl.BlockSpec((8, D), lambda i, j: (0, 0)),
        ],
        out_shape=[jax.ShapeDtypeStruct((S, D), F32), jax.ShapeDtypeStruct((8, D), F32)],
        scratch_shapes=[pltpu.VMEM((tm, D), F32)],
        compiler_params=_cparams(),
    )(dproj, w, dres, x, gain)


def _out_proj(ycat, w, res, name):
    S, K = ycat.shape
    D = w.shape[1]
    tm = _rows(S, 512)

    def body(y_ref, w_ref, r_ref, o_ref):
        o_ref[...] = r_ref[...] + _dot(y_ref[...], w_ref[...])

    return pl.pallas_call(
        body, name=name, grid=(S // tm,),
        in_specs=[
            pl.BlockSpec((tm, K), lambda i: (i, 0)),
            pl.BlockSpec((K, D), lambda i: (0, 0)),
            pl.BlockSpec((tm, D), lambda i: (i, 0)),
        ],
        out_specs=pl.BlockSpec((tm, D), lambda i: (i, 0)),
        out_shape=jax.ShapeDtypeStruct((S, D), F32),
        compiler_params=_cparams(),
    )(ycat, w, res)


def _out_proj_bwd(dx, w, name):
    S, D = dx.shape
    K = w.shape[0]
    tm = _rows(S, 512)

    def body(d_ref, w_ref, o_ref):
        o_ref[...] = _dot_nt(d_ref[...].astype(BF16), w_ref[...])

    return pl.pallas_call(
        body, name=name, grid=(S // tm,),
        in_specs=[pl.BlockSpec((tm, D), lambda i: (i, 0)), pl.BlockSpec((K, D), lambda i: (0, 0))],
        out_specs=pl.BlockSpec((tm, K), lambda i: (i, 0)),
        out_shape=jax.ShapeDtypeStruct((S, K), F32),
        compiler_params=_cparams(),
    )(dx, w)


CONV_COLS = 256


def _shift_down(z, halo, k, row):
    out = pltpu.roll(z, k, 0)
    for n in range(k):
        out = jnp.where(row == n, halo[8 - k + n:8 - k + n + 1, :], out)
    return out


def _shift_up(g, halo, k, row, ts):
    out = pltpu.roll(g, ts - k, 0)
    for n in range(k):
        out = jnp.where(row == ts - k + n, halo[n:n + 1, :], out)
    return out


def _conv_fwd(proj, cw, cb, nconv, name):
    S = proj.shape[0]
    ncb = nconv // CONV_COLS
    ts = _rows(S, 512)
    hb = ts // 8

    def body(b_ref, c_ref, u_ref, ch_ref, uh_ref, w_ref, bias_ref, o_ref):
        i = pl.program_id(1)
        z = c_ref[...] * u_ref[...]
        halo = jnp.where(i > 0, ch_ref[...] * uh_ref[...], 0.0)
        row = lax.broadcasted_iota(jnp.int32, z.shape, 0)
        w = w_ref[...]
        yc = w[0:1, :] * _shift_down(z, halo, 2, row) + w[1:2, :] * _shift_down(z, halo, 1, row) + w[2:3, :] * z
        o_ref[...] = (b_ref[...] * (yc + bias_ref[...])).astype(BF16)

    def blk(unit):
        return pl.BlockSpec((ts, CONV_COLS), lambda cbi, i: (i, unit * ncb + cbi))

    def prev(unit):
        return pl.BlockSpec((8, CONV_COLS), lambda cbi, i: (jnp.maximum(i * hb - 1, 0), unit * ncb + cbi))

    return pl.pallas_call(
        body, name=name, grid=(ncb, S // ts),
        in_specs=[blk(0), blk(1), blk(2), prev(1), prev(2),
                  pl.BlockSpec((8, CONV_COLS), lambda cbi, i: (0, cbi)),
                  pl.BlockSpec((1, CONV_COLS), lambda cbi, i: (0, cbi))],
        out_specs=pl.BlockSpec((ts, CONV_COLS), lambda cbi, i: (i, cbi)),
        out_shape=jax.ShapeDtypeStruct((S, nconv), BF16),
        compiler_params=_cparams(),
    )(proj, proj, proj, proj, proj, cw, cb)


def _conv_bwd(proj, dy, cw, cb, nconv, name):
    S = proj.shape[0]
    ncb = nconv // CONV_COLS
    ts = _rows(S, 512)
    hb = ts // 8
    nblk = S // ts

    def body(b_ref, c_ref, u_ref, dy_ref, ch_ref, uh_ref, bn_ref, dyn_ref, w_ref, bias_ref,
             db_ref, dc_ref, du_ref, dw_ref):
        i = pl.program_id(1)

        @pl.when(i == 0)
        def _():
            dw_ref[...] = jnp.zeros_like(dw_ref)

        c = c_ref[...]
        u = u_ref[...]
        bg = b_ref[...]
        dy_ = dy_ref[...]
        z = c * u
        halo = jnp.where(i > 0, ch_ref[...] * uh_ref[...], 0.0)
        row = lax.broadcasted_iota(jnp.int32, z.shape, 0)
        w = w_ref[...]
        z2 = _shift_down(z, halo, 2, row)
        z1 = _shift_down(z, halo, 1, row)
        yc = w[0:1, :] * z2 + w[1:2, :] * z1 + w[2:3, :] * z
        db_ref[...] = (dy_ * (yc + bias_ref[...])).astype(BF16)
        g = dy_ * bg
        gnext = jnp.where(i < nblk - 1, dyn_ref[...] * bn_ref[...], 0.0)
        dz = w[2:3, :] * g + w[1:2, :] * _shift_up(g, gnext, 1, row, ts) + w[0:1, :] * _shift_up(g, gnext, 2, row, ts)
        dc_ref[...] = (dz * u).astype(BF16)
        du_ref[...] = (dz * c).astype(BF16)
        r8 = lax.broadcasted_iota(jnp.int32, (8, CONV_COLS), 0)
        sums = [jnp.sum(g * z2, axis=0, keepdims=True), jnp.sum(g * z1, axis=0, keepdims=True),
                jnp.sum(g * z, axis=0, keepdims=True), jnp.sum(g, axis=0, keepdims=True)]
        upd = jnp.zeros((8, CONV_COLS), F32)
        for n, sv in enumerate(sums):
            upd = jnp.where(r8 == n, sv, upd)
        dw_ref[...] += upd

    def blk(unit):
        return pl.BlockSpec((ts, CONV_COLS), lambda cbi, i: (i, unit * ncb + cbi))

    def prev(unit):
        return pl.BlockSpec((8, CONV_COLS), lambda cbi, i: (jnp.maximum(i * hb - 1, 0), unit * ncb + cbi))

    def nxt(unit):
        return pl.BlockSpec((8, CONV_COLS), lambda cbi, i: (jnp.minimum((i + 1) * hb, S // 8 - 1), unit * ncb + cbi))

    o = pl.BlockSpec((ts, CONV_COLS), lambda cbi, i: (i, cbi))
    return pl.pallas_call(
        body, name=name, grid=(ncb, nblk),
        in_specs=[blk(0), blk(1), blk(2), blk(0), prev(1), prev(2), nxt(0), nxt(0),
                  pl.BlockSpec((8, CONV_COLS), lambda cbi, i: (0, cbi)),
                  pl.BlockSpec((1, CONV_COLS), lambda cbi, i: (0, cbi))],
        out_specs=[o, o, o, pl.BlockSpec((8, CONV_COLS), lambda cbi, i: (0, cbi))],
        out_shape=[jax.ShapeDtypeStruct((S, nconv), BF16)] * 3 + [jax.ShapeDtypeStruct((8, nconv), F32)],
        compiler_params=_cparams(),
    )(proj, proj, proj, dy, proj, proj, proj, dy, cw, cb)


def _group_ones(n):
    r = lax.broadcasted_iota(jnp.int32, (n, n), 0) // HEAD_DIM
    c = lax.broadcasted_iota(jnp.int32, (n, n), 1) // HEAD_DIM
    return jnp.where(r == c, 1.0, 0.0).astype(BF16)


def _qk_norm(proj, gain_t, unit0, nsb, scale, name):
    S = proj.shape[0]
    nb = nsb // CONV_COLS
    ts = _rows(S, 512)

    def body(x_ref, g_ref, o_ref):
        x = x_ref[...]
        ss = _split_dot(x * x, _group_ones(CONV_COLS))
        r = lax.rsqrt(ss * (1.0 / HEAD_DIM) + EPS)
        o_ref[...] = ((x * r) * g_ref[...] * scale).astype(BF16)

    return pl.pallas_call(
        body, name=name, grid=(nb, S // ts),
        in_specs=[pl.BlockSpec((ts, CONV_COLS), lambda u, i: (i, unit0 + u)),
                  pl.BlockSpec((1, CONV_COLS), lambda u, i: (0, 0))],
        out_specs=pl.BlockSpec((ts, CONV_COLS), lambda u, i: (i, u)),
        out_shape=jax.ShapeDtypeStruct((S, nsb), BF16),
        compiler_params=_cparams(),
    )(proj, gain_t)


def _qk_norm_bwd(proj, dout, gain_t, unit0, nsb, scale, name):
    S = proj.shape[0]
    nb = nsb // CONV_COLS
    ts = _rows(S, 512)

    def body(x_ref, d_ref, g_ref, dx_ref, dg_ref):
        @pl.when(pl.program_id(1) == 0)
        def _():
            dg_ref[...] = jnp.zeros_like(dg_ref)

        x = x_ref[...]
        g = g_ref[...]
        ones = _group_ones(CONV_COLS)
        ss = _split_dot(x * x, ones)
        r = lax.rsqrt(ss * (1.0 / HEAD_DIM) + EPS)
        xn = x * r
        dh = d_ref[...] * scale
        dxn = dh * g
        m = _split_dot(dxn * xn, ones) * (1.0 / HEAD_DIM)
        dx_ref[...] = (r * (dxn - xn * m)).astype(BF16)
        dg_ref[...] += jnp.broadcast_to(jnp.sum(dh * xn, axis=0, keepdims=True), dg_ref.shape)

    return pl.pallas_call(
        body, name=name, grid=(nb, S // ts),
        in_specs=[pl.BlockSpec((ts, CONV_COLS), lambda u, i: (i, unit0 + u)),
                  pl.BlockSpec((ts, CONV_COLS), lambda u, i: (i, u)),
                  pl.BlockSpec((1, CONV_COLS), lambda u, i: (0, 0))],
        out_specs=[pl.BlockSpec((ts, CONV_COLS), lambda u, i: (i, u)),
                   pl.BlockSpec((8, CONV_COLS), lambda u, i: (0, u))],
        out_shape=[jax.ShapeDtypeStruct((S, nsb), BF16), jax.ShapeDtypeStruct((8, nsb), F32)],
        compiler_params=_cparams(),
    )(proj, dout, gain_t)


def _softplus_parts(z):
    t = jnp.exp(-jnp.abs(z))
    return jnp.maximum(z, 0.0) + jnp.log(1.0 + t), t


def _head_masks():
    lane = lax.broadcasted_iota(jnp.int32, (1, LANES), 1)
    return [lane < HEAD_DIM, lane >= HEAD_DIM], lane


def _attn_fwd(q, k, v, name):
    S, nsb = q.shape
    T = ATT_TILE
    hp = nsb // LANES
    nb = S // T
    assert nb <= HEAD_DIM

    def body(q_ref, k_ref, v_ref, y_ref, cs_ref, c_ref, acc):
        i = pl.program_id(1)
        masks, lane = _head_masks()
        qv = q_ref[...]
        qm = [jnp.where(m, qv, jnp.zeros_like(qv)) for m in masks]
        r_i = lax.broadcasted_iota(jnp.int32, (T, T), 0)
        c_i = lax.broadcasted_iota(jnp.int32, (T, T), 1)
        suffix = jnp.where(r_i >= c_i, 1.0, 0.0).astype(BF16)
        causal = c_i < r_i
        c_ref[...] = jnp.zeros_like(c_ref)
        acc[...] = jnp.zeros_like(acc)
        cs_ref[...] = jnp.zeros_like(cs_ref)

        def step(j, diag):
            off = pl.multiple_of(j * T, T)
            kj = k_ref[pl.ds(off, T), :]
            vj = v_ref[pl.ds(off, T), :]
            for h in range(2):
                z = _dot_nt(qm[h], kj)
                sp, _ = _softplus_parts(z)
                lk = -sp
                if diag:
                    lk = jnp.where(causal, lk, 0.0)
                inc = _split_dot(lk, suffix)
                c = c_ref[h]
                a = jnp.exp(z + inc + c)
                if diag:
                    a = jnp.where(causal, a, 0.0)
                vm = jnp.where(masks[h], vj, jnp.zeros_like(vj))
                acc[...] += _dot(a.astype(BF16), vm)
                cs_ref[...] = jnp.where(lane == j + HEAD_DIM * h, c, cs_ref[...])
                c_ref[h] = c + inc[:, 0:1]

        step(i, True)

        def loop(n, carry):
            step(i - 1 - n, False)
            return carry

        lax.fori_loop(0, i, loop, 0)
        y_ref[...] = acc[...].astype(BF16)

    return pl.pallas_call(
        body, name=name, grid=(hp, nb),
        in_specs=[pl.BlockSpec((T, LANES), lambda p, i: (i, p)),
                  pl.BlockSpec((S, LANES), lambda p, i: (0, p)),
                  pl.BlockSpec((S, LANES), lambda p, i: (0, p))],
        out_specs=[pl.BlockSpec((T, LANES), lambda p, i: (i, p)),
                   pl.BlockSpec((None, T, LANES), lambda p, i: (p, i, 0))],
        out_shape=[jax.ShapeDtypeStruct((S, nsb), BF16), jax.ShapeDtypeStruct((hp, S, LANES), F32)],
        scratch_shapes=[pltpu.VMEM((2, T, 1), F32), pltpu.VMEM((T, LANES), F32)],
        compiler_params=_cparams(),
    )(q, k, v)


def _attn_bwd(q, k, v, dy, col0, carry, name):
    S, nsb = q.shape
    T = ATT_TILE
    hp = nsb // LANES
    nb = S // T

    def body(q_ref, k_ref, v_ref, dy_ref, cs_ref, dq_ref, dk_ref, dv_ref, e_ref, acc):
        i = pl.program_id(1)

        @pl.when(i == 0)
        def _():
            dk_ref[...] = jnp.zeros_like(dk_ref)
            dv_ref[...] = jnp.zeros_like(dv_ref)

        masks, lane = _head_masks()
        qv = q_ref[...]
        dyb = dy_ref[...].astype(BF16)
        qm = [jnp.where(m, qv, jnp.zeros_like(qv)) for m in masks]
        dym = [jnp.where(m, dyb, jnp.zeros_like(dyb)) for m in masks]
        r_i = lax.broadcasted_iota(jnp.int32, (T, T), 0)
        c_i = lax.broadcasted_iota(jnp.int32, (T, T), 1)
        suffix = jnp.where(r_i >= c_i, 1.0, 0.0).astype(BF16)
        prefix = jnp.where(r_i <= c_i, 1.0, 0.0).astype(BF16)
        causal = c_i < r_i
        e_ref[...] = jnp.zeros_like(e_ref)
        acc[...] = jnp.zeros_like(acc)

        def step(j, diag):
            off = pl.multiple_of(j * T, T)
            kj = k_ref[pl.ds(off, T), :]
            vj = v_ref[pl.ds(off, T), :]
            csv = cs_ref[...]
            for h in range(2):
                z = _dot_nt(qm[h], kj)
                sp, t = _softplus_parts(z)
                rcp = 1.0 / (1.0 + t)
                beta = jnp.where(z >= 0.0, rcp, t * rcp)
                lk = -sp
                if diag:
                    lk = jnp.where(causal, lk, 0.0)
                inc = _split_dot(lk, suffix)
                c = jnp.sum(jnp.where(lane == j + HEAD_DIM * h, csv, 0.0), axis=-1, keepdims=True)
                a = jnp.exp(z + inc + c)
                if diag:
                    a = jnp.where(causal, a, 0.0)
                da = _dot_nt(dym[h], vj)
                e = a * da
                big_e = e_ref[h] + _split_dot(e, prefix)
                dz = e - beta * big_e
                if diag:
                    dz = jnp.where(causal, dz, 0.0)
                e_ref[h] = big_e[:, T - 1:T]
                dzb = dz.astype(BF16)
                km = jnp.where(masks[h], kj, jnp.zeros_like(kj))
                acc[...] += _dot(dzb, km)
                dk_ref[pl.ds(off, T), :] += _dot_tn(dzb, qm[h])
                dv_ref[pl.ds(off, T), :] += _dot_tn(a.astype(BF16), dym[h])

        def loop(j, carry_):
            step(j, False)
            return carry_

        lax.fori_loop(0, i, loop, 0)
        step(i, True)
        dq_ref[...] = acc[...]

    return pl.pallas_call(
        body, name=name, grid=(hp, nb),
        in_specs=[pl.BlockSpec((T, LANES), lambda p, i: (i, p)),
                  pl.BlockSpec((S, LANES), lambda p, i: (0, p)),
                  pl.BlockSpec((S, LANES), lambda p, i: (0, p)),
                  pl.BlockSpec((T, LANES), lambda p, i: (i, col0 + p)),
                  pl.BlockSpec((None, T, LANES), lambda p, i: (p, i, 0))],
        out_specs=[pl.BlockSpec((T, LANES), lambda p, i: (i, p)),
                   pl.BlockSpec((S, LANES), lambda p, i: (0, p)),
                   pl.BlockSpec((S, LANES), lambda p, i: (0, p))],
        out_shape=[jax.ShapeDtypeStruct((S, nsb), F32)] * 3,
        scratch_shapes=[pltpu.VMEM((2, T, 1), F32), pltpu.VMEM((T, LANES), F32)],
        compiler_params=_cparams(),
    )(q, k, v, dy, carry)


def _ple(x, p, tgt, gain, wpg, wpp, name):
    S, D = x.shape
    P = p.shape[1]
    ns, _, nc = wpp.shape
    tm = _rows(S, 256)

    def body(x_ref, p_ref, t_ref, g_ref, wpg_ref, wpp_ref, dx_ref, h_ref, du_ref, dpp_ref, loss_ref, dg_ref):
        @pl.when(pl.program_id(0) == 0)
        def _():
            loss_ref[...] = jnp.zeros_like(loss_ref)
            dg_ref[...] = jnp.zeros_like(dg_ref)

        x_ = x_ref[...]
        g = g_ref[...]
        h, xn, r = _rms(x_, g)
        hb = h.astype(BF16)
        h_ref[...] = hb
        gate = jax.nn.sigmoid(_dot(hb, wpg_ref[...]))
        pb = p_ref[...].astype(BF16)
        pp = jnp.concatenate([_dot(pb, wpp_ref[n]) for n in range(ns)], axis=1)
        err = (x_ + gate * pp) - t_ref[...]
        loss_ref[...] += (0.5 / D) * jnp.sum(err * err)
        dy = err * (1.0 / D)
        du = ((dy * pp) * (gate * (1.0 - gate))).astype(BF16)
        du_ref[...] = du
        dpp_ref[...] = (dy * gate).astype(BF16)
        dx, dg = _rms_bwd(_dot_nt(du, wpg_ref[...]), xn, r, g)
        dx_ref[...] = dy + dx
        dg_ref[...] += jnp.broadcast_to(dg, dg_ref.shape)

    row = pl.BlockSpec((tm, D), lambda i: (i, 0))
    return pl.pallas_call(
        body, name=name, grid=(S // tm,),
        in_specs=[row, pl.BlockSpec((tm, P), lambda i: (i, 0)), row,
                  pl.BlockSpec((1, D), lambda i: (0, 0)),
                  pl.BlockSpec((D, D), lambda i: (0, 0)),
                  pl.BlockSpec((ns, P, nc), lambda i: (0, 0, 0))],
        out_specs=[row, row, row, row,
                   pl.BlockSpec((8, LANES), lambda i: (0, 0)),
                   pl.BlockSpec((8, D), lambda i: (0, 0))],
        out_shape=[jax.ShapeDtypeStruct((S, D), F32)] + [jax.ShapeDtypeStruct((S, D), BF16)] * 3
        + [jax.ShapeDtypeStruct((8, LANES), F32), jax.ShapeDtypeStruct((8, D), F32)],
        compiler_params=_cparams(),
    )(x, p, tgt, gain, wpg, wpp)


def _elementwise(fn, ins, n_out, name):
    R, C = ins[0].shape
    tr = _rows(R, 512)

    def body(*refs):
        outs = fn(*[r[...] for r in refs[:len(ins)]])
        for o_ref, o in zip(refs[len(ins):], outs):
            o_ref[...] = o

    spec = pl.BlockSpec((tr, C), lambda i: (i, 0))
    return pl.pallas_call(
        body, name=name, grid=(R // tr,), in_specs=[spec] * len(ins), out_specs=[spec] * n_out,
        out_shape=[jax.ShapeDtypeStruct((R, C), F32)] * n_out, compiler_params=_cparams(),
    )(*ins)


def _adamw(w, g, m, v):
    m = ADAM_B1 * m + (1.0 - ADAM_B1) * g
    v = ADAM_B2 * v + (1.0 - ADAM_B2) * jnp.square(g)
    m_hat = m / (1.0 - ADAM_B1 ** ADAM_STEP)
    v_hat = v / (1.0 - ADAM_B2 ** ADAM_STEP)
    delta = -ADAM_LR * (m_hat / (jnp.sqrt(v_hat) + ADAM_EPS) + ADAM_WD * w)
    return delta, m, v


def _place():
    x, y, c = lax.axis_index("x"), lax.axis_index("y"), lax.axis_index("c")
    chips = [(1 - x, y), (x, 1 - y), (1 - x, 1 - y)]
    return x, y, c, chips


def _half(ref, c, axis_rows):
    n = ref.shape[-2]
    start = pl.multiple_of(c * (n // 2), 8)
    idx = (slice(None),) * (len(ref.shape) - 2) + (pl.ds(start, n // 2), slice(None))
    return ref.at[idx]


def _gather_weights(shards, small, name):
    n = len(shards)

    def body(*refs):
        ins, small_in = refs[:n], refs[n]
        outs, small_out = refs[n + 1:2 * n + 1], refs[2 * n + 1]
        lsem, ssem, rsem, sm_s, sm_r = refs[2 * n + 2:]
        x, y, c, chips = _place()
        j = 2 * x + y
        sib = (x, y, 1 - c)

        local = [pltpu.make_async_copy(ins[a], outs[a].at[j], lsem.at[a]) for a in range(n)]
        for cp in local:
            cp.start()
        small_out[j] = small_in[...]
        small_cp = [pltpu.make_async_remote_copy(
            src_ref=small_in, dst_ref=small_out.at[j], send_sem=sm_s.at[k], recv_sem=sm_r.at[k],
            device_id=(*chip, c), device_id_type=MESH) for k, chip in enumerate(chips)]
        for cp in small_cp:
            cp.start()

        def ici(a, k, chip, jj, dev):
            return pltpu.make_async_remote_copy(
                src_ref=_half(ins[a], c, True) if dev is not None else _half(outs[a].at[jj], c, True),
                dst_ref=_half(outs[a].at[jj], c, True),
                send_sem=ssem.at[a, k], recv_sem=rsem.at[a, k],
                device_id=dev if dev is not None else (*chip, c), device_id_type=MESH)

        first = []
        for a in range(n):
            for k, chip in enumerate(chips):
                cp = ici(a, k, chip, j, (*chip, c))
                cp.start()
                first.append(cp)
        passed = []
        for a in range(n):
            for k, chip in enumerate(chips):
                jj = 2 * chip[0] + chip[1]
                ici(a, k, chip, jj, None).wait_recv()
                fw = pltpu.make_async_remote_copy(
                    src_ref=_half(outs[a].at[jj], c, True), dst_ref=_half(outs[a].at[jj], c, True),
                    send_sem=ssem.at[a, 3 + k], recv_sem=rsem.at[a, 3 + k], device_id=sib, device_id_type=MESH)
                fw.start()
                passed.append(fw)
        for a in range(n):
            for k, chip in enumerate(chips):
                jj = 2 * chip[0] + chip[1]
                pltpu.make_async_remote_copy(
                    src_ref=_half(outs[a].at[jj], 1 - c, True), dst_ref=_half(outs[a].at[jj], 1 - c, True),
                    send_sem=ssem.at[a, 3 + k], recv_sem=rsem.at[a, 3 + k], device_id=sib,
                    device_id_type=MESH).wait_recv()
        for cp in small_cp:
            cp.wait()
        for cp in first + passed:
            cp.wait_send()
        for cp in local:
            cp.wait()

    return pl.pallas_call(
        body, name=name,
        in_specs=[HBM] * n + [VMEM_WHOLE],
        out_specs=[HBM] * n + [VMEM_WHOLE],
        out_shape=[jax.ShapeDtypeStruct((N_SHARDS,) + s.shape, s.dtype) for s in shards]
        + [jax.ShapeDtypeStruct((N_SHARDS,) + small.shape, small.dtype)],
        scratch_shapes=[pltpu.SemaphoreType.DMA((n,)), pltpu.SemaphoreType.DMA((n, 6)),
                        pltpu.SemaphoreType.DMA((n, 6)), pltpu.SemaphoreType.DMA((3,)),
                        pltpu.SemaphoreType.DMA((3,))],
    )(*shards, small)


def _swap_halves(grads, small, name):
    n = len(grads)

    def body(*refs):
        ins, small_in = refs[:n], refs[n]
        outs, small_out = refs[n + 1:2 * n + 1], refs[2 * n + 1]
        buf, ssem, rsem, sm_s, sm_r = refs[2 * n + 2:]
        x, y, c, _ = _place()
        me = 4 * x + 2 * y + c
        sib = (x, y, 1 - c)
        cps = [pltpu.make_async_remote_copy(
            src_ref=_half(ins[a], 1 - c, True), dst_ref=outs[a], send_sem=ssem.at[a], recv_sem=rsem.at[a],
            device_id=sib, device_id_type=MESH) for a in range(n)]
        for cp in cps:
            cp.start()
        buf[me] = small_in[...]
        peers = [(fx, fy, fc) for fx in (0, 1) for fy in (0, 1) for fc in (0, 1)][1:]
        sm = []
        for k, (fx, fy, fc) in enumerate(peers):
            dev = (1 - x if fx else x, 1 - y if fy else y, 1 - c if fc else c)
            cp = pltpu.make_async_remote_copy(
                src_ref=small_in, dst_ref=buf.at[me], send_sem=sm_s.at[k], recv_sem=sm_r.at[k],
                device_id=dev, device_id_type=MESH)
            cp.start()
            sm.append(cp)
        for cp in sm:
            cp.wait()
        tot = buf[0]
        for d in range(1, N_DEV):
            tot = tot + buf[d]
        small_out[...] = tot
        for cp in cps:
            cp.wait()

    return pl.pallas_call(
        body, name=name,
        in_specs=[HBM] * n + [VMEM_WHOLE],
        out_specs=[HBM] * n + [VMEM_WHOLE],
        out_shape=[jax.ShapeDtypeStruct((g.shape[0], g.shape[1] // 2, g.shape[2]), F32) for g in grads]
        + [jax.ShapeDtypeStruct(small.shape, F32)],
        scratch_shapes=[pltpu.VMEM((N_DEV,) + small.shape, F32),
                        pltpu.SemaphoreType.DMA((n,)), pltpu.SemaphoreType.DMA((n,)),
                        pltpu.SemaphoreType.DMA((N_DEV - 1,)), pltpu.SemaphoreType.DMA((N_DEV - 1,))],
    )(*grads, small)


def _scatter_chip_sums(csums, name):
    n = len(csums)

    def body(*refs):
        ins, outs = refs[:n], refs[n:2 * n]
        ssem, rsem = refs[2 * n:]
        x, y, c, chips = _place()
        cps = []
        for a in range(n):
            for k, chip in enumerate(chips):
                jj = 2 * chip[0] + chip[1]
                cp = pltpu.make_async_remote_copy(
                    src_ref=ins[a].at[jj], dst_ref=outs[a].at[k], send_sem=ssem.at[a, k], recv_sem=rsem.at[a, k],
                    device_id=(*chip, c), device_id_type=MESH)
                cp.start()
                cps.append(cp)
        for cp in cps:
            cp.wait()

    return pl.pallas_call(
        body, name=name, in_specs=[HBM] * n, out_specs=[HBM] * n,
        out_shape=[jax.ShapeDtypeStruct((3,) + g.shape[1:], F32) for g in csums],
        scratch_shapes=[pltpu.SemaphoreType.DMA((n, 3)), pltpu.SemaphoreType.DMA((n, 3))],
    )(*csums)


def _join_halves(halves, name):
    n = len(halves)

    def body(*refs):
        ins, outs = refs[:n], refs[n:2 * n]
        lsem, ssem, rsem = refs[2 * n:]
        x, y, c, _ = _place()
        cps = []
        for a in range(n):
            loc = pltpu.make_async_copy(ins[a], _half(outs[a], c, True), lsem.at[a])
            loc.start()
            rem = pltpu.make_async_remote_copy(
                src_ref=ins[a], dst_ref=_half(outs[a], c, True), send_sem=ssem.at[a], recv_sem=rsem.at[a],
                device_id=(x, y, 1 - c), device_id_type=MESH)
            rem.start()
            cps += [loc, rem]
        for cp in cps:
            cp.wait()

    return pl.pallas_call(
        body, name=name, in_specs=[HBM] * n, out_specs=[HBM] * n,
        out_shape=[jax.ShapeDtypeStruct((2 * h.shape[0], h.shape[1]), F32) for h in halves],
        scratch_shapes=[pltpu.SemaphoreType.DMA((n,))] * 3,
    )(*halves)


def _pad_rows(a, rows, cols):
    return jnp.pad(a, ((0, rows - a.shape[0]), (0, cols - a.shape[1])))


def kernel(x, p, ffn1_norm, ffn1_w_gate, ffn1_w_up, ffn1_w_down, mix_norm, w_in, conv_w, conv_b, q_norm, k_norm, w_out, ffn2_norm, ffn2_w_gate, ffn2_w_up, ffn2_w_down, ple_norm, ple_w_gate, ple_w_proj, loss_target, m_ffn1_norm, m_ffn1_w_gate, m_ffn1_w_up, m_ffn1_w_down, m_mix_norm, m_w_in, m_conv_w, m_conv_b, m_q_norm, m_k_norm, m_w_out, m_ffn2_norm, m_ffn2_w_gate, m_ffn2_w_up, m_ffn2_w_down, m_ple_norm, m_ple_w_gate, m_ple_w_proj, v_ffn1_norm, v_ffn1_w_gate, v_ffn1_w_up, v_ffn1_w_down, v_mix_norm, v_w_in, v_conv_w, v_conv_b, v_q_norm, v_k_norm, v_w_out, v_ffn2_norm, v_ffn2_w_gate, v_ffn2_w_up, v_ffn2_w_down, v_ple_norm, v_ple_w_gate, v_ple_w_proj):
    big = dict(ffn1_w_gate=ffn1_w_gate, ffn1_w_up=ffn1_w_up, ffn1_w_down=ffn1_w_down, w_in=w_in, w_out=w_out,
               ffn2_w_gate=ffn2_w_gate, ffn2_w_up=ffn2_w_up, ffn2_w_down=ffn2_w_down,
               ple_w_gate=ple_w_gate, ple_w_proj=ple_w_proj)
    big_m = dict(ffn1_w_gate=m_ffn1_w_gate, ffn1_w_up=m_ffn1_w_up, ffn1_w_down=m_ffn1_w_down, w_in=m_w_in,
                 w_out=m_w_out, ffn2_w_gate=m_ffn2_w_gate, ffn2_w_up=m_ffn2_w_up, ffn2_w_down=m_ffn2_w_down,
                 ple_w_gate=m_ple_w_gate, ple_w_proj=m_ple_w_proj)
    big_v = dict(ffn1_w_gate=v_ffn1_w_gate, ffn1_w_up=v_ffn1_w_up, ffn1_w_down=v_ffn1_w_down, w_in=v_w_in,
                 w_out=v_w_out, ffn2_w_gate=v_ffn2_w_gate, ffn2_w_up=v_ffn2_w_up, ffn2_w_down=v_ffn2_w_down,
                 ple_w_gate=v_ple_w_gate, ple_w_proj=v_ple_w_proj)
    names = list(big)
    xs = x[0]
    ps = p[0, 0]
    tgt = loss_target[0]
    S, D = xs.shape
    nconv = conv_b.shape[1]
    nsb = D - nconv
    cwl = conv_w.shape[2]
    jchip = 2 * lax.axis_index("x") + lax.axis_index("y")
    core = lax.axis_index("c")

    shards = [big[k][0].astype(BF16) for k in names]
    gathered = _gather_weights(shards, _pad_rows(conv_w[0], 8, LANES), "gather_weights")
    W = dict(zip(names, gathered[:-1]))
    cw_full = jnp.transpose(gathered[-1][:, :, :cwl], (1, 0, 2)).reshape(8, N_SHARDS * cwl)
    wout_full = W["w_out"].reshape(-1, D)
    wpg_full = W["ple_w_gate"].reshape(-1, D)
    qg = jnp.tile(q_norm, (1, CONV_COLS // HEAD_DIM))
    kg = jnp.tile(k_norm, (1, CONV_COLS // HEAD_DIM))
    n_units = nconv // CONV_COLS

    x1, h1, a1, b1 = _ffn_fwd(xs, ffn1_norm, W["ffn1_w_gate"], W["ffn1_w_up"], W["ffn1_w_down"], "ffn1_fwd")
    proj, h2 = _norm_proj(x1, mix_norm, W["w_in"], "mix_in_proj")
    y_conv = _conv_fwd(proj, cw_full, conv_b, nconv, "conv_fwd")
    qs = _qk_norm(proj, qg, 3 * n_units, nsb, HEAD_DIM ** -0.5, "q_norm_fwd")
    kh = _qk_norm(proj, kg, 4 * n_units, nsb, 1.0, "k_norm_fwd")
    vb = proj[:, 3 * nconv + 2 * nsb:].astype(BF16)
    y_sb, carry = _attn_fwd(qs, kh, vb, "attn_fwd")
    ycat = jnp.concatenate([y_conv, y_sb], axis=1)
    x2 = _out_proj(ycat, wout_full, x1, "mix_out_proj")
    x3, h3, a3, b3 = _ffn_fwd(x2, ffn2_norm, W["ffn2_w_gate"], W["ffn2_w_up"], W["ffn2_w_down"], "ffn2_fwd")

    dx3, h4, du4, dpp, loss_blk, dg_ple = _ple(x3, ps, tgt, ple_norm, wpg_full, W["ple_w_proj"], "ple_loss")
    G = {}
    tk = _rows(S, 512)
    nk = S // tk
    kd = wpg_full.shape[0] // N_SHARDS
    G["ple_w_gate"] = _tn_matmul(
        h4, du4, pl.BlockSpec((tk, kd), lambda m, k: (k, m)), pl.BlockSpec((tk, D), lambda m, k: (k, 0)),
        (N_SHARDS, kd, D), pl.BlockSpec((None, kd, D), lambda m, k: (m, 0, 0)), (N_SHARDS, nk), "ple_w_gate_grad")
    P = ps.shape[1]
    npp = D // N_SHARDS
    G["ple_w_proj"] = _tn_matmul(
        ps, dpp, pl.BlockSpec((tk, P), lambda m, k: (k, 0)), pl.BlockSpec((tk, npp), lambda m, k: (k, m)),
        (N_SHARDS, P, npp), pl.BlockSpec((None, P, npp), lambda m, k: (m, 0, 0)), (N_SHARDS, nk), "ple_w_proj_grad")

    def ffn_grads(pre, h, s, da, db, dy):
        fs = s.shape[2]
        hs = pl.BlockSpec((tk, D), lambda m, k: (k, 0))
        ss = pl.BlockSpec((None, tk, fs), lambda m, k: (m, k, 0))
        G[pre + "_w_gate"] = _tn_matmul(h, da, hs, ss, (N_SHARDS, D, fs),
                                        pl.BlockSpec((None, D, fs), lambda m, k: (m, 0, 0)), (N_SHARDS, nk), pre + "_w_gate_grad")
        G[pre + "_w_up"] = _tn_matmul(h, db, hs, ss, (N_SHARDS, D, fs),
                                      pl.BlockSpec((None, D, fs), lambda m, k: (m, 0, 0)), (N_SHARDS, nk), pre + "_w_up_grad")
        G[pre + "_w_down"] = _tn_matmul(s, dy, ss, hs, (N_SHARDS, fs, D),
                                        pl.BlockSpec((None, fs, D), lambda m, k: (m, 0, 0)), (N_SHARDS, nk), pre + "_w_down_grad")

    dx2, s3, da3, db3, dy3, dg_ffn2 = _ffn_bwd(dx3, x2, ffn2_norm, a3, b3, W["ffn2_w_gate"], W["ffn2_w_up"], W["ffn2_w_down"], "ffn2_bwd")
    ffn_grads("ffn2", h3, s3, da3, db3, dy3)

    dycat = _out_proj_bwd(dx2, wout_full, "mix_out_proj_bwd")
    ko = wout_full.shape[0] // N_SHARDS
    G["w_out"] = _tn_matmul(
        ycat, dx2, pl.BlockSpec((tk, ko), lambda m, k: (k, m)), pl.BlockSpec((tk, D), lambda m, k: (k, 0)),
        (N_SHARDS, ko, D), pl.BlockSpec((None, ko, D), lambda m, k: (m, 0, 0)), (N_SHARDS, nk), "w_out_grad")
    db_, dc_, du_, dwb = _conv_bwd(proj, dycat, cw_full, conv_b, nconv, "conv_bwd")
    dqs, dkh, dv = _attn_bwd(qs, kh, vb, dycat, nconv // LANES, carry, "attn_bwd")
    dq, dg_q = _qk_norm_bwd(proj, dqs, qg, 3 * n_units, nsb, HEAD_DIM ** -0.5, "q_norm_bwd")
    dk, dg_k = _qk_norm_bwd(proj, dkh, kg, 4 * n_units, nsb, 1.0, "k_norm_bwd")
    dproj = jnp.concatenate([db_, dc_, du_, dq, dk, dv.astype(BF16)], axis=1)
    nin = W["w_in"].shape[2]
    G["w_in"] = _tn_matmul(
        h2, dproj, pl.BlockSpec((tk, D), lambda m, k: (k, 0)), pl.BlockSpec((tk, nin), lambda m, k: (k, m)),
        (N_SHARDS, D, nin), pl.BlockSpec((None, D, nin), lambda m, k: (m, 0, 0)), (N_SHARDS, nk), "w_in_grad")
    dx1, dg_mix = _norm_proj_bwd(dproj, W["w_in"], dx2, x1, mix_norm, "mix_in_proj_bwd")

    dx0, s1, da1, db1, dy1, dg_ffn1 = _ffn_bwd(dx1, xs, ffn1_norm, a1, b1, W["ffn1_w_gate"], W["ffn1_w_up"], W["ffn1_w_down"], "ffn1_bwd")
    ffn_grads("ffn1", h1, s1, da1, db1, dy1)

    assert D >= nconv and D % LANES == 0
    fold = lambda t: t[0].reshape(-1, HEAD_DIM).sum(axis=0)[None, :]
    small_rows = [dg_ffn1[0:1], dg_mix[0:1], dg_ffn2[0:1], dg_ple[0:1],
                  _pad_rows(dwb[3:4], 1, D), _pad_rows(dwb[0:3], 3, D),
                  _pad_rows(fold(dg_q), 1, D), _pad_rows(fold(dg_k), 1, D), _pad_rows(loss_blk[0:1, 0:1], 1, D)]
    small = _pad_rows(jnp.concatenate(small_rows, axis=0), SMALL_ROWS, D)

    grads = [G[k] for k in names]
    swapped = _swap_halves(grads, small, "grad_swap_halves")
    small_sum = swapped[-1]
    csums = []
    for a, (g, r) in enumerate(zip(grads, swapped[:-1])):
        r2 = g.shape[1] // 2
        mine = lax.dynamic_slice_in_dim(g, core * r2, r2, axis=1)
        (cs,) = _elementwise(lambda u, w_: (u + w_,), [mine.reshape(-1, g.shape[2]), r.reshape(-1, g.shape[2])], 1,
                             f"chip_sum_{names[a]}")
        csums.append(cs.reshape(N_SHARDS, r2, g.shape[2]))
    got = _scatter_chip_sums(csums, "grad_scatter")
    halves = []
    for a, (cs, gt) in enumerate(zip(csums, got)):
        own = lax.dynamic_index_in_dim(cs, jchip, axis=0, keepdims=False)
        (tot,) = _elementwise(lambda o, g0, g1, g2: (((o + g0) + g1) + g2,), [own, gt[0], gt[1], gt[2]], 1,
                              f"shard_sum_{names[a]}")
        halves.append(tot)
    full = _join_halves(halves, "grad_join_halves")

    out_g, out_d, out_m, out_v = {}, {}, {}, {}
    for a, k in enumerate(names):
        shp = big[k].shape
        g2 = full[a]
        d_, m_, v_ = _elementwise(_adamw, [big[k][0].reshape(g2.shape), g2, big_m[k][0].reshape(g2.shape),
                                           big_v[k][0].reshape(g2.shape)], 3, f"adamw_{k}")
        out_g[k], out_d[k], out_m[k], out_v[k] = (t.reshape(shp) for t in (g2, d_, m_, v_))

    sm_names = ["ffn1_norm", "mix_norm", "ffn2_norm", "ple_norm", "conv_b", "conv_w", "q_norm", "k_norm"]
    sm_w = dict(ffn1_norm=ffn1_norm, mix_norm=mix_norm, ffn2_norm=ffn2_norm, ple_norm=ple_norm, conv_b=conv_b,
                conv_w=conv_w[0], q_norm=q_norm, k_norm=k_norm)
    sm_m = dict(ffn1_norm=m_ffn1_norm, mix_norm=m_mix_norm, ffn2_norm=m_ffn2_norm, ple_norm=m_ple_norm,
                conv_b=m_conv_b, conv_w=m_conv_w[0], q_norm=m_q_norm, k_norm=m_k_norm)
    sm_v = dict(ffn1_norm=v_ffn1_norm, mix_norm=v_mix_norm, ffn2_norm=v_ffn2_norm, ple_norm=v_ple_norm,
                conv_b=v_conv_b, conv_w=v_conv_w[0], q_norm=v_q_norm, k_norm=v_k_norm)
    sm_g = dict(ffn1_norm=small_sum[0:1], mix_norm=small_sum[1:2], ffn2_norm=small_sum[2:3], ple_norm=small_sum[3:4],
                conv_b=small_sum[4:5, :nconv],
                conv_w=lax.dynamic_slice_in_dim(small_sum[5:8, :nconv], jchip * cwl, cwl, axis=1),
                q_norm=small_sum[8:9, :HEAD_DIM], k_norm=small_sum[9:10, :HEAD_DIM])
    loss = small_sum[10, 0]
    pack = lambda d: _pad_rows(jnp.concatenate([_pad_rows(d[k], d[k].shape[0], D) for k in sm_names], axis=0), SMALL_ROWS, D)
    sd, smm, svv = _elementwise(_adamw, [pack(sm_w), pack(sm_g), pack(sm_m), pack(sm_v)], 3, "adamw_small")
    row = 0
    for k in sm_names:
        r_, c_ = sm_w[k].shape
        shp = (1, r_, c_) if k == "conv_w" else (r_, c_)
        out_g[k] = sm_g[k].reshape(shp)
        out_d[k], out_m[k], out_v[k] = (t[row:row + r_, :c_].reshape(shp) for t in (sd, smm, svv))
        row += r_

    order = ["ffn1_norm", "ffn1_w_gate", "ffn1_w_up", "ffn1_w_down", "mix_norm", "w_in", "conv_w", "conv_b",
             "q_norm", "k_norm", "w_out", "ffn2_norm", "ffn2_w_gate", "ffn2_w_up", "ffn2_w_down", "ple_norm",
             "ple_w_gate", "ple_w_proj"]
    return (loss, dx0[None], *[out_g[k] for k in order], *[out_d[k] for k in order],
            *[out_m[k] for k in order], *[out_v[k] for k in order])
```

```python
import jax
import jax.numpy as jnp
from jax import lax
from jax.experimental import pallas as pl
from jax.experimental.pallas import tpu as pltpu

F32 = jnp.float32
BF16 = jnp.bfloat16
MESH = pl.DeviceIdType.MESH

EPS = 1e-6
HEAD_DIM = 64
LANES = 128
FFN_RES = 0.5
ADAM_LR = 0.001
ADAM_B1 = 0.9
ADAM_B2 = 0.999
ADAM_EPS = 1e-08
ADAM_WD = 0.01
ADAM_STEP = 10
N_SHARDS = 4
N_DEV = 8
ATT_TILE = 256
VMEM_LIMIT = 52 * 1024 * 1024
SMALL_ROWS = 16
HBM = pl.BlockSpec(memory_space=pltpu.HBM)
VMEM_WHOLE = pl.BlockSpec(memory_space=pltpu.VMEM)


def _cparams(**kw):
    return pltpu.CompilerParams(vmem_limit_bytes=VMEM_LIMIT, **kw)


def _dot(a, b):
    return jnp.dot(a, b, preferred_element_type=F32)


def _dot_nt(a, b):
    return lax.dot_general(a, b, (((1,), (1,)), ((), ())), preferred_element_type=F32)


def _dot_tn(a, b):
    return lax.dot_general(a, b, (((0,), (0,)), ((), ())), preferred_element_type=F32)


def _split_dot(x, m):
    hi = x.astype(BF16)
    lo = (x - hi.astype(F32)).astype(BF16)
    return _dot(hi, m) + _dot(lo, m)


def _rms(x, g):
    r = lax.rsqrt(jnp.mean(x * x, axis=-1, keepdims=True) + EPS)
    xn = x * r
    return xn * g, xn, r


def _rms_bwd(dh, xn, r, g):
    dxn = dh * g
    dx = r * (dxn - xn * jnp.mean(dxn * xn, axis=-1, keepdims=True))
    return dx, jnp.sum(dh * xn, axis=0, keepdims=True)


def _rows(n, cap=512):
    for t in (512, 448, 384, 352, 256, 192, 176, 128, 96, 88, 64, 48, 32, 16, 8):
        if t <= cap and n % t == 0:
            return t
    raise ValueError(f"no row tile for {n}")


def _ffn_fwd(x, gain, wg, wu, wd, name):
    S, D = x.shape
    ns, _, fs = wg.shape
    tm = _rows(S, 512)

    def body(x_ref, g_ref, wg_ref, wu_ref, wd_ref, xo_ref, h_ref, a_ref, b_ref, hs, acc):
        j = pl.program_id(1)

        @pl.when(j == 0)
        def _():
            h, _, _ = _rms(x_ref[...], g_ref[...])
            hb = h.astype(BF16)
            hs[...] = hb
            h_ref[...] = hb
            acc[...] = jnp.zeros_like(acc)

        hb = hs[...]
        a = _dot(hb, wg_ref[...])
        b = _dot(hb, wu_ref[...])
        a_ref[...] = a
        b_ref[...] = b
        s = (a * jax.nn.sigmoid(a)) * b
        acc[...] += _dot(s.astype(BF16), wd_ref[...])

        @pl.when(j == ns - 1)
        def _():
            xo_ref[...] = x_ref[...] + FFN_RES * acc[...]

    return pl.pallas_call(
        body, name=name, grid=(S // tm, ns),
        in_specs=[
            pl.BlockSpec((tm, D), lambda i, j: (i, 0)),
            pl.BlockSpec((1, D), lambda i, j: (0, 0)),
            pl.BlockSpec((None, D, fs), lambda i, j: (j, 0, 0)),
            pl.BlockSpec((None, D, fs), lambda i, j: (j, 0, 0)),
            pl.BlockSpec((None, fs, D), lambda i, j: (j, 0, 0)),
        ],
        out_specs=[
            pl.BlockSpec((tm, D), lambda i, j: (i, 0)),
            pl.BlockSpec((tm, D), lambda i, j: (i, 0)),
            pl.BlockSpec((None, tm, fs), lambda i, j: (j, i, 0)),
            pl.BlockSpec((None, tm, fs), lambda i, j: (j, i, 0)),
        ],
        out_shape=[
            jax.ShapeDtypeStruct((S, D), F32),
            jax.ShapeDtypeStruct((S, D), BF16),
            jax.ShapeDtypeStruct((ns, S, fs), F32),
            jax.ShapeDtypeStruct((ns, S, fs), F32),
        ],
        scratch_shapes=[pltpu.VMEM((tm, D), BF16), pltpu.VMEM((tm, D), F32)],
        compiler_params=_cparams(),
    )(x, gain, wg, wu, wd)


def _ffn_bwd(dxo, x, gain, a, b, wg, wu, wd, name):
    S, D = x.shape
    ns, _, fs = wg.shape
    tm = _rows(S, 256)

    def body(dxo_ref, x_ref, g_ref, a_ref, b_ref, wg_ref, wu_ref, wd_ref,
             dx_ref, s_ref, da_ref, db_ref, dy_ref, dg_ref, dys, acc):
        i = pl.program_id(0)
        j = pl.program_id(1)

        @pl.when((i == 0) & (j == 0))
        def _():
            dg_ref[...] = jnp.zeros_like(dg_ref)

        @pl.when(j == 0)
        def _():
            dy = (FFN_RES * dxo_ref[...]).astype(BF16)
            dys[...] = dy
            dy_ref[...] = dy
            acc[...] = jnp.zeros_like(acc)

        av = a_ref[...]
        bv = b_ref[...]
        ds = _dot_nt(dys[...], wd_ref[...])
        sig = jax.nn.sigmoid(av)
        sl = av * sig
        s_ref[...] = (sl * bv).astype(BF16)
        da = (ds * bv * (sig * (1.0 + av * (1.0 - sig)))).astype(BF16)
        db = (ds * sl).astype(BF16)
        da_ref[...] = da
        db_ref[...] = db
        acc[...] += _dot_nt(da, wg_ref[...]) + _dot_nt(db, wu_ref[...])

        @pl.when(j == ns - 1)
        def _():
            g = g_ref[...]
            _, xn, r = _rms(x_ref[...], g)
            dx, dg = _rms_bwd(acc[...], xn, r, g)
            dx_ref[...] = dxo_ref[...] + dx
            dg_ref[...] += jnp.broadcast_to(dg, dg_ref.shape)

    return pl.pallas_call(
        body, name=name, grid=(S // tm, ns),
        in_specs=[
            pl.BlockSpec((tm, D), lambda i, j: (i, 0)),
            pl.BlockSpec((tm, D), lambda i, j: (i, 0)),
            pl.BlockSpec((1, D), lambda i, j: (0, 0)),
            pl.BlockSpec((None, tm, fs), lambda i, j: (j, i, 0)),
            pl.BlockSpec((None, tm, fs), lambda i, j: (j, i, 0)),
            pl.BlockSpec((None, D, fs), lambda i, j: (j, 0, 0)),
            pl.BlockSpec((None, D, fs), lambda i, j: (j, 0, 0)),
            pl.BlockSpec((None, fs, D), lambda i, j: (j, 0, 0)),
        ],
        out_specs=[
            pl.BlockSpec((tm, D), lambda i, j: (i, 0)),
            pl.BlockSpec((None, tm, fs), lambda i, j: (j, i, 0)),
            pl.BlockSpec((None, tm, fs), lambda i, j: (j, i, 0)),
            pl.BlockSpec((None, tm, fs), lambda i, j: (j, i, 0)),
            pl.BlockSpec((tm, D), lambda i, j: (i, 0)),
            pl.BlockSpec((8, D), lambda i, j: (0, 0)),
        ],
        out_shape=[
            jax.ShapeDtypeStruct((S, D), F32),
            jax.ShapeDtypeStruct((ns, S, fs), BF16),
            jax.ShapeDtypeStruct((ns, S, fs), BF16),
            jax.ShapeDtypeStruct((ns, S, fs), BF16),
            jax.ShapeDtypeStruct((S, D), BF16),
            jax.ShapeDtypeStruct((8, D), F32),
        ],
        scratch_shapes=[pltpu.VMEM((tm, D), BF16), pltpu.VMEM((tm, D), F32)],
        compiler_params=_cparams(),
    )(dxo, x, gain, a, b, wg, wu, wd)


def _tn_matmul(a, b, a_spec, b_spec, o_shape, o_spec, grid, name):
    kaxis = len(grid) - 1

    def body(a_ref, b_ref, o_ref):
        @pl.when(pl.program_id(kaxis) == 0)
        def _():
            o_ref[...] = jnp.zeros_like(o_ref)

        o_ref[...] += _dot_tn(a_ref[...].astype(BF16), b_ref[...].astype(BF16))

    return pl.pallas_call(
        body, name=name, grid=grid, in_specs=[a_spec, b_spec], out_specs=o_spec,
        out_shape=jax.ShapeDtypeStruct(o_shape, F32), compiler_params=_cparams(),
    )(a, b)


def _norm_proj(x, gain, w, name):
    S, D = x.shape
    ns, _, n = w.shape
    tm = _rows(S, 512)

    def body(x_ref, g_ref, w_ref, o_ref, h_ref, hs):
        @pl.when(pl.program_id(1) == 0)
        def _():
            h, _, _ = _rms(x_ref[...], g_ref[...])
            hb = h.astype(BF16)
            hs[...] = hb
            h_ref[...] = hb

        o_ref[...] = _dot(hs[...], w_ref[...])

    return pl.pallas_call(
        body, name=name, grid=(S // tm, ns),
        in_specs=[
            pl.BlockSpec((tm, D), lambda i, j: (i, 0)),
            pl.BlockSpec((1, D), lambda i, j: (0, 0)),
            pl.BlockSpec((None, D, n), lambda i, j: (j, 0, 0)),
        ],
        out_specs=[
            pl.BlockSpec((tm, n), lambda i, j: (i, j)),
            pl.BlockSpec((tm, D), lambda i, j: (i, 0)),
        ],
        out_shape=[jax.ShapeDtypeStruct((S, ns * n), F32), jax.ShapeDtypeStruct((S, D), BF16)],
        scratch_shapes=[pltpu.VMEM((tm, D), BF16)],
        compiler_params=_cparams(),
    )(x, gain, w)


def _norm_proj_bwd(dproj, w, dres, x, gain, name):
    S, D = x.shape
    ns, _, n = w.shape
    tm = _rows(S, 256)

    def body(dp_ref, w_ref, dres_ref, x_ref, g_ref, dx_ref, dg_ref, acc):
        i = pl.program_id(0)
        j = pl.program_id(1)

        @pl.when((i == 0) & (j == 0))
        def _():
            dg_ref[...] = jnp.zeros_like(dg_ref)

        @pl.when(j == 0)
        def _():
            acc[...] = jnp.zeros_like(acc)

        acc[...] += _dot_nt(dp_ref[...], w_ref[...])

        @pl.when(j == ns - 1)
        def _():
            g = g_ref[...]
            _, xn, r = _rms(x_ref[...], g)
            dx, dg = _rms_bwd(acc[...], xn, r, g)
            dx_ref[...] = dres_ref[...] + dx
            dg_ref[...] += jnp.broadcast_to(dg, dg_ref.shape)

    return pl.pallas_call(
        body, name=name, grid=(S // tm, ns),
        in_specs=[
            pl.BlockSpec((tm, n), lambda i, j: (i, j)),
            pl.BlockSpec((None, D, n), lambda i, j: (j, 0, 0)),
            pl.BlockSpec((tm, D), lambda i, j: (i, 0)),
            pl.BlockSpec((tm, D), lambda i, j: (i, 0)),
            pl.BlockSpec((1, D), lambda i, j: (0, 0)),
        ],
        out_specs=[
            pl.BlockSpec((tm, D), lambda i, j: (i, 0)),
            pl.BlockSpec((8, D), lambda i, j: (0, 0)),
        ],
        out_shape=[jax.ShapeDtypeStruct((S, D), F32), jax.ShapeDtypeStruct((8, D), F32)],
        scratch_shapes=[pltpu.VMEM((tm, D), F32)],
        compiler_params=_cparams(),
    )(dproj, w, dres, x, gain)


def _out_proj(ycat, w, res, name):
    S, K = ycat.shape
    D = w.shape[1]
    tm = _rows(S, 512)

    def body(y_ref, w_ref, r_ref, o_ref):
        o_ref[...] = r_ref[...] + _dot(y_ref[...], w_ref[...])

    return pl.pallas_call(
        body, name=name, grid=(S // tm,),
        in_specs=[
            pl.BlockSpec((tm, K), lambda i: (i, 0)),
            pl.BlockSpec((K, D), lambda i: (0, 0)),
            pl.BlockSpec((tm, D), lambda i: (i, 0)),
        ],
        out_specs=pl.BlockSpec((tm, D), lambda i: (i, 0)),
        out_shape=jax.ShapeDtypeStruct((S, D), F32),
        compiler_params=_cparams(),
    )(ycat, w, res)


def _out_proj_bwd(dx, w, name):
    S, D = dx.shape
    K = w.shape[0]
    tm = _rows(S, 512)

    def body(d_ref, w_ref, o_ref):
        o_ref[...] = _dot_nt(d_ref[...].astype(BF16), w_ref[...])

    return pl.pallas_call(
        body, name=name, grid=(S // tm,),
        in_specs=[pl.BlockSpec((tm, D), lambda i: (i, 0)), pl.BlockSpec((K, D), lambda i: (0, 0))],
        out_specs=pl.BlockSpec((tm, K), lambda i: (i, 0)),
        out_shape=jax.ShapeDtypeStruct((S, K), F32),
        compiler_params=_cparams(),
    )(dx, w)


CONV_COLS = 256


def _shift_down(z, halo, k, row):
    out = pltpu.roll(z, k, 0)
    for n in range(k):
        out = jnp.where(row == n, halo[8 - k + n:8 - k + n + 1, :], out)
    return out


def _shift_up(g, halo, k, row, ts):
    out = pltpu.roll(g, ts - k, 0)
    for n in range(k):
        out = jnp.where(row == ts - k + n, halo[n:n + 1, :], out)
    return out


def _conv_fwd(proj, cw, cb, nconv, name):
    S = proj.shape[0]
    ncb = nconv // CONV_COLS
    ts = _rows(S, 512)
    hb = ts // 8

    def body(b_ref, c_ref, u_ref, ch_ref, uh_ref, w_ref, bias_ref, o_ref):
        i = pl.program_id(1)
        z = c_ref[...] * u_ref[...]
        halo = jnp.where(i > 0, ch_ref[...] * uh_ref[...], 0.0)
        row = lax.broadcasted_iota(jnp.int32, z.shape, 0)
        w = w_ref[...]
        yc = w[0:1, :] * _shift_down(z, halo, 2, row) + w[1:2, :] * _shift_down(z, halo, 1, row) + w[2:3, :] * z
        o_ref[...] = (b_ref[...] * (yc + bias_ref[...])).astype(BF16)

    def blk(unit):
        return pl.BlockSpec((ts, CONV_COLS), lambda cbi, i: (i, unit * ncb + cbi))

    def prev(unit):
        return pl.BlockSpec((8, CONV_COLS), lambda cbi, i: (jnp.maximum(i * hb - 1, 0), unit * ncb + cbi))

    return pl.pallas_call(
        body, name=name, grid=(ncb, S // ts),
        in_specs=[blk(0), blk(1), blk(2), prev(1), prev(2),
                  pl.BlockSpec((8, CONV_COLS), lambda cbi, i: (0, cbi)),
                  pl.BlockSpec((1, CONV_COLS), lambda cbi, i: (0, cbi))],
        out_specs=pl.BlockSpec((ts, CONV_COLS), lambda cbi, i: (i, cbi)),
        out_shape=jax.ShapeDtypeStruct((S, nconv), BF16),
        compiler_params=_cparams(),
    )(proj, proj, proj, proj, proj, cw, cb)


def _conv_bwd(proj, dy, cw, cb, nconv, name):
    S = proj.shape[0]
    ncb = nconv // CONV_COLS
    ts = _rows(S, 512)
    hb = ts // 8
    nblk = S // ts

    def body(b_ref, c_ref, u_ref, dy_ref, ch_ref, uh_ref, bn_ref, dyn_ref, w_ref, bias_ref,
             db_ref, dc_ref, du_ref, dw_ref):
        i = pl.program_id(1)

        @pl.when(i == 0)
        def _():
            dw_ref[...] = jnp.zeros_like(dw_ref)

        c = c_ref[...]
        u = u_ref[...]
        bg = b_ref[...]
        dy_ = dy_ref[...]
        z = c * u
        halo = jnp.where(i > 0, ch_ref[...] * uh_ref[...], 0.0)
        row = lax.broadcasted_iota(jnp.int32, z.shape, 0)
        w = w_ref[...]
        z2 = _shift_down(z, halo, 2, row)
        z1 = _shift_down(z, halo, 1, row)
        yc = w[0:1, :] * z2 + w[1:2, :] * z1 + w[2:3, :] * z
        db_ref[...] = (dy_ * (yc + bias_ref[...])).astype(BF16)
        g = dy_ * bg
        gnext = jnp.where(i < nblk - 1, dyn_ref[...] * bn_ref[...], 0.0)
        dz = w[2:3, :] * g + w[1:2, :] * _shift_up(g, gnext, 1, row, ts) + w[0:1, :] * _shift_up(g, gnext, 2, row, ts)
        dc_ref[...] = (dz * u).astype(BF16)
        du_ref[...] = (dz * c).astype(BF16)
        r8 = lax.broadcasted_iota(jnp.int32, (8, CONV_COLS), 0)
        sums = [jnp.sum(g * z2, axis=0, keepdims=True), jnp.sum(g * z1, axis=0, keepdims=True),
                jnp.sum(g * z, axis=0, keepdims=True), jnp.sum(g, axis=0, keepdims=True)]
        upd = jnp.zeros((8, CONV_COLS), F32)
        for n, sv in enumerate(sums):
            upd = jnp.where(r8 == n, sv, upd)
        dw_ref[...] += upd

    def blk(unit):
        return pl.BlockSpec((ts, CONV_COLS), lambda cbi, i: (i, unit * ncb + cbi))

    def prev(unit):
        return pl.BlockSpec((8, CONV_COLS), lambda cbi, i: (jnp.maximum(i * hb - 1, 0), unit * ncb + cbi))

    def nxt(unit):
        return pl.BlockSpec((8, CONV_COLS), lambda cbi, i: (jnp.minimum((i + 1) * hb, S // 8 - 1), unit * ncb + cbi))

    o = pl.BlockSpec((ts, CONV_COLS), lambda cbi, i: (i, cbi))
    return pl.pallas_call(
        body, name=name, grid=(ncb, nblk),
        in_specs=[blk(0), blk(1), blk(2), blk(0), prev(1), prev(2), nxt(0), nxt(0),
                  pl.BlockSpec((8, CONV_COLS), lambda cbi, i: (0, cbi)),
                  pl.BlockSpec((1, CONV_COLS), lambda cbi, i: (0, cbi))],
        out_specs=[o, o, o, pl.BlockSpec((8, CONV_COLS), lambda cbi, i: (0, cbi))],
        out_shape=[jax.ShapeDtypeStruct((S, nconv), BF16)] * 3 + [jax.ShapeDtypeStruct((8, nconv), F32)],
        compiler_params=_cparams(),
    )(proj, proj, proj, dy, proj, proj, proj, dy, cw, cb)


def _group_ones(n):
    r = lax.broadcasted_iota(jnp.int32, (n, n), 0) // HEAD_DIM
    c = lax.broadcasted_iota(jnp.int32, (n, n), 1) // HEAD_DIM
    return jnp.where(r == c, 1.0, 0.0).astype(BF16)


def _qk_norm(proj, gain_t, unit0, nsb, scale, name):
    S = proj.shape[0]
    nb = nsb // CONV_COLS
    ts = _rows(S, 512)

    def body(x_ref, g_ref, o_ref):
        x = x_ref[...]
        ss = _split_dot(x * x, _group_ones(CONV_COLS))
        r = lax.rsqrt(ss * (1.0 / HEAD_DIM) + EPS)
        o_ref[...] = ((x * r) * g_ref[...] * scale).astype(BF16)

    return pl.pallas_call(
        body, name=name, grid=(nb, S // ts),
        in_specs=[pl.BlockSpec((ts, CONV_COLS), lambda u, i: (i, unit0 + u)),
                  pl.BlockSpec((1, CONV_COLS), lambda u, i: (0, 0))],
        out_specs=pl.BlockSpec((ts, CONV_COLS), lambda u, i: (i, u)),
        out_shape=jax.ShapeDtypeStruct((S, nsb), BF16),
        compiler_params=_cparams(),
    )(proj, gain_t)


def _qk_norm_bwd(proj, dout, gain_t, unit0, nsb, scale, name):
    S = proj.shape[0]
    nb = nsb // CONV_COLS
    ts = _rows(S, 512)

    def body(x_ref, d_ref, g_ref, dx_ref, dg_ref):
        @pl.when(pl.program_id(1) == 0)
        def _():
            dg_ref[...] = jnp.zeros_like(dg_ref)

        x = x_ref[...]
        g = g_ref[...]
        ones = _group_ones(CONV_COLS)
        ss = _split_dot(x * x, ones)
        r = lax.rsqrt(ss * (1.0 / HEAD_DIM) + EPS)
        xn = x * r
        dh = d_ref[...] * scale
        dxn = dh * g
        m = _split_dot(dxn * xn, ones) * (1.0 / HEAD_DIM)
        dx_ref[...] = (r * (dxn - xn * m)).astype(BF16)
        dg_ref[...] += jnp.broadcast_to(jnp.sum(dh * xn, axis=0, keepdims=True), dg_ref.shape)

    return pl.pallas_call(
        body, name=name, grid=(nb, S // ts),
        in_specs=[pl.BlockSpec((ts, CONV_COLS), lambda u, i: (i, unit0 + u)),
                  pl.BlockSpec((ts, CONV_COLS), lambda u, i: (i, u)),
                  pl.BlockSpec((1, CONV_COLS), lambda u, i: (0, 0))],
        out_specs=[pl.BlockSpec((ts, CONV_COLS), lambda u, i: (i, u)),
                   pl.BlockSpec((8, CONV_COLS), lambda u, i: (0, u))],
        out_shape=[jax.ShapeDtypeStruct((S, nsb), BF16), jax.ShapeDtypeStruct((8, nsb), F32)],
        compiler_params=_cparams(),
    )(proj, dout, gain_t)


Z_CLAMP = 80.0
N_SLOTS = 3


def _head_masks():
    lane = lax.broadcasted_iota(jnp.int32, (1, LANES), 1)
    return [lane < HEAD_DIM, lane >= HEAD_DIM], lane


def _tile_consts(T):
    r_i = lax.broadcasted_iota(jnp.int32, (T, T), 0)
    c_i = lax.broadcasted_iota(jnp.int32, (T, T), 1)
    neg_suffix = jnp.where(r_i >= c_i, -1.0, 0.0).astype(BF16)
    prefix = jnp.where(r_i <= c_i, 1.0, 0.0).astype(BF16)
    return neg_suffix, prefix, c_i < r_i


def _pipeline(n_items, stages):
    depth = len(stages)
    last = jnp.maximum(n_items - 1, 0)

    def group(g, carry):
        for u in range(N_SLOTS):
            m = g * N_SLOTS + u
            for k in reversed(range(depth)):
                t = m - k
                valid = (t >= 0) & (t < n_items)
                stages[k](jnp.clip(t, 0, last), (u - k) % N_SLOTS, valid)
        return carry

    lax.fori_loop(0, (n_items + depth - 1 + N_SLOTS - 1) // N_SLOTS, group, 0)


def _attn_fwd(q, k, v, name):
    S, nsb = q.shape
    T = ATT_TILE
    hp = nsb // LANES
    nb = S // T
    assert nb <= HEAD_DIM

    def body(q_ref, k_ref, v_ref, y_ref, cs_ref, c_ref, acc, z_st, inc_st):
        i = pl.program_id(1)

        @pl.when((pl.program_id(0) == 0) & (i == 0))
        def _():
            z_st[...] = jnp.zeros_like(z_st)
            inc_st[...] = jnp.zeros_like(inc_st)

        masks, lane = _head_masks()
        qv = q_ref[...]
        qm = [jnp.where(m, qv, jnp.zeros_like(qv)) for m in masks]
        neg_suffix, _, causal = _tile_consts(T)
        c_ref[...] = jnp.zeros_like(c_ref)
        acc[...] = jnp.zeros_like(acc)
        cs_ref[...] = jnp.zeros_like(cs_ref)

        def blk(ref, j):
            return ref[pl.ds(pl.multiple_of(j * T, T), T), :]

        def scores(j, slot):
            kj = blk(k_ref, j)
            for h in range(2):
                z_st[slot, h] = jnp.minimum(_dot_nt(qm[h], kj), Z_CLAMP)

        def suffix_sums(slot, diag):
            for h in range(2):
                sp = jnp.log(1.0 + jnp.exp(z_st[slot, h]))
                if diag:
                    sp = jnp.where(causal, sp, 0.0)
                inc_st[slot, h] = _split_dot(sp, neg_suffix)

        def weights(j, slot, valid, diag):
            vj = blk(v_ref, j)
            for h in range(2):
                inc = inc_st[slot, h]
                c = c_ref[h]
                a = jnp.exp(z_st[slot, h] + inc + c)
                if diag:
                    a = jnp.where(causal, a, 0.0)
                upd = _dot(a.astype(BF16), vj)
                tot = inc[:, 0:1]
                col = j + HEAD_DIM * h
                keep = masks[h]
                if valid is not None:
                    keep = keep & valid
                    tot = jnp.where(valid, tot, 0.0)
                    col = jnp.where(valid, col, -1)
                acc[...] += jnp.where(keep, upd, 0.0)
                cs_ref[...] = jnp.where(lane == col, c, cs_ref[...])
                c_ref[h] = c + tot

        scores(i, 0)
        suffix_sums(0, True)
        weights(i, 0, None, True)

        @pl.when(i > 0)
        def _():
            key = lambda t: jnp.maximum(i - 1 - t, 0)
            _pipeline(i, [lambda t, slot, valid: scores(key(t), slot),
                          lambda t, slot, valid: suffix_sums(slot, False),
                          lambda t, slot, valid: weights(key(t), slot, valid, False)])

        y_ref[...] = acc[...].astype(BF16)

    return pl.pallas_call(
        body, name=name, grid=(hp, nb),
        in_specs=[pl.BlockSpec((T, LANES), lambda p, i: (i, p)),
                  pl.BlockSpec((S, LANES), lambda p, i: (0, p)),
                  pl.BlockSpec((S, LANES), lambda p, i: (0, p))],
        out_specs=[pl.BlockSpec((T, LANES), lambda p, i: (i, p)),
                   pl.BlockSpec((None, T, LANES), lambda p, i: (p, i, 0))],
        out_shape=[jax.ShapeDtypeStruct((S, nsb), BF16), jax.ShapeDtypeStruct((hp, S, LANES), F32)],
        scratch_shapes=[pltpu.VMEM((2, T, 1), F32), pltpu.VMEM((T, LANES), F32),
                        pltpu.VMEM((N_SLOTS, 2, T, T), F32), pltpu.VMEM((N_SLOTS, 2, T, T), F32)],
        compiler_params=_cparams(),
    )(q, k, v)


def _attn_bwd(q, k, v, dy, col0, carry, name):
    S, nsb = q.shape
    T = ATT_TILE
    hp = nsb // LANES
    nb = S // T

    def body(q_ref, k_ref, v_ref, dy_ref, cs_ref, dq_ref, dk_ref, dv_ref, e_ref, acc,
             z_st, da_st, b_st, inc_st, a_st, e_st, p_st):
        i = pl.program_id(1)

        @pl.when((pl.program_id(0) == 0) & (i == 0))
        def _():
            for st in (z_st, da_st, b_st, inc_st, a_st, e_st, p_st):
                st[...] = jnp.zeros_like(st)

        @pl.when(i == 0)
        def _():
            dk_ref[...] = jnp.zeros_like(dk_ref)
            dv_ref[...] = jnp.zeros_like(dv_ref)

        masks, lane = _head_masks()
        qv = q_ref[...]
        dyb = dy_ref[...].astype(BF16)
        qm = [jnp.where(m, qv, jnp.zeros_like(qv)) for m in masks]
        dym = [jnp.where(m, dyb, jnp.zeros_like(dyb)) for m in masks]
        neg_suffix, prefix, causal = _tile_consts(T)
        e_ref[...] = jnp.zeros_like(e_ref)
        acc[...] = jnp.zeros_like(acc)

        def blk(ref, j):
            return ref[pl.ds(pl.multiple_of(j * T, T), T), :]

        def scores(j, slot):
            kj = blk(k_ref, j)
            vj = blk(v_ref, j)
            for h in range(2):
                z_st[slot, h] = jnp.minimum(_dot_nt(qm[h], kj), Z_CLAMP)
                da_st[slot, h] = _dot_nt(dym[h], vj)

        def suffix_sums(slot, diag):
            for h in range(2):
                u = jnp.exp(z_st[slot, h])
                w = 1.0 + u
                b_st[slot, h] = u / w
                sp = jnp.log(w)
                if diag:
                    sp = jnp.where(causal, sp, 0.0)
                inc_st[slot, h] = _split_dot(sp, neg_suffix)

        def probs(j, slot, diag):
            csv = cs_ref[...]
            for h in range(2):
                c = jnp.sum(jnp.where(lane == j + HEAD_DIM * h, csv, 0.0), axis=-1, keepdims=True)
                a = jnp.exp(z_st[slot, h] + inc_st[slot, h] + c)
                if diag:
                    a = jnp.where(causal, a, 0.0)
                a_st[slot, h] = a.astype(BF16)
                e = a * da_st[slot, h]
                e_st[slot, h] = e
                p_st[slot, h] = _split_dot(e, prefix)

        def grads(j, slot, valid, diag):
            kj = blk(k_ref, j)
            off = pl.multiple_of(j * T, T)
            for h in range(2):
                p = p_st[slot, h]
                dz = e_st[slot, h] - b_st[slot, h] * (e_ref[h] + p)
                if diag:
                    dz = jnp.where(causal, dz, 0.0)
                dzb = dz.astype(BF16)
                upd_q = _dot(dzb, kj)
                upd_k = _dot_tn(dzb, qm[h])
                upd_v = _dot_tn(a_st[slot, h], dym[h])
                tot = p[:, T - 1:T]
                keep = masks[h]
                if valid is not None:
                    keep = keep & valid
                    upd_k, upd_v, tot = (jnp.where(valid, t_, 0.0) for t_ in (upd_k, upd_v, tot))
                acc[...] += jnp.where(keep, upd_q, 0.0)
                dk_ref[pl.ds(off, T), :] += upd_k
                dv_ref[pl.ds(off, T), :] += upd_v
                e_ref[h] += tot

        @pl.when(i > 0)
        def _():
            _pipeline(i, [lambda t, slot, valid: scores(t, slot),
                          lambda t, slot, valid: suffix_sums(slot, False),
                          lambda t, slot, valid: probs(t, slot, False),
                          lambda t, slot, valid: grads(t, slot, valid, False)])

        scores(i, 0)
        suffix_sums(0, True)
        probs(i, 0, True)
        grads(i, 0, None, True)
        dq_ref[...] = acc[...]

    return pl.pallas_call(
        body, name=name, grid=(hp, nb),
        in_specs=[pl.BlockSpec((T, LANES), lambda p, i: (i, p)),
                  pl.BlockSpec((S, LANES), lambda p, i: (0, p)),
                  pl.BlockSpec((S, LANES), lambda p, i: (0, p)),
                  pl.BlockSpec((T, LANES), lambda p, i: (i, col0 + p)),
                  pl.BlockSpec((None, T, LANES), lambda p, i: (p, i, 0))],
        out_specs=[pl.BlockSpec((T, LANES), lambda p, i: (i, p)),
                   pl.BlockSpec((S, LANES), lambda p, i: (0, p)),
                   pl.BlockSpec((S, LANES), lambda p, i: (0, p))],
        out_shape=[jax.ShapeDtypeStruct((S, nsb), F32)] * 3,
        scratch_shapes=[pltpu.VMEM((2, T, 1), F32), pltpu.VMEM((T, LANES), F32)]
        + [pltpu.VMEM((N_SLOTS, 2, T, T), dt) for dt in (F32, F32, F32, F32, BF16, F32, F32)],
        compiler_params=_cparams(),
    )(q, k, v, dy, carry)


def _ple(x, p, tgt, gain, wpg, wpp, name):
    S, D = x.shape
    P = p.shape[1]
    ns, _, nc = wpp.shape
    tm = _rows(S, 256)

    def body(x_ref, p_ref, t_ref, g_ref, wpg_ref, wpp_ref, dx_ref, h_ref, du_ref, dpp_ref, loss_ref, dg_ref):
        @pl.when(pl.program_id(0) == 0)
        def _():
            loss_ref[...] = jnp.zeros_like(loss_ref)
            dg_ref[...] = jnp.zeros_like(dg_ref)

        x_ = x_ref[...]
        g = g_ref[...]
        h, xn, r = _rms(x_, g)
        hb = h.astype(BF16)
        h_ref[...] = hb
        gate = jax.nn.sigmoid(_dot(hb, wpg_ref[...]))
        pb = p_ref[...].astype(BF16)
        pp = jnp.concatenate([_dot(pb, wpp_ref[n]) for n in range(ns)], axis=1)
        err = (x_ + gate * pp) - t_ref[...]
        loss_ref[...] += (0.5 / D) * jnp.sum(err * err)
        dy = err * (1.0 / D)
        du = ((dy * pp) * (gate * (1.0 - gate))).astype(BF16)
        du_ref[...] = du
        dpp_ref[...] = (dy * gate).astype(BF16)
        dx, dg = _rms_bwd(_dot_nt(du, wpg_ref[...]), xn, r, g)
        dx_ref[...] = dy + dx
        dg_ref[...] += jnp.broadcast_to(dg, dg_ref.shape)

    row = pl.BlockSpec((tm, D), lambda i: (i, 0))
    return pl.pallas_call(
        body, name=name, grid=(S // tm,),
        in_specs=[row, pl.BlockSpec((tm, P), lambda i: (i, 0)), row,
                  pl.BlockSpec((1, D), lambda i: (0, 0)),
                  pl.BlockSpec((D, D), lambda i: (0, 0)),
                  pl.BlockSpec((ns, P, nc), lambda i: (0, 0, 0))],
        out_specs=[row, row, row, row,
                   pl.BlockSpec((8, LANES), lambda i: (0, 0)),
                   pl.BlockSpec((8, D), lambda i: (0, 0))],
        out_shape=[jax.ShapeDtypeStruct((S, D), F32)] + [jax.ShapeDtypeStruct((S, D), BF16)] * 3
        + [jax.ShapeDtypeStruct((8, LANES), F32), jax.ShapeDtypeStruct((8, D), F32)],
        compiler_params=_cparams(),
    )(x, p, tgt, gain, wpg, wpp)


def _elementwise(fn, ins, n_out, name):
    R, C = ins[0].shape
    tr = _rows(R, 512)

    def body(*refs):
        outs = fn(*[r[...] for r in refs[:len(ins)]])
        for o_ref, o in zip(refs[len(ins):], outs):
            o_ref[...] = o

    spec = pl.BlockSpec((tr, C), lambda i: (i, 0))
    return pl.pallas_call(
        body, name=name, grid=(R // tr,), in_specs=[spec] * len(ins), out_specs=[spec] * n_out,
        out_shape=[jax.ShapeDtypeStruct((R, C), F32)] * n_out, compiler_params=_cparams(),
    )(*ins)


def _adamw(w, g, m, v):
    m = ADAM_B1 * m + (1.0 - ADAM_B1) * g
    v = ADAM_B2 * v + (1.0 - ADAM_B2) * jnp.square(g)
    m_hat = m / (1.0 - ADAM_B1 ** ADAM_STEP)
    v_hat = v / (1.0 - ADAM_B2 ** ADAM_STEP)
    delta = -ADAM_LR * (m_hat / (jnp.sqrt(v_hat) + ADAM_EPS) + ADAM_WD * w)
    return delta, m, v


def _place():
    x, y, c = lax.axis_index("x"), lax.axis_index("y"), lax.axis_index("c")
    chips = [(1 - x, y), (x, 1 - y), (1 - x, 1 - y)]
    return x, y, c, chips


def _half(ref, c, axis_rows):
    n = ref.shape[-2]
    start = pl.multiple_of(c * (n // 2), 8)
    idx = (slice(None),) * (len(ref.shape) - 2) + (pl.ds(start, n // 2), slice(None))
    return ref.at[idx]


def _gather_weights(shards, small, name):
    n = len(shards)

    def body(*refs):
        ins, small_in = refs[:n], refs[n]
        outs, small_out = refs[n + 1:2 * n + 1], refs[2 * n + 1]
        lsem, lrsem, ssem, rsem, sm_s, sm_r = refs[2 * n + 2:]
        x, y, c, chips = _place()
        j = 2 * x + y
        sib = (x, y, 1 - c)

        local = [pltpu.make_async_remote_copy(
            src_ref=ins[a], dst_ref=outs[a].at[j], send_sem=lsem.at[a], recv_sem=lrsem.at[a],
            device_id=sib, device_id_type=MESH) for a in range(n)]
        for cp in local:
            cp.start()
        small_out[j] = small_in[...]
        small_cp = [pltpu.make_async_remote_copy(
            src_ref=small_in, dst_ref=small_out.at[j], send_sem=sm_s.at[k], recv_sem=sm_r.at[k],
            device_id=(*chip, c), device_id_type=MESH) for k, chip in enumerate(chips)]
        for cp in small_cp:
            cp.start()

        def ici(a, k, chip, jj, dev):
            return pltpu.make_async_remote_copy(
                src_ref=_half(ins[a], c, True) if dev is not None else _half(outs[a].at[jj], c, True),
                dst_ref=_half(outs[a].at[jj], c, True),
                send_sem=ssem.at[a, k], recv_sem=rsem.at[a, k],
                device_id=dev if dev is not None else (*chip, c), device_id_type=MESH)

        first = []
        for a in range(n):
            for k, chip in enumerate(chips):
                cp = ici(a, k, chip, j, (*chip, c))
                cp.start()
                first.append(cp)
        passed = []
        for a in range(n):
            for k, chip in enumerate(chips):
                jj = 2 * chip[0] + chip[1]
                ici(a, k, chip, jj, None).wait_recv()
                fw = pltpu.make_async_remote_copy(
                    src_ref=_half(outs[a].at[jj], c, True), dst_ref=_half(outs[a].at[jj], c, True),
                    send_sem=ssem.at[a, 3 + k], recv_sem=rsem.at[a, 3 + k], device_id=sib, device_id_type=MESH)
                fw.start()
                passed.append(fw)
        for a in range(n):
            for k, chip in enumerate(chips):
                jj = 2 * chip[0] + chip[1]
                pltpu.make_async_remote_copy(
                    src_ref=_half(outs[a].at[jj], 1 - c, True), dst_ref=_half(outs[a].at[jj], 1 - c, True),
                    send_sem=ssem.at[a, 3 + k], recv_sem=rsem.at[a, 3 + k], device_id=sib,
                    device_id_type=MESH).wait_recv()
        for cp in small_cp:
            cp.wait()
        for cp in first + passed:
            cp.wait_send()
        for cp in local:
            cp.wait()

    return pl.pallas_call(
        body, name=name,
        in_specs=[HBM] * n + [VMEM_WHOLE],
        out_specs=[HBM] * n + [VMEM_WHOLE],
        out_shape=[jax.ShapeDtypeStruct((N_SHARDS,) + s.shape, s.dtype) for s in shards]
        + [jax.ShapeDtypeStruct((N_SHARDS,) + small.shape, small.dtype)],
        scratch_shapes=[pltpu.SemaphoreType.DMA((n,)), pltpu.SemaphoreType.DMA((n,)),
                        pltpu.SemaphoreType.DMA((n, 6)), pltpu.SemaphoreType.DMA((n, 6)),
                        pltpu.SemaphoreType.DMA((3,)), pltpu.SemaphoreType.DMA((3,))],
    )(*shards, small)


def _swap_halves(grads, small, name):
    n = len(grads)

    def body(*refs):
        ins, small_in = refs[:n], refs[n]
        outs, small_out = refs[n + 1:2 * n + 1], refs[2 * n + 1]
        buf, ssem, rsem, sm_s, sm_r = refs[2 * n + 2:]
        x, y, c, _ = _place()
        me = 4 * x + 2 * y + c
        sib = (x, y, 1 - c)
        cps = [pltpu.make_async_remote_copy(
            src_ref=_half(ins[a], 1 - c, True), dst_ref=outs[a], send_sem=ssem.at[a], recv_sem=rsem.at[a],
            device_id=sib, device_id_type=MESH) for a in range(n)]
        for cp in cps:
            cp.start()
        buf[me] = small_in[...]
        peers = [(fx, fy, fc) for fx in (0, 1) for fy in (0, 1) for fc in (0, 1)][1:]
        sm = []
        for k, (fx, fy, fc) in enumerate(peers):
            dev = (1 - x if fx else x, 1 - y if fy else y, 1 - c if fc else c)
            cp = pltpu.make_async_remote_copy(
                src_ref=small_in, dst_ref=buf.at[me], send_sem=sm_s.at[k], recv_sem=sm_r.at[k],
                device_id=dev, device_id_type=MESH)
            cp.start()
            sm.append(cp)
        for cp in sm:
            cp.wait()
        tot = buf[0]
        for d in range(1, N_DEV):
            tot = tot + buf[d]
        small_out[...] = tot
        for cp in cps:
            cp.wait()

    return pl.pallas_call(
        body, name=name,
        in_specs=[HBM] * n + [VMEM_WHOLE],
        out_specs=[HBM] * n + [VMEM_WHOLE],
        out_shape=[jax.ShapeDtypeStruct((g.shape[0], g.shape[1] // 2, g.shape[2]), F32) for g in grads]
        + [jax.ShapeDtypeStruct(small.shape, F32)],
        scratch_shapes=[pltpu.VMEM((N_DEV,) + small.shape, F32),
                        pltpu.SemaphoreType.DMA((n,)), pltpu.SemaphoreType.DMA((n,)),
                        pltpu.SemaphoreType.DMA((N_DEV - 1,)), pltpu.SemaphoreType.DMA((N_DEV - 1,))],
    )(*grads, small)


def _chip_sum(g, recv, core, name):
    ns, R, C = g.shape
    r2 = R // 2
    tr = _rows(r2, 512)
    nrb = r2 // tr

    def body(core_ref, g_ref, r_ref, o_ref, ob_ref):
        s = g_ref[...] + r_ref[...]
        o_ref[...] = s
        ob_ref[...] = s.astype(BF16)

    out = pl.BlockSpec((None, tr, C), lambda s, i, cr: (s, i, 0))
    return pl.pallas_call(
        body, name=name,
        grid_spec=pltpu.PrefetchScalarGridSpec(
            num_scalar_prefetch=1, grid=(ns, nrb),
            in_specs=[pl.BlockSpec((None, tr, C), lambda s, i, cr: (s, cr[0] * nrb + i, 0)), out],
            out_specs=[out, out]),
        out_shape=[jax.ShapeDtypeStruct((ns, r2, C), F32), jax.ShapeDtypeStruct((ns, r2, C), BF16)],
        compiler_params=_cparams(),
    )(core, g, recv)


def _shard_sum(csum, got, place, name):
    _, r2, C = csum.shape
    tr = _rows(r2, 512)
    nrb = r2 // tr

    def body(place_ref, c_ref, g0_ref, g1_ref, g2_ref, o_ref):
        o_ref[...] = ((c_ref[...] + g0_ref[...].astype(F32)) + g1_ref[...].astype(F32)) + g2_ref[...].astype(F32)

    def got_spec(k):
        return pl.BlockSpec((None, tr, C), lambda i, pr: (k, i, 0))

    return pl.pallas_call(
        body, name=name,
        grid_spec=pltpu.PrefetchScalarGridSpec(
            num_scalar_prefetch=1, grid=(nrb,),
            in_specs=[pl.BlockSpec((None, tr, C), lambda i, pr: (pr[0], i, 0)), got_spec(0), got_spec(1), got_spec(2)],
            out_specs=pl.BlockSpec((tr, C), lambda i, pr: (pr[1] * nrb + i, 0))),
        out_shape=jax.ShapeDtypeStruct((2 * r2, C), F32),
        compiler_params=_cparams(),
    )(place, csum, got, got, got)


def _scatter_chip_sums(csums, name):
    n = len(csums)

    def body(*refs):
        ins, outs = refs[:n], refs[n:2 * n]
        ssem, rsem = refs[2 * n:]
        x, y, c, chips = _place()
        cps = []
        for a in range(n):
            for k, chip in enumerate(chips):
                jj = 2 * chip[0] + chip[1]
                cp = pltpu.make_async_remote_copy(
                    src_ref=ins[a].at[jj], dst_ref=outs[a].at[k], send_sem=ssem.at[a, k], recv_sem=rsem.at[a, k],
                    device_id=(*chip, c), device_id_type=MESH)
                cp.start()
                cps.append(cp)
        for cp in cps:
            cp.wait()

    return pl.pallas_call(
        body, name=name, in_specs=[HBM] * n, out_specs=[HBM] * n,
        out_shape=[jax.ShapeDtypeStruct((3,) + g.shape[1:], g.dtype) for g in csums],
        scratch_shapes=[pltpu.SemaphoreType.DMA((n, 3)), pltpu.SemaphoreType.DMA((n, 3))],
    )(*csums)


def _join_halves(fulls, name):
    n = len(fulls)

    def body(*refs):
        outs = refs[n:2 * n]
        ssem, rsem = refs[2 * n:]
        x, y, c, _ = _place()
        cps = [pltpu.make_async_remote_copy(
            src_ref=_half(outs[a], c, True), dst_ref=_half(outs[a], c, True), send_sem=ssem.at[a],
            recv_sem=rsem.at[a], device_id=(x, y, 1 - c), device_id_type=MESH) for a in range(n)]
        for cp in cps:
            cp.start()
        for cp in cps:
            cp.wait()

    return pl.pallas_call(
        body, name=name, in_specs=[HBM] * n, out_specs=[HBM] * n,
        out_shape=[jax.ShapeDtypeStruct(f.shape, F32) for f in fulls],
        input_output_aliases={a: a for a in range(n)},
        scratch_shapes=[pltpu.SemaphoreType.DMA((n,))] * 2,
    )(*fulls)


def _pad_rows(a, rows, cols):
    return jnp.pad(a, ((0, rows - a.shape[0]), (0, cols - a.shape[1])))


def kernel(x, p, ffn1_norm, ffn1_w_gate, ffn1_w_up, ffn1_w_down, mix_norm, w_in, conv_w, conv_b, q_norm, k_norm, w_out, ffn2_norm, ffn2_w_gate, ffn2_w_up, ffn2_w_down, ple_norm, ple_w_gate, ple_w_proj, loss_target, m_ffn1_norm, m_ffn1_w_gate, m_ffn1_w_up, m_ffn1_w_down, m_mix_norm, m_w_in, m_conv_w, m_conv_b, m_q_norm, m_k_norm, m_w_out, m_ffn2_norm, m_ffn2_w_gate, m_ffn2_w_up, m_ffn2_w_down, m_ple_norm, m_ple_w_gate, m_ple_w_proj, v_ffn1_norm, v_ffn1_w_gate, v_ffn1_w_up, v_ffn1_w_down, v_mix_norm, v_w_in, v_conv_w, v_conv_b, v_q_norm, v_k_norm, v_w_out, v_ffn2_norm, v_ffn2_w_gate, v_ffn2_w_up, v_ffn2_w_down, v_ple_norm, v_ple_w_gate, v_ple_w_proj):
    big = dict(ffn1_w_gate=ffn1_w_gate, ffn1_w_up=ffn1_w_up, ffn1_w_down=ffn1_w_down, w_in=w_in, w_out=w_out,
               ffn2_w_gate=ffn2_w_gate, ffn2_w_up=ffn2_w_up, ffn2_w_down=ffn2_w_down,
               ple_w_gate=ple_w_gate, ple_w_proj=ple_w_proj)
    big_m = dict(ffn1_w_gate=m_ffn1_w_gate, ffn1_w_up=m_ffn1_w_up, ffn1_w_down=m_ffn1_w_down, w_in=m_w_in,
                 w_out=m_w_out, ffn2_w_gate=m_ffn2_w_gate, ffn2_w_up=m_ffn2_w_up, ffn2_w_down=m_ffn2_w_down,
                 ple_w_gate=m_ple_w_gate, ple_w_proj=m_ple_w_proj)
    big_v = dict(ffn1_w_gate=v_ffn1_w_gate, ffn1_w_up=v_ffn1_w_up, ffn1_w_down=v_ffn1_w_down, w_in=v_w_in,
                 w_out=v_w_out, ffn2_w_gate=v_ffn2_w_gate, ffn2_w_up=v_ffn2_w_up, ffn2_w_down=v_ffn2_w_down,
                 ple_w_gate=v_ple_w_gate, ple_w_proj=v_ple_w_proj)
    names = list(big)
    xs = x[0]
    ps = p[0, 0]
    tgt = loss_target[0]
    S, D = xs.shape
    nconv = conv_b.shape[1]
    nsb = D - nconv
    cwl = conv_w.shape[2]
    jchip = 2 * lax.axis_index("x") + lax.axis_index("y")
    core = lax.axis_index("c")

    shards = [big[k][0].astype(BF16) for k in names]
    gathered = _gather_weights(shards, _pad_rows(conv_w[0], 8, LANES), "gather_weights")
    W = dict(zip(names, gathered[:-1]))
    cw_full = jnp.transpose(gathered[-1][:, :, :cwl], (1, 0, 2)).reshape(8, N_SHARDS * cwl)
    wout_full = W["w_out"].reshape(-1, D)
    wpg_full = W["ple_w_gate"].reshape(-1, D)
    qg = jnp.tile(q_norm, (1, CONV_COLS // HEAD_DIM))
    kg = jnp.tile(k_norm, (1, CONV_COLS // HEAD_DIM))
    n_units = nconv // CONV_COLS

    x1, h1, a1, b1 = _ffn_fwd(xs, ffn1_norm, W["ffn1_w_gate"], W["ffn1_w_up"], W["ffn1_w_down"], "ffn1_fwd")
    proj, h2 = _norm_proj(x1, mix_norm, W["w_in"], "mix_in_proj")
    y_conv = _conv_fwd(proj, cw_full, conv_b, nconv, "conv_fwd")
    qs = _qk_norm(proj, qg, 3 * n_units, nsb, HEAD_DIM ** -0.5, "q_norm_fwd")
    kh = _qk_norm(proj, kg, 4 * n_units, nsb, 1.0, "k_norm_fwd")
    vb = proj[:, 3 * nconv + 2 * nsb:].astype(BF16)
    y_sb, carry = _attn_fwd(qs, kh, vb, "attn_fwd")
    ycat = jnp.concatenate([y_conv, y_sb], axis=1)
    x2 = _out_proj(ycat, wout_full, x1, "mix_out_proj")
    x3, h3, a3, b3 = _ffn_fwd(x2, ffn2_norm, W["ffn2_w_gate"], W["ffn2_w_up"], W["ffn2_w_down"], "ffn2_fwd")

    dx3, h4, du4, dpp, loss_blk, dg_ple = _ple(x3, ps, tgt, ple_norm, wpg_full, W["ple_w_proj"], "ple_loss")
    G = {}
    tk = _rows(S, 512)
    nk = S // tk
    kd = wpg_full.shape[0] // N_SHARDS
    G["ple_w_gate"] = _tn_matmul(
        h4, du4, pl.BlockSpec((tk, kd), lambda m, k: (k, m)), pl.BlockSpec((tk, D), lambda m, k: (k, 0)),
        (N_SHARDS, kd, D), pl.BlockSpec((None, kd, D), lambda m, k: (m, 0, 0)), (N_SHARDS, nk), "ple_w_gate_grad")
    P = ps.shape[1]
    npp = D // N_SHARDS
    G["ple_w_proj"] = _tn_matmul(
        ps, dpp, pl.BlockSpec((tk, P), lambda m, k: (k, 0)), pl.BlockSpec((tk, npp), lambda m, k: (k, m)),
        (N_SHARDS, P, npp), pl.BlockSpec((None, P, npp), lambda m, k: (m, 0, 0)), (N_SHARDS, nk), "ple_w_proj_grad")

    def ffn_grads(pre, h, s, da, db, dy):
        fs = s.shape[2]
        hs = pl.BlockSpec((tk, D), lambda m, k: (k, 0))
        ss = pl.BlockSpec((None, tk, fs), lambda m, k: (m, k, 0))
        G[pre + "_w_gate"] = _tn_matmul(h, da, hs, ss, (N_SHARDS, D, fs),
                                        pl.BlockSpec((None, D, fs), lambda m, k: (m, 0, 0)), (N_SHARDS, nk), pre + "_w_gate_grad")
        G[pre + "_w_up"] = _tn_matmul(h, db, hs, ss, (N_SHARDS, D, fs),
                                      pl.BlockSpec((None, D, fs), lambda m, k: (m, 0, 0)), (N_SHARDS, nk), pre + "_w_up_grad")
        G[pre + "_w_down"] = _tn_matmul(s, dy, ss, hs, (N_SHARDS, fs, D),
                                        pl.BlockSpec((None, fs, D), lambda m, k: (m, 0, 0)), (N_SHARDS, nk), pre + "_w_down_grad")

    dx2, s3, da3, db3, dy3, dg_ffn2 = _ffn_bwd(dx3, x2, ffn2_norm, a3, b3, W["ffn2_w_gate"], W["ffn2_w_up"], W["ffn2_w_down"], "ffn2_bwd")
    ffn_grads("ffn2", h3, s3, da3, db3, dy3)

    dycat = _out_proj_bwd(dx2, wout_full, "mix_out_proj_bwd")
    ko = wout_full.shape[0] // N_SHARDS
    G["w_out"] = _tn_matmul(
        ycat, dx2, pl.BlockSpec((tk, ko), lambda m, k: (k, m)), pl.BlockSpec((tk, D), lambda m, k: (k, 0)),
        (N_SHARDS, ko, D), pl.BlockSpec((None, ko, D), lambda m, k: (m, 0, 0)), (N_SHARDS, nk), "w_out_grad")
    db_, dc_, du_, dwb = _conv_bwd(proj, dycat, cw_full, conv_b, nconv, "conv_bwd")
    dqs, dkh, dv = _attn_bwd(qs, kh, vb, dycat, nconv // LANES, carry, "attn_bwd")
    dq, dg_q = _qk_norm_bwd(proj, dqs, qg, 3 * n_units, nsb, HEAD_DIM ** -0.5, "q_norm_bwd")
    dk, dg_k = _qk_norm_bwd(proj, dkh, kg, 4 * n_units, nsb, 1.0, "k_norm_bwd")
    dproj = jnp.concatenate([db_, dc_, du_, dq, dk, dv.astype(BF16)], axis=1)
    nin = W["w_in"].shape[2]
    G["w_in"] = _tn_matmul(
        h2, dproj, pl.BlockSpec((tk, D), lambda m, k: (k, 0)), pl.BlockSpec((tk, nin), lambda m, k: (k, m)),
        (N_SHARDS, D, nin), pl.BlockSpec((None, D, nin), lambda m, k: (m, 0, 0)), (N_SHARDS, nk), "w_in_grad")
    dx1, dg_mix = _norm_proj_bwd(dproj, W["w_in"], dx2, x1, mix_norm, "mix_in_proj_bwd")

    dx0, s1, da1, db1, dy1, dg_ffn1 = _ffn_bwd(dx1, xs, ffn1_norm, a1, b1, W["ffn1_w_gate"], W["ffn1_w_up"], W["ffn1_w_down"], "ffn1_bwd")
    ffn_grads("ffn1", h1, s1, da1, db1, dy1)

    assert D >= nconv and D % LANES == 0
    fold = lambda t: t[0].reshape(-1, HEAD_DIM).sum(axis=0)[None, :]
    small_rows = [dg_ffn1[0:1], dg_mix[0:1], dg_ffn2[0:1], dg_ple[0:1],
                  _pad_rows(dwb[3:4], 1, D), _pad_rows(dwb[0:3], 3, D),
                  _pad_rows(fold(dg_q), 1, D), _pad_rows(fold(dg_k), 1, D), _pad_rows(loss_blk[0:1, 0:1], 1, D)]
    small = _pad_rows(jnp.concatenate(small_rows, axis=0), SMALL_ROWS, D)

    grads = [G[k] for k in names]
    swapped = _swap_halves(grads, small, "grad_swap_halves")
    small_sum = swapped[-1]
    core_arr = jnp.reshape(core, (1,)).astype(jnp.int32)
    place = jnp.stack([jchip, core]).astype(jnp.int32)
    csums, csums_bf = zip(*[_chip_sum(g, r, core_arr, f"chip_sum_{names[a]}")
                            for a, (g, r) in enumerate(zip(grads, swapped[:-1]))])
    got = _scatter_chip_sums(list(csums_bf), "grad_scatter")
    full = _join_halves([_shard_sum(cs, gt, place, f"shard_sum_{names[a]}")
                         for a, (cs, gt) in enumerate(zip(csums, got))], "grad_join_halves")

    out_g, out_d, out_m, out_v = {}, {}, {}, {}
    for a, k in enumerate(names):
        shp = big[k].shape
        g2 = full[a]
        d_, m_, v_ = _elementwise(_adamw, [big[k][0].reshape(g2.shape), g2, big_m[k][0].reshape(g2.shape),
                                           big_v[k][0].reshape(g2.shape)], 3, f"adamw_{k}")
        out_g[k], out_d[k], out_m[k], out_v[k] = (t.reshape(shp) for t in (g2, d_, m_, v_))

    sm_names = ["ffn1_norm", "mix_norm", "ffn2_norm", "ple_norm", "conv_b", "conv_w", "q_norm", "k_norm"]
    sm_w = dict(ffn1_norm=ffn1_norm, mix_norm=mix_norm, ffn2_norm=ffn2_norm, ple_norm=ple_norm, conv_b=conv_b,
                conv_w=conv_w[0], q_norm=q_norm, k_norm=k_norm)
    sm_m = dict(ffn1_norm=m_ffn1_norm, mix_norm=m_mix_norm, ffn2_norm=m_ffn2_norm, ple_norm=m_ple_norm,
                conv_b=m_conv_b, conv_w=m_conv_w[0], q_norm=m_q_norm, k_norm=m_k_norm)
    sm_v = dict(ffn1_norm=v_ffn1_norm, mix_norm=v_mix_norm, ffn2_norm=v_ffn2_norm, ple_norm=v_ple_norm,
                conv_b=v_conv_b, conv_w=v_conv_w[0], q_norm=v_q_norm, k_norm=v_k_norm)
    sm_g = dict(ffn1_norm=small_sum[0:1], mix_norm=small_sum[1:2], ffn2_norm=small_sum[2:3], ple_norm=small_sum[3:4],
                conv_b=small_sum[4:5, :nconv],
                conv_w=lax.dynamic_slice_in_dim(small_sum[5:8, :nconv], jchip * cwl, cwl, axis=1),
                q_norm=small_sum[8:9, :HEAD_DIM], k_norm=small_sum[9:10, :HEAD_DIM])
    loss = small_sum[10, 0]
    pack = lambda d: _pad_rows(jnp.concatenate([_pad_rows(d[k], d[k].shape[0], D) for k in sm_names], axis=0), SMALL_ROWS, D)
    sd, smm, svv = _elementwise(_adamw, [pack(sm_w), pack(sm_g), pack(sm_m), pack(sm_v)], 3, "adamw_small")
    row = 0
    for k in sm_names:
        r_, c_ = sm_w[k].shape
        shp = (1, r_, c_) if k == "conv_w" else (r_, c_)
        out_g[k] = sm_g[k].reshape(shp)
        out_d[k], out_m[k], out_v[k] = (t[row:row + r_, :c_].reshape(shp) for t in (sd, smm, svv))
        row += r_

    order = ["ffn1_norm", "ffn1_w_gate", "ffn1_w_up", "ffn1_w_down", "mix_norm", "w_in", "conv_w", "conv_b",
             "q_norm", "k_norm", "w_out", "ffn2_norm", "ffn2_w_gate", "ffn2_w_up", "ffn2_w_down", "ple_norm",
             "ple_w_gate", "ple_w_proj"]
    return (loss, dx0[None], *[out_g[k] for k in order], *[out_d[k] for k in order],
            *[out_m[k] for k in order], *[out_v[k] for k in order])
```

```python
import jax
import jax.numpy as jnp
from jax import lax
from jax.experimental import pallas as pl
from jax.experimental.pallas import tpu as pltpu

F32 = jnp.float32
BF16 = jnp.bfloat16
MESH = pl.DeviceIdType.MESH

EPS = 1e-6
HEAD_DIM = 64
LANES = 128
FFN_RES = 0.5
ADAM_LR = 0.001
ADAM_B1 = 0.9
ADAM_B2 = 0.999
ADAM_EPS = 1e-08
ADAM_WD = 0.01
ADAM_STEP = 10
N_SHARDS = 4
N_DEV = 8
ATT_TILE = 256
VMEM_LIMIT = 52 * 1024 * 1024
SMALL_ROWS = 16
HBM = pl.BlockSpec(memory_space=pltpu.HBM)
VMEM_WHOLE = pl.BlockSpec(memory_space=pltpu.VMEM)


def _cparams(**kw):
    return pltpu.CompilerParams(vmem_limit_bytes=VMEM_LIMIT, **kw)


def _dot(a, b):
    return jnp.dot(a, b, preferred_element_type=F32)


def _dot_nt(a, b):
    return lax.dot_general(a, b, (((1,), (1,)), ((), ())), preferred_element_type=F32)


def _dot_tn(a, b):
    return lax.dot_general(a, b, (((0,), (0,)), ((), ())), preferred_element_type=F32)


def _split_dot(x, m):
    hi = x.astype(BF16)
    lo = (x - hi.astype(F32)).astype(BF16)
    return _dot(hi, m) + _dot(lo, m)


def _rms(x, g):
    r = lax.rsqrt(jnp.mean(x * x, axis=-1, keepdims=True) + EPS)
    xn = x * r
    return xn * g, xn, r


def _rms_bwd(dh, xn, r, g):
    dxn = dh * g
    dx = r * (dxn - xn * jnp.mean(dxn * xn, axis=-1, keepdims=True))
    return dx, jnp.sum(dh * xn, axis=0, keepdims=True)


def _rows(n, cap=512):
    for t in (512, 448, 384, 352, 256, 192, 176, 128, 96, 88, 64, 48, 32, 16, 8):
        if t <= cap and n % t == 0:
            return t
    raise ValueError(f"no row tile for {n}")


def _ffn_fwd(x, gain, wg, wu, wd, name):
    S, D = x.shape
    ns, _, fs = wg.shape
    tm = _rows(S, 512)

    def body(x_ref, g_ref, wg_ref, wu_ref, wd_ref, xo_ref, h_ref, a_ref, b_ref, hs, acc):
        j = pl.program_id(1)

        @pl.when(j == 0)
        def _():
            h, _, _ = _rms(x_ref[...], g_ref[...])
            hb = h.astype(BF16)
            hs[...] = hb
            h_ref[...] = hb
            acc[...] = jnp.zeros_like(acc)

        hb = hs[...]
        a = _dot(hb, wg_ref[...])
        b = _dot(hb, wu_ref[...])
        a_ref[...] = a
        b_ref[...] = b
        s = (a * jax.nn.sigmoid(a)) * b
        acc[...] += _dot(s.astype(BF16), wd_ref[...])

        @pl.when(j == ns - 1)
        def _():
            xo_ref[...] = x_ref[...] + FFN_RES * acc[...]

    return pl.pallas_call(
        body, name=name, grid=(S // tm, ns),
        in_specs=[
            pl.BlockSpec((tm, D), lambda i, j: (i, 0)),
            pl.BlockSpec((1, D), lambda i, j: (0, 0)),
            pl.BlockSpec((None, D, fs), lambda i, j: (j, 0, 0)),
            pl.BlockSpec((None, D, fs), lambda i, j: (j, 0, 0)),
            pl.BlockSpec((None, fs, D), lambda i, j: (j, 0, 0)),
        ],
        out_specs=[
            pl.BlockSpec((tm, D), lambda i, j: (i, 0)),
            pl.BlockSpec((tm, D), lambda i, j: (i, 0)),
            pl.BlockSpec((None, tm, fs), lambda i, j: (j, i, 0)),
            pl.BlockSpec((None, tm, fs), lambda i, j: (j, i, 0)),
        ],
        out_shape=[
            jax.ShapeDtypeStruct((S, D), F32),
            jax.ShapeDtypeStruct((S, D), BF16),
            jax.ShapeDtypeStruct((ns, S, fs), F32),
            jax.ShapeDtypeStruct((ns, S, fs), F32),
        ],
        scratch_shapes=[pltpu.VMEM((tm, D), BF16), pltpu.VMEM((tm, D), F32)],
        compiler_params=_cparams(),
    )(x, gain, wg, wu, wd)


def _ffn_bwd(dxo, x, gain, a, b, wg, wu, wd, name):
    S, D = x.shape
    ns, _, fs = wg.shape
    tm = _rows(S, 256)

    def body(dxo_ref, x_ref, g_ref, a_ref, b_ref, wg_ref, wu_ref, wd_ref,
             dx_ref, s_ref, da_ref, db_ref, dy_ref, dg_ref, dys, acc):
        i = pl.program_id(0)
        j = pl.program_id(1)

        @pl.when((i == 0) & (j == 0))
        def _():
            dg_ref[...] = jnp.zeros_like(dg_ref)

        @pl.when(j == 0)
        def _():
            dy = (FFN_RES * dxo_ref[...]).astype(BF16)
            dys[...] = dy
            dy_ref[...] = dy
            acc[...] = jnp.zeros_like(acc)

        av = a_ref[...]
        bv = b_ref[...]
        ds = _dot_nt(dys[...], wd_ref[...])
        sig = jax.nn.sigmoid(av)
        sl = av * sig
        s_ref[...] = (sl * bv).astype(BF16)
        da = (ds * bv * (sig * (1.0 + av * (1.0 - sig)))).astype(BF16)
        db = (ds * sl).astype(BF16)
        da_ref[...] = da
        db_ref[...] = db
        acc[...] += _dot_nt(da, wg_ref[...]) + _dot_nt(db, wu_ref[...])

        @pl.when(j == ns - 1)
        def _():
            g = g_ref[...]
            _, xn, r = _rms(x_ref[...], g)
            dx, dg = _rms_bwd(acc[...], xn, r, g)
            dx_ref[...] = dxo_ref[...] + dx
            dg_ref[...] += jnp.broadcast_to(dg, dg_ref.shape)

    return pl.pallas_call(
        body, name=name, grid=(S // tm, ns),
        in_specs=[
            pl.BlockSpec((tm, D), lambda i, j: (i, 0)),
            pl.BlockSpec((tm, D), lambda i, j: (i, 0)),
            pl.BlockSpec((1, D), lambda i, j: (0, 0)),
            pl.BlockSpec((None, tm, fs), lambda i, j: (j, i, 0)),
            pl.BlockSpec((None, tm, fs), lambda i, j: (j, i, 0)),
            pl.BlockSpec((None, D, fs), lambda i, j: (j, 0, 0)),
            pl.BlockSpec((None, D, fs), lambda i, j: (j, 0, 0)),
            pl.BlockSpec((None, fs, D), lambda i, j: (j, 0, 0)),
        ],
        out_specs=[
            pl.BlockSpec((tm, D), lambda i, j: (i, 0)),
            pl.BlockSpec((None, tm, fs), lambda i, j: (j, i, 0)),
            pl.BlockSpec((None, tm, fs), lambda i, j: (j, i, 0)),
            pl.BlockSpec((None, tm, fs), lambda i, j: (j, i, 0)),
            pl.BlockSpec((tm, D), lambda i, j: (i, 0)),
            pl.BlockSpec((8, D), lambda i, j: (0, 0)),
        ],
        out_shape=[
            jax.ShapeDtypeStruct((S, D), F32),
            jax.ShapeDtypeStruct((ns, S, fs), BF16),
            jax.ShapeDtypeStruct((ns, S, fs), BF16),
            jax.ShapeDtypeStruct((ns, S, fs), BF16),
            jax.ShapeDtypeStruct((S, D), BF16),
            jax.ShapeDtypeStruct((8, D), F32),
        ],
        scratch_shapes=[pltpu.VMEM((tm, D), BF16), pltpu.VMEM((tm, D), F32)],
        compiler_params=_cparams(),
    )(dxo, x, gain, a, b, wg, wu, wd)


def _tn_matmul(a, b, a_spec, b_spec, o_shape, o_spec, grid, name):
    kaxis = len(grid) - 1

    def body(a_ref, b_ref, o_ref):
        @pl.when(pl.program_id(kaxis) == 0)
        def _():
            o_ref[...] = jnp.zeros_like(o_ref)

        o_ref[...] += _dot_tn(a_ref[...].astype(BF16), b_ref[...].astype(BF16))

    return pl.pallas_call(
        body, name=name, grid=grid, in_specs=[a_spec, b_spec], out_specs=o_spec,
        out_shape=jax.ShapeDtypeStruct(o_shape, F32), compiler_params=_cparams(),
    )(a, b)


def _norm_proj(x, gain, w, name):
    S, D = x.shape
    ns, _, n = w.shape
    tm = _rows(S, 512)

    def body(x_ref, g_ref, w_ref, o_ref, h_ref, hs):
        @pl.when(pl.program_id(1) == 0)
        def _():
            h, _, _ = _rms(x_ref[...], g_ref[...])
            hb = h.astype(BF16)
            hs[...] = hb
            h_ref[...] = hb

        o_ref[...] = _dot(hs[...], w_ref[...])

    return pl.pallas_call(
        body, name=name, grid=(S // tm, ns),
        in_specs=[
            pl.BlockSpec((tm, D), lambda i, j: (i, 0)),
            pl.BlockSpec((1, D), lambda i, j: (0, 0)),
            pl.BlockSpec((None, D, n), lambda i, j: (j, 0, 0)),
        ],
        out_specs=[
            pl.BlockSpec((tm, n), lambda i, j: (i, j)),
            pl.BlockSpec((tm, D), lambda i, j: (i, 0)),
        ],
        out_shape=[jax.ShapeDtypeStruct((S, ns * n), F32), jax.ShapeDtypeStruct((S, D), BF16)],
        scratch_shapes=[pltpu.VMEM((tm, D), BF16)],
        compiler_params=_cparams(),
    )(x, gain, w)


def _norm_proj_bwd(dproj, w, dres, x, gain, name):
    S, D = x.shape
    ns, _, n = w.shape
    tm = _rows(S, 256)

    def body(dp_ref, w_ref, dres_ref, x_ref, g_ref, dx_ref, dg_ref, acc):
        i = pl.program_id(0)
        j = pl.program_id(1)

        @pl.when((i == 0) & (j == 0))
        def _():
            dg_ref[...] = jnp.zeros_like(dg_ref)

        @pl.when(j == 0)
        def _():
            acc[...] = jnp.zeros_like(acc)

        acc[...] += _dot_nt(dp_ref[...], w_ref[...])

        @pl.when(j == ns - 1)
        def _():
            g = g_ref[...]
            _, xn, r = _rms(x_ref[...], g)
            dx, dg = _rms_bwd(acc[...], xn, r, g)
            dx_ref[...] = dres_ref[...] + dx
            dg_ref[...] += jnp.broadcast_to(dg, dg_ref.shape)

    return pl.pallas_call(
        body, name=name, grid=(S // tm, ns),
        in_specs=[
            pl.BlockSpec((tm, n), lambda i, j: (i, j)),
            pl.BlockSpec((None, D, n), lambda i, j: (j, 0, 0)),
            pl.BlockSpec((tm, D), lambda i, j: (i, 0)),
            pl.BlockSpec((tm, D), lambda i, j: (i, 0)),
            pl.BlockSpec((1, D), lambda i, j: (0, 0)),
        ],
        out_specs=[
            pl.BlockSpec((tm, D), lambda i, j: (i, 0)),
            pl.BlockSpec((8, D), lambda i, j: (0, 0)),
        ],
        out_shape=[jax.ShapeDtypeStruct((S, D), F32), jax.ShapeDtypeStruct((8, D), F32)],
        scratch_shapes=[pltpu.VMEM((tm, D), F32)],
        compiler_params=_cparams(),
    )(dproj, w, dres, x, gain)


def _out_proj(ycat, w, res, name):
    S, K = ycat.shape
    D = w.shape[1]
    tm = _rows(S, 512)

    def body(y_ref, w_ref, r_ref, o_ref):
        o_ref[...] = r_ref[...] + _dot(y_ref[...], w_ref[...])

    return pl.pallas_call(
        body, name=name, grid=(S // tm,),
        in_specs=[
            pl.BlockSpec((tm, K), lambda i: (i, 0)),
            pl.BlockSpec((K, D), lambda i: (0, 0)),
            pl.BlockSpec((tm, D), lambda i: (i, 0)),
        ],
        out_specs=pl.BlockSpec((tm, D), lambda i: (i, 0)),
        out_shape=jax.ShapeDtypeStruct((S, D), F32),
        compiler_params=_cparams(),
    )(ycat, w, res)


def _out_proj_bwd(dx, w, name):
    S, D = dx.shape
    K = w.shape[0]
    tm = _rows(S, 512)

    def body(d_ref, w_ref, o_ref):
        o_ref[...] = _dot_nt(d_ref[...].astype(BF16), w_ref[...])

    return pl.pallas_call(
        body, name=name, grid=(S // tm,),
        in_specs=[pl.BlockSpec((tm, D), lambda i: (i, 0)), pl.BlockSpec((K, D), lambda i: (0, 0))],
        out_specs=pl.BlockSpec((tm, K), lambda i: (i, 0)),
        out_shape=jax.ShapeDtypeStruct((S, K), F32),
        compiler_params=_cparams(),
    )(dx, w)


CONV_COLS = 256


def _shift_down(z, halo, k, row):
    out = pltpu.roll(z, k, 0)
    for n in range(k):
        out = jnp.where(row == n, halo[8 - k + n:8 - k + n + 1, :], out)
    return out


def _shift_up(g, halo, k, row, ts):
    out = pltpu.roll(g, ts - k, 0)
    for n in range(k):
        out = jnp.where(row == ts - k + n, halo[n:n + 1, :], out)
    return out


def _conv_fwd(proj, cw, cb, nconv, name):
    S = proj.shape[0]
    ncb = nconv // CONV_COLS
    ts = _rows(S, 512)
    hb = ts // 8

    def body(b_ref, c_ref, u_ref, ch_ref, uh_ref, w_ref, bias_ref, o_ref):
        i = pl.program_id(1)
        z = c_ref[...] * u_ref[...]
        halo = jnp.where(i > 0, ch_ref[...] * uh_ref[...], 0.0)
        row = lax.broadcasted_iota(jnp.int32, z.shape, 0)
        w = w_ref[...]
        yc = w[0:1, :] * _shift_down(z, halo, 2, row) + w[1:2, :] * _shift_down(z, halo, 1, row) + w[2:3, :] * z
        o_ref[...] = (b_ref[...] * (yc + bias_ref[...])).astype(BF16)

    def blk(unit):
        return pl.BlockSpec((ts, CONV_COLS), lambda cbi, i: (i, unit * ncb + cbi))

    def prev(unit):
        return pl.BlockSpec((8, CONV_COLS), lambda cbi, i: (jnp.maximum(i * hb - 1, 0), unit * ncb + cbi))

    return pl.pallas_call(
        body, name=name, grid=(ncb, S // ts),
        in_specs=[blk(0), blk(1), blk(2), prev(1), prev(2),
                  pl.BlockSpec((8, CONV_COLS), lambda cbi, i: (0, cbi)),
                  pl.BlockSpec((1, CONV_COLS), lambda cbi, i: (0, cbi))],
        out_specs=pl.BlockSpec((ts, CONV_COLS), lambda cbi, i: (i, cbi)),
        out_shape=jax.ShapeDtypeStruct((S, nconv), BF16),
        compiler_params=_cparams(),
    )(proj, proj, proj, proj, proj, cw, cb)


def _conv_bwd(proj, dy, cw, cb, nconv, name):
    S = proj.shape[0]
    ncb = nconv // CONV_COLS
    ts = _rows(S, 512)
    hb = ts // 8
    nblk = S // ts

    def body(b_ref, c_ref, u_ref, dy_ref, ch_ref, uh_ref, bn_ref, dyn_ref, w_ref, bias_ref,
             db_ref, dc_ref, du_ref, dw_ref):
        i = pl.program_id(1)

        @pl.when(i == 0)
        def _():
            dw_ref[...] = jnp.zeros_like(dw_ref)

        c = c_ref[...]
        u = u_ref[...]
        bg = b_ref[...]
        dy_ = dy_ref[...]
        z = c * u
        halo = jnp.where(i > 0, ch_ref[...] * uh_ref[...], 0.0)
        row = lax.broadcasted_iota(jnp.int32, z.shape, 0)
        w = w_ref[...]
        z2 = _shift_down(z, halo, 2, row)
        z1 = _shift_down(z, halo, 1, row)
        yc = w[0:1, :] * z2 + w[1:2, :] * z1 + w[2:3, :] * z
        db_ref[...] = (dy_ * (yc + bias_ref[...])).astype(BF16)
        g = dy_ * bg
        gnext = jnp.where(i < nblk - 1, dyn_ref[...] * bn_ref[...], 0.0)
        dz = w[2:3, :] * g + w[1:2, :] * _shift_up(g, gnext, 1, row, ts) + w[0:1, :] * _shift_up(g, gnext, 2, row, ts)
        dc_ref[...] = (dz * u).astype(BF16)
        du_ref[...] = (dz * c).astype(BF16)
        r8 = lax.broadcasted_iota(jnp.int32, (8, CONV_COLS), 0)
        sums = [jnp.sum(g * z2, axis=0, keepdims=True), jnp.sum(g * z1, axis=0, keepdims=True),
                jnp.sum(g * z, axis=0, keepdims=True), jnp.sum(g, axis=0, keepdims=True)]
        upd = jnp.zeros((8, CONV_COLS), F32)
        for n, sv in enumerate(sums):
            upd = jnp.where(r8 == n, sv, upd)
        dw_ref[...] += upd

    def blk(unit):
        return pl.BlockSpec((ts, CONV_COLS), lambda cbi, i: (i, unit * ncb + cbi))

    def prev(unit):
        return pl.BlockSpec((8, CONV_COLS), lambda cbi, i: (jnp.maximum(i * hb - 1, 0), unit * ncb + cbi))

    def nxt(unit):
        return pl.BlockSpec((8, CONV_COLS), lambda cbi, i: (jnp.minimum((i + 1) * hb, S // 8 - 1), unit * ncb + cbi))

    o = pl.BlockSpec((ts, CONV_COLS), lambda cbi, i: (i, cbi))
    return pl.pallas_call(
        body, name=name, grid=(ncb, nblk),
        in_specs=[blk(0), blk(1), blk(2), blk(0), prev(1), prev(2), nxt(0), nxt(0),
                  pl.BlockSpec((8, CONV_COLS), lambda cbi, i: (0, cbi)),
                  pl.BlockSpec((1, CONV_COLS), lambda cbi, i: (0, cbi))],
        out_specs=[o, o, o, pl.BlockSpec((8, CONV_COLS), lambda cbi, i: (0, cbi))],
        out_shape=[jax.ShapeDtypeStruct((S, nconv), BF16)] * 3 + [jax.ShapeDtypeStruct((8, nconv), F32)],
        compiler_params=_cparams(),
    )(proj, proj, proj, dy, proj, proj, proj, dy, cw, cb)


def _group_ones(n):
    r = lax.broadcasted_iota(jnp.int32, (n, n), 0) // HEAD_DIM
    c = lax.broadcasted_iota(jnp.int32, (n, n), 1) // HEAD_DIM
    return jnp.where(r == c, 1.0, 0.0).astype(BF16)


def _qk_norm(proj, gain_t, unit0, nsb, scale, name):
    S = proj.shape[0]
    nb = nsb // CONV_COLS
    ts = _rows(S, 512)

    def body(x_ref, g_ref, o_ref):
        x = x_ref[...]
        ss = _split_dot(x * x, _group_ones(CONV_COLS))
        r = lax.rsqrt(ss * (1.0 / HEAD_DIM) + EPS)
        o_ref[...] = ((x * r) * g_ref[...] * scale).astype(BF16)

    return pl.pallas_call(
        body, name=name, grid=(nb, S // ts),
        in_specs=[pl.BlockSpec((ts, CONV_COLS), lambda u, i: (i, unit0 + u)),
                  pl.BlockSpec((1, CONV_COLS), lambda u, i: (0, 0))],
        out_specs=pl.BlockSpec((ts, CONV_COLS), lambda u, i: (i, u)),
        out_shape=jax.ShapeDtypeStruct((S, nsb), BF16),
        compiler_params=_cparams(),
    )(proj, gain_t)


def _qk_norm_bwd(proj, dout, gain_t, unit0, nsb, scale, name):
    S = proj.shape[0]
    nb = nsb // CONV_COLS
    ts = _rows(S, 512)

    def body(x_ref, d_ref, g_ref, dx_ref, dg_ref):
        @pl.when(pl.program_id(1) == 0)
        def _():
            dg_ref[...] = jnp.zeros_like(dg_ref)

        x = x_ref[...]
        g = g_ref[...]
        ones = _group_ones(CONV_COLS)
        ss = _split_dot(x * x, ones)
        r = lax.rsqrt(ss * (1.0 / HEAD_DIM) + EPS)
        xn = x * r
        dh = d_ref[...] * scale
        dxn = dh * g
        m = _split_dot(dxn * xn, ones) * (1.0 / HEAD_DIM)
        dx_ref[...] = (r * (dxn - xn * m)).astype(BF16)
        dg_ref[...] += jnp.broadcast_to(jnp.sum(dh * xn, axis=0, keepdims=True), dg_ref.shape)

    return pl.pallas_call(
        body, name=name, grid=(nb, S // ts),
        in_specs=[pl.BlockSpec((ts, CONV_COLS), lambda u, i: (i, unit0 + u)),
                  pl.BlockSpec((ts, CONV_COLS), lambda u, i: (i, u)),
                  pl.BlockSpec((1, CONV_COLS), lambda u, i: (0, 0))],
        out_specs=[pl.BlockSpec((ts, CONV_COLS), lambda u, i: (i, u)),
                   pl.BlockSpec((8, CONV_COLS), lambda u, i: (0, u))],
        out_shape=[jax.ShapeDtypeStruct((S, nsb), BF16), jax.ShapeDtypeStruct((8, nsb), F32)],
        compiler_params=_cparams(),
    )(proj, dout, gain_t)


Z_CLAMP = 80.0
N_SLOTS = 3


def _head_masks():
    lane = lax.broadcasted_iota(jnp.int32, (1, LANES), 1)
    return [lane < HEAD_DIM, lane >= HEAD_DIM], lane


def _tile_consts(T):
    r_i = lax.broadcasted_iota(jnp.int32, (T, T), 0)
    c_i = lax.broadcasted_iota(jnp.int32, (T, T), 1)
    neg_suffix = jnp.where(r_i >= c_i, -1.0, 0.0).astype(BF16)
    prefix = jnp.where(r_i <= c_i, 1.0, 0.0).astype(BF16)
    return neg_suffix, prefix, c_i < r_i


def _pipeline(n, stages, first_special=False, last_special=False):
    depth = len(stages)
    head = depth if first_special else depth - 1
    off = 0 if last_special else 1
    for m in range(head):
        for k in reversed(range(min(m, depth - 1) + 1)):
            stages[k](m - k, (m - k) % N_SLOTS, first_special and m == k)

    def trip(m, u):
        for k in reversed(range(depth)):
            stages[k](m - k, (head + u - k) % N_SLOTS, False)

    def group(g, carry):
        for u in range(N_SLOTS):
            trip(head + g * N_SLOTS + u, u)
        return carry

    count = n - 1 + off - head
    full = count // N_SLOTS
    lax.fori_loop(0, full, group, 0)
    for r in range(N_SLOTS):
        @pl.when(count - full * N_SLOTS == r)
        def _(r=r):
            for u in range(r):
                trip(head + full * N_SLOTS + u, u)
            for e in range(depth - off):
                for k in reversed(range(e + off, depth)):
                    t = n - 1 - (k - e - off)
                    stages[k](t, (head + r + e - k) % N_SLOTS, last_special and k == e + off)


def _sequential(n, stages, first_special=False, last_special=False):
    def one(t, special):
        for st in stages:
            st(t, 0, special)

    if first_special:
        one(0, True)

    def loop(t, carry):
        one(t, False)
        return carry

    lax.fori_loop(1 if first_special else 0, n - 1 if last_special else n, loop, 0)
    if last_special:
        one(n - 1, True)


def _attn_fwd(q, k, v, name):
    S, nsb = q.shape
    T = ATT_TILE
    hp = nsb // LANES
    nb = S // T
    assert nb <= HEAD_DIM

    def body(q_ref, k_ref, v_ref, y_ref, cs_ref, c_ref, acc, z_st, inc_st):
        i = pl.program_id(1)
        masks, lane = _head_masks()
        qv = q_ref[...]
        qm = [jnp.where(m, qv, jnp.zeros_like(qv)) for m in masks]
        neg_suffix, _, causal = _tile_consts(T)
        c_ref[...] = jnp.zeros_like(c_ref)
        acc[...] = jnp.zeros_like(acc)
        cs_ref[...] = jnp.zeros_like(cs_ref)

        def blk(ref, j):
            return ref[pl.ds(pl.multiple_of(j * T, T), T), :]

        def scores(t, slot, diag):
            kj = blk(k_ref, i - t)
            for h in range(2):
                z_st[slot, h] = jnp.minimum(_dot_nt(qm[h], kj), Z_CLAMP)

        def suffix_sums(t, slot, diag):
            for h in range(2):
                sp = jnp.log(1.0 + jnp.exp(z_st[slot, h]))
                if diag:
                    sp = jnp.where(causal, sp, 0.0)
                inc_st[slot, h] = _dot(sp.astype(BF16), neg_suffix)

        def weights(t, slot, diag):
            vj = blk(v_ref, i - t)
            for h in range(2):
                inc = inc_st[slot, h]
                c = c_ref[h]
                a = jnp.exp(z_st[slot, h] + inc + c)
                if diag:
                    a = jnp.where(causal, a, 0.0)
                upd = _dot(a.astype(BF16), vj)
                acc[...] += jnp.where(masks[h], upd, 0.0)
                cs_ref[...] = jnp.where(lane == i - t + HEAD_DIM * h, c, cs_ref[...])
                c_ref[h] = c + inc[:, 0:1]

        stages = [scores, suffix_sums, weights]

        @pl.when(i + 1 >= len(stages))
        def _():
            _pipeline(i + 1, stages, first_special=True)

        @pl.when(i + 1 < len(stages))
        def _():
            _sequential(i + 1, stages, first_special=True)

        y_ref[...] = acc[...].astype(BF16)

    return pl.pallas_call(
        body, name=name, grid=(hp, nb),
        in_specs=[pl.BlockSpec((T, LANES), lambda p, i: (i, p)),
                  pl.BlockSpec((S, LANES), lambda p, i: (0, p)),
                  pl.BlockSpec((S, LANES), lambda p, i: (0, p))],
        out_specs=[pl.BlockSpec((T, LANES), lambda p, i: (i, p)),
                   pl.BlockSpec((None, T, LANES), lambda p, i: (p, i, 0))],
        out_shape=[jax.ShapeDtypeStruct((S, nsb), BF16), jax.ShapeDtypeStruct((hp, S, LANES), F32)],
        scratch_shapes=[pltpu.VMEM((2, T, 1), F32), pltpu.VMEM((T, LANES), F32),
                        pltpu.VMEM((N_SLOTS, 2, T, T), F32), pltpu.VMEM((N_SLOTS, 2, T, T), F32)],
        compiler_params=_cparams(),
    )(q, k, v)


def _attn_bwd(q, k, v, dy, col0, carry, name):
    S, nsb = q.shape
    T = ATT_TILE
    hp = nsb // LANES
    nb = S // T

    def body(q_ref, k_ref, v_ref, dy_ref, cs_ref, dq_ref, dk_ref, dv_ref, e_ref, acc,
             z_st, da_st, b_st, inc_st, a_st, e_st, p_st):
        i = pl.program_id(1)

        @pl.when(i == 0)
        def _():
            dk_ref[...] = jnp.zeros_like(dk_ref)
            dv_ref[...] = jnp.zeros_like(dv_ref)

        masks, lane = _head_masks()
        qv = q_ref[...]
        dyb = dy_ref[...].astype(BF16)
        qm = [jnp.where(m, qv, jnp.zeros_like(qv)) for m in masks]
        dym = [jnp.where(m, dyb, jnp.zeros_like(dyb)) for m in masks]
        neg_suffix, prefix, causal = _tile_consts(T)
        e_ref[...] = jnp.zeros_like(e_ref)
        acc[...] = jnp.zeros_like(acc)

        def blk(ref, j):
            return ref[pl.ds(pl.multiple_of(j * T, T), T), :]

        def scores(t, slot, diag):
            kj = blk(k_ref, t)
            vj = blk(v_ref, t)
            for h in range(2):
                z_st[slot, h] = jnp.minimum(_dot_nt(qm[h], kj), Z_CLAMP)
                da_st[slot, h] = _dot_nt(dym[h], vj)

        def suffix_sums(t, slot, diag):
            for h in range(2):
                u = jnp.exp(z_st[slot, h])
                w = 1.0 + u
                b_st[slot, h] = u / w
                sp = jnp.log(w)
                if diag:
                    sp = jnp.where(causal, sp, 0.0)
                inc_st[slot, h] = _dot(sp.astype(BF16), neg_suffix)

        def probs(t, slot, diag):
            csv = cs_ref[...]
            for h in range(2):
                c = jnp.sum(jnp.where(lane == t + HEAD_DIM * h, csv, 0.0), axis=-1, keepdims=True)
                a = jnp.exp(z_st[slot, h] + inc_st[slot, h] + c)
                if diag:
                    a = jnp.where(causal, a, 0.0)
                a_st[slot, h] = a.astype(BF16)
                e = a * da_st[slot, h]
                e_st[slot, h] = e
                p_st[slot, h] = _dot(e.astype(BF16), prefix)

        def grads(t, slot, diag):
            kj = blk(k_ref, t)
            off = pl.multiple_of(t * T, T)
            for h in range(2):
                p = p_st[slot, h]
                dz = e_st[slot, h] - b_st[slot, h] * (e_ref[h] + p)
                if diag:
                    dz = jnp.where(causal, dz, 0.0)
                dzb = dz.astype(BF16)
                acc[...] += jnp.where(masks[h], _dot(dzb, kj), 0.0)
                dk_ref[pl.ds(off, T), :] += _dot_tn(dzb, qm[h])
                dv_ref[pl.ds(off, T), :] += _dot_tn(a_st[slot, h], dym[h])
                e_ref[h] += p[:, T - 1:T]

        stages = [scores, suffix_sums, probs, grads]

        @pl.when(i + 1 >= len(stages))
        def _():
            _pipeline(i + 1, stages, last_special=True)

        @pl.when(i + 1 < len(stages))
        def _():
            _sequential(i + 1, stages, last_special=True)

        dq_ref[...] = acc[...]

    return pl.pallas_call(
        body, name=name, grid=(hp, nb),
        in_specs=[pl.BlockSpec((T, LANES), lambda p, i: (i, p)),
                  pl.BlockSpec((S, LANES), lambda p, i: (0, p)),
                  pl.BlockSpec((S, LANES), lambda p, i: (0, p)),
                  pl.BlockSpec((T, LANES), lambda p, i: (i, col0 + p)),
                  pl.BlockSpec((None, T, LANES), lambda p, i: (p, i, 0))],
        out_specs=[pl.BlockSpec((T, LANES), lambda p, i: (i, p)),
                   pl.BlockSpec((S, LANES), lambda p, i: (0, p)),
                   pl.BlockSpec((S, LANES), lambda p, i: (0, p))],
        out_shape=[jax.ShapeDtypeStruct((S, nsb), F32)] * 3,
        scratch_shapes=[pltpu.VMEM((2, T, 1), F32), pltpu.VMEM((T, LANES), F32)]
        + [pltpu.VMEM((N_SLOTS, 2, T, T), dt) for dt in (F32, F32, F32, F32, BF16, F32, F32)],
        compiler_params=_cparams(),
    )(q, k, v, dy, carry)


def _ple(x, p, tgt, gain, wpg, wpp, name):
    S, D = x.shape
    P = p.shape[1]
    ns, _, nc = wpp.shape
    tm = _rows(S, 256)

    def body(x_ref, p_ref, t_ref, g_ref, wpg_ref, wpp_ref, dx_ref, h_ref, du_ref, dpp_ref, loss_ref, dg_ref):
        @pl.when(pl.program_id(0) == 0)
        def _():
            loss_ref[...] = jnp.zeros_like(loss_ref)
            dg_ref[...] = jnp.zeros_like(dg_ref)

        x_ = x_ref[...]
        g = g_ref[...]
        h, xn, r = _rms(x_, g)
        hb = h.astype(BF16)
        h_ref[...] = hb
        gate = jax.nn.sigmoid(_dot(hb, wpg_ref[...]))
        pb = p_ref[...].astype(BF16)
        pp = jnp.concatenate([_dot(pb, wpp_ref[n]) for n in range(ns)], axis=1)
        err = (x_ + gate * pp) - t_ref[...]
        loss_ref[...] += (0.5 / D) * jnp.sum(err * err)
        dy = err * (1.0 / D)
        du = ((dy * pp) * (gate * (1.0 - gate))).astype(BF16)
        du_ref[...] = du
        dpp_ref[...] = (dy * gate).astype(BF16)
        dx, dg = _rms_bwd(_dot_nt(du, wpg_ref[...]), xn, r, g)
        dx_ref[...] = dy + dx
        dg_ref[...] += jnp.broadcast_to(dg, dg_ref.shape)

    row = pl.BlockSpec((tm, D), lambda i: (i, 0))
    return pl.pallas_call(
        body, name=name, grid=(S // tm,),
        in_specs=[row, pl.BlockSpec((tm, P), lambda i: (i, 0)), row,
                  pl.BlockSpec((1, D), lambda i: (0, 0)),
                  pl.BlockSpec((D, D), lambda i: (0, 0)),
                  pl.BlockSpec((ns, P, nc), lambda i: (0, 0, 0))],
        out_specs=[row, row, row, row,
                   pl.BlockSpec((8, LANES), lambda i: (0, 0)),
                   pl.BlockSpec((8, D), lambda i: (0, 0))],
        out_shape=[jax.ShapeDtypeStruct((S, D), F32)] + [jax.ShapeDtypeStruct((S, D), BF16)] * 3
        + [jax.ShapeDtypeStruct((8, LANES), F32), jax.ShapeDtypeStruct((8, D), F32)],
        compiler_params=_cparams(),
    )(x, p, tgt, gain, wpg, wpp)


def _elementwise(fn, ins, n_out, name):
    R, C = ins[0].shape
    tr = _rows(R, 512)

    def body(*refs):
        outs = fn(*[r[...] for r in refs[:len(ins)]])
        for o_ref, o in zip(refs[len(ins):], outs):
            o_ref[...] = o

    spec = pl.BlockSpec((tr, C), lambda i: (i, 0))
    return pl.pallas_call(
        body, name=name, grid=(R // tr,), in_specs=[spec] * len(ins), out_specs=[spec] * n_out,
        out_shape=[jax.ShapeDtypeStruct((R, C), F32)] * n_out, compiler_params=_cparams(),
    )(*ins)


def _adamw(w, g, m, v):
    m = ADAM_B1 * m + (1.0 - ADAM_B1) * g
    v = ADAM_B2 * v + (1.0 - ADAM_B2) * jnp.square(g)
    m_hat = m / (1.0 - ADAM_B1 ** ADAM_STEP)
    v_hat = v / (1.0 - ADAM_B2 ** ADAM_STEP)
    delta = -ADAM_LR * (m_hat / (jnp.sqrt(v_hat) + ADAM_EPS) + ADAM_WD * w)
    return delta, m, v


def _place():
    x, y, c = lax.axis_index("x"), lax.axis_index("y"), lax.axis_index("c")
    chips = [(1 - x, y), (x, 1 - y), (1 - x, 1 - y)]
    return x, y, c, chips


def _half(ref, c, axis_rows):
    n = ref.shape[-2]
    start = pl.multiple_of(c * (n // 2), 8)
    idx = (slice(None),) * (len(ref.shape) - 2) + (pl.ds(start, n // 2), slice(None))
    return ref.at[idx]


def _gather_weights(shards, small, name):
    n = len(shards)

    def body(*refs):
        ins, small_in = refs[:n], refs[n]
        outs, small_out = refs[n + 1:2 * n + 1], refs[2 * n + 1]
        lsem, lrsem, ssem, rsem, sm_s, sm_r = refs[2 * n + 2:]
        x, y, c, chips = _place()
        j = 2 * x + y
        sib = (x, y, 1 - c)

        local = [pltpu.make_async_remote_copy(
            src_ref=ins[a], dst_ref=outs[a].at[j], send_sem=lsem.at[a], recv_sem=lrsem.at[a],
            device_id=sib, device_id_type=MESH) for a in range(n)]
        for cp in local:
            cp.start()
        small_out[j] = small_in[...]
        small_cp = [pltpu.make_async_remote_copy(
            src_ref=small_in, dst_ref=small_out.at[j], send_sem=sm_s.at[k], recv_sem=sm_r.at[k],
            device_id=(*chip, c), device_id_type=MESH) for k, chip in enumerate(chips)]
        for cp in small_cp:
            cp.start()

        def ici(a, k, chip, jj, dev):
            return pltpu.make_async_remote_copy(
                src_ref=_half(ins[a], c, True) if dev is not None else _half(outs[a].at[jj], c, True),
                dst_ref=_half(outs[a].at[jj], c, True),
                send_sem=ssem.at[a, k], recv_sem=rsem.at[a, k],
                device_id=dev if dev is not None else (*chip, c), device_id_type=MESH)

        first = []
        for a in range(n):
            for k, chip in enumerate(chips):
                cp = ici(a, k, chip, j, (*chip, c))
                cp.start()
                first.append(cp)
        passed = []
        for a in range(n):
            for k, chip in enumerate(chips):
                jj = 2 * chip[0] + chip[1]
                ici(a, k, chip, jj, None).wait_recv()
                fw = pltpu.make_async_remote_copy(
                    src_ref=_half(outs[a].at[jj], c, True), dst_ref=_half(outs[a].at[jj], c, True),
                    send_sem=ssem.at[a, 3 + k], recv_sem=rsem.at[a, 3 + k], device_id=sib, device_id_type=MESH)
                fw.start()
                passed.append(fw)
        for a in range(n):
            for k, chip in enumerate(chips):
                jj = 2 * chip[0] + chip[1]
                pltpu.make_async_remote_copy(
                    src_ref=_half(outs[a].at[jj], 1 - c, True), dst_ref=_half(outs[a].at[jj], 1 - c, True),
                    send_sem=ssem.at[a, 3 + k], recv_sem=rsem.at[a, 3 + k], device_id=sib,
                    device_id_type=MESH).wait_recv()
        for cp in small_cp:
            cp.wait()
        for cp in first + passed:
            cp.wait_send()
        for cp in local:
            cp.wait()

    return pl.pallas_call(
        body, name=name,
        in_specs=[HBM] * n + [VMEM_WHOLE],
        out_specs=[HBM] * n + [VMEM_WHOLE],
        out_shape=[jax.ShapeDtypeStruct((N_SHARDS,) + s.shape, s.dtype) for s in shards]
        + [jax.ShapeDtypeStruct((N_SHARDS,) + small.shape, small.dtype)],
        scratch_shapes=[pltpu.SemaphoreType.DMA((n,)), pltpu.SemaphoreType.DMA((n,)),
                        pltpu.SemaphoreType.DMA((n, 6)), pltpu.SemaphoreType.DMA((n, 6)),
                        pltpu.SemaphoreType.DMA((3,)), pltpu.SemaphoreType.DMA((3,))],
    )(*shards, small)


def _swap_halves(grads, small, name):
    n = len(grads)

    def body(*refs):
        ins, small_in = refs[:n], refs[n]
        outs, small_out = refs[n + 1:2 * n + 1], refs[2 * n + 1]
        buf, ssem, rsem, sm_s, sm_r = refs[2 * n + 2:]
        x, y, c, _ = _place()
        me = 4 * x + 2 * y + c
        sib = (x, y, 1 - c)
        cps = [pltpu.make_async_remote_copy(
            src_ref=_half(ins[a], 1 - c, True), dst_ref=outs[a], send_sem=ssem.at[a], recv_sem=rsem.at[a],
            device_id=sib, device_id_type=MESH) for a in range(n)]
        for cp in cps:
            cp.start()
        buf[me] = small_in[...]
        peers = [(fx, fy, fc) for fx in (0, 1) for fy in (0, 1) for fc in (0, 1)][1:]
        sm = []
        for k, (fx, fy, fc) in enumerate(peers):
            dev = (1 - x if fx else x, 1 - y if fy else y, 1 - c if fc else c)
            cp = pltpu.make_async_remote_copy(
                src_ref=small_in, dst_ref=buf.at[me], send_sem=sm_s.at[k], recv_sem=sm_r.at[k],
                device_id=dev, device_id_type=MESH)
            cp.start()
            sm.append(cp)
        for cp in sm:
            cp.wait()
        tot = buf[0]
        for d in range(1, N_DEV):
            tot = tot + buf[d]
        small_out[...] = tot
        for cp in cps:
            cp.wait()

    return pl.pallas_call(
        body, name=name,
        in_specs=[HBM] * n + [VMEM_WHOLE],
        out_specs=[HBM] * n + [VMEM_WHOLE],
        out_shape=[jax.ShapeDtypeStruct((g.shape[0], g.shape[1] // 2, g.shape[2]), F32) for g in grads]
        + [jax.ShapeDtypeStruct(small.shape, F32)],
        scratch_shapes=[pltpu.VMEM((N_DEV,) + small.shape, F32),
                        pltpu.SemaphoreType.DMA((n,)), pltpu.SemaphoreType.DMA((n,)),
                        pltpu.SemaphoreType.DMA((N_DEV - 1,)), pltpu.SemaphoreType.DMA((N_DEV - 1,))],
    )(*grads, small)


def _chip_sum(g, recv, core, name):
    ns, R, C = g.shape
    r2 = R // 2
    tr = _rows(r2, 512)
    nrb = r2 // tr

    def body(core_ref, g_ref, r_ref, o_ref, ob_ref):
        s = g_ref[...] + r_ref[...]
        o_ref[...] = s
        ob_ref[...] = s.astype(BF16)

    out = pl.BlockSpec((None, tr, C), lambda s, i, cr: (s, i, 0))
    return pl.pallas_call(
        body, name=name,
        grid_spec=pltpu.PrefetchScalarGridSpec(
            num_scalar_prefetch=1, grid=(ns, nrb),
            in_specs=[pl.BlockSpec((None, tr, C), lambda s, i, cr: (s, cr[0] * nrb + i, 0)), out],
            out_specs=[out, out]),
        out_shape=[jax.ShapeDtypeStruct((ns, r2, C), F32), jax.ShapeDtypeStruct((ns, r2, C), BF16)],
        compiler_params=_cparams(),
    )(core, g, recv)


def _shard_sum(csum, got, place, name):
    _, r2, C = csum.shape
    tr = _rows(r2, 512)
    nrb = r2 // tr

    def body(place_ref, c_ref, g0_ref, g1_ref, g2_ref, o_ref):
        o_ref[...] = ((c_ref[...] + g0_ref[...].astype(F32)) + g1_ref[...].astype(F32)) + g2_ref[...].astype(F32)

    def got_spec(k):
        return pl.BlockSpec((None, tr, C), lambda i, pr: (k, i, 0))

    return pl.pallas_call(
        body, name=name,
        grid_spec=pltpu.PrefetchScalarGridSpec(
            num_scalar_prefetch=1, grid=(nrb,),
            in_specs=[pl.BlockSpec((None, tr, C), lambda i, pr: (pr[0], i, 0)), got_spec(0), got_spec(1), got_spec(2)],
            out_specs=pl.BlockSpec((tr, C), lambda i, pr: (pr[1] * nrb + i, 0))),
        out_shape=jax.ShapeDtypeStruct((2 * r2, C), F32),
        compiler_params=_cparams(),
    )(place, csum, got, got, got)


def _scatter_chip_sums(csums, name):
    n = len(csums)

    def body(*refs):
        ins, outs = refs[:n], refs[n:2 * n]
        ssem, rsem = refs[2 * n:]
        x, y, c, chips = _place()
        cps = []
        for a in range(n):
            for k, chip in enumerate(chips):
                jj = 2 * chip[0] + chip[1]
                cp = pltpu.make_async_remote_copy(
                    src_ref=ins[a].at[jj], dst_ref=outs[a].at[k], send_sem=ssem.at[a, k], recv_sem=rsem.at[a, k],
                    device_id=(*chip, c), device_id_type=MESH)
                cp.start()
                cps.append(cp)
        for cp in cps:
            cp.wait()

    return pl.pallas_call(
        body, name=name, in_specs=[HBM] * n, out_specs=[HBM] * n,
        out_shape=[jax.ShapeDtypeStruct((3,) + g.shape[1:], g.dtype) for g in csums],
        scratch_shapes=[pltpu.SemaphoreType.DMA((n, 3)), pltpu.SemaphoreType.DMA((n, 3))],
    )(*csums)


def _join_halves(fulls, name):
    n = len(fulls)

    def body(*refs):
        outs = refs[n:2 * n]
        ssem, rsem = refs[2 * n:]
        x, y, c, _ = _place()
        cps = [pltpu.make_async_remote_copy(
            src_ref=_half(outs[a], c, True), dst_ref=_half(outs[a], c, True), send_sem=ssem.at[a],
            recv_sem=rsem.at[a], device_id=(x, y, 1 - c), device_id_type=MESH) for a in range(n)]
        for cp in cps:
            cp.start()
        for cp in cps:
            cp.wait()

    return pl.pallas_call(
        body, name=name, in_specs=[HBM] * n, out_specs=[HBM] * n,
        out_shape=[jax.ShapeDtypeStruct(f.shape, F32) for f in fulls],
        input_output_aliases={a: a for a in range(n)},
        scratch_shapes=[pltpu.SemaphoreType.DMA((n,))] * 2,
    )(*fulls)


def _pad_rows(a, rows, cols):
    return jnp.pad(a, ((0, rows - a.shape[0]), (0, cols - a.shape[1])))


def kernel(x, p, ffn1_norm, ffn1_w_gate, ffn1_w_up, ffn1_w_down, mix_norm, w_in, conv_w, conv_b, q_norm, k_norm, w_out, ffn2_norm, ffn2_w_gate, ffn2_w_up, ffn2_w_down, ple_norm, ple_w_gate, ple_w_proj, loss_target, m_ffn1_norm, m_ffn1_w_gate, m_ffn1_w_up, m_ffn1_w_down, m_mix_norm, m_w_in, m_conv_w, m_conv_b, m_q_norm, m_k_norm, m_w_out, m_ffn2_norm, m_ffn2_w_gate, m_ffn2_w_up, m_ffn2_w_down, m_ple_norm, m_ple_w_gate, m_ple_w_proj, v_ffn1_norm, v_ffn1_w_gate, v_ffn1_w_up, v_ffn1_w_down, v_mix_norm, v_w_in, v_conv_w, v_conv_b, v_q_norm, v_k_norm, v_w_out, v_ffn2_norm, v_ffn2_w_gate, v_ffn2_w_up, v_ffn2_w_down, v_ple_norm, v_ple_w_gate, v_ple_w_proj):
    big = dict(ffn1_w_gate=ffn1_w_gate, ffn1_w_up=ffn1_w_up, ffn1_w_down=ffn1_w_down, w_in=w_in, w_out=w_out,
               ffn2_w_gate=ffn2_w_gate, ffn2_w_up=ffn2_w_up, ffn2_w_down=ffn2_w_down,
               ple_w_gate=ple_w_gate, ple_w_proj=ple_w_proj)
    big_m = dict(ffn1_w_gate=m_ffn1_w_gate, ffn1_w_up=m_ffn1_w_up, ffn1_w_down=m_ffn1_w_down, w_in=m_w_in,
                 w_out=m_w_out, ffn2_w_gate=m_ffn2_w_gate, ffn2_w_up=m_ffn2_w_up, ffn2_w_down=m_ffn2_w_down,
                 ple_w_gate=m_ple_w_gate, ple_w_proj=m_ple_w_proj)
    big_v = dict(ffn1_w_gate=v_ffn1_w_gate, ffn1_w_up=v_ffn1_w_up, ffn1_w_down=v_ffn1_w_down, w_in=v_w_in,
                 w_out=v_w_out, ffn2_w_gate=v_ffn2_w_gate, ffn2_w_up=v_ffn2_w_up, ffn2_w_down=v_ffn2_w_down,
                 ple_w_gate=v_ple_w_gate, ple_w_proj=v_ple_w_proj)
    names = list(big)
    xs = x[0]
    ps = p[0, 0]
    tgt = loss_target[0]
    S, D = xs.shape
    nconv = conv_b.shape[1]
    nsb = D - nconv
    cwl = conv_w.shape[2]
    jchip = 2 * lax.axis_index("x") + lax.axis_index("y")
    core = lax.axis_index("c")

    shards = [big[k][0].astype(BF16) for k in names]
    gathered = _gather_weights(shards, _pad_rows(conv_w[0], 8, LANES), "gather_weights")
    W = dict(zip(names, gathered[:-1]))
    cw_full = jnp.transpose(gathered[-1][:, :, :cwl], (1, 0, 2)).reshape(8, N_SHARDS * cwl)
    wout_full = W["w_out"].reshape(-1, D)
    wpg_full = W["ple_w_gate"].reshape(-1, D)
    qg = jnp.tile(q_norm, (1, CONV_COLS // HEAD_DIM))
    kg = jnp.tile(k_norm, (1, CONV_COLS // HEAD_DIM))
    n_units = nconv // CONV_COLS

    x1, h1, a1, b1 = _ffn_fwd(xs, ffn1_norm, W["ffn1_w_gate"], W["ffn1_w_up"], W["ffn1_w_down"], "ffn1_fwd")
    proj, h2 = _norm_proj(x1, mix_norm, W["w_in"], "mix_in_proj")
    y_conv = _conv_fwd(proj, cw_full, conv_b, nconv, "conv_fwd")
    qs = _qk_norm(proj, qg, 3 * n_units, nsb, HEAD_DIM ** -0.5, "q_norm_fwd")
    kh = _qk_norm(proj, kg, 4 * n_units, nsb, 1.0, "k_norm_fwd")
    vb = proj[:, 3 * nconv + 2 * nsb:].astype(BF16)
    y_sb, carry = _attn_fwd(qs, kh, vb, "attn_fwd")
    ycat = jnp.concatenate([y_conv, y_sb], axis=1)
    x2 = _out_proj(ycat, wout_full, x1, "mix_out_proj")
    x3, h3, a3, b3 = _ffn_fwd(x2, ffn2_norm, W["ffn2_w_gate"], W["ffn2_w_up"], W["ffn2_w_down"], "ffn2_fwd")

    dx3, h4, du4, dpp, loss_blk, dg_ple = _ple(x3, ps, tgt, ple_norm, wpg_full, W["ple_w_proj"], "ple_loss")
    G = {}
    tk = _rows(S, 512)
    nk = S // tk
    kd = wpg_full.shape[0] // N_SHARDS
    G["ple_w_gate"] = _tn_matmul(
        h4, du4, pl.BlockSpec((tk, kd), lambda m, k: (k, m)), pl.BlockSpec((tk, D), lambda m, k: (k, 0)),
        (N_SHARDS, kd, D), pl.BlockSpec((None, kd, D), lambda m, k: (m, 0, 0)), (N_SHARDS, nk), "ple_w_gate_grad")
    P = ps.shape[1]
    npp = D // N_SHARDS
    G["ple_w_proj"] = _tn_matmul(
        ps, dpp, pl.BlockSpec((tk, P), lambda m, k: (k, 0)), pl.BlockSpec((tk, npp), lambda m, k: (k, m)),
        (N_SHARDS, P, npp), pl.BlockSpec((None, P, npp), lambda m, k: (m, 0, 0)), (N_SHARDS, nk), "ple_w_proj_grad")

    def ffn_grads(pre, h, s, da, db, dy):
        fs = s.shape[2]
        hs = pl.BlockSpec((tk, D), lambda m, k: (k, 0))
        ss = pl.BlockSpec((None, tk, fs), lambda m, k: (m, k, 0))
        G[pre + "_w_gate"] = _tn_matmul(h, da, hs, ss, (N_SHARDS, D, fs),
                                        pl.BlockSpec((None, D, fs), lambda m, k: (m, 0, 0)), (N_SHARDS, nk), pre + "_w_gate_grad")
        G[pre + "_w_up"] = _tn_matmul(h, db, hs, ss, (N_SHARDS, D, fs),
                                      pl.BlockSpec((None, D, fs), lambda m, k: (m, 0, 0)), (N_SHARDS, nk), pre + "_w_up_grad")
        G[pre + "_w_down"] = _tn_matmul(s, dy, ss, hs, (N_SHARDS, fs, D),
                                        pl.BlockSpec((None, fs, D), lambda m, k: (m, 0, 0)), (N_SHARDS, nk), pre + "_w_down_grad")

    dx2, s3, da3, db3, dy3, dg_ffn2 = _ffn_bwd(dx3, x2, ffn2_norm, a3, b3, W["ffn2_w_gate"], W["ffn2_w_up"], W["ffn2_w_down"], "ffn2_bwd")
    ffn_grads("ffn2", h3, s3, da3, db3, dy3)

    dycat = _out_proj_bwd(dx2, wout_full, "mix_out_proj_bwd")
    ko = wout_full.shape[0] // N_SHARDS
    G["w_out"] = _tn_matmul(
        ycat, dx2, pl.BlockSpec((tk, ko), lambda m, k: (k, m)), pl.BlockSpec((tk, D), lambda m, k: (k, 0)),
        (N_SHARDS, ko, D), pl.BlockSpec((None, ko, D), lambda m, k: (m, 0, 0)), (N_SHARDS, nk), "w_out_grad")
    db_, dc_, du_, dwb = _conv_bwd(proj, dycat, cw_full, conv_b, nconv, "conv_bwd")
    dqs, dkh, dv = _attn_bwd(qs, kh, vb, dycat, nconv // LANES, carry, "attn_bwd")
    dq, dg_q = _qk_norm_bwd(proj, dqs, qg, 3 * n_units, nsb, HEAD_DIM ** -0.5, "q_norm_bwd")
    dk, dg_k = _qk_norm_bwd(proj, dkh, kg, 4 * n_units, nsb, 1.0, "k_norm_bwd")
    dproj = jnp.concatenate([db_, dc_, du_, dq, dk, dv.astype(BF16)], axis=1)
    nin = W["w_in"].shape[2]
    G["w_in"] = _tn_matmul(
        h2, dproj, pl.BlockSpec((tk, D), lambda m, k: (k, 0)), pl.BlockSpec((tk, nin), lambda m, k: (k, m)),
        (N_SHARDS, D, nin), pl.BlockSpec((None, D, nin), lambda m, k: (m, 0, 0)), (N_SHARDS, nk), "w_in_grad")
    dx1, dg_mix = _norm_proj_bwd(dproj, W["w_in"], dx2, x1, mix_norm, "mix_in_proj_bwd")

    dx0, s1, da1, db1, dy1, dg_ffn1 = _ffn_bwd(dx1, xs, ffn1_norm, a1, b1, W["ffn1_w_gate"], W["ffn1_w_up"], W["ffn1_w_down"], "ffn1_bwd")
    ffn_grads("ffn1", h1, s1, da1, db1, dy1)

    assert D >= nconv and D % LANES == 0
    fold = lambda t: t[0].reshape(-1, HEAD_DIM).sum(axis=0)[None, :]
    small_rows = [dg_ffn1[0:1], dg_mix[0:1], dg_ffn2[0:1], dg_ple[0:1],
                  _pad_rows(dwb[3:4], 1, D), _pad_rows(dwb[0:3], 3, D),
                  _pad_rows(fold(dg_q), 1, D), _pad_rows(fold(dg_k), 1, D), _pad_rows(loss_blk[0:1, 0:1], 1, D)]
    small = _pad_rows(jnp.concatenate(small_rows, axis=0), SMALL_ROWS, D)

    grads = [G[k] for k in names]
    swapped = _swap_halves(grads, small, "grad_swap_halves")
    small_sum = swapped[-1]
    core_arr = jnp.reshape(core, (1,)).astype(jnp.int32)
    place = jnp.stack([jchip, core]).astype(jnp.int32)
    csums, csums_bf = zip(*[_chip_sum(g, r, core_arr, f"chip_sum_{names[a]}")
                            for a, (g, r) in enumerate(zip(grads, swapped[:-1]))])
    got = _scatter_chip_sums(list(csums_bf), "grad_scatter")
    full = _join_halves([_shard_sum(cs, gt, place, f"shard_sum_{names[a]}")
                         for a, (cs, gt) in enumerate(zip(csums, got))], "grad_join_halves")

    out_g, out_d, out_m, out_v = {}, {}, {}, {}
    for a, k in enumerate(names):
        shp = big[k].shape
        g2 = full[a]
        d_, m_, v_ = _elementwise(_adamw, [big[k][0].reshape(g2.shape), g2, big_m[k][0].reshape(g2.shape),
                                           big_v[k][0].reshape(g2.shape)], 3, f"adamw_{k}")
        out_g[k], out_d[k], out_m[k], out_v[k] = (t.reshape(shp) for t in (g2, d_, m_, v_))

    sm_names = ["ffn1_norm", "mix_norm", "ffn2_norm", "ple_norm", "conv_b", "conv_w", "q_norm", "k_norm"]
    sm_w = dict(ffn1_norm=ffn1_norm, mix_norm=mix_norm, ffn2_norm=ffn2_norm, ple_norm=ple_norm, conv_b=conv_b,
                conv_w=conv_w[0], q_norm=q_norm, k_norm=k_norm)
    sm_m = dict(ffn1_norm=m_ffn1_norm, mix_norm=m_mix_norm, ffn2_norm=m_ffn2_norm, ple_norm=m_ple_norm,
                conv_b=m_conv_b, conv_w=m_conv_w[0], q_norm=m_q_norm, k_norm=m_k_norm)
    sm_v = dict(ffn1_norm=v_ffn1_norm, mix_norm=v_mix_norm, ffn2_norm=v_ffn2_norm, ple_norm=v_ple_norm,
                conv_b=v_conv_b, conv_w=v_conv_w[0], q_norm=v_q_norm, k_norm=v_k_norm)
    sm_g = dict(ffn1_norm=small_sum[0:1], mix_norm=small_sum[1:2], ffn2_norm=small_sum[2:3], ple_norm=small_sum[3:4],
                conv_b=small_sum[4:5, :nconv],
                conv_w=lax.dynamic_slice_in_dim(small_sum[5:8, :nconv], jchip * cwl, cwl, axis=1),
                q_norm=small_sum[8:9, :HEAD_DIM], k_norm=small_sum[9:10, :HEAD_DIM])
    loss = small_sum[10, 0]
    pack = lambda d: _pad_rows(jnp.concatenate([_pad_rows(d[k], d[k].shape[0], D) for k in sm_names], axis=0), SMALL_ROWS, D)
    sd, smm, svv = _elementwise(_adamw, [pack(sm_w), pack(sm_g), pack(sm_m), pack(sm_v)], 3, "adamw_small")
    row = 0
    for k in sm_names:
        r_, c_ = sm_w[k].shape
        shp = (1, r_, c_) if k == "conv_w" else (r_, c_)
        out_g[k] = sm_g[k].reshape(shp)
        out_d[k], out_m[k], out_v[k] = (t[row:row + r_, :c_].reshape(shp) for t in (sd, smm, svv))
        row += r_

    order = ["ffn1_norm", "ffn1_w_gate", "ffn1_w_up", "ffn1_w_down", "mix_norm", "w_in", "conv_w", "conv_b",
             "q_norm", "k_norm", "w_out", "ffn2_norm", "ffn2_w_gate", "ffn2_w_up", "ffn2_w_down", "ple_norm",
             "ple_w_gate", "ple_w_proj"]
    return (loss, dx0[None], *[out_g[k] for k in order], *[out_d[k] for k in order],
            *[out_m[k] for k in order], *[out_v[k] for k in order])
```

```python
import jax
import jax.numpy as jnp
from jax import lax
from jax.experimental import pallas as pl
from jax.experimental.pallas import tpu as pltpu

F32 = jnp.float32
BF16 = jnp.bfloat16
MESH = pl.DeviceIdType.MESH

EPS = 1e-6
HEAD_DIM = 64
LANES = 128
FFN_RES = 0.5
ADAM_LR = 0.001
ADAM_B1 = 0.9
ADAM_B2 = 0.999
ADAM_EPS = 1e-08
ADAM_WD = 0.01
ADAM_STEP = 10
N_SHARDS = 4
N_DEV = 8
ATT_TILE = 256
VMEM_LIMIT = 52 * 1024 * 1024
SMALL_ROWS = 16
HBM = pl.BlockSpec(memory_space=pltpu.HBM)
VMEM_WHOLE = pl.BlockSpec(memory_space=pltpu.VMEM)


def _cparams(**kw):
    return pltpu.CompilerParams(vmem_limit_bytes=VMEM_LIMIT, **kw)


def _dot(a, b):
    return jnp.dot(a, b, preferred_element_type=F32)


def _dot_nt(a, b):
    return lax.dot_general(a, b, (((1,), (1,)), ((), ())), preferred_element_type=F32)


def _dot_tn(a, b):
    return lax.dot_general(a, b, (((0,), (0,)), ((), ())), preferred_element_type=F32)


def _split_dot(x, m):
    hi = x.astype(BF16)
    lo = (x - hi.astype(F32)).astype(BF16)
    return _dot(hi, m) + _dot(lo, m)


def _rms(x, g):
    r = lax.rsqrt(jnp.mean(x * x, axis=-1, keepdims=True) + EPS)
    xn = x * r
    return xn * g, xn, r


def _rms_bwd(dh, xn, r, g):
    dxn = dh * g
    dx = r * (dxn - xn * jnp.mean(dxn * xn, axis=-1, keepdims=True))
    return dx, jnp.sum(dh * xn, axis=0, keepdims=True)


def _rows(n, cap=512):
    for t in (2048, 1024, 512, 448, 384, 352, 256, 192, 176, 128, 96, 88, 64, 48, 32, 16, 8):
        if t <= cap and n % t == 0:
            return t
    raise ValueError(f"no row tile for {n}")


def _ffn_fwd(x, gain, wg, wu, wd, name):
    S, D = x.shape
    ns, _, fs = wg.shape
    tm = _rows(S, 512)

    def body(x_ref, g_ref, wg_ref, wu_ref, wd_ref, xo_ref, h_ref, a_ref, b_ref, hs, acc):
        j = pl.program_id(1)

        @pl.when(j == 0)
        def _():
            h, _, _ = _rms(x_ref[...], g_ref[...])
            hb = h.astype(BF16)
            hs[...] = hb
            h_ref[...] = hb
            acc[...] = jnp.zeros_like(acc)

        hb = hs[...]
        a = _dot(hb, wg_ref[...])
        b = _dot(hb, wu_ref[...])
        a_ref[...] = a
        b_ref[...] = b
        s = (a * jax.nn.sigmoid(a)) * b
        acc[...] += _dot(s.astype(BF16), wd_ref[...])

        @pl.when(j == ns - 1)
        def _():
            xo_ref[...] = x_ref[...] + FFN_RES * acc[...]

    return pl.pallas_call(
        body, name=name, grid=(S // tm, ns),
        in_specs=[
            pl.BlockSpec((tm, D), lambda i, j: (i, 0)),
            pl.BlockSpec((1, D), lambda i, j: (0, 0)),
            pl.BlockSpec((None, D, fs), lambda i, j: (j, 0, 0)),
            pl.BlockSpec((None, D, fs), lambda i, j: (j, 0, 0)),
            pl.BlockSpec((None, fs, D), lambda i, j: (j, 0, 0)),
        ],
        out_specs=[
            pl.BlockSpec((tm, D), lambda i, j: (i, 0)),
            pl.BlockSpec((tm, D), lambda i, j: (i, 0)),
            pl.BlockSpec((None, tm, fs), lambda i, j: (j, i, 0)),
            pl.BlockSpec((None, tm, fs), lambda i, j: (j, i, 0)),
        ],
        out_shape=[
            jax.ShapeDtypeStruct((S, D), F32),
            jax.ShapeDtypeStruct((S, D), BF16),
            jax.ShapeDtypeStruct((ns, S, fs), F32),
            jax.ShapeDtypeStruct((ns, S, fs), F32),
        ],
        scratch_shapes=[pltpu.VMEM((tm, D), BF16), pltpu.VMEM((tm, D), F32)],
        compiler_params=_cparams(),
    )(x, gain, wg, wu, wd)


def _ffn_bwd(dxo, x, gain, a, b, wg, wu, wd, name):
    S, D = x.shape
    ns, _, fs = wg.shape
    tm = _rows(S, 256)

    def body(dxo_ref, x_ref, g_ref, a_ref, b_ref, wg_ref, wu_ref, wd_ref,
             dx_ref, s_ref, da_ref, db_ref, dy_ref, dg_ref, dys, acc):
        i = pl.program_id(0)
        j = pl.program_id(1)

        @pl.when((i == 0) & (j == 0))
        def _():
            dg_ref[...] = jnp.zeros_like(dg_ref)

        @pl.when(j == 0)
        def _():
            dy = (FFN_RES * dxo_ref[...]).astype(BF16)
            dys[...] = dy
            dy_ref[...] = dy
            acc[...] = jnp.zeros_like(acc)

        av = a_ref[...]
        bv = b_ref[...]
        ds = _dot_nt(dys[...], wd_ref[...])
        sig = jax.nn.sigmoid(av)
        sl = av * sig
        s_ref[...] = (sl * bv).astype(BF16)
        da = (ds * bv * (sig * (1.0 + av * (1.0 - sig)))).astype(BF16)
        db = (ds * sl).astype(BF16)
        da_ref[...] = da
        db_ref[...] = db
        acc[...] += _dot_nt(da, wg_ref[...]) + _dot_nt(db, wu_ref[...])

        @pl.when(j == ns - 1)
        def _():
            g = g_ref[...]
            _, xn, r = _rms(x_ref[...], g)
            dx, dg = _rms_bwd(acc[...], xn, r, g)
            dx_ref[...] = dxo_ref[...] + dx
            dg_ref[...] += jnp.broadcast_to(dg, dg_ref.shape)

    return pl.pallas_call(
        body, name=name, grid=(S // tm, ns),
        in_specs=[
            pl.BlockSpec((tm, D), lambda i, j: (i, 0)),
            pl.BlockSpec((tm, D), lambda i, j: (i, 0)),
            pl.BlockSpec((1, D), lambda i, j: (0, 0)),
            pl.BlockSpec((None, tm, fs), lambda i, j: (j, i, 0)),
            pl.BlockSpec((None, tm, fs), lambda i, j: (j, i, 0)),
            pl.BlockSpec((None, D, fs), lambda i, j: (j, 0, 0)),
            pl.BlockSpec((None, D, fs), lambda i, j: (j, 0, 0)),
            pl.BlockSpec((None, fs, D), lambda i, j: (j, 0, 0)),
        ],
        out_specs=[
            pl.BlockSpec((tm, D), lambda i, j: (i, 0)),
            pl.BlockSpec((None, tm, fs), lambda i, j: (j, i, 0)),
            pl.BlockSpec((None, tm, fs), lambda i, j: (j, i, 0)),
            pl.BlockSpec((None, tm, fs), lambda i, j: (j, i, 0)),
            pl.BlockSpec((tm, D), lambda i, j: (i, 0)),
            pl.BlockSpec((8, D), lambda i, j: (0, 0)),
        ],
        out_shape=[
            jax.ShapeDtypeStruct((S, D), F32),
            jax.ShapeDtypeStruct((ns, S, fs), BF16),
            jax.ShapeDtypeStruct((ns, S, fs), BF16),
            jax.ShapeDtypeStruct((ns, S, fs), BF16),
            jax.ShapeDtypeStruct((S, D), BF16),
            jax.ShapeDtypeStruct((8, D), F32),
        ],
        scratch_shapes=[pltpu.VMEM((tm, D), BF16), pltpu.VMEM((tm, D), F32)],
        compiler_params=_cparams(),
    )(dxo, x, gain, a, b, wg, wu, wd)


def _tn_matmul(a, b, a_spec, b_spec, o_shape, o_spec, grid, name):
    kaxis = len(grid) - 1

    def body(a_ref, b_ref, o_ref):
        @pl.when(pl.program_id(kaxis) == 0)
        def _():
            o_ref[...] = jnp.zeros_like(o_ref)

        o_ref[...] += _dot_tn(a_ref[...].astype(BF16), b_ref[...].astype(BF16))

    return pl.pallas_call(
        body, name=name, grid=grid, in_specs=[a_spec, b_spec], out_specs=o_spec,
        out_shape=jax.ShapeDtypeStruct(o_shape, F32), compiler_params=_cparams(),
    )(a, b)


def _norm_proj(x, gain, w, name):
    S, D = x.shape
    ns, _, n = w.shape
    tm = _rows(S, 512)

    def body(x_ref, g_ref, w_ref, o_ref, h_ref, hs):
        @pl.when(pl.program_id(1) == 0)
        def _():
            h, _, _ = _rms(x_ref[...], g_ref[...])
            hb = h.astype(BF16)
            hs[...] = hb
            h_ref[...] = hb

        o_ref[...] = _dot(hs[...], w_ref[...])

    return pl.pallas_call(
        body, name=name, grid=(S // tm, ns),
        in_specs=[
            pl.BlockSpec((tm, D), lambda i, j: (i, 0)),
            pl.BlockSpec((1, D), lambda i, j: (0, 0)),
            pl.BlockSpec((None, D, n), lambda i, j: (j, 0, 0)),
        ],
        out_specs=[
            pl.BlockSpec((tm, n), lambda i, j: (i, j)),
            pl.BlockSpec((tm, D), lambda i, j: (i, 0)),
        ],
        out_shape=[jax.ShapeDtypeStruct((S, ns * n), F32), jax.ShapeDtypeStruct((S, D), BF16)],
        scratch_shapes=[pltpu.VMEM((tm, D), BF16)],
        compiler_params=_cparams(),
    )(x, gain, w)


def _norm_proj_bwd(dproj, w, dres, x, gain, name):
    S, D = x.shape
    ns, _, n = w.shape
    tm = _rows(S, 512)

    def body(dp_ref, w_ref, dres_ref, x_ref, g_ref, dx_ref, dg_ref, acc):
        i = pl.program_id(0)
        j = pl.program_id(1)

        @pl.when((i == 0) & (j == 0))
        def _():
            dg_ref[...] = jnp.zeros_like(dg_ref)

        @pl.when(j == 0)
        def _():
            acc[...] = jnp.zeros_like(acc)

        acc[...] += _dot_nt(dp_ref[...], w_ref[...])

        @pl.when(j == ns - 1)
        def _():
            g = g_ref[...]
            _, xn, r = _rms(x_ref[...], g)
            dx, dg = _rms_bwd(acc[...], xn, r, g)
            dx_ref[...] = dres_ref[...] + dx
            dg_ref[...] += jnp.broadcast_to(dg, dg_ref.shape)

    return pl.pallas_call(
        body, name=name, grid=(S // tm, ns),
        in_specs=[
            pl.BlockSpec((tm, n), lambda i, j: (i, j)),
            pl.BlockSpec((None, D, n), lambda i, j: (j, 0, 0)),
            pl.BlockSpec((tm, D), lambda i, j: (i, 0)),
            pl.BlockSpec((tm, D), lambda i, j: (i, 0)),
            pl.BlockSpec((1, D), lambda i, j: (0, 0)),
        ],
        out_specs=[
            pl.BlockSpec((tm, D), lambda i, j: (i, 0)),
            pl.BlockSpec((8, D), lambda i, j: (0, 0)),
        ],
        out_shape=[jax.ShapeDtypeStruct((S, D), F32), jax.ShapeDtypeStruct((8, D), F32)],
        scratch_shapes=[pltpu.VMEM((tm, D), F32)],
        compiler_params=_cparams(),
    )(dproj, w, dres, x, gain)


def _out_proj(ycat, w, res, name):
    S, K = ycat.shape
    D = w.shape[1]
    tm = _rows(S, 512)

    def body(y_ref, w_ref, r_ref, o_ref):
        o_ref[...] = r_ref[...] + _dot(y_ref[...], w_ref[...])

    return pl.pallas_call(
        body, name=name, grid=(S // tm,),
        in_specs=[
            pl.BlockSpec((tm, K), lambda i: (i, 0)),
            pl.BlockSpec((K, D), lambda i: (0, 0)),
            pl.BlockSpec((tm, D), lambda i: (i, 0)),
        ],
        out_specs=pl.BlockSpec((tm, D), lambda i: (i, 0)),
        out_shape=jax.ShapeDtypeStruct((S, D), F32),
        compiler_params=_cparams(),
    )(ycat, w, res)


def _out_proj_bwd(dx, w, name):
    S, D = dx.shape
    K = w.shape[0]
    tm = _rows(S, 512)

    def body(d_ref, w_ref, o_ref):
        o_ref[...] = _dot_nt(d_ref[...].astype(BF16), w_ref[...])

    return pl.pallas_call(
        body, name=name, grid=(S // tm,),
        in_specs=[pl.BlockSpec((tm, D), lambda i: (i, 0)), pl.BlockSpec((K, D), lambda i: (0, 0))],
        out_specs=pl.BlockSpec((tm, K), lambda i: (i, 0)),
        out_shape=jax.ShapeDtypeStruct((S, K), F32),
        compiler_params=_cparams(),
    )(dx, w)


CONV_COLS = 256


def _shift_down(z, halo, k, row):
    out = pltpu.roll(z, k, 0)
    for n in range(k):
        out = jnp.where(row == n, halo[8 - k + n:8 - k + n + 1, :], out)
    return out


def _shift_up(g, halo, k, row, ts):
    out = pltpu.roll(g, ts - k, 0)
    for n in range(k):
        out = jnp.where(row == ts - k + n, halo[n:n + 1, :], out)
    return out


def _conv_fwd(proj, cw, cb, nconv, name):
    S = proj.shape[0]
    ncb = nconv // CONV_COLS
    ts = _rows(S, 512)
    hb = ts // 8

    def body(b_ref, c_ref, u_ref, ch_ref, uh_ref, w_ref, bias_ref, o_ref):
        i = pl.program_id(1)
        z = c_ref[...] * u_ref[...]
        halo = jnp.where(i > 0, ch_ref[...] * uh_ref[...], 0.0)
        row = lax.broadcasted_iota(jnp.int32, z.shape, 0)
        w = w_ref[...]
        yc = w[0:1, :] * _shift_down(z, halo, 2, row) + w[1:2, :] * _shift_down(z, halo, 1, row) + w[2:3, :] * z
        o_ref[...] = (b_ref[...] * (yc + bias_ref[...])).astype(BF16)

    def blk(unit):
        return pl.BlockSpec((ts, CONV_COLS), lambda cbi, i: (i, unit * ncb + cbi))

    def prev(unit):
        return pl.BlockSpec((8, CONV_COLS), lambda cbi, i: (jnp.maximum(i * hb - 1, 0), unit * ncb + cbi))

    return pl.pallas_call(
        body, name=name, grid=(ncb, S // ts),
        in_specs=[blk(0), blk(1), blk(2), prev(1), prev(2),
                  pl.BlockSpec((8, CONV_COLS), lambda cbi, i: (0, cbi)),
                  pl.BlockSpec((1, CONV_COLS), lambda cbi, i: (0, cbi))],
        out_specs=pl.BlockSpec((ts, CONV_COLS), lambda cbi, i: (i, cbi)),
        out_shape=jax.ShapeDtypeStruct((S, nconv), BF16),
        compiler_params=_cparams(),
    )(proj, proj, proj, proj, proj, cw, cb)


def _conv_bwd(proj, dy, cw, cb, nconv, name):
    S = proj.shape[0]
    ncb = nconv // CONV_COLS
    ts = _rows(S, 512)
    hb = ts // 8
    nblk = S // ts

    def body(b_ref, c_ref, u_ref, dy_ref, ch_ref, uh_ref, bn_ref, dyn_ref, w_ref, bias_ref,
             db_ref, dc_ref, du_ref, dw_ref):
        i = pl.program_id(1)

        @pl.when(i == 0)
        def _():
            dw_ref[...] = jnp.zeros_like(dw_ref)

        c = c_ref[...]
        u = u_ref[...]
        bg = b_ref[...]
        dy_ = dy_ref[...]
        z = c * u
        halo = jnp.where(i > 0, ch_ref[...] * uh_ref[...], 0.0)
        row = lax.broadcasted_iota(jnp.int32, z.shape, 0)
        w = w_ref[...]
        z2 = _shift_down(z, halo, 2, row)
        z1 = _shift_down(z, halo, 1, row)
        yc = w[0:1, :] * z2 + w[1:2, :] * z1 + w[2:3, :] * z
        db_ref[...] = (dy_ * (yc + bias_ref[...])).astype(BF16)
        g = dy_ * bg
        gnext = jnp.where(i < nblk - 1, dyn_ref[...] * bn_ref[...], 0.0)
        dz = w[2:3, :] * g + w[1:2, :] * _shift_up(g, gnext, 1, row, ts) + w[0:1, :] * _shift_up(g, gnext, 2, row, ts)
        dc_ref[...] = (dz * u).astype(BF16)
        du_ref[...] = (dz * c).astype(BF16)
        r8 = lax.broadcasted_iota(jnp.int32, (8, CONV_COLS), 0)
        sums = [jnp.sum(g * z2, axis=0, keepdims=True), jnp.sum(g * z1, axis=0, keepdims=True),
                jnp.sum(g * z, axis=0, keepdims=True), jnp.sum(g, axis=0, keepdims=True)]
        upd = jnp.zeros((8, CONV_COLS), F32)
        for n, sv in enumerate(sums):
            upd = jnp.where(r8 == n, sv, upd)
        dw_ref[...] += upd

    def blk(unit):
        return pl.BlockSpec((ts, CONV_COLS), lambda cbi, i: (i, unit * ncb + cbi))

    def prev(unit):
        return pl.BlockSpec((8, CONV_COLS), lambda cbi, i: (jnp.maximum(i * hb - 1, 0), unit * ncb + cbi))

    def nxt(unit):
        return pl.BlockSpec((8, CONV_COLS), lambda cbi, i: (jnp.minimum((i + 1) * hb, S // 8 - 1), unit * ncb + cbi))

    o = pl.BlockSpec((ts, CONV_COLS), lambda cbi, i: (i, cbi))
    return pl.pallas_call(
        body, name=name, grid=(ncb, nblk),
        in_specs=[blk(0), blk(1), blk(2), blk(0), prev(1), prev(2), nxt(0), nxt(0),
                  pl.BlockSpec((8, CONV_COLS), lambda cbi, i: (0, cbi)),
                  pl.BlockSpec((1, CONV_COLS), lambda cbi, i: (0, cbi))],
        out_specs=[o, o, o, pl.BlockSpec((8, CONV_COLS), lambda cbi, i: (0, cbi))],
        out_shape=[jax.ShapeDtypeStruct((S, nconv), BF16)] * 3 + [jax.ShapeDtypeStruct((8, nconv), F32)],
        compiler_params=_cparams(),
    )(proj, proj, proj, dy, proj, proj, proj, dy, cw, cb)


def _group_ones(n):
    r = lax.broadcasted_iota(jnp.int32, (n, n), 0) // HEAD_DIM
    c = lax.broadcasted_iota(jnp.int32, (n, n), 1) // HEAD_DIM
    return jnp.where(r == c, 1.0, 0.0).astype(BF16)


def _qk_norm(proj, gain_t, unit0, nsb, scale, name):
    S = proj.shape[0]
    nb = nsb // CONV_COLS
    ts = _rows(S, 512)

    def body(x_ref, g_ref, o_ref):
        x = x_ref[...]
        ss = _split_dot(x * x, _group_ones(CONV_COLS))
        r = lax.rsqrt(ss * (1.0 / HEAD_DIM) + EPS)
        o_ref[...] = ((x * r) * g_ref[...] * scale).astype(BF16)

    return pl.pallas_call(
        body, name=name, grid=(nb, S // ts),
        in_specs=[pl.BlockSpec((ts, CONV_COLS), lambda u, i: (i, unit0 + u)),
                  pl.BlockSpec((1, CONV_COLS), lambda u, i: (0, 0))],
        out_specs=pl.BlockSpec((ts, CONV_COLS), lambda u, i: (i, u)),
        out_shape=jax.ShapeDtypeStruct((S, nsb), BF16),
        compiler_params=_cparams(),
    )(proj, gain_t)


def _qk_norm_bwd(proj, dout, gain_t, unit0, nsb, scale, name):
    S = proj.shape[0]
    nb = nsb // CONV_COLS
    ts = _rows(S, 512)

    def body(x_ref, d_ref, g_ref, dx_ref, dg_ref):
        @pl.when(pl.program_id(1) == 0)
        def _():
            dg_ref[...] = jnp.zeros_like(dg_ref)

        x = x_ref[...]
        g = g_ref[...]
        ones = _group_ones(CONV_COLS)
        ss = _split_dot(x * x, ones)
        r = lax.rsqrt(ss * (1.0 / HEAD_DIM) + EPS)
        xn = x * r
        dh = d_ref[...] * scale
        dxn = dh * g
        m = _split_dot(dxn * xn, ones) * (1.0 / HEAD_DIM)
        dx_ref[...] = (r * (dxn - xn * m)).astype(BF16)
        dg_ref[...] += jnp.broadcast_to(jnp.sum(dh * xn, axis=0, keepdims=True), dg_ref.shape)

    return pl.pallas_call(
        body, name=name, grid=(nb, S // ts),
        in_specs=[pl.BlockSpec((ts, CONV_COLS), lambda u, i: (i, unit0 + u)),
                  pl.BlockSpec((ts, CONV_COLS), lambda u, i: (i, u)),
                  pl.BlockSpec((1, CONV_COLS), lambda u, i: (0, 0))],
        out_specs=[pl.BlockSpec((ts, CONV_COLS), lambda u, i: (i, u)),
                   pl.BlockSpec((8, CONV_COLS), lambda u, i: (0, u))],
        out_shape=[jax.ShapeDtypeStruct((S, nsb), BF16), jax.ShapeDtypeStruct((8, nsb), F32)],
        compiler_params=_cparams(),
    )(proj, dout, gain_t)


Z_CLAMP = 80.0
N_SLOTS = 3
SAT_LIMIT = 120.0


def _head_masks():
    lane = lax.broadcasted_iota(jnp.int32, (1, LANES), 1)
    return [lane < HEAD_DIM, lane >= HEAD_DIM], lane


def _tile_consts(T):
    r_i = lax.broadcasted_iota(jnp.int32, (T, T), 0)
    c_i = lax.broadcasted_iota(jnp.int32, (T, T), 1)
    neg_suffix = jnp.where(r_i >= c_i, -1.0, 0.0).astype(BF16)
    prefix = jnp.where(r_i <= c_i, 1.0, 0.0).astype(BF16)
    return neg_suffix, prefix, c_i < r_i


def _pipeline(n, stages, first_special=False, last_special=False, saturated=None):
    depth = len(stages)
    head = depth if first_special else depth - 1
    off = 0 if last_special else 1
    for m in range(head):
        for k in reversed(range(min(m, depth - 1) + 1)):
            stages[k](m - k, (m - k) % N_SLOTS, first_special and m == k)

    def trip(m, u):
        for k in reversed(range(depth)):
            stages[k](m - k, (head + u - k) % N_SLOTS, False)

    def group(g, carry):
        for u in range(N_SLOTS):
            trip(head + g * N_SLOTS + u, u)
        return carry

    count = n - 1 + off - head
    full = count // N_SLOTS
    if saturated is None:
        lax.fori_loop(0, full, group, 0)
        go_on, done = True, n
    else:
        def more(state):
            return (state[0] < full) & (state[1] == 0)

        def step(state):
            group(state[0], 0)
            return state[0] + 1, saturated().astype(jnp.int32)

        groups, stop = lax.while_loop(more, step, (jnp.int32(0), saturated().astype(jnp.int32)))
        go_on = stop == 0
        done = jnp.where(go_on, n, head - depth + 1 + N_SLOTS * groups)
    for r in range(N_SLOTS):
        @pl.when((count - full * N_SLOTS == r) & go_on)
        def _(r=r):
            for u in range(r):
                trip(head + full * N_SLOTS + u, u)
            for e in range(depth - off):
                for k in reversed(range(e + off, depth)):
                    t = n - 1 - (k - e - off)
                    stages[k](t, (head + r + e - k) % N_SLOTS, last_special and k == e + off)
    return done


def _sequential(n, stages, first_special=False, last_special=False):
    def one(t, special):
        for st in stages:
            st(t, 0, special)

    if first_special:
        one(0, True)

    def loop(t, carry):
        one(t, False)
        return carry

    lax.fori_loop(1 if first_special else 0, n - 1 if last_special else n, loop, 0)
    if last_special:
        one(n - 1, True)


def _attn_fwd(q, k, v, name):
    S, nsb = q.shape
    T = ATT_TILE
    hp = nsb // LANES
    nb = S // T
    assert nb <= HEAD_DIM

    def body(q_ref, k_ref, v_ref, y_ref, cs_ref, c_ref, acc, z_st, inc_st):
        i = pl.program_id(1)
        masks, lane = _head_masks()
        qv = q_ref[...]
        qm = [jnp.where(m, qv, jnp.zeros_like(qv)) for m in masks]
        neg_suffix, _, causal = _tile_consts(T)
        c_ref[...] = jnp.zeros_like(c_ref)
        acc[...] = jnp.zeros_like(acc)
        cs_ref[...] = jnp.zeros_like(cs_ref)

        def blk(ref, j):
            return ref[pl.ds(pl.multiple_of(j * T, T), T), :]

        def scores(t, slot, diag):
            kj = blk(k_ref, i - t)
            for h in range(2):
                z_st[slot, h] = jnp.minimum(_dot_nt(qm[h], kj), Z_CLAMP)

        def suffix_sums(t, slot, diag):
            for h in range(2):
                sp = jnp.log(1.0 + jnp.exp(z_st[slot, h]))
                if diag:
                    sp = jnp.where(causal, sp, 0.0)
                inc_st[slot, h] = _dot(sp.astype(BF16), neg_suffix)

        def weights(t, slot, diag):
            vj = blk(v_ref, i - t)
            for h in range(2):
                inc = inc_st[slot, h]
                c = c_ref[h]
                a = jnp.exp(z_st[slot, h] + inc + c)
                if diag:
                    a = jnp.where(causal, a, 0.0)
                upd = _dot(a.astype(BF16), vj)
                acc[...] += jnp.where(masks[h], upd, 0.0)
                cs_ref[...] = jnp.where(lane == i - t + HEAD_DIM * h, c, cs_ref[...])
                c_ref[h] = c + inc[:, 0:1]

        stages = [scores, suffix_sums, weights]

        def saturated():
            return jnp.max(c_ref[...]) < -SAT_LIMIT

        def note(used):
            cs_ref[...] = jnp.where(lane == LANES - 1, used.astype(F32), cs_ref[...])

        @pl.when(i + 1 >= len(stages))
        def _():
            note(_pipeline(i + 1, stages, first_special=True, saturated=saturated))

        @pl.when(i + 1 < len(stages))
        def _():
            _sequential(i + 1, stages, first_special=True)
            note(i + 1)

        y_ref[...] = acc[...].astype(BF16)

    return pl.pallas_call(
        body, name=name, grid=(hp, nb),
        in_specs=[pl.BlockSpec((T, LANES), lambda p, i: (i, p)),
                  pl.BlockSpec((S, LANES), lambda p, i: (0, p)),
                  pl.BlockSpec((S, LANES), lambda p, i: (0, p))],
        out_specs=[pl.BlockSpec((T, LANES), lambda p, i: (i, p)),
                   pl.BlockSpec((None, T, LANES), lambda p, i: (p, i, 0))],
        out_shape=[jax.ShapeDtypeStruct((S, nsb), BF16), jax.ShapeDtypeStruct((hp, S, LANES), F32)],
        scratch_shapes=[pltpu.VMEM((2, T, 1), F32), pltpu.VMEM((T, LANES), F32),
                        pltpu.VMEM((N_SLOTS, 2, T, T), F32), pltpu.VMEM((N_SLOTS, 2, T, T), F32)],
        compiler_params=_cparams(),
    )(q, k, v)


def _attn_bwd(q, k, v, dy, col0, carry, name):
    S, nsb = q.shape
    T = ATT_TILE
    hp = nsb // LANES
    nb = S // T

    def body(q_ref, k_ref, v_ref, dy_ref, cs_ref, dq_ref, dk_ref, dv_ref, e_ref, acc,
             z_st, da_st, b_st, inc_st, a_st, e_st, p_st):
        i = pl.program_id(1)

        @pl.when(i == 0)
        def _():
            dk_ref[...] = jnp.zeros_like(dk_ref)
            dv_ref[...] = jnp.zeros_like(dv_ref)

        masks, lane = _head_masks()
        qv = q_ref[...]
        dyb = dy_ref[...].astype(BF16)
        qm = [jnp.where(m, qv, jnp.zeros_like(qv)) for m in masks]
        dym = [jnp.where(m, dyb, jnp.zeros_like(dyb)) for m in masks]
        neg_suffix, prefix, causal = _tile_consts(T)
        e_ref[...] = jnp.zeros_like(e_ref)
        acc[...] = jnp.zeros_like(acc)

        def blk(ref, j):
            return ref[pl.ds(pl.multiple_of(j * T, T), T), :]

        used = jnp.max(jnp.where(lane == LANES - 1, cs_ref[...], 0.0)).astype(jnp.int32)
        n = jnp.clip(used, 1, i + 1)
        first = i + 1 - n

        def scores(t, slot, diag):
            kj = blk(k_ref, first + t)
            vj = blk(v_ref, first + t)
            for h in range(2):
                z_st[slot, h] = jnp.minimum(_dot_nt(qm[h], kj), Z_CLAMP)
                da_st[slot, h] = _dot_nt(dym[h], vj)

        def suffix_sums(t, slot, diag):
            for h in range(2):
                u = jnp.exp(z_st[slot, h])
                w = 1.0 + u
                b_st[slot, h] = u / w
                sp = jnp.log(w)
                if diag:
                    sp = jnp.where(causal, sp, 0.0)
                inc_st[slot, h] = _dot(sp.astype(BF16), neg_suffix)

        def probs(t, slot, diag):
            csv = cs_ref[...]
            for h in range(2):
                c = jnp.sum(jnp.where(lane == first + t + HEAD_DIM * h, csv, 0.0), axis=-1, keepdims=True)
                a = jnp.exp(z_st[slot, h] + inc_st[slot, h] + c)
                if diag:
                    a = jnp.where(causal, a, 0.0)
                a_st[slot, h] = a.astype(BF16)
                e = a * da_st[slot, h]
                e_st[slot, h] = e
                p_st[slot, h] = _dot(e.astype(BF16), prefix)

        def grads(t, slot, diag):
            kj = blk(k_ref, first + t)
            off = pl.multiple_of((first + t) * T, T)
            for h in range(2):
                p = p_st[slot, h]
                dz = e_st[slot, h] - b_st[slot, h] * (e_ref[h] + p)
                if diag:
                    dz = jnp.where(causal, dz, 0.0)
                dzb = dz.astype(BF16)
                acc[...] += jnp.where(masks[h], _dot(dzb, kj), 0.0)
                dk_ref[pl.ds(off, T), :] += _dot_tn(dzb, qm[h])
                dv_ref[pl.ds(off, T), :] += _dot_tn(a_st[slot, h], dym[h])
                e_ref[h] += p[:, T - 1:T]

        stages = [scores, suffix_sums, probs, grads]

        @pl.when(n >= len(stages))
        def _():
            _pipeline(n, stages, last_special=True)

        @pl.when(n < len(stages))
        def _():
            _sequential(n, stages, last_special=True)

        dq_ref[...] = acc[...]

    return pl.pallas_call(
        body, name=name, grid=(hp, nb),
        in_specs=[pl.BlockSpec((T, LANES), lambda p, i: (i, p)),
                  pl.BlockSpec((S, LANES), lambda p, i: (0, p)),
                  pl.BlockSpec((S, LANES), lambda p, i: (0, p)),
                  pl.BlockSpec((T, LANES), lambda p, i: (i, col0 + p)),
                  pl.BlockSpec((None, T, LANES), lambda p, i: (p, i, 0))],
        out_specs=[pl.BlockSpec((T, LANES), lambda p, i: (i, p)),
                   pl.BlockSpec((S, LANES), lambda p, i: (0, p)),
                   pl.BlockSpec((S, LANES), lambda p, i: (0, p))],
        out_shape=[jax.ShapeDtypeStruct((S, nsb), F32)] * 3,
        scratch_shapes=[pltpu.VMEM((2, T, 1), F32), pltpu.VMEM((T, LANES), F32)]
        + [pltpu.VMEM((N_SLOTS, 2, T, T), dt) for dt in (F32, F32, F32, F32, BF16, F32, F32)],
        compiler_params=_cparams(),
    )(q, k, v, dy, carry)


def _ple(x, p, tgt, gain, wpg, wpp, name):
    S, D = x.shape
    P = p.shape[1]
    ns, _, nc = wpp.shape
    tm = _rows(S, 256)

    def body(x_ref, p_ref, t_ref, g_ref, wpg_ref, wpp_ref, dx_ref, h_ref, du_ref, dpp_ref, loss_ref, dg_ref):
        @pl.when(pl.program_id(0) == 0)
        def _():
            loss_ref[...] = jnp.zeros_like(loss_ref)
            dg_ref[...] = jnp.zeros_like(dg_ref)

        x_ = x_ref[...]
        g = g_ref[...]
        h, xn, r = _rms(x_, g)
        hb = h.astype(BF16)
        h_ref[...] = hb
        gate = jax.nn.sigmoid(_dot(hb, wpg_ref[...]))
        pb = p_ref[...].astype(BF16)
        pp = jnp.concatenate([_dot(pb, wpp_ref[n]) for n in range(ns)], axis=1)
        err = (x_ + gate * pp) - t_ref[...]
        loss_ref[...] += (0.5 / D) * jnp.sum(err * err)
        dy = err * (1.0 / D)
        du = ((dy * pp) * (gate * (1.0 - gate))).astype(BF16)
        du_ref[...] = du
        dpp_ref[...] = (dy * gate).astype(BF16)
        dx, dg = _rms_bwd(_dot_nt(du, wpg_ref[...]), xn, r, g)
        dx_ref[...] = dy + dx
        dg_ref[...] += jnp.broadcast_to(dg, dg_ref.shape)

    row = pl.BlockSpec((tm, D), lambda i: (i, 0))
    return pl.pallas_call(
        body, name=name, grid=(S // tm,),
        in_specs=[row, pl.BlockSpec((tm, P), lambda i: (i, 0)), row,
                  pl.BlockSpec((1, D), lambda i: (0, 0)),
                  pl.BlockSpec((D, D), lambda i: (0, 0)),
                  pl.BlockSpec((ns, P, nc), lambda i: (0, 0, 0))],
        out_specs=[row, row, row, row,
                   pl.BlockSpec((8, LANES), lambda i: (0, 0)),
                   pl.BlockSpec((8, D), lambda i: (0, 0))],
        out_shape=[jax.ShapeDtypeStruct((S, D), F32)] + [jax.ShapeDtypeStruct((S, D), BF16)] * 3
        + [jax.ShapeDtypeStruct((8, LANES), F32), jax.ShapeDtypeStruct((8, D), F32)],
        compiler_params=_cparams(),
    )(x, p, tgt, gain, wpg, wpp)


def _elementwise(fn, ins, n_out, name):
    R, C = ins[0].shape
    tr = _rows(R, 512)

    def body(*refs):
        outs = fn(*[r[...] for r in refs[:len(ins)]])
        for o_ref, o in zip(refs[len(ins):], outs):
            o_ref[...] = o

    spec = pl.BlockSpec((tr, C), lambda i: (i, 0))
    return pl.pallas_call(
        body, name=name, grid=(R // tr,), in_specs=[spec] * len(ins), out_specs=[spec] * n_out,
        out_shape=[jax.ShapeDtypeStruct((R, C), F32)] * n_out, compiler_params=_cparams(),
    )(*ins)


def _adamw(w, g, m, v):
    m = ADAM_B1 * m + (1.0 - ADAM_B1) * g
    v = ADAM_B2 * v + (1.0 - ADAM_B2) * jnp.square(g)
    m_hat = m / (1.0 - ADAM_B1 ** ADAM_STEP)
    v_hat = v / (1.0 - ADAM_B2 ** ADAM_STEP)
    delta = -ADAM_LR * (m_hat / (jnp.sqrt(v_hat) + ADAM_EPS) + ADAM_WD * w)
    return delta, m, v


def _place():
    x, y, c = lax.axis_index("x"), lax.axis_index("y"), lax.axis_index("c")
    chips = [(1 - x, y), (x, 1 - y), (1 - x, 1 - y)]
    return x, y, c, chips


def _half(ref, c, axis_rows):
    n = ref.shape[-2]
    start = pl.multiple_of(c * (n // 2), 8)
    idx = (slice(None),) * (len(ref.shape) - 2) + (pl.ds(start, n // 2), slice(None))
    return ref.at[idx]


def _gather_weights(shards, small, name):
    n = len(shards)

    def body(*refs):
        ins, small_in = refs[:n], refs[n]
        outs, small_out = refs[n + 1:2 * n + 1], refs[2 * n + 1]
        lsem, lrsem, ssem, rsem, sm_s, sm_r = refs[2 * n + 2:]
        x, y, c, chips = _place()
        j = 2 * x + y
        sib = (x, y, 1 - c)

        local = [pltpu.make_async_remote_copy(
            src_ref=ins[a], dst_ref=outs[a].at[j], send_sem=lsem.at[a], recv_sem=lrsem.at[a],
            device_id=sib, device_id_type=MESH) for a in range(n)]
        for cp in local:
            cp.start()
        small_out[j] = small_in[...]
        small_cp = [pltpu.make_async_remote_copy(
            src_ref=small_in, dst_ref=small_out.at[j], send_sem=sm_s.at[k], recv_sem=sm_r.at[k],
            device_id=(*chip, c), device_id_type=MESH) for k, chip in enumerate(chips)]
        for cp in small_cp:
            cp.start()

        def ici(a, k, chip, jj, dev):
            return pltpu.make_async_remote_copy(
                src_ref=_half(ins[a], c, True) if dev is not None else _half(outs[a].at[jj], c, True),
                dst_ref=_half(outs[a].at[jj], c, True),
                send_sem=ssem.at[a, k], recv_sem=rsem.at[a, k],
                device_id=dev if dev is not None else (*chip, c), device_id_type=MESH)

        first = []
        for a in range(n):
            for k, chip in enumerate(chips):
                cp = ici(a, k, chip, j, (*chip, c))
                cp.start()
                first.append(cp)
        passed = []
        for a in range(n):
            for k, chip in enumerate(chips):
                jj = 2 * chip[0] + chip[1]
                ici(a, k, chip, jj, None).wait_recv()
                fw = pltpu.make_async_remote_copy(
                    src_ref=_half(outs[a].at[jj], c, True), dst_ref=_half(outs[a].at[jj], c, True),
                    send_sem=ssem.at[a, 3 + k], recv_sem=rsem.at[a, 3 + k], device_id=sib, device_id_type=MESH)
                fw.start()
                passed.append(fw)
        for a in range(n):
            for k, chip in enumerate(chips):
                jj = 2 * chip[0] + chip[1]
                pltpu.make_async_remote_copy(
                    src_ref=_half(outs[a].at[jj], 1 - c, True), dst_ref=_half(outs[a].at[jj], 1 - c, True),
                    send_sem=ssem.at[a, 3 + k], recv_sem=rsem.at[a, 3 + k], device_id=sib,
                    device_id_type=MESH).wait_recv()
        for cp in small_cp:
            cp.wait()
        for cp in first + passed:
            cp.wait_send()
        for cp in local:
            cp.wait()

    return pl.pallas_call(
        body, name=name,
        in_specs=[HBM] * n + [VMEM_WHOLE],
        out_specs=[HBM] * n + [VMEM_WHOLE],
        out_shape=[jax.ShapeDtypeStruct((N_SHARDS,) + s.shape, s.dtype) for s in shards]
        + [jax.ShapeDtypeStruct((N_SHARDS,) + small.shape, small.dtype)],
        scratch_shapes=[pltpu.SemaphoreType.DMA((n,)), pltpu.SemaphoreType.DMA((n,)),
                        pltpu.SemaphoreType.DMA((n, 6)), pltpu.SemaphoreType.DMA((n, 6)),
                        pltpu.SemaphoreType.DMA((3,)), pltpu.SemaphoreType.DMA((3,))],
    )(*shards, small)


def _swap_halves(grads, small, name):
    n = len(grads)

    def body(*refs):
        ins, small_in = refs[:n], refs[n]
        outs, small_out = refs[n + 1:2 * n + 1], refs[2 * n + 1]
        buf, ssem, rsem, sm_s, sm_r = refs[2 * n + 2:]
        x, y, c, _ = _place()
        me = 4 * x + 2 * y + c
        sib = (x, y, 1 - c)
        cps = [pltpu.make_async_remote_copy(
            src_ref=_half(ins[a], 1 - c, True), dst_ref=outs[a], send_sem=ssem.at[a], recv_sem=rsem.at[a],
            device_id=sib, device_id_type=MESH) for a in range(n)]
        for cp in cps:
            cp.start()
        buf[me] = small_in[...]
        peers = [(fx, fy, fc) for fx in (0, 1) for fy in (0, 1) for fc in (0, 1)][1:]
        sm = []
        for k, (fx, fy, fc) in enumerate(peers):
            dev = (1 - x if fx else x, 1 - y if fy else y, 1 - c if fc else c)
            cp = pltpu.make_async_remote_copy(
                src_ref=small_in, dst_ref=buf.at[me], send_sem=sm_s.at[k], recv_sem=sm_r.at[k],
                device_id=dev, device_id_type=MESH)
            cp.start()
            sm.append(cp)
        for cp in sm:
            cp.wait()
        tot = buf[0]
        for d in range(1, N_DEV):
            tot = tot + buf[d]
        small_out[...] = tot
        for cp in cps:
            cp.wait()

    return pl.pallas_call(
        body, name=name,
        in_specs=[HBM] * n + [VMEM_WHOLE],
        out_specs=[HBM] * n + [VMEM_WHOLE],
        out_shape=[jax.ShapeDtypeStruct((g.shape[0], g.shape[1] // 2, g.shape[2]), F32) for g in grads]
        + [jax.ShapeDtypeStruct(small.shape, F32)],
        scratch_shapes=[pltpu.VMEM((N_DEV,) + small.shape, F32),
                        pltpu.SemaphoreType.DMA((n,)), pltpu.SemaphoreType.DMA((n,)),
                        pltpu.SemaphoreType.DMA((N_DEV - 1,)), pltpu.SemaphoreType.DMA((N_DEV - 1,))],
    )(*grads, small)


def _chip_sum(g, recv, core, name):
    ns, R, C = g.shape
    r2 = R // 2
    tr = _rows(r2, 512)
    nrb = r2 // tr

    def body(core_ref, g_ref, r_ref, o_ref, ob_ref):
        s = g_ref[...] + r_ref[...]
        o_ref[...] = s
        ob_ref[...] = s.astype(BF16)

    out = pl.BlockSpec((None, tr, C), lambda s, i, cr: (s, i, 0))
    return pl.pallas_call(
        body, name=name,
        grid_spec=pltpu.PrefetchScalarGridSpec(
            num_scalar_prefetch=1, grid=(ns, nrb),
            in_specs=[pl.BlockSpec((None, tr, C), lambda s, i, cr: (s, cr[0] * nrb + i, 0)), out],
            out_specs=[out, out]),
        out_shape=[jax.ShapeDtypeStruct((ns, r2, C), F32), jax.ShapeDtypeStruct((ns, r2, C), BF16)],
        compiler_params=_cparams(),
    )(core, g, recv)


def _shard_sum(csum, got, place, name):
    _, r2, C = csum.shape
    tr = _rows(r2, 512)
    nrb = r2 // tr

    def body(place_ref, c_ref, g0_ref, g1_ref, g2_ref, o_ref):
        o_ref[...] = ((c_ref[...] + g0_ref[...].astype(F32)) + g1_ref[...].astype(F32)) + g2_ref[...].astype(F32)

    def got_spec(k):
        return pl.BlockSpec((None, tr, C), lambda i, pr: (k, i, 0))

    return pl.pallas_call(
        body, name=name,
        grid_spec=pltpu.PrefetchScalarGridSpec(
            num_scalar_prefetch=1, grid=(nrb,),
            in_specs=[pl.BlockSpec((None, tr, C), lambda i, pr: (pr[0], i, 0)), got_spec(0), got_spec(1), got_spec(2)],
            out_specs=pl.BlockSpec((tr, C), lambda i, pr: (pr[1] * nrb + i, 0))),
        out_shape=jax.ShapeDtypeStruct((2 * r2, C), F32),
        compiler_params=_cparams(),
    )(place, csum, got, got, got)


def _scatter_chip_sums(csums, name):
    n = len(csums)

    def body(*refs):
        ins, outs = refs[:n], refs[n:2 * n]
        ssem, rsem = refs[2 * n:]
        x, y, c, chips = _place()
        cps = []
        for a in range(n):
            for k, chip in enumerate(chips):
                jj = 2 * chip[0] + chip[1]
                cp = pltpu.make_async_remote_copy(
                    src_ref=ins[a].at[jj], dst_ref=outs[a].at[k], send_sem=ssem.at[a, k], recv_sem=rsem.at[a, k],
                    device_id=(*chip, c), device_id_type=MESH)
                cp.start()
                cps.append(cp)
        for cp in cps:
            cp.wait()

    return pl.pallas_call(
        body, name=name, in_specs=[HBM] * n, out_specs=[HBM] * n,
        out_shape=[jax.ShapeDtypeStruct((3,) + g.shape[1:], g.dtype) for g in csums],
        scratch_shapes=[pltpu.SemaphoreType.DMA((n, 3)), pltpu.SemaphoreType.DMA((n, 3))],
    )(*csums)


def _join_halves(fulls, name):
    n = len(fulls)

    def body(*refs):
        outs = refs[n:2 * n]
        ssem, rsem = refs[2 * n:]
        x, y, c, _ = _place()
        cps = [pltpu.make_async_remote_copy(
            src_ref=_half(outs[a], c, True), dst_ref=_half(outs[a], c, True), send_sem=ssem.at[a],
            recv_sem=rsem.at[a], device_id=(x, y, 1 - c), device_id_type=MESH) for a in range(n)]
        for cp in cps:
            cp.start()
        for cp in cps:
            cp.wait()

    return pl.pallas_call(
        body, name=name, in_specs=[HBM] * n, out_specs=[HBM] * n,
        out_shape=[jax.ShapeDtypeStruct(f.shape, F32) for f in fulls],
        input_output_aliases={a: a for a in range(n)},
        scratch_shapes=[pltpu.SemaphoreType.DMA((n,))] * 2,
    )(*fulls)


def _pad_rows(a, rows, cols):
    return jnp.pad(a, ((0, rows - a.shape[0]), (0, cols - a.shape[1])))


def kernel(x, p, ffn1_norm, ffn1_w_gate, ffn1_w_up, ffn1_w_down, mix_norm, w_in, conv_w, conv_b, q_norm, k_norm, w_out, ffn2_norm, ffn2_w_gate, ffn2_w_up, ffn2_w_down, ple_norm, ple_w_gate, ple_w_proj, loss_target, m_ffn1_norm, m_ffn1_w_gate, m_ffn1_w_up, m_ffn1_w_down, m_mix_norm, m_w_in, m_conv_w, m_conv_b, m_q_norm, m_k_norm, m_w_out, m_ffn2_norm, m_ffn2_w_gate, m_ffn2_w_up, m_ffn2_w_down, m_ple_norm, m_ple_w_gate, m_ple_w_proj, v_ffn1_norm, v_ffn1_w_gate, v_ffn1_w_up, v_ffn1_w_down, v_mix_norm, v_w_in, v_conv_w, v_conv_b, v_q_norm, v_k_norm, v_w_out, v_ffn2_norm, v_ffn2_w_gate, v_ffn2_w_up, v_ffn2_w_down, v_ple_norm, v_ple_w_gate, v_ple_w_proj):
    big = dict(ffn1_w_gate=ffn1_w_gate, ffn1_w_up=ffn1_w_up, ffn1_w_down=ffn1_w_down, w_in=w_in, w_out=w_out,
               ffn2_w_gate=ffn2_w_gate, ffn2_w_up=ffn2_w_up, ffn2_w_down=ffn2_w_down,
               ple_w_gate=ple_w_gate, ple_w_proj=ple_w_proj)
    big_m = dict(ffn1_w_gate=m_ffn1_w_gate, ffn1_w_up=m_ffn1_w_up, ffn1_w_down=m_ffn1_w_down, w_in=m_w_in,
                 w_out=m_w_out, ffn2_w_gate=m_ffn2_w_gate, ffn2_w_up=m_ffn2_w_up, ffn2_w_down=m_ffn2_w_down,
                 ple_w_gate=m_ple_w_gate, ple_w_proj=m_ple_w_proj)
    big_v = dict(ffn1_w_gate=v_ffn1_w_gate, ffn1_w_up=v_ffn1_w_up, ffn1_w_down=v_ffn1_w_down, w_in=v_w_in,
                 w_out=v_w_out, ffn2_w_gate=v_ffn2_w_gate, ffn2_w_up=v_ffn2_w_up, ffn2_w_down=v_ffn2_w_down,
                 ple_w_gate=v_ple_w_gate, ple_w_proj=v_ple_w_proj)
    names = list(big)
    xs = x[0]
    ps = p[0, 0]
    tgt = loss_target[0]
    S, D = xs.shape
    nconv = conv_b.shape[1]
    nsb = D - nconv
    cwl = conv_w.shape[2]
    jchip = 2 * lax.axis_index("x") + lax.axis_index("y")
    core = lax.axis_index("c")

    shards = [big[k][0].astype(BF16) for k in names]
    gathered = _gather_weights(shards, _pad_rows(conv_w[0], 8, LANES), "gather_weights")
    W = dict(zip(names, gathered[:-1]))
    cw_full = jnp.transpose(gathered[-1][:, :, :cwl], (1, 0, 2)).reshape(8, N_SHARDS * cwl)
    wout_full = W["w_out"].reshape(-1, D)
    wpg_full = W["ple_w_gate"].reshape(-1, D)
    qg = jnp.tile(q_norm, (1, CONV_COLS // HEAD_DIM))
    kg = jnp.tile(k_norm, (1, CONV_COLS // HEAD_DIM))
    n_units = nconv // CONV_COLS

    x1, h1, a1, b1 = _ffn_fwd(xs, ffn1_norm, W["ffn1_w_gate"], W["ffn1_w_up"], W["ffn1_w_down"], "ffn1_fwd")
    proj, h2 = _norm_proj(x1, mix_norm, W["w_in"], "mix_in_proj")
    y_conv = _conv_fwd(proj, cw_full, conv_b, nconv, "conv_fwd")
    qs = _qk_norm(proj, qg, 3 * n_units, nsb, HEAD_DIM ** -0.5, "q_norm_fwd")
    kh = _qk_norm(proj, kg, 4 * n_units, nsb, 1.0, "k_norm_fwd")
    vb = proj[:, 3 * nconv + 2 * nsb:].astype(BF16)
    y_sb, carry = _attn_fwd(qs, kh, vb, "attn_fwd")
    ycat = jnp.concatenate([y_conv, y_sb], axis=1)
    x2 = _out_proj(ycat, wout_full, x1, "mix_out_proj")
    x3, h3, a3, b3 = _ffn_fwd(x2, ffn2_norm, W["ffn2_w_gate"], W["ffn2_w_up"], W["ffn2_w_down"], "ffn2_fwd")

    dx3, h4, du4, dpp, loss_blk, dg_ple = _ple(x3, ps, tgt, ple_norm, wpg_full, W["ple_w_proj"], "ple_loss")
    G = {}
    tk = _rows(S, 2048)
    nk = S // tk
    kd = wpg_full.shape[0] // N_SHARDS
    G["ple_w_gate"] = _tn_matmul(
        h4, du4, pl.BlockSpec((tk, kd), lambda m, k: (k, m)), pl.BlockSpec((tk, D), lambda m, k: (k, 0)),
        (N_SHARDS, kd, D), pl.BlockSpec((None, kd, D), lambda m, k: (m, 0, 0)), (N_SHARDS, nk), "ple_w_gate_grad")
    P = ps.shape[1]
    npp = D // N_SHARDS
    G["ple_w_proj"] = _tn_matmul(
        ps, dpp, pl.BlockSpec((tk, P), lambda m, k: (k, 0)), pl.BlockSpec((tk, npp), lambda m, k: (k, m)),
        (N_SHARDS, P, npp), pl.BlockSpec((None, P, npp), lambda m, k: (m, 0, 0)), (N_SHARDS, nk), "ple_w_proj_grad")

    def ffn_grads(pre, h, s, da, db, dy):
        fs = s.shape[2]
        hs = pl.BlockSpec((tk, D), lambda m, k: (k, 0))
        ss = pl.BlockSpec((None, tk, fs), lambda m, k: (m, k, 0))
        G[pre + "_w_gate"] = _tn_matmul(h, da, hs, ss, (N_SHARDS, D, fs),
                                        pl.BlockSpec((None, D, fs), lambda m, k: (m, 0, 0)), (N_SHARDS, nk), pre + "_w_gate_grad")
        G[pre + "_w_up"] = _tn_matmul(h, db, hs, ss, (N_SHARDS, D, fs),
                                      pl.BlockSpec((None, D, fs), lambda m, k: (m, 0, 0)), (N_SHARDS, nk), pre + "_w_up_grad")
        G[pre + "_w_down"] = _tn_matmul(s, dy, ss, hs, (N_SHARDS, fs, D),
                                        pl.BlockSpec((None, fs, D), lambda m, k: (m, 0, 0)), (N_SHARDS, nk), pre + "_w_down_grad")

    dx2, s3, da3, db3, dy3, dg_ffn2 = _ffn_bwd(dx3, x2, ffn2_norm, a3, b3, W["ffn2_w_gate"], W["ffn2_w_up"], W["ffn2_w_down"], "ffn2_bwd")
    ffn_grads("ffn2", h3, s3, da3, db3, dy3)

    dycat = _out_proj_bwd(dx2, wout_full, "mix_out_proj_bwd")
    ko = wout_full.shape[0] // N_SHARDS
    G["w_out"] = _tn_matmul(
        ycat, dx2, pl.BlockSpec((tk, ko), lambda m, k: (k, m)), pl.BlockSpec((tk, D), lambda m, k: (k, 0)),
        (N_SHARDS, ko, D), pl.BlockSpec((None, ko, D), lambda m, k: (m, 0, 0)), (N_SHARDS, nk), "w_out_grad")
    db_, dc_, du_, dwb = _conv_bwd(proj, dycat, cw_full, conv_b, nconv, "conv_bwd")
    dqs, dkh, dv = _attn_bwd(qs, kh, vb, dycat, nconv // LANES, carry, "attn_bwd")
    dq, dg_q = _qk_norm_bwd(proj, dqs, qg, 3 * n_units, nsb, HEAD_DIM ** -0.5, "q_norm_bwd")
    dk, dg_k = _qk_norm_bwd(proj, dkh, kg, 4 * n_units, nsb, 1.0, "k_norm_bwd")
    dproj = jnp.concatenate([db_, dc_, du_, dq, dk, dv.astype(BF16)], axis=1)
    nin = W["w_in"].shape[2]
    G["w_in"] = _tn_matmul(
        h2, dproj, pl.BlockSpec((tk, D), lambda m, k: (k, 0)), pl.BlockSpec((tk, nin), lambda m, k: (k, m)),
        (N_SHARDS, D, nin), pl.BlockSpec((None, D, nin), lambda m, k: (m, 0, 0)), (N_SHARDS, nk), "w_in_grad")
    dx1, dg_mix = _norm_proj_bwd(dproj, W["w_in"], dx2, x1, mix_norm, "mix_in_proj_bwd")

    dx0, s1, da1, db1, dy1, dg_ffn1 = _ffn_bwd(dx1, xs, ffn1_norm, a1, b1, W["ffn1_w_gate"], W["ffn1_w_up"], W["ffn1_w_down"], "ffn1_bwd")
    ffn_grads("ffn1", h1, s1, da1, db1, dy1)

    assert D >= nconv and D % LANES == 0
    fold = lambda t: t[0].reshape(-1, HEAD_DIM).sum(axis=0)[None, :]
    small_rows = [dg_ffn1[0:1], dg_mix[0:1], dg_ffn2[0:1], dg_ple[0:1],
                  _pad_rows(dwb[3:4], 1, D), _pad_rows(dwb[0:3], 3, D),
                  _pad_rows(fold(dg_q), 1, D), _pad_rows(fold(dg_k), 1, D), _pad_rows(loss_blk[0:1, 0:1], 1, D)]
    small = _pad_rows(jnp.concatenate(small_rows, axis=0), SMALL_ROWS, D)

    grads = [G[k] for k in names]
    swapped = _swap_halves(grads, small, "grad_swap_halves")
    small_sum = swapped[-1]
    core_arr = jnp.reshape(core, (1,)).astype(jnp.int32)
    place = jnp.stack([jchip, core]).astype(jnp.int32)
    csums, csums_bf = zip(*[_chip_sum(g, r, core_arr, f"chip_sum_{names[a]}")
                            for a, (g, r) in enumerate(zip(grads, swapped[:-1]))])
    got = _scatter_chip_sums(list(csums_bf), "grad_scatter")
    full = _join_halves([_shard_sum(cs, gt, place, f"shard_sum_{names[a]}")
                         for a, (cs, gt) in enumerate(zip(csums, got))], "grad_join_halves")

    out_g, out_d, out_m, out_v = {}, {}, {}, {}
    for a, k in enumerate(names):
        shp = big[k].shape
        g2 = full[a]
        d_, m_, v_ = _elementwise(_adamw, [big[k][0].reshape(g2.shape), g2, big_m[k][0].reshape(g2.shape),
                                           big_v[k][0].reshape(g2.shape)], 3, f"adamw_{k}")
        out_g[k], out_d[k], out_m[k], out_v[k] = (t.reshape(shp) for t in (g2, d_, m_, v_))

    sm_names = ["ffn1_norm", "mix_norm", "ffn2_norm", "ple_norm", "conv_b", "conv_w", "q_norm", "k_norm"]
    sm_w = dict(ffn1_norm=ffn1_norm, mix_norm=mix_norm, ffn2_norm=ffn2_norm, ple_norm=ple_norm, conv_b=conv_b,
                conv_w=conv_w[0], q_norm=q_norm, k_norm=k_norm)
    sm_m = dict(ffn1_norm=m_ffn1_norm, mix_norm=m_mix_norm, ffn2_norm=m_ffn2_norm, ple_norm=m_ple_norm,
                conv_b=m_conv_b, conv_w=m_conv_w[0], q_norm=m_q_norm, k_norm=m_k_norm)
    sm_v = dict(ffn1_norm=v_ffn1_norm, mix_norm=v_mix_norm, ffn2_norm=v_ffn2_norm, ple_norm=v_ple_norm,
                conv_b=v_conv_b, conv_w=v_conv_w[0], q_norm=v_q_norm, k_norm=v_k_norm)
    sm_g = dict(ffn1_norm=small_sum[0:1], mix_norm=small_sum[1:2], ffn2_norm=small_sum[2:3], ple_norm=small_sum[3:4],
                conv_b=small_sum[4:5, :nconv],
                conv_w=lax.dynamic_slice_in_dim(small_sum[5:8, :nconv], jchip * cwl, cwl, axis=1),
                q_norm=small_sum[8:9, :HEAD_DIM], k_norm=small_sum[9:10, :HEAD_DIM])
    loss = small_sum[10, 0]
    pack = lambda d: _pad_rows(jnp.concatenate([_pad_rows(d[k], d[k].shape[0], D) for k in sm_names], axis=0), SMALL_ROWS, D)
    sd, smm, svv = _elementwise(_adamw, [pack(sm_w), pack(sm_g), pack(sm_m), pack(sm_v)], 3, "adamw_small")
    row = 0
    for k in sm_names:
        r_, c_ = sm_w[k].shape
        shp = (1, r_, c_) if k == "conv_w" else (r_, c_)
        out_g[k] = sm_g[k].reshape(shp)
        out_d[k], out_m[k], out_v[k] = (t[row:row + r_, :c_].reshape(shp) for t in (sd, smm, svv))
        row += r_

    order = ["ffn1_norm", "ffn1_w_gate", "ffn1_w_up", "ffn1_w_down", "mix_norm", "w_in", "conv_w", "conv_b",
             "q_norm", "k_norm", "w_out", "ffn2_norm", "ffn2_w_gate", "ffn2_w_up", "ffn2_w_down", "ple_norm",
             "ple_w_gate", "ple_w_proj"]
    return (loss, dx0[None], *[out_g[k] for k in order], *[out_d[k] for k in order],
            *[out_m[k] for k in order], *[out_v[k] for k in order])
```

```python
import jax
import jax.numpy as jnp
from jax import lax
from jax.experimental import pallas as pl
from jax.experimental.pallas import tpu as pltpu

F32 = jnp.float32
BF16 = jnp.bfloat16
MESH = pl.DeviceIdType.MESH

EPS = 1e-6
HEAD_DIM = 64
LANES = 128
FFN_RES = 0.5
ADAM_LR = 0.001
ADAM_B1 = 0.9
ADAM_B2 = 0.999
ADAM_EPS = 1e-08
ADAM_WD = 0.01
ADAM_STEP = 10
N_SHARDS = 4
N_DEV = 8
ATT_TILE = 256
VMEM_LIMIT = 52 * 1024 * 1024
SMALL_ROWS = 16
HBM = pl.BlockSpec(memory_space=pltpu.HBM)
VMEM_WHOLE = pl.BlockSpec(memory_space=pltpu.VMEM)


def _cparams(**kw):
    return pltpu.CompilerParams(vmem_limit_bytes=VMEM_LIMIT, **kw)


def _dot(a, b):
    return jnp.dot(a, b, preferred_element_type=F32)


def _dot_nt(a, b):
    return lax.dot_general(a, b, (((1,), (1,)), ((), ())), preferred_element_type=F32)


def _dot_tn(a, b):
    return lax.dot_general(a, b, (((0,), (0,)), ((), ())), preferred_element_type=F32)


def _split_dot(x, m):
    hi = x.astype(BF16)
    lo = (x - hi.astype(F32)).astype(BF16)
    return _dot(hi, m) + _dot(lo, m)


def _rms(x, g):
    r = lax.rsqrt(jnp.mean(x * x, axis=-1, keepdims=True) + EPS)
    xn = x * r
    return xn * g, xn, r


def _rms_bwd(dh, xn, r, g):
    dxn = dh * g
    dx = r * (dxn - xn * jnp.mean(dxn * xn, axis=-1, keepdims=True))
    return dx, jnp.sum(dh * xn, axis=0, keepdims=True)


def _rows(n, cap=512):
    for t in (2048, 1024, 512, 448, 384, 352, 256, 192, 176, 128, 96, 88, 64, 48, 32, 16, 8):
        if t <= cap and n % t == 0:
            return t
    raise ValueError(f"no row tile for {n}")


def _ffn_fwd(x, gain, wg, wu, wd, name):
    S, D = x.shape
    ns, _, fs = wg.shape
    tm = _rows(S, 512)

    def body(x_ref, g_ref, wg_ref, wu_ref, wd_ref, xo_ref, h_ref, a_ref, b_ref, hs, acc):
        j = pl.program_id(1)

        @pl.when(j == 0)
        def _():
            h, _, _ = _rms(x_ref[...], g_ref[...])
            hb = h.astype(BF16)
            hs[...] = hb
            h_ref[...] = hb
            acc[...] = jnp.zeros_like(acc)

        hb = hs[...]
        a = _dot(hb, wg_ref[...])
        b = _dot(hb, wu_ref[...])
        a_ref[...] = a
        b_ref[...] = b
        s = (a * jax.nn.sigmoid(a)) * b
        acc[...] += _dot(s.astype(BF16), wd_ref[...])

        @pl.when(j == ns - 1)
        def _():
            xo_ref[...] = x_ref[...] + FFN_RES * acc[...]

    return pl.pallas_call(
        body, name=name, grid=(S // tm, ns),
        in_specs=[
            pl.BlockSpec((tm, D), lambda i, j: (i, 0)),
            pl.BlockSpec((1, D), lambda i, j: (0, 0)),
            pl.BlockSpec((None, D, fs), lambda i, j: (j, 0, 0)),
            pl.BlockSpec((None, D, fs), lambda i, j: (j, 0, 0)),
            pl.BlockSpec((None, fs, D), lambda i, j: (j, 0, 0)),
        ],
        out_specs=[
            pl.BlockSpec((tm, D), lambda i, j: (i, 0)),
            pl.BlockSpec((tm, D), lambda i, j: (i, 0)),
            pl.BlockSpec((None, tm, fs), lambda i, j: (j, i, 0)),
            pl.BlockSpec((None, tm, fs), lambda i, j: (j, i, 0)),
        ],
        out_shape=[
            jax.ShapeDtypeStruct((S, D), F32),
            jax.ShapeDtypeStruct((S, D), BF16),
            jax.ShapeDtypeStruct((ns, S, fs), F32),
            jax.ShapeDtypeStruct((ns, S, fs), F32),
        ],
        scratch_shapes=[pltpu.VMEM((tm, D), BF16), pltpu.VMEM((tm, D), F32)],
        compiler_params=_cparams(),
    )(x, gain, wg, wu, wd)


def _ffn_bwd_act(dxo, a, b, wd, name):
    S, D = dxo.shape
    ns, fs, _ = wd.shape
    tm = _rows(S, 512)

    def body(dxo_ref, a_ref, b_ref, wd_ref, s_ref, da_ref, db_ref, dy_ref, dys):
        @pl.when(pl.program_id(1) == 0)
        def _():
            dy = (FFN_RES * dxo_ref[...]).astype(BF16)
            dys[...] = dy
            dy_ref[...] = dy

        av = a_ref[...]
        bv = b_ref[...]
        ds = _dot_nt(dys[...], wd_ref[...])
        sig = jax.nn.sigmoid(av)
        sl = av * sig
        s_ref[...] = (sl * bv).astype(BF16)
        da_ref[...] = (ds * bv * (sig * (1.0 + av * (1.0 - sig)))).astype(BF16)
        db_ref[...] = (ds * sl).astype(BF16)

    act = pl.BlockSpec((None, tm, fs), lambda i, j: (j, i, 0))
    row = pl.BlockSpec((tm, D), lambda i, j: (i, 0))
    return pl.pallas_call(
        body, name=name, grid=(S // tm, ns),
        in_specs=[row, act, act, pl.BlockSpec((None, fs, D), lambda i, j: (j, 0, 0))],
        out_specs=[act, act, act, row],
        out_shape=[jax.ShapeDtypeStruct((ns, S, fs), BF16)] * 3 + [jax.ShapeDtypeStruct((S, D), BF16)],
        scratch_shapes=[pltpu.VMEM((tm, D), BF16)],
        compiler_params=_cparams(),
    )(dxo, a, b, wd)


def _ffn_bwd_in(da, db, wg, wu, dres, x, gain, name):
    S, D = x.shape
    ns, _, fs = wg.shape
    tm = _rows(S, 512)

    def body(da_ref, db_ref, wg_ref, wu_ref, dres_ref, x_ref, g_ref, dx_ref, dg_ref, acc):
        i = pl.program_id(0)
        j = pl.program_id(1)

        @pl.when((i == 0) & (j == 0))
        def _():
            dg_ref[...] = jnp.zeros_like(dg_ref)

        @pl.when(j == 0)
        def _():
            acc[...] = jnp.zeros_like(acc)

        acc[...] += _dot_nt(da_ref[...], wg_ref[...]) + _dot_nt(db_ref[...], wu_ref[...])

        @pl.when(j == ns - 1)
        def _():
            g = g_ref[...]
            _, xn, r = _rms(x_ref[...], g)
            dx, dg = _rms_bwd(acc[...], xn, r, g)
            dx_ref[...] = dres_ref[...] + dx
            dg_ref[...] += jnp.broadcast_to(dg, dg_ref.shape)

    act = pl.BlockSpec((None, tm, fs), lambda i, j: (j, i, 0))
    row = pl.BlockSpec((tm, D), lambda i, j: (i, 0))
    wsp = pl.BlockSpec((None, D, fs), lambda i, j: (j, 0, 0))
    return pl.pallas_call(
        body, name=name, grid=(S // tm, ns),
        in_specs=[act, act, wsp, wsp, row, row, pl.BlockSpec((1, D), lambda i, j: (0, 0))],
        out_specs=[row, pl.BlockSpec((8, D), lambda i, j: (0, 0))],
        out_shape=[jax.ShapeDtypeStruct((S, D), F32), jax.ShapeDtypeStruct((8, D), F32)],
        scratch_shapes=[pltpu.VMEM((tm, D), F32)],
        compiler_params=_cparams(),
    )(da, db, wg, wu, dres, x, gain)


def _tn_matmul(a, b, a_spec, b_spec, o_shape, o_spec, grid, name):
    kaxis = len(grid) - 1

    def body(a_ref, b_ref, o_ref):
        @pl.when(pl.program_id(kaxis) == 0)
        def _():
            o_ref[...] = jnp.zeros_like(o_ref)

        o_ref[...] += _dot_tn(a_ref[...].astype(BF16), b_ref[...].astype(BF16))

    return pl.pallas_call(
        body, name=name, grid=grid, in_specs=[a_spec, b_spec], out_specs=o_spec,
        out_shape=jax.ShapeDtypeStruct(o_shape, F32), compiler_params=_cparams(),
    )(a, b)


def _norm_proj(x, gain, w, name):
    S, D = x.shape
    ns, _, n = w.shape
    tm = _rows(S, 512)

    def body(x_ref, g_ref, w_ref, o_ref, h_ref, hs):
        @pl.when(pl.program_id(1) == 0)
        def _():
            h, _, _ = _rms(x_ref[...], g_ref[...])
            hb = h.astype(BF16)
            hs[...] = hb
            h_ref[...] = hb

        o_ref[...] = _dot(hs[...], w_ref[...])

    return pl.pallas_call(
        body, name=name, grid=(S // tm, ns),
        in_specs=[
            pl.BlockSpec((tm, D), lambda i, j: (i, 0)),
            pl.BlockSpec((1, D), lambda i, j: (0, 0)),
            pl.BlockSpec((None, D, n), lambda i, j: (j, 0, 0)),
        ],
        out_specs=[
            pl.BlockSpec((tm, n), lambda i, j: (i, j)),
            pl.BlockSpec((tm, D), lambda i, j: (i, 0)),
        ],
        out_shape=[jax.ShapeDtypeStruct((S, ns * n), F32), jax.ShapeDtypeStruct((S, D), BF16)],
        scratch_shapes=[pltpu.VMEM((tm, D), BF16)],
        compiler_params=_cparams(),
    )(x, gain, w)


def _norm_proj_bwd(dproj, w, dres, x, gain, name):
    S, D = x.shape
    ns, _, n = w.shape
    tm = _rows(S, 512)

    def body(dp_ref, w_ref, dres_ref, x_ref, g_ref, dx_ref, dg_ref, acc):
        i = pl.program_id(0)
        j = pl.program_id(1)

        @pl.when((i == 0) & (j == 0))
        def _():
            dg_ref[...] = jnp.zeros_like(dg_ref)

        @pl.when(j == 0)
        def _():
            acc[...] = jnp.zeros_like(acc)

        acc[...] += _dot_nt(dp_ref[...], w_ref[...])

        @pl.when(j == ns - 1)
        def _():
            g = g_ref[...]
            _, xn, r = _rms(x_ref[...], g)
            dx, dg = _rms_bwd(acc[...], xn, r, g)
            dx_ref[...] = dres_ref[...] + dx
            dg_ref[...] += jnp.broadcast_to(dg, dg_ref.shape)

    return pl.pallas_call(
        body, name=name, grid=(S // tm, ns),
        in_specs=[
            pl.BlockSpec((tm, n), lambda i, j: (i, j)),
            pl.BlockSpec((None, D, n), lambda i, j: (j, 0, 0)),
            pl.BlockSpec((tm, D), lambda i, j: (i, 0)),
            pl.BlockSpec((tm, D), lambda i, j: (i, 0)),
            pl.BlockSpec((1, D), lambda i, j: (0, 0)),
        ],
        out_specs=[
            pl.BlockSpec((tm, D), lambda i, j: (i, 0)),
            pl.BlockSpec((8, D), lambda i, j: (0, 0)),
        ],
        out_shape=[jax.ShapeDtypeStruct((S, D), F32), jax.ShapeDtypeStruct((8, D), F32)],
        scratch_shapes=[pltpu.VMEM((tm, D), F32)],
        compiler_params=_cparams(),
    )(dproj, w, dres, x, gain)


def _out_proj(ycat, w, res, name):
    S, K = ycat.shape
    D = w.shape[1]
    tm = _rows(S, 512)

    def body(y_ref, w_ref, r_ref, o_ref):
        o_ref[...] = r_ref[...] + _dot(y_ref[...], w_ref[...])

    return pl.pallas_call(
        body, name=name, grid=(S // tm,),
        in_specs=[
            pl.BlockSpec((tm, K), lambda i: (i, 0)),
            pl.BlockSpec((K, D), lambda i: (0, 0)),
            pl.BlockSpec((tm, D), lambda i: (i, 0)),
        ],
        out_specs=pl.BlockSpec((tm, D), lambda i: (i, 0)),
        out_shape=jax.ShapeDtypeStruct((S, D), F32),
        compiler_params=_cparams(),
    )(ycat, w, res)


def _out_proj_bwd(dx, w, name):
    S, D = dx.shape
    K = w.shape[0]
    tm = _rows(S, 512)

    def body(d_ref, w_ref, o_ref):
        o_ref[...] = _dot_nt(d_ref[...].astype(BF16), w_ref[...])

    return pl.pallas_call(
        body, name=name, grid=(S // tm,),
        in_specs=[pl.BlockSpec((tm, D), lambda i: (i, 0)), pl.BlockSpec((K, D), lambda i: (0, 0))],
        out_specs=pl.BlockSpec((tm, K), lambda i: (i, 0)),
        out_shape=jax.ShapeDtypeStruct((S, K), F32),
        compiler_params=_cparams(),
    )(dx, w)


CONV_COLS = 256


def _shift_down(z, halo, k, row):
    out = pltpu.roll(z, k, 0)
    for n in range(k):
        out = jnp.where(row == n, halo[8 - k + n:8 - k + n + 1, :], out)
    return out


def _shift_up(g, halo, k, row, ts):
    out = pltpu.roll(g, ts - k, 0)
    for n in range(k):
        out = jnp.where(row == ts - k + n, halo[n:n + 1, :], out)
    return out


def _conv_fwd(proj, cw, cb, nconv, name):
    S = proj.shape[0]
    ncb = nconv // CONV_COLS
    ts = _rows(S, 512)
    hb = ts // 8

    def body(b_ref, c_ref, u_ref, ch_ref, uh_ref, w_ref, bias_ref, o_ref):
        i = pl.program_id(1)
        z = c_ref[...] * u_ref[...]
        halo = jnp.where(i > 0, ch_ref[...] * uh_ref[...], 0.0)
        row = lax.broadcasted_iota(jnp.int32, z.shape, 0)
        w = w_ref[...]
        yc = w[0:1, :] * _shift_down(z, halo, 2, row) + w[1:2, :] * _shift_down(z, halo, 1, row) + w[2:3, :] * z
        o_ref[...] = (b_ref[...] * (yc + bias_ref[...])).astype(BF16)

    def blk(unit):
        return pl.BlockSpec((ts, CONV_COLS), lambda cbi, i: (i, unit * ncb + cbi))

    def prev(unit):
        return pl.BlockSpec((8, CONV_COLS), lambda cbi, i: (jnp.maximum(i * hb - 1, 0), unit * ncb + cbi))

    return pl.pallas_call(
        body, name=name, grid=(ncb, S // ts),
        in_specs=[blk(0), blk(1), blk(2), prev(1), prev(2),
                  pl.BlockSpec((8, CONV_COLS), lambda cbi, i: (0, cbi)),
                  pl.BlockSpec((1, CONV_COLS), lambda cbi, i: (0, cbi))],
        out_specs=pl.BlockSpec((ts, CONV_COLS), lambda cbi, i: (i, cbi)),
        out_shape=jax.ShapeDtypeStruct((S, nconv), BF16),
        compiler_params=_cparams(),
    )(proj, proj, proj, proj, proj, cw, cb)


def _conv_bwd(proj, dy, cw, cb, nconv, name):
    S = proj.shape[0]
    ncb = nconv // CONV_COLS
    ts = _rows(S, 512)
    hb = ts // 8
    nblk = S // ts

    def body(b_ref, c_ref, u_ref, dy_ref, ch_ref, uh_ref, bn_ref, dyn_ref, w_ref, bias_ref,
             db_ref, dc_ref, du_ref, dw_ref):
        i = pl.program_id(1)

        @pl.when(i == 0)
        def _():
            dw_ref[...] = jnp.zeros_like(dw_ref)

        c = c_ref[...]
        u = u_ref[...]
        bg = b_ref[...]
        dy_ = dy_ref[...]
        z = c * u
        halo = jnp.where(i > 0, ch_ref[...] * uh_ref[...], 0.0)
        row = lax.broadcasted_iota(jnp.int32, z.shape, 0)
        w = w_ref[...]
        z2 = _shift_down(z, halo, 2, row)
        z1 = _shift_down(z, halo, 1, row)
        yc = w[0:1, :] * z2 + w[1:2, :] * z1 + w[2:3, :] * z
        db_ref[...] = (dy_ * (yc + bias_ref[...])).astype(BF16)
        g = dy_ * bg
        gnext = jnp.where(i < nblk - 1, dyn_ref[...] * bn_ref[...], 0.0)
        dz = w[2:3, :] * g + w[1:2, :] * _shift_up(g, gnext, 1, row, ts) + w[0:1, :] * _shift_up(g, gnext, 2, row, ts)
        dc_ref[...] = (dz * u).astype(BF16)
        du_ref[...] = (dz * c).astype(BF16)
        r8 = lax.broadcasted_iota(jnp.int32, (8, CONV_COLS), 0)
        sums = [jnp.sum(g * z2, axis=0, keepdims=True), jnp.sum(g * z1, axis=0, keepdims=True),
                jnp.sum(g * z, axis=0, keepdims=True), jnp.sum(g, axis=0, keepdims=True)]
        upd = jnp.zeros((8, CONV_COLS), F32)
        for n, sv in enumerate(sums):
            upd = jnp.where(r8 == n, sv, upd)
        dw_ref[...] += upd

    def blk(unit):
        return pl.BlockSpec((ts, CONV_COLS), lambda cbi, i: (i, unit * ncb + cbi))

    def prev(unit):
        return pl.BlockSpec((8, CONV_COLS), lambda cbi, i: (jnp.maximum(i * hb - 1, 0), unit * ncb + cbi))

    def nxt(unit):
        return pl.BlockSpec((8, CONV_COLS), lambda cbi, i: (jnp.minimum((i + 1) * hb, S // 8 - 1), unit * ncb + cbi))

    o = pl.BlockSpec((ts, CONV_COLS), lambda cbi, i: (i, cbi))
    return pl.pallas_call(
        body, name=name, grid=(ncb, nblk),
        in_specs=[blk(0), blk(1), blk(2), blk(0), prev(1), prev(2), nxt(0), nxt(0),
                  pl.BlockSpec((8, CONV_COLS), lambda cbi, i: (0, cbi)),
                  pl.BlockSpec((1, CONV_COLS), lambda cbi, i: (0, cbi))],
        out_specs=[o, o, o, pl.BlockSpec((8, CONV_COLS), lambda cbi, i: (0, cbi))],
        out_shape=[jax.ShapeDtypeStruct((S, nconv), BF16)] * 3 + [jax.ShapeDtypeStruct((8, nconv), F32)],
        compiler_params=_cparams(),
    )(proj, proj, proj, dy, proj, proj, proj, dy, cw, cb)


def _group_ones(n):
    r = lax.broadcasted_iota(jnp.int32, (n, n), 0) // HEAD_DIM
    c = lax.broadcasted_iota(jnp.int32, (n, n), 1) // HEAD_DIM
    return jnp.where(r == c, 1.0, 0.0).astype(BF16)


def _qk_norm(proj, gain_t, unit0, nsb, scale, name):
    S = proj.shape[0]
    nb = nsb // CONV_COLS
    ts = _rows(S, 512)

    def body(x_ref, g_ref, o_ref):
        x = x_ref[...]
        ss = _split_dot(x * x, _group_ones(CONV_COLS))
        r = lax.rsqrt(ss * (1.0 / HEAD_DIM) + EPS)
        o_ref[...] = ((x * r) * g_ref[...] * scale).astype(BF16)

    return pl.pallas_call(
        body, name=name, grid=(nb, S // ts),
        in_specs=[pl.BlockSpec((ts, CONV_COLS), lambda u, i: (i, unit0 + u)),
                  pl.BlockSpec((1, CONV_COLS), lambda u, i: (0, 0))],
        out_specs=pl.BlockSpec((ts, CONV_COLS), lambda u, i: (i, u)),
        out_shape=jax.ShapeDtypeStruct((S, nsb), BF16),
        compiler_params=_cparams(),
    )(proj, gain_t)


def _qk_norm_bwd(proj, dout, gain_t, unit0, nsb, scale, name):
    S = proj.shape[0]
    nb = nsb // CONV_COLS
    ts = _rows(S, 512)

    def body(x_ref, d_ref, g_ref, dx_ref, dg_ref):
        @pl.when(pl.program_id(1) == 0)
        def _():
            dg_ref[...] = jnp.zeros_like(dg_ref)

        x = x_ref[...]
        g = g_ref[...]
        ones = _group_ones(CONV_COLS)
        ss = _split_dot(x * x, ones)
        r = lax.rsqrt(ss * (1.0 / HEAD_DIM) + EPS)
        xn = x * r
        dh = d_ref[...] * scale
        dxn = dh * g
        m = _split_dot(dxn * xn, ones) * (1.0 / HEAD_DIM)
        dx_ref[...] = (r * (dxn - xn * m)).astype(BF16)
        dg_ref[...] += jnp.broadcast_to(jnp.sum(dh * xn, axis=0, keepdims=True), dg_ref.shape)

    return pl.pallas_call(
        body, name=name, grid=(nb, S // ts),
        in_specs=[pl.BlockSpec((ts, CONV_COLS), lambda u, i: (i, unit0 + u)),
                  pl.BlockSpec((ts, CONV_COLS), lambda u, i: (i, u)),
                  pl.BlockSpec((1, CONV_COLS), lambda u, i: (0, 0))],
        out_specs=[pl.BlockSpec((ts, CONV_COLS), lambda u, i: (i, u)),
                   pl.BlockSpec((8, CONV_COLS), lambda u, i: (0, u))],
        out_shape=[jax.ShapeDtypeStruct((S, nsb), BF16), jax.ShapeDtypeStruct((8, nsb), F32)],
        compiler_params=_cparams(),
    )(proj, dout, gain_t)


Z_CLAMP = 80.0
N_SLOTS = 3
SAT_LIMIT = 120.0


def _head_masks():
    lane = lax.broadcasted_iota(jnp.int32, (1, LANES), 1)
    return [lane < HEAD_DIM, lane >= HEAD_DIM], lane


def _tile_consts(T):
    r_i = lax.broadcasted_iota(jnp.int32, (T, T), 0)
    c_i = lax.broadcasted_iota(jnp.int32, (T, T), 1)
    neg_suffix = jnp.where(r_i >= c_i, -1.0, 0.0).astype(BF16)
    prefix = jnp.where(r_i <= c_i, 1.0, 0.0).astype(BF16)
    return neg_suffix, prefix, c_i < r_i


def _pipeline(n, stages, first_special=False, last_special=False, saturated=None, extra_head=0):
    depth = len(stages)
    head = (depth if first_special else depth - 1) + extra_head
    off = 0 if last_special else 1
    for m in range(head):
        for k in reversed(range(min(m, depth - 1) + 1)):
            stages[k](m - k, (m - k) % N_SLOTS, first_special and m == k)

    def trip(m, u):
        for k in reversed(range(depth)):
            stages[k](m - k, (head + u - k) % N_SLOTS, False)

    def group(g, carry):
        for u in range(N_SLOTS):
            trip(head + g * N_SLOTS + u, u)
        return carry

    count = n - 1 + off - head
    full = count // N_SLOTS
    if saturated is None:
        lax.fori_loop(0, full, group, 0)
        go_on, done = True, n
    else:
        def more(state):
            return (state[0] < full) & (state[1] == 0)

        def step(state):
            group(state[0], 0)
            return state[0] + 1, saturated().astype(jnp.int32)

        groups, stop = lax.while_loop(more, step, (jnp.int32(0), saturated().astype(jnp.int32)))
        go_on = stop == 0
        done = jnp.where(go_on, n, head - depth + 1 + N_SLOTS * groups)
    for r in range(N_SLOTS):
        @pl.when((count - full * N_SLOTS == r) & go_on)
        def _(r=r):
            for u in range(r):
                trip(head + full * N_SLOTS + u, u)
            for e in range(depth - off):
                for k in reversed(range(e + off, depth)):
                    t = n - 1 - (k - e - off)
                    stages[k](t, (head + r + e - k) % N_SLOTS, last_special and k == e + off)
    return done


def _sweep(n, stages, finish, first_special=False, last_special=False, saturated=None, extra_head=0):
    depth = len(stages)
    least = (depth if first_special else depth - 1) + extra_head + (1 if last_special else 0)
    for short in range(1, least):
        @pl.when(n == short)
        def _(short=short):
            for m in range(short + depth - 1):
                for k in reversed(range(depth)):
                    t = m - k
                    if 0 <= t < short:
                        stages[k](t, t % N_SLOTS, (first_special and t == 0) or (last_special and t == short - 1))
            finish(short)

    @pl.when(n >= least)
    def _():
        finish(_pipeline(n, stages, first_special, last_special, saturated, extra_head))


def _attn_fwd(q, k, v, name):
    S, nsb = q.shape
    T = ATT_TILE
    hp = nsb // LANES
    nb = S // T
    assert nb <= HEAD_DIM

    def body(q_ref, k_ref, v_ref, y_ref, cs_ref, c_ref, acc, z_st, inc_st):
        i = pl.program_id(1)
        masks, lane = _head_masks()
        qv = q_ref[...]
        qm = [jnp.where(m, qv, jnp.zeros_like(qv)) for m in masks]
        neg_suffix, _, causal = _tile_consts(T)
        c_ref[...] = jnp.zeros_like(c_ref)
        acc[...] = jnp.zeros_like(acc)
        cs_ref[...] = jnp.zeros_like(cs_ref)

        def blk(ref, j):
            return ref[pl.ds(pl.multiple_of(j * T, T), T), :]

        def scores(t, slot, diag):
            kj = blk(k_ref, i - t)
            for h in range(2):
                z_st[slot, h] = jnp.minimum(_dot_nt(qm[h], kj), Z_CLAMP)

        def suffix_sums(t, slot, diag):
            for h in range(2):
                sp = jnp.log(1.0 + jnp.exp(z_st[slot, h]))
                if diag:
                    sp = jnp.where(causal, sp, 0.0)
                inc_st[slot, h] = _dot(sp.astype(BF16), neg_suffix)

        def weights(t, slot, diag):
            vj = blk(v_ref, i - t)
            for h in range(2):
                inc = inc_st[slot, h]
                c = c_ref[h]
                a = jnp.exp(z_st[slot, h] + inc + c)
                if diag:
                    a = jnp.where(causal, a, 0.0)
                upd = _dot(a.astype(BF16), vj)
                acc[...] += jnp.where(masks[h], upd, 0.0)
                cs_ref[...] = jnp.where(lane == i - t + HEAD_DIM * h, c, cs_ref[...])
                c_ref[h] = c + inc[:, 0:1]

        stages = [scores, suffix_sums, weights]

        def saturated():
            return jnp.max(c_ref[...]) < -SAT_LIMIT

        def note(used):
            cs_ref[...] = jnp.where(lane == LANES - 1, jnp.asarray(used).astype(F32), cs_ref[...])

        _sweep(i + 1, stages, note, first_special=True, saturated=saturated, extra_head=1)
        y_ref[...] = acc[...].astype(BF16)

    return pl.pallas_call(
        body, name=name, grid=(hp, nb),
        in_specs=[pl.BlockSpec((T, LANES), lambda p, i: (i, p)),
                  pl.BlockSpec((S, LANES), lambda p, i: (0, p)),
                  pl.BlockSpec((S, LANES), lambda p, i: (0, p))],
        out_specs=[pl.BlockSpec((T, LANES), lambda p, i: (i, p)),
                   pl.BlockSpec((None, T, LANES), lambda p, i: (p, i, 0))],
        out_shape=[jax.ShapeDtypeStruct((S, nsb), BF16), jax.ShapeDtypeStruct((hp, S, LANES), F32)],
        scratch_shapes=[pltpu.VMEM((2, T, 1), F32), pltpu.VMEM((T, LANES), F32),
                        pltpu.VMEM((N_SLOTS, 2, T, T), F32), pltpu.VMEM((N_SLOTS, 2, T, T), F32)],
        compiler_params=_cparams(),
    )(q, k, v)


def _attn_bwd(q, k, v, dy, col0, carry, name):
    S, nsb = q.shape
    T = ATT_TILE
    hp = nsb // LANES
    nb = S // T

    def body(q_ref, k_ref, v_ref, dy_ref, cs_ref, dq_ref, dk_ref, dv_ref, e_ref, acc,
             z_st, da_st, b_st, inc_st, a_st, e_st, p_st):
        i = pl.program_id(1)

        @pl.when(i == 0)
        def _():
            dk_ref[...] = jnp.zeros_like(dk_ref)
            dv_ref[...] = jnp.zeros_like(dv_ref)

        masks, lane = _head_masks()
        qv = q_ref[...]
        dyb = dy_ref[...].astype(BF16)
        qm = [jnp.where(m, qv, jnp.zeros_like(qv)) for m in masks]
        dym = [jnp.where(m, dyb, jnp.zeros_like(dyb)) for m in masks]
        neg_suffix, prefix, causal = _tile_consts(T)
        e_ref[...] = jnp.zeros_like(e_ref)
        acc[...] = jnp.zeros_like(acc)

        def blk(ref, j):
            return ref[pl.ds(pl.multiple_of(j * T, T), T), :]

        used = jnp.max(jnp.where(lane == LANES - 1, cs_ref[...], 0.0)).astype(jnp.int32)
        n = jnp.clip(used, 1, i + 1)
        first = i + 1 - n

        def scores(t, slot, diag):
            kj = blk(k_ref, first + t)
            vj = blk(v_ref, first + t)
            for h in range(2):
                z_st[slot, h] = jnp.minimum(_dot_nt(qm[h], kj), Z_CLAMP)
                da_st[slot, h] = _dot_nt(dym[h], vj)

        def suffix_sums(t, slot, diag):
            for h in range(2):
                u = jnp.exp(z_st[slot, h])
                w = 1.0 + u
                b_st[slot, h] = u / w
                sp = jnp.log(w)
                if diag:
                    sp = jnp.where(causal, sp, 0.0)
                inc_st[slot, h] = _dot(sp.astype(BF16), neg_suffix)

        def probs(t, slot, diag):
            csv = cs_ref[...]
            for h in range(2):
                c = jnp.sum(jnp.where(lane == first + t + HEAD_DIM * h, csv, 0.0), axis=-1, keepdims=True)
                a = jnp.exp(z_st[slot, h] + inc_st[slot, h] + c)
                if diag:
                    a = jnp.where(causal, a, 0.0)
                a_st[slot, h] = a.astype(BF16)
                e = a * da_st[slot, h]
                e_st[slot, h] = e
                p_st[slot, h] = _dot(e.astype(BF16), prefix)

        def grads(t, slot, diag):
            kj = blk(k_ref, first + t)
            off = pl.multiple_of((first + t) * T, T)
            for h in range(2):
                p = p_st[slot, h]
                dz = e_st[slot, h] - b_st[slot, h] * (e_ref[h] + p)
                if diag:
                    dz = jnp.where(causal, dz, 0.0)
                dzb = dz.astype(BF16)
                acc[...] += jnp.where(masks[h], _dot(dzb, kj), 0.0)
                dk_ref[pl.ds(off, T), :] += _dot_tn(dzb, qm[h])
                dv_ref[pl.ds(off, T), :] += _dot_tn(a_st[slot, h], dym[h])
                e_ref[h] += p[:, T - 1:T]

        stages = [scores, suffix_sums, probs, grads]

        _sweep(n, stages, lambda done: None, last_special=True)
        dq_ref[...] = acc[...]

    return pl.pallas_call(
        body, name=name, grid=(hp, nb),
        in_specs=[pl.BlockSpec((T, LANES), lambda p, i: (i, p)),
                  pl.BlockSpec((S, LANES), lambda p, i: (0, p)),
                  pl.BlockSpec((S, LANES), lambda p, i: (0, p)),
                  pl.BlockSpec((T, LANES), lambda p, i: (i, col0 + p)),
                  pl.BlockSpec((None, T, LANES), lambda p, i: (p, i, 0))],
        out_specs=[pl.BlockSpec((T, LANES), lambda p, i: (i, p)),
                   pl.BlockSpec((S, LANES), lambda p, i: (0, p)),
                   pl.BlockSpec((S, LANES), lambda p, i: (0, p))],
        out_shape=[jax.ShapeDtypeStruct((S, nsb), F32)] * 3,
        scratch_shapes=[pltpu.VMEM((2, T, 1), F32), pltpu.VMEM((T, LANES), F32)]
        + [pltpu.VMEM((N_SLOTS, 2, T, T), dt) for dt in (F32, F32, F32, F32, BF16, F32, F32)],
        compiler_params=_cparams(),
    )(q, k, v, dy, carry)


def _ple(x, p, tgt, gain, wpg, wpp, name):
    S, D = x.shape
    P = p.shape[1]
    ns, _, nc = wpp.shape
    tm = _rows(S, 256)

    def body(x_ref, p_ref, t_ref, g_ref, wpg_ref, wpp_ref, dx_ref, h_ref, du_ref, dpp_ref, loss_ref, dg_ref):
        @pl.when(pl.program_id(0) == 0)
        def _():
            loss_ref[...] = jnp.zeros_like(loss_ref)
            dg_ref[...] = jnp.zeros_like(dg_ref)

        x_ = x_ref[...]
        g = g_ref[...]
        h, xn, r = _rms(x_, g)
        hb = h.astype(BF16)
        h_ref[...] = hb
        gate = jax.nn.sigmoid(_dot(hb, wpg_ref[...]))
        pb = p_ref[...].astype(BF16)
        pp = jnp.concatenate([_dot(pb, wpp_ref[n]) for n in range(ns)], axis=1)
        err = (x_ + gate * pp) - t_ref[...]
        loss_ref[...] += (0.5 / D) * jnp.sum(err * err)
        dy = err * (1.0 / D)
        du = ((dy * pp) * (gate * (1.0 - gate))).astype(BF16)
        du_ref[...] = du
        dpp_ref[...] = (dy * gate).astype(BF16)
        dx, dg = _rms_bwd(_dot_nt(du, wpg_ref[...]), xn, r, g)
        dx_ref[...] = dy + dx
        dg_ref[...] += jnp.broadcast_to(dg, dg_ref.shape)

    row = pl.BlockSpec((tm, D), lambda i: (i, 0))
    return pl.pallas_call(
        body, name=name, grid=(S // tm,),
        in_specs=[row, pl.BlockSpec((tm, P), lambda i: (i, 0)), row,
                  pl.BlockSpec((1, D), lambda i: (0, 0)),
                  pl.BlockSpec((D, D), lambda i: (0, 0)),
                  pl.BlockSpec((ns, P, nc), lambda i: (0, 0, 0))],
        out_specs=[row, row, row, row,
                   pl.BlockSpec((8, LANES), lambda i: (0, 0)),
                   pl.BlockSpec((8, D), lambda i: (0, 0))],
        out_shape=[jax.ShapeDtypeStruct((S, D), F32)] + [jax.ShapeDtypeStruct((S, D), BF16)] * 3
        + [jax.ShapeDtypeStruct((8, LANES), F32), jax.ShapeDtypeStruct((8, D), F32)],
        compiler_params=_cparams(),
    )(x, p, tgt, gain, wpg, wpp)


def _elementwise(fn, ins, n_out, name):
    R, C = ins[0].shape
    tr = _rows(R, 512)

    def body(*refs):
        outs = fn(*[r[...] for r in refs[:len(ins)]])
        for o_ref, o in zip(refs[len(ins):], outs):
            o_ref[...] = o

    spec = pl.BlockSpec((tr, C), lambda i: (i, 0))
    return pl.pallas_call(
        body, name=name, grid=(R // tr,), in_specs=[spec] * len(ins), out_specs=[spec] * n_out,
        out_shape=[jax.ShapeDtypeStruct((R, C), F32)] * n_out, compiler_params=_cparams(),
    )(*ins)


def _adamw(w, g, m, v):
    m = ADAM_B1 * m + (1.0 - ADAM_B1) * g
    v = ADAM_B2 * v + (1.0 - ADAM_B2) * jnp.square(g)
    m_hat = m / (1.0 - ADAM_B1 ** ADAM_STEP)
    v_hat = v / (1.0 - ADAM_B2 ** ADAM_STEP)
    delta = -ADAM_LR * (m_hat / (jnp.sqrt(v_hat) + ADAM_EPS) + ADAM_WD * w)
    return delta, m, v


def _place():
    x, y, c = lax.axis_index("x"), lax.axis_index("y"), lax.axis_index("c")
    chips = [(1 - x, y), (x, 1 - y), (1 - x, 1 - y)]
    return x, y, c, chips


def _half(ref, c, axis_rows):
    n = ref.shape[-2]
    start = pl.multiple_of(c * (n // 2), 8)
    idx = (slice(None),) * (len(ref.shape) - 2) + (pl.ds(start, n // 2), slice(None))
    return ref.at[idx]


def _gather_weights(shards, small, name):
    n = len(shards)

    def body(*refs):
        ins, small_in = refs[:n], refs[n]
        outs, small_out = refs[n + 1:2 * n + 1], refs[2 * n + 1]
        lsem, lrsem, ssem, rsem, sm_s, sm_r = refs[2 * n + 2:]
        x, y, c, chips = _place()
        j = 2 * x + y
        sib = (x, y, 1 - c)

        local = [pltpu.make_async_remote_copy(
            src_ref=ins[a], dst_ref=outs[a].at[j], send_sem=lsem.at[a], recv_sem=lrsem.at[a],
            device_id=sib, device_id_type=MESH) for a in range(n)]
        for cp in local:
            cp.start()
        small_out[j] = small_in[...]
        small_cp = [pltpu.make_async_remote_copy(
            src_ref=small_in, dst_ref=small_out.at[j], send_sem=sm_s.at[k], recv_sem=sm_r.at[k],
            device_id=(*chip, c), device_id_type=MESH) for k, chip in enumerate(chips)]
        for cp in small_cp:
            cp.start()

        def ici(a, k, chip, jj, dev):
            return pltpu.make_async_remote_copy(
                src_ref=_half(ins[a], c, True) if dev is not None else _half(outs[a].at[jj], c, True),
                dst_ref=_half(outs[a].at[jj], c, True),
                send_sem=ssem.at[a, k], recv_sem=rsem.at[a, k],
                device_id=dev if dev is not None else (*chip, c), device_id_type=MESH)

        first = []
        for a in range(n):
            for k, chip in enumerate(chips):
                cp = ici(a, k, chip, j, (*chip, c))
                cp.start()
                first.append(cp)
        passed = []
        for a in range(n):
            for k, chip in enumerate(chips):
                jj = 2 * chip[0] + chip[1]
                ici(a, k, chip, jj, None).wait_recv()
                fw = pltpu.make_async_remote_copy(
                    src_ref=_half(outs[a].at[jj], c, True), dst_ref=_half(outs[a].at[jj], c, True),
                    send_sem=ssem.at[a, 3 + k], recv_sem=rsem.at[a, 3 + k], device_id=sib, device_id_type=MESH)
                fw.start()
                passed.append(fw)
        for a in range(n):
            for k, chip in enumerate(chips):
                jj = 2 * chip[0] + chip[1]
                pltpu.make_async_remote_copy(
                    src_ref=_half(outs[a].at[jj], 1 - c, True), dst_ref=_half(outs[a].at[jj], 1 - c, True),
                    send_sem=ssem.at[a, 3 + k], recv_sem=rsem.at[a, 3 + k], device_id=sib,
                    device_id_type=MESH).wait_recv()
        for cp in small_cp:
            cp.wait()
        for cp in first + passed:
            cp.wait_send()
        for cp in local:
            cp.wait()

    return pl.pallas_call(
        body, name=name,
        in_specs=[HBM] * n + [VMEM_WHOLE],
        out_specs=[HBM] * n + [VMEM_WHOLE],
        out_shape=[jax.ShapeDtypeStruct((N_SHARDS,) + s.shape, s.dtype) for s in shards]
        + [jax.ShapeDtypeStruct((N_SHARDS,) + small.shape, small.dtype)],
        scratch_shapes=[pltpu.SemaphoreType.DMA((n,)), pltpu.SemaphoreType.DMA((n,)),
                        pltpu.SemaphoreType.DMA((n, 6)), pltpu.SemaphoreType.DMA((n, 6)),
                        pltpu.SemaphoreType.DMA((3,)), pltpu.SemaphoreType.DMA((3,))],
    )(*shards, small)


def _swap_halves(grads, small, name):
    n = len(grads)

    def body(*refs):
        ins, small_in = refs[:n], refs[n]
        outs, small_out = refs[n + 1:2 * n + 1], refs[2 * n + 1]
        buf, ssem, rsem, sm_s, sm_r = refs[2 * n + 2:]
        x, y, c, _ = _place()
        me = 4 * x + 2 * y + c
        sib = (x, y, 1 - c)
        cps = [pltpu.make_async_remote_copy(
            src_ref=_half(ins[a], 1 - c, True), dst_ref=outs[a], send_sem=ssem.at[a], recv_sem=rsem.at[a],
            device_id=sib, device_id_type=MESH) for a in range(n)]
        for cp in cps:
            cp.start()
        buf[me] = small_in[...]
        peers = [(fx, fy, fc) for fx in (0, 1) for fy in (0, 1) for fc in (0, 1)][1:]
        sm = []
        for k, (fx, fy, fc) in enumerate(peers):
            dev = (1 - x if fx else x, 1 - y if fy else y, 1 - c if fc else c)
            cp = pltpu.make_async_remote_copy(
                src_ref=small_in, dst_ref=buf.at[me], send_sem=sm_s.at[k], recv_sem=sm_r.at[k],
                device_id=dev, device_id_type=MESH)
            cp.start()
            sm.append(cp)
        for cp in sm:
            cp.wait()
        tot = buf[0]
        for d in range(1, N_DEV):
            tot = tot + buf[d]
        small_out[...] = tot
        for cp in cps:
            cp.wait()

    return pl.pallas_call(
        body, name=name,
        in_specs=[HBM] * n + [VMEM_WHOLE],
        out_specs=[HBM] * n + [VMEM_WHOLE],
        out_shape=[jax.ShapeDtypeStruct((g.shape[0], g.shape[1] // 2, g.shape[2]), F32) for g in grads]
        + [jax.ShapeDtypeStruct(small.shape, F32)],
        scratch_shapes=[pltpu.VMEM((N_DEV,) + small.shape, F32),
                        pltpu.SemaphoreType.DMA((n,)), pltpu.SemaphoreType.DMA((n,)),
                        pltpu.SemaphoreType.DMA((N_DEV - 1,)), pltpu.SemaphoreType.DMA((N_DEV - 1,))],
    )(*grads, small)


def _chip_sum(g, recv, core, name):
    ns, R, C = g.shape
    r2 = R // 2
    tr = _rows(r2, 512)
    nrb = r2 // tr

    def body(core_ref, g_ref, r_ref, o_ref, ob_ref):
        s = g_ref[...] + r_ref[...]
        o_ref[...] = s
        ob_ref[...] = s.astype(BF16)

    out = pl.BlockSpec((None, tr, C), lambda s, i, cr: (s, i, 0))
    return pl.pallas_call(
        body, name=name,
        grid_spec=pltpu.PrefetchScalarGridSpec(
            num_scalar_prefetch=1, grid=(ns, nrb),
            in_specs=[pl.BlockSpec((None, tr, C), lambda s, i, cr: (s, cr[0] * nrb + i, 0)), out],
            out_specs=[out, out]),
        out_shape=[jax.ShapeDtypeStruct((ns, r2, C), F32), jax.ShapeDtypeStruct((ns, r2, C), BF16)],
        compiler_params=_cparams(),
    )(core, g, recv)


def _shard_sum(csum, got, place, name):
    _, r2, C = csum.shape
    tr = _rows(r2, 512)
    nrb = r2 // tr

    def body(place_ref, c_ref, g0_ref, g1_ref, g2_ref, o_ref):
        o_ref[...] = ((c_ref[...] + g0_ref[...].astype(F32)) + g1_ref[...].astype(F32)) + g2_ref[...].astype(F32)

    def got_spec(k):
        return pl.BlockSpec((None, tr, C), lambda i, pr: (k, i, 0))

    return pl.pallas_call(
        body, name=name,
        grid_spec=pltpu.PrefetchScalarGridSpec(
            num_scalar_prefetch=1, grid=(nrb,),
            in_specs=[pl.BlockSpec((None, tr, C), lambda i, pr: (pr[0], i, 0)), got_spec(0), got_spec(1), got_spec(2)],
            out_specs=pl.BlockSpec((tr, C), lambda i, pr: (pr[1] * nrb + i, 0))),
        out_shape=jax.ShapeDtypeStruct((2 * r2, C), F32),
        compiler_params=_cparams(),
    )(place, csum, got, got, got)


def _scatter_chip_sums(csums, name):
    n = len(csums)

    def body(*refs):
        ins, outs = refs[:n], refs[n:2 * n]
        ssem, rsem = refs[2 * n:]
        x, y, c, chips = _place()
        cps = []
        for a in range(n):
            for k, chip in enumerate(chips):
                jj = 2 * chip[0] + chip[1]
                cp = pltpu.make_async_remote_copy(
                    src_ref=ins[a].at[jj], dst_ref=outs[a].at[k], send_sem=ssem.at[a, k], recv_sem=rsem.at[a, k],
                    device_id=(*chip, c), device_id_type=MESH)
                cp.start()
                cps.append(cp)
        for cp in cps:
            cp.wait()

    return pl.pallas_call(
        body, name=name, in_specs=[HBM] * n, out_specs=[HBM] * n,
        out_shape=[jax.ShapeDtypeStruct((3,) + g.shape[1:], g.dtype) for g in csums],
        scratch_shapes=[pltpu.SemaphoreType.DMA((n, 3)), pltpu.SemaphoreType.DMA((n, 3))],
    )(*csums)


def _join_halves(fulls, name):
    n = len(fulls)

    def body(*refs):
        outs = refs[n:2 * n]
        ssem, rsem = refs[2 * n:]
        x, y, c, _ = _place()
        cps = [pltpu.make_async_remote_copy(
            src_ref=_half(outs[a], c, True), dst_ref=_half(outs[a], c, True), send_sem=ssem.at[a],
            recv_sem=rsem.at[a], device_id=(x, y, 1 - c), device_id_type=MESH) for a in range(n)]
        for cp in cps:
            cp.start()
        for cp in cps:
            cp.wait()

    return pl.pallas_call(
        body, name=name, in_specs=[HBM] * n, out_specs=[HBM] * n,
        out_shape=[jax.ShapeDtypeStruct(f.shape, F32) for f in fulls],
        input_output_aliases={a: a for a in range(n)},
        scratch_shapes=[pltpu.SemaphoreType.DMA((n,))] * 2,
    )(*fulls)


def _pad_rows(a, rows, cols):
    return jnp.pad(a, ((0, rows - a.shape[0]), (0, cols - a.shape[1])))


def kernel(x, p, ffn1_norm, ffn1_w_gate, ffn1_w_up, ffn1_w_down, mix_norm, w_in, conv_w, conv_b, q_norm, k_norm, w_out, ffn2_norm, ffn2_w_gate, ffn2_w_up, ffn2_w_down, ple_norm, ple_w_gate, ple_w_proj, loss_target, m_ffn1_norm, m_ffn1_w_gate, m_ffn1_w_up, m_ffn1_w_down, m_mix_norm, m_w_in, m_conv_w, m_conv_b, m_q_norm, m_k_norm, m_w_out, m_ffn2_norm, m_ffn2_w_gate, m_ffn2_w_up, m_ffn2_w_down, m_ple_norm, m_ple_w_gate, m_ple_w_proj, v_ffn1_norm, v_ffn1_w_gate, v_ffn1_w_up, v_ffn1_w_down, v_mix_norm, v_w_in, v_conv_w, v_conv_b, v_q_norm, v_k_norm, v_w_out, v_ffn2_norm, v_ffn2_w_gate, v_ffn2_w_up, v_ffn2_w_down, v_ple_norm, v_ple_w_gate, v_ple_w_proj):
    big = dict(ffn1_w_gate=ffn1_w_gate, ffn1_w_up=ffn1_w_up, ffn1_w_down=ffn1_w_down, w_in=w_in, w_out=w_out,
               ffn2_w_gate=ffn2_w_gate, ffn2_w_up=ffn2_w_up, ffn2_w_down=ffn2_w_down,
               ple_w_gate=ple_w_gate, ple_w_proj=ple_w_proj)
    big_m = dict(ffn1_w_gate=m_ffn1_w_gate, ffn1_w_up=m_ffn1_w_up, ffn1_w_down=m_ffn1_w_down, w_in=m_w_in,
                 w_out=m_w_out, ffn2_w_gate=m_ffn2_w_gate, ffn2_w_up=m_ffn2_w_up, ffn2_w_down=m_ffn2_w_down,
                 ple_w_gate=m_ple_w_gate, ple_w_proj=m_ple_w_proj)
    big_v = dict(ffn1_w_gate=v_ffn1_w_gate, ffn1_w_up=v_ffn1_w_up, ffn1_w_down=v_ffn1_w_down, w_in=v_w_in,
                 w_out=v_w_out, ffn2_w_gate=v_ffn2_w_gate, ffn2_w_up=v_ffn2_w_up, ffn2_w_down=v_ffn2_w_down,
                 ple_w_gate=v_ple_w_gate, ple_w_proj=v_ple_w_proj)
    names = list(big)
    xs = x[0]
    ps = p[0, 0]
    tgt = loss_target[0]
    S, D = xs.shape
    nconv = conv_b.shape[1]
    nsb = D - nconv
    cwl = conv_w.shape[2]
    jchip = 2 * lax.axis_index("x") + lax.axis_index("y")
    core = lax.axis_index("c")

    shards = [big[k][0].astype(BF16) for k in names]
    gathered = _gather_weights(shards, _pad_rows(conv_w[0], 8, LANES), "gather_weights")
    W = dict(zip(names, gathered[:-1]))
    cw_full = jnp.transpose(gathered[-1][:, :, :cwl], (1, 0, 2)).reshape(8, N_SHARDS * cwl)
    wout_full = W["w_out"].reshape(-1, D)
    wpg_full = W["ple_w_gate"].reshape(-1, D)
    qg = jnp.tile(q_norm, (1, CONV_COLS // HEAD_DIM))
    kg = jnp.tile(k_norm, (1, CONV_COLS // HEAD_DIM))
    n_units = nconv // CONV_COLS

    x1, h1, a1, b1 = _ffn_fwd(xs, ffn1_norm, W["ffn1_w_gate"], W["ffn1_w_up"], W["ffn1_w_down"], "ffn1_fwd")
    proj, h2 = _norm_proj(x1, mix_norm, W["w_in"], "mix_in_proj")
    y_conv = _conv_fwd(proj, cw_full, conv_b, nconv, "conv_fwd")
    qs = _qk_norm(proj, qg, 3 * n_units, nsb, HEAD_DIM ** -0.5, "q_norm_fwd")
    kh = _qk_norm(proj, kg, 4 * n_units, nsb, 1.0, "k_norm_fwd")
    vb = proj[:, 3 * nconv + 2 * nsb:].astype(BF16)
    y_sb, carry = _attn_fwd(qs, kh, vb, "attn_fwd")
    ycat = jnp.concatenate([y_conv, y_sb], axis=1)
    x2 = _out_proj(ycat, wout_full, x1, "mix_out_proj")
    x3, h3, a3, b3 = _ffn_fwd(x2, ffn2_norm, W["ffn2_w_gate"], W["ffn2_w_up"], W["ffn2_w_down"], "ffn2_fwd")

    dx3, h4, du4, dpp, loss_blk, dg_ple = _ple(x3, ps, tgt, ple_norm, wpg_full, W["ple_w_proj"], "ple_loss")
    G = {}
    tk = _rows(S, 2048)
    nk = S // tk
    kd = wpg_full.shape[0] // N_SHARDS
    G["ple_w_gate"] = _tn_matmul(
        h4, du4, pl.BlockSpec((tk, kd), lambda m, k: (k, m)), pl.BlockSpec((tk, D), lambda m, k: (k, 0)),
        (N_SHARDS, kd, D), pl.BlockSpec((None, kd, D), lambda m, k: (m, 0, 0)), (N_SHARDS, nk), "ple_w_gate_grad")
    P = ps.shape[1]
    npp = D // N_SHARDS
    G["ple_w_proj"] = _tn_matmul(
        ps, dpp, pl.BlockSpec((tk, P), lambda m, k: (k, 0)), pl.BlockSpec((tk, npp), lambda m, k: (k, m)),
        (N_SHARDS, P, npp), pl.BlockSpec((None, P, npp), lambda m, k: (m, 0, 0)), (N_SHARDS, nk), "ple_w_proj_grad")

    def ffn_grads(pre, h, s, da, db, dy):
        fs = s.shape[2]
        hs = pl.BlockSpec((tk, D), lambda m, k: (k, 0))
        ss = pl.BlockSpec((None, tk, fs), lambda m, k: (m, k, 0))
        G[pre + "_w_gate"] = _tn_matmul(h, da, hs, ss, (N_SHARDS, D, fs),
                                        pl.BlockSpec((None, D, fs), lambda m, k: (m, 0, 0)), (N_SHARDS, nk), pre + "_w_gate_grad")
        G[pre + "_w_up"] = _tn_matmul(h, db, hs, ss, (N_SHARDS, D, fs),
                                      pl.BlockSpec((None, D, fs), lambda m, k: (m, 0, 0)), (N_SHARDS, nk), pre + "_w_up_grad")
        G[pre + "_w_down"] = _tn_matmul(s, dy, ss, hs, (N_SHARDS, fs, D),
                                        pl.BlockSpec((None, fs, D), lambda m, k: (m, 0, 0)), (N_SHARDS, nk), pre + "_w_down_grad")

    s3, da3, db3, dy3 = _ffn_bwd_act(dx3, a3, b3, W["ffn2_w_down"], "ffn2_bwd_act")
    dx2, dg_ffn2 = _ffn_bwd_in(da3, db3, W["ffn2_w_gate"], W["ffn2_w_up"], dx3, x2, ffn2_norm, "ffn2_bwd_in")
    ffn_grads("ffn2", h3, s3, da3, db3, dy3)

    dycat = _out_proj_bwd(dx2, wout_full, "mix_out_proj_bwd")
    ko = wout_full.shape[0] // N_SHARDS
    G["w_out"] = _tn_matmul(
        ycat, dx2, pl.BlockSpec((tk, ko), lambda m, k: (k, m)), pl.BlockSpec((tk, D), lambda m, k: (k, 0)),
        (N_SHARDS, ko, D), pl.BlockSpec((None, ko, D), lambda m, k: (m, 0, 0)), (N_SHARDS, nk), "w_out_grad")
    db_, dc_, du_, dwb = _conv_bwd(proj, dycat, cw_full, conv_b, nconv, "conv_bwd")
    dqs, dkh, dv = _attn_bwd(qs, kh, vb, dycat, nconv // LANES, carry, "attn_bwd")
    dq, dg_q = _qk_norm_bwd(proj, dqs, qg, 3 * n_units, nsb, HEAD_DIM ** -0.5, "q_norm_bwd")
    dk, dg_k = _qk_norm_bwd(proj, dkh, kg, 4 * n_units, nsb, 1.0, "k_norm_bwd")
    dproj = jnp.concatenate([db_, dc_, du_, dq, dk, dv.astype(BF16)], axis=1)
    nin = W["w_in"].shape[2]
    G["w_in"] = _tn_matmul(
        h2, dproj, pl.BlockSpec((tk, D), lambda m, k: (k, 0)), pl.BlockSpec((tk, nin), lambda m, k: (k, m)),
        (N_SHARDS, D, nin), pl.BlockSpec((None, D, nin), lambda m, k: (m, 0, 0)), (N_SHARDS, nk), "w_in_grad")
    dx1, dg_mix = _norm_proj_bwd(dproj, W["w_in"], dx2, x1, mix_norm, "mix_in_proj_bwd")

    s1, da1, db1, dy1 = _ffn_bwd_act(dx1, a1, b1, W["ffn1_w_down"], "ffn1_bwd_act")
    dx0, dg_ffn1 = _ffn_bwd_in(da1, db1, W["ffn1_w_gate"], W["ffn1_w_up"], dx1, xs, ffn1_norm, "ffn1_bwd_in")
    ffn_grads("ffn1", h1, s1, da1, db1, dy1)

    assert D >= nconv and D % LANES == 0
    fold = lambda t: t[0].reshape(-1, HEAD_DIM).sum(axis=0)[None, :]
    small_rows = [dg_ffn1[0:1], dg_mix[0:1], dg_ffn2[0:1], dg_ple[0:1],
                  _pad_rows(dwb[3:4], 1, D), _pad_rows(dwb[0:3], 3, D),
                  _pad_rows(fold(dg_q), 1, D), _pad_rows(fold(dg_k), 1, D), _pad_rows(loss_blk[0:1, 0:1], 1, D)]
    small = _pad_rows(jnp.concatenate(small_rows, axis=0), SMALL_ROWS, D)

    grads = [G[k] for k in names]
    swapped = _swap_halves(grads, small, "grad_swap_halves")
    small_sum = swapped[-1]
    core_arr = jnp.reshape(core, (1,)).astype(jnp.int32)
    place = jnp.stack([jchip, core]).astype(jnp.int32)
    csums, csums_bf = zip(*[_chip_sum(g, r, core_arr, f"chip_sum_{names[a]}")
                            for a, (g, r) in enumerate(zip(grads, swapped[:-1]))])
    got = _scatter_chip_sums(list(csums_bf), "grad_scatter")
    full = _join_halves([_shard_sum(cs, gt, place, f"shard_sum_{names[a]}")
                         for a, (cs, gt) in enumerate(zip(csums, got))], "grad_join_halves")

    out_g, out_d, out_m, out_v = {}, {}, {}, {}
    for a, k in enumerate(names):
        shp = big[k].shape
        g2 = full[a]
        d_, m_, v_ = _elementwise(_adamw, [big[k][0].reshape(g2.shape), g2, big_m[k][0].reshape(g2.shape),
                                           big_v[k][0].reshape(g2.shape)], 3, f"adamw_{k}")
        out_g[k], out_d[k], out_m[k], out_v[k] = (t.reshape(shp) for t in (g2, d_, m_, v_))

    sm_names = ["ffn1_norm", "mix_norm", "ffn2_norm", "ple_norm", "conv_b", "conv_w", "q_norm", "k_norm"]
    sm_w = dict(ffn1_norm=ffn1_norm, mix_norm=mix_norm, ffn2_norm=ffn2_norm, ple_norm=ple_norm, conv_b=conv_b,
                conv_w=conv_w[0], q_norm=q_norm, k_norm=k_norm)
    sm_m = dict(ffn1_norm=m_ffn1_norm, mix_norm=m_mix_norm, ffn2_norm=m_ffn2_norm, ple_norm=m_ple_norm,
                conv_b=m_conv_b, conv_w=m_conv_w[0], q_norm=m_q_norm, k_norm=m_k_norm)
    sm_v = dict(ffn1_norm=v_ffn1_norm, mix_norm=v_mix_norm, ffn2_norm=v_ffn2_norm, ple_norm=v_ple_norm,
                conv_b=v_conv_b, conv_w=v_conv_w[0], q_norm=v_q_norm, k_norm=v_k_norm)
    sm_g = dict(ffn1_norm=small_sum[0:1], mix_norm=small_sum[1:2], ffn2_norm=small_sum[2:3], ple_norm=small_sum[3:4],
                conv_b=small_sum[4:5, :nconv],
                conv_w=lax.dynamic_slice_in_dim(small_sum[5:8, :nconv], jchip * cwl, cwl, axis=1),
                q_norm=small_sum[8:9, :HEAD_DIM], k_norm=small_sum[9:10, :HEAD_DIM])
    loss = small_sum[10, 0]
    pack = lambda d: _pad_rows(jnp.concatenate([_pad_rows(d[k], d[k].shape[0], D) for k in sm_names], axis=0), SMALL_ROWS, D)
    sd, smm, svv = _elementwise(_adamw, [pack(sm_w), pack(sm_g), pack(sm_m), pack(sm_v)], 3, "adamw_small")
    row = 0
    for k in sm_names:
        r_, c_ = sm_w[k].shape
        shp = (1, r_, c_) if k == "conv_w" else (r_, c_)
        out_g[k] = sm_g[k].reshape(shp)
        out_d[k], out_m[k], out_v[k] = (t[row:row + r_, :c_].reshape(shp) for t in (sd, smm, svv))
        row += r_

    order = ["ffn1_norm", "ffn1_w_gate", "ffn1_w_up", "ffn1_w_down", "mix_norm", "w_in", "conv_w", "conv_b",
             "q_norm", "k_norm", "w_out", "ffn2_norm", "ffn2_w_gate", "ffn2_w_up", "ffn2_w_down", "ple_norm",
             "ple_w_gate", "ple_w_proj"]
    return (loss, dx0[None], *[out_g[k] for k in order], *[out_d[k] for k in order],
            *[out_m[k] for k in order], *[out_v[k] for k in order])
```

```python
import jax
import jax.numpy as jnp
from jax import lax
from jax.experimental import pallas as pl
from jax.experimental.pallas import tpu as pltpu

F32 = jnp.float32
BF16 = jnp.bfloat16
MESH = pl.DeviceIdType.MESH

EPS = 1e-6
HEAD_DIM = 64
LANES = 128
FFN_RES = 0.5
ADAM_LR = 0.001
ADAM_B1 = 0.9
ADAM_B2 = 0.999
ADAM_EPS = 1e-08
ADAM_WD = 0.01
ADAM_STEP = 10
N_SHARDS = 4
N_DEV = 8
ATT_TILE = 256
VMEM_LIMIT = 52 * 1024 * 1024
SMALL_ROWS = 16
HBM = pl.BlockSpec(memory_space=pltpu.HBM)
VMEM_WHOLE = pl.BlockSpec(memory_space=pltpu.VMEM)


def _cparams(**kw):
    return pltpu.CompilerParams(vmem_limit_bytes=VMEM_LIMIT, **kw)


def _dot(a, b):
    return jnp.dot(a, b, preferred_element_type=F32)


def _dot_nt(a, b):
    return lax.dot_general(a, b, (((1,), (1,)), ((), ())), preferred_element_type=F32)


def _dot_tn(a, b):
    return lax.dot_general(a, b, (((0,), (0,)), ((), ())), preferred_element_type=F32)


def _split_dot(x, m):
    hi = x.astype(BF16)
    lo = (x - hi.astype(F32)).astype(BF16)
    return _dot(hi, m) + _dot(lo, m)


def _rms(x, g):
    r = lax.rsqrt(jnp.mean(x * x, axis=-1, keepdims=True) + EPS)
    xn = x * r
    return xn * g, xn, r


def _rms_bwd(dh, xn, r, g):
    dxn = dh * g
    dx = r * (dxn - xn * jnp.mean(dxn * xn, axis=-1, keepdims=True))
    return dx, jnp.sum(dh * xn, axis=0, keepdims=True)


def _rows(n, cap=512):
    for t in (2048, 1024, 512, 448, 384, 352, 256, 192, 176, 128, 96, 88, 64, 48, 32, 16, 8):
        if t <= cap and n % t == 0:
            return t
    raise ValueError(f"no row tile for {n}")


class _Ride:
    def __init__(self, ins, out_shapes, sems, make):
        self.ins, self.out_shapes, self.sems, self.make = list(ins), list(out_shapes), list(sems), make

    def split(self, refs, n_in, n_out, n_scratch):
        ni, no = len(self.ins), len(self.out_shapes)
        ins, rin = refs[:n_in], refs[n_in:n_in + ni]
        outs = refs[n_in + ni:n_in + ni + n_out]
        rout = refs[n_in + ni + n_out:n_in + ni + n_out + no]
        rest = refs[n_in + ni + n_out + no:]
        return ins, outs, rest[:n_scratch], lambda: self.make(rin, rout, rest[n_scratch:])


_NO_RIDE = _Ride([], [], [], lambda i, o, s: [])


def _ride_along(copies, first, last):
    @pl.when(first)
    def _():
        for cp in copies():
            cp.start()

    @pl.when(last)
    def _():
        for cp in copies():
            cp.wait()


def _ffn_fwd(x, gain, wg, wu, wd, name, ride=_NO_RIDE):
    S, D = x.shape
    ns, _, fs = wg.shape
    tm = _rows(S, 512)
    ni = S // tm

    def body(*refs):
        (x_ref, g_ref, wg_ref, wu_ref, wd_ref), (xo_ref, h_ref, a_ref, b_ref), (hs, acc), copies = ride.split(refs, 5, 4, 2)
        j = pl.program_id(1)
        _ride_along(copies, (pl.program_id(0) == 0) & (j == 0), (pl.program_id(0) == ni - 1) & (j == ns - 1))

        @pl.when(j == 0)
        def _():
            h, _, _ = _rms(x_ref[...], g_ref[...])
            hb = h.astype(BF16)
            hs[...] = hb
            h_ref[...] = hb
            acc[...] = jnp.zeros_like(acc)

        hb = hs[...]
        a = _dot(hb, wg_ref[...])
        b = _dot(hb, wu_ref[...])
        a_ref[...] = a.astype(BF16)
        b_ref[...] = b.astype(BF16)
        s = (a * jax.nn.sigmoid(a)) * b
        acc[...] += _dot(s.astype(BF16), wd_ref[...])

        @pl.when(j == ns - 1)
        def _():
            xo_ref[...] = x_ref[...] + FFN_RES * acc[...]

    return pl.pallas_call(
        body, name=name, grid=(ni, ns),
        in_specs=[
            pl.BlockSpec((tm, D), lambda i, j: (i, 0)),
            pl.BlockSpec((1, D), lambda i, j: (0, 0)),
            pl.BlockSpec((None, D, fs), lambda i, j: (j, 0, 0)),
            pl.BlockSpec((None, D, fs), lambda i, j: (j, 0, 0)),
            pl.BlockSpec((None, fs, D), lambda i, j: (j, 0, 0)),
        ] + [HBM] * len(ride.ins),
        out_specs=[
            pl.BlockSpec((tm, D), lambda i, j: (i, 0)),
            pl.BlockSpec((tm, D), lambda i, j: (i, 0)),
            pl.BlockSpec((None, tm, fs), lambda i, j: (j, i, 0)),
            pl.BlockSpec((None, tm, fs), lambda i, j: (j, i, 0)),
        ] + [HBM] * len(ride.out_shapes),
        out_shape=[
            jax.ShapeDtypeStruct((S, D), F32),
            jax.ShapeDtypeStruct((S, D), BF16),
            jax.ShapeDtypeStruct((ns, S, fs), BF16),
            jax.ShapeDtypeStruct((ns, S, fs), BF16),
        ] + ride.out_shapes,
        scratch_shapes=[pltpu.VMEM((tm, D), BF16), pltpu.VMEM((tm, D), F32)] + ride.sems,
        compiler_params=_cparams(),
    )(x, gain, wg, wu, wd, *ride.ins)


def _ffn_bwd_act(dxo, a, b, wd, name, ride=_NO_RIDE):
    S, D = dxo.shape
    ns, fs, _ = wd.shape
    tm = _rows(S, 512)
    ni = S // tm

    def body(*refs):
        (dxo_ref, a_ref, b_ref, wd_ref), (s_ref, da_ref, db_ref, dy_ref), (dys,), copies = ride.split(refs, 4, 4, 1)
        j = pl.program_id(1)
        _ride_along(copies, (pl.program_id(0) == 0) & (j == 0), (pl.program_id(0) == ni - 1) & (j == ns - 1))

        @pl.when(j == 0)
        def _():
            dy = (FFN_RES * dxo_ref[...]).astype(BF16)
            dys[...] = dy
            dy_ref[...] = dy

        av = a_ref[...].astype(F32)
        bv = b_ref[...].astype(F32)
        ds = _dot_nt(dys[...], wd_ref[...])
        sig = jax.nn.sigmoid(av)
        sl = av * sig
        s_ref[...] = (sl * bv).astype(BF16)
        da_ref[...] = (ds * bv * (sig * (1.0 + av * (1.0 - sig)))).astype(BF16)
        db_ref[...] = (ds * sl).astype(BF16)

    act = pl.BlockSpec((None, tm, fs), lambda i, j: (j, i, 0))
    row = pl.BlockSpec((tm, D), lambda i, j: (i, 0))
    return pl.pallas_call(
        body, name=name, grid=(ni, ns),
        in_specs=[row, act, act, pl.BlockSpec((None, fs, D), lambda i, j: (j, 0, 0))] + [HBM] * len(ride.ins),
        out_specs=[act, act, act, row] + [HBM] * len(ride.out_shapes),
        out_shape=[jax.ShapeDtypeStruct((ns, S, fs), BF16)] * 3 + [jax.ShapeDtypeStruct((S, D), BF16)]
        + ride.out_shapes,
        scratch_shapes=[pltpu.VMEM((tm, D), BF16)] + ride.sems,
        compiler_params=_cparams(),
    )(dxo, a, b, wd, *ride.ins)


def _ffn_bwd_in(da, db, wg, wu, dres, x, gain, name):
    S, D = x.shape
    ns, _, fs = wg.shape
    tm = _rows(S, 512)

    def body(da_ref, db_ref, wg_ref, wu_ref, dres_ref, x_ref, g_ref, dx_ref, dg_ref, acc):
        i = pl.program_id(0)
        j = pl.program_id(1)

        @pl.when((i == 0) & (j == 0))
        def _():
            dg_ref[...] = jnp.zeros_like(dg_ref)

        @pl.when(j == 0)
        def _():
            acc[...] = jnp.zeros_like(acc)

        acc[...] += _dot_nt(da_ref[...], wg_ref[...]) + _dot_nt(db_ref[...], wu_ref[...])

        @pl.when(j == ns - 1)
        def _():
            g = g_ref[...]
            _, xn, r = _rms(x_ref[...], g)
            dx, dg = _rms_bwd(acc[...], xn, r, g)
            dx_ref[...] = dres_ref[...] + dx
            dg_ref[...] += jnp.broadcast_to(dg, dg_ref.shape)

    act = pl.BlockSpec((None, tm, fs), lambda i, j: (j, i, 0))
    row = pl.BlockSpec((tm, D), lambda i, j: (i, 0))
    wsp = pl.BlockSpec((None, D, fs), lambda i, j: (j, 0, 0))
    return pl.pallas_call(
        body, name=name, grid=(S // tm, ns),
        in_specs=[act, act, wsp, wsp, row, row, pl.BlockSpec((1, D), lambda i, j: (0, 0))],
        out_specs=[row, pl.BlockSpec((8, D), lambda i, j: (0, 0))],
        out_shape=[jax.ShapeDtypeStruct((S, D), F32), jax.ShapeDtypeStruct((8, D), F32)],
        scratch_shapes=[pltpu.VMEM((tm, D), F32)],
        compiler_params=_cparams(),
    )(da, db, wg, wu, dres, x, gain)


def _tn_matmul(a, b, a_spec, b_spec, o_shape, o_spec, grid, name):
    kaxis = len(grid) - 1

    def body(a_ref, b_ref, o_ref):
        @pl.when(pl.program_id(kaxis) == 0)
        def _():
            o_ref[...] = jnp.zeros_like(o_ref)

        o_ref[...] += _dot_tn(a_ref[...].astype(BF16), b_ref[...].astype(BF16))

    return pl.pallas_call(
        body, name=name, grid=grid, in_specs=[a_spec, b_spec], out_specs=o_spec,
        out_shape=jax.ShapeDtypeStruct(o_shape, F32), compiler_params=_cparams(),
    )(a, b)


def _norm_proj(x, gain, w, name):
    S, D = x.shape
    ns, _, n = w.shape
    tm = _rows(S, 512)

    def body(x_ref, g_ref, w_ref, o_ref, h_ref, hs):
        @pl.when(pl.program_id(1) == 0)
        def _():
            h, _, _ = _rms(x_ref[...], g_ref[...])
            hb = h.astype(BF16)
            hs[...] = hb
            h_ref[...] = hb

        o_ref[...] = _dot(hs[...], w_ref[...])

    return pl.pallas_call(
        body, name=name, grid=(S // tm, ns),
        in_specs=[
            pl.BlockSpec((tm, D), lambda i, j: (i, 0)),
            pl.BlockSpec((1, D), lambda i, j: (0, 0)),
            pl.BlockSpec((None, D, n), lambda i, j: (j, 0, 0)),
        ],
        out_specs=[
            pl.BlockSpec((tm, n), lambda i, j: (i, j)),
            pl.BlockSpec((tm, D), lambda i, j: (i, 0)),
        ],
        out_shape=[jax.ShapeDtypeStruct((S, ns * n), F32), jax.ShapeDtypeStruct((S, D), BF16)],
        scratch_shapes=[pltpu.VMEM((tm, D), BF16)],
        compiler_params=_cparams(),
    )(x, gain, w)


def _norm_proj_bwd(dproj, w, dres, x, gain, name):
    S, D = x.shape
    ns, _, n = w.shape
    tm = _rows(S, 512)

    def body(dp_ref, w_ref, dres_ref, x_ref, g_ref, dx_ref, dg_ref, acc):
        i = pl.program_id(0)
        j = pl.program_id(1)

        @pl.when((i == 0) & (j == 0))
        def _():
            dg_ref[...] = jnp.zeros_like(dg_ref)

        @pl.when(j == 0)
        def _():
            acc[...] = jnp.zeros_like(acc)

        acc[...] += _dot_nt(dp_ref[...], w_ref[...])

        @pl.when(j == ns - 1)
        def _():
            g = g_ref[...]
            _, xn, r = _rms(x_ref[...], g)
            dx, dg = _rms_bwd(acc[...], xn, r, g)
            dx_ref[...] = dres_ref[...] + dx
            dg_ref[...] += jnp.broadcast_to(dg, dg_ref.shape)

    return pl.pallas_call(
        body, name=name, grid=(S // tm, ns),
        in_specs=[
            pl.BlockSpec((tm, n), lambda i, j: (i, j)),
            pl.BlockSpec((None, D, n), lambda i, j: (j, 0, 0)),
            pl.BlockSpec((tm, D), lambda i, j: (i, 0)),
            pl.BlockSpec((tm, D), lambda i, j: (i, 0)),
            pl.BlockSpec((1, D), lambda i, j: (0, 0)),
        ],
        out_specs=[
            pl.BlockSpec((tm, D), lambda i, j: (i, 0)),
            pl.BlockSpec((8, D), lambda i, j: (0, 0)),
        ],
        out_shape=[jax.ShapeDtypeStruct((S, D), F32), jax.ShapeDtypeStruct((8, D), F32)],
        scratch_shapes=[pltpu.VMEM((tm, D), F32)],
        compiler_params=_cparams(),
    )(dproj, w, dres, x, gain)


def _out_proj(ycat, w, res, name):
    S, K = ycat.shape
    D = w.shape[1]
    tm = _rows(S, 512)

    def body(y_ref, w_ref, r_ref, o_ref):
        o_ref[...] = r_ref[...] + _dot(y_ref[...], w_ref[...])

    return pl.pallas_call(
        body, name=name, grid=(S // tm,),
        in_specs=[
            pl.BlockSpec((tm, K), lambda i: (i, 0)),
            pl.BlockSpec((K, D), lambda i: (0, 0)),
            pl.BlockSpec((tm, D), lambda i: (i, 0)),
        ],
        out_specs=pl.BlockSpec((tm, D), lambda i: (i, 0)),
        out_shape=jax.ShapeDtypeStruct((S, D), F32),
        compiler_params=_cparams(),
    )(ycat, w, res)


def _out_proj_bwd(dx, w, name):
    S, D = dx.shape
    K = w.shape[0]
    tm = _rows(S, 512)

    def body(d_ref, w_ref, o_ref):
        o_ref[...] = _dot_nt(d_ref[...].astype(BF16), w_ref[...])

    return pl.pallas_call(
        body, name=name, grid=(S // tm,),
        in_specs=[pl.BlockSpec((tm, D), lambda i: (i, 0)), pl.BlockSpec((K, D), lambda i: (0, 0))],
        out_specs=pl.BlockSpec((tm, K), lambda i: (i, 0)),
        out_shape=jax.ShapeDtypeStruct((S, K), F32),
        compiler_params=_cparams(),
    )(dx, w)


CONV_COLS = 256


def _shift_down(z, halo, k, row):
    out = pltpu.roll(z, k, 0)
    for n in range(k):
        out = jnp.where(row == n, halo[8 - k + n:8 - k + n + 1, :], out)
    return out


def _shift_up(g, halo, k, row, ts):
    out = pltpu.roll(g, ts - k, 0)
    for n in range(k):
        out = jnp.where(row == ts - k + n, halo[n:n + 1, :], out)
    return out


def _conv_fwd(proj, cw, cb, nconv, name):
    S = proj.shape[0]
    ncb = nconv // CONV_COLS
    ts = _rows(S, 512)
    hb = ts // 8

    def body(b_ref, c_ref, u_ref, ch_ref, uh_ref, w_ref, bias_ref, o_ref):
        i = pl.program_id(1)
        z = c_ref[...] * u_ref[...]
        halo = jnp.where(i > 0, ch_ref[...] * uh_ref[...], 0.0)
        row = lax.broadcasted_iota(jnp.int32, z.shape, 0)
        w = w_ref[...]
        yc = w[0:1, :] * _shift_down(z, halo, 2, row) + w[1:2, :] * _shift_down(z, halo, 1, row) + w[2:3, :] * z
        o_ref[...] = (b_ref[...] * (yc + bias_ref[...])).astype(BF16)

    def blk(unit):
        return pl.BlockSpec((ts, CONV_COLS), lambda cbi, i: (i, unit * ncb + cbi))

    def prev(unit):
        return pl.BlockSpec((8, CONV_COLS), lambda cbi, i: (jnp.maximum(i * hb - 1, 0), unit * ncb + cbi))

    return pl.pallas_call(
        body, name=name, grid=(ncb, S // ts),
        in_specs=[blk(0), blk(1), blk(2), prev(1), prev(2),
                  pl.BlockSpec((8, CONV_COLS), lambda cbi, i: (0, cbi)),
                  pl.BlockSpec((1, CONV_COLS), lambda cbi, i: (0, cbi))],
        out_specs=pl.BlockSpec((ts, CONV_COLS), lambda cbi, i: (i, cbi)),
        out_shape=jax.ShapeDtypeStruct((S, nconv), BF16),
        compiler_params=_cparams(),
    )(proj, proj, proj, proj, proj, cw, cb)


def _conv_bwd(proj, dy, cw, cb, nconv, name):
    S = proj.shape[0]
    ncb = nconv // CONV_COLS
    ts = _rows(S, 512)
    hb = ts // 8
    nblk = S // ts

    def body(b_ref, c_ref, u_ref, dy_ref, ch_ref, uh_ref, bn_ref, dyn_ref, w_ref, bias_ref,
             db_ref, dc_ref, du_ref, dw_ref):
        i = pl.program_id(1)

        @pl.when(i == 0)
        def _():
            dw_ref[...] = jnp.zeros_like(dw_ref)

        c = c_ref[...]
        u = u_ref[...]
        bg = b_ref[...]
        dy_ = dy_ref[...]
        z = c * u
        halo = jnp.where(i > 0, ch_ref[...] * uh_ref[...], 0.0)
        row = lax.broadcasted_iota(jnp.int32, z.shape, 0)
        w = w_ref[...]
        z2 = _shift_down(z, halo, 2, row)
        z1 = _shift_down(z, halo, 1, row)
        yc = w[0:1, :] * z2 + w[1:2, :] * z1 + w[2:3, :] * z
        db_ref[...] = (dy_ * (yc + bias_ref[...])).astype(BF16)
        g = dy_ * bg
        gnext = jnp.where(i < nblk - 1, dyn_ref[...] * bn_ref[...], 0.0)
        dz = w[2:3, :] * g + w[1:2, :] * _shift_up(g, gnext, 1, row, ts) + w[0:1, :] * _shift_up(g, gnext, 2, row, ts)
        dc_ref[...] = (dz * u).astype(BF16)
        du_ref[...] = (dz * c).astype(BF16)
        r8 = lax.broadcasted_iota(jnp.int32, (8, CONV_COLS), 0)
        sums = [jnp.sum(g * z2, axis=0, keepdims=True), jnp.sum(g * z1, axis=0, keepdims=True),
                jnp.sum(g * z, axis=0, keepdims=True), jnp.sum(g, axis=0, keepdims=True)]
        upd = jnp.zeros((8, CONV_COLS), F32)
        for n, sv in enumerate(sums):
            upd = jnp.where(r8 == n, sv, upd)
        dw_ref[...] += upd

    def blk(unit):
        return pl.BlockSpec((ts, CONV_COLS), lambda cbi, i: (i, unit * ncb + cbi))

    def prev(unit):
        return pl.BlockSpec((8, CONV_COLS), lambda cbi, i: (jnp.maximum(i * hb - 1, 0), unit * ncb + cbi))

    def nxt(unit):
        return pl.BlockSpec((8, CONV_COLS), lambda cbi, i: (jnp.minimum((i + 1) * hb, S // 8 - 1), unit * ncb + cbi))

    o = pl.BlockSpec((ts, CONV_COLS), lambda cbi, i: (i, cbi))
    return pl.pallas_call(
        body, name=name, grid=(ncb, nblk),
        in_specs=[blk(0), blk(1), blk(2), blk(0), prev(1), prev(2), nxt(0), nxt(0),
                  pl.BlockSpec((8, CONV_COLS), lambda cbi, i: (0, cbi)),
                  pl.BlockSpec((1, CONV_COLS), lambda cbi, i: (0, cbi))],
        out_specs=[o, o, o, pl.BlockSpec((8, CONV_COLS), lambda cbi, i: (0, cbi))],
        out_shape=[jax.ShapeDtypeStruct((S, nconv), BF16)] * 3 + [jax.ShapeDtypeStruct((8, nconv), F32)],
        compiler_params=_cparams(),
    )(proj, proj, proj, dy, proj, proj, proj, dy, cw, cb)


def _group_ones(n):
    r = lax.broadcasted_iota(jnp.int32, (n, n), 0) // HEAD_DIM
    c = lax.broadcasted_iota(jnp.int32, (n, n), 1) // HEAD_DIM
    return jnp.where(r == c, 1.0, 0.0).astype(BF16)


def _qk_norm(proj, gain_t, unit0, nsb, scale, name):
    S = proj.shape[0]
    nb = nsb // CONV_COLS
    ts = _rows(S, 512)

    def body(x_ref, g_ref, o_ref):
        x = x_ref[...]
        ss = _split_dot(x * x, _group_ones(CONV_COLS))
        r = lax.rsqrt(ss * (1.0 / HEAD_DIM) + EPS)
        o_ref[...] = ((x * r) * g_ref[...] * scale).astype(BF16)

    return pl.pallas_call(
        body, name=name, grid=(nb, S // ts),
        in_specs=[pl.BlockSpec((ts, CONV_COLS), lambda u, i: (i, unit0 + u)),
                  pl.BlockSpec((1, CONV_COLS), lambda u, i: (0, 0))],
        out_specs=pl.BlockSpec((ts, CONV_COLS), lambda u, i: (i, u)),
        out_shape=jax.ShapeDtypeStruct((S, nsb), BF16),
        compiler_params=_cparams(),
    )(proj, gain_t)


def _qk_norm_bwd(proj, dout, gain_t, unit0, nsb, scale, name):
    S = proj.shape[0]
    nb = nsb // CONV_COLS
    ts = _rows(S, 512)

    def body(x_ref, d_ref, g_ref, dx_ref, dg_ref):
        @pl.when(pl.program_id(1) == 0)
        def _():
            dg_ref[...] = jnp.zeros_like(dg_ref)

        x = x_ref[...]
        g = g_ref[...]
        ones = _group_ones(CONV_COLS)
        ss = _split_dot(x * x, ones)
        r = lax.rsqrt(ss * (1.0 / HEAD_DIM) + EPS)
        xn = x * r
        dh = d_ref[...] * scale
        dxn = dh * g
        m = _split_dot(dxn * xn, ones) * (1.0 / HEAD_DIM)
        dx_ref[...] = (r * (dxn - xn * m)).astype(BF16)
        dg_ref[...] += jnp.broadcast_to(jnp.sum(dh * xn, axis=0, keepdims=True), dg_ref.shape)

    return pl.pallas_call(
        body, name=name, grid=(nb, S // ts),
        in_specs=[pl.BlockSpec((ts, CONV_COLS), lambda u, i: (i, unit0 + u)),
                  pl.BlockSpec((ts, CONV_COLS), lambda u, i: (i, u)),
                  pl.BlockSpec((1, CONV_COLS), lambda u, i: (0, 0))],
        out_specs=[pl.BlockSpec((ts, CONV_COLS), lambda u, i: (i, u)),
                   pl.BlockSpec((8, CONV_COLS), lambda u, i: (0, u))],
        out_shape=[jax.ShapeDtypeStruct((S, nsb), BF16), jax.ShapeDtypeStruct((8, nsb), F32)],
        compiler_params=_cparams(),
    )(proj, dout, gain_t)


Z_CLAMP = 80.0
N_SLOTS = 3
SAT_LIMIT = 120.0


def _head_masks():
    lane = lax.broadcasted_iota(jnp.int32, (1, LANES), 1)
    return [lane < HEAD_DIM, lane >= HEAD_DIM], lane


def _tile_consts(T):
    r_i = lax.broadcasted_iota(jnp.int32, (T, T), 0)
    c_i = lax.broadcasted_iota(jnp.int32, (T, T), 1)
    neg_suffix = jnp.where(r_i >= c_i, -1.0, 0.0).astype(BF16)
    prefix = jnp.where(r_i <= c_i, 1.0, 0.0).astype(BF16)
    return neg_suffix, prefix, c_i < r_i


def _pipeline(n, stages, first_special=False, last_special=False, saturated=None, extra_head=0):
    depth = len(stages)
    head = (depth if first_special else depth - 1) + extra_head
    off = 0 if last_special else 1
    for m in range(head):
        for k in reversed(range(min(m, depth - 1) + 1)):
            stages[k](m - k, (m - k) % N_SLOTS, first_special and m == k)

    def trip(m, u):
        for k in reversed(range(depth)):
            stages[k](m - k, (head + u - k) % N_SLOTS, False)

    def group(g, carry):
        for u in range(N_SLOTS):
            trip(head + g * N_SLOTS + u, u)
        return carry

    count = n - 1 + off - head
    full = count // N_SLOTS
    if saturated is None:
        lax.fori_loop(0, full, group, 0)
        go_on, done = True, n
    else:
        def more(state):
            return (state[0] < full) & (state[1] == 0)

        def step(state):
            group(state[0], 0)
            return state[0] + 1, saturated().astype(jnp.int32)

        groups, stop = lax.while_loop(more, step, (jnp.int32(0), saturated().astype(jnp.int32)))
        go_on = stop == 0
        done = jnp.where(go_on, n, head - depth + 1 + N_SLOTS * groups)
    for r in range(N_SLOTS):
        @pl.when((count - full * N_SLOTS == r) & go_on)
        def _(r=r):
            for u in range(r):
                trip(head + full * N_SLOTS + u, u)
            for e in range(depth - off):
                for k in reversed(range(e + off, depth)):
                    t = n - 1 - (k - e - off)
                    stages[k](t, (head + r + e - k) % N_SLOTS, last_special and k == e + off)
    return done


def _sweep(n, stages, finish, first_special=False, last_special=False, saturated=None, extra_head=0):
    depth = len(stages)
    least = (depth if first_special else depth - 1) + extra_head + (1 if last_special else 0)
    for short in range(1, least):
        @pl.when(n == short)
        def _(short=short):
            for m in range(short + depth - 1):
                for k in reversed(range(depth)):
                    t = m - k
                    if 0 <= t < short:
                        stages[k](t, t % N_SLOTS, (first_special and t == 0) or (last_special and t == short - 1))
            finish(short)

    @pl.when(n >= least)
    def _():
        finish(_pipeline(n, stages, first_special, last_special, saturated, extra_head))


def _attn_fwd(q, k, v, name):
    S, nsb = q.shape
    T = ATT_TILE
    hp = nsb // LANES
    nb = S // T
    assert nb <= HEAD_DIM

    def body(q_ref, k_ref, v_ref, y_ref, cs_ref, c_ref, acc, z_st, inc_st):
        i = pl.program_id(1)
        masks, lane = _head_masks()
        qv = q_ref[...]
        qm = [jnp.where(m, qv, jnp.zeros_like(qv)) for m in masks]
        neg_suffix, _, causal = _tile_consts(T)
        c_ref[...] = jnp.zeros_like(c_ref)
        acc[...] = jnp.zeros_like(acc)
        cs_ref[...] = jnp.zeros_like(cs_ref)

        def blk(ref, j):
            return ref[pl.ds(pl.multiple_of(j * T, T), T), :]

        def scores(t, slot, diag):
            kj = blk(k_ref, i - t)
            for h in range(2):
                z_st[slot, h] = jnp.minimum(_dot_nt(qm[h], kj), Z_CLAMP)

        def suffix_sums(t, slot, diag):
            for h in range(2):
                sp = jnp.log(1.0 + jnp.exp(z_st[slot, h]))
                if diag:
                    sp = jnp.where(causal, sp, 0.0)
                inc_st[slot, h] = _dot(sp.astype(BF16), neg_suffix)

        def weights(t, slot, diag):
            vj = blk(v_ref, i - t)
            for h in range(2):
                inc = inc_st[slot, h]
                c = c_ref[h]
                a = jnp.exp(z_st[slot, h] + inc + c)
                if diag:
                    a = jnp.where(causal, a, 0.0)
                upd = _dot(a.astype(BF16), vj)
                acc[...] += jnp.where(masks[h], upd, 0.0)
                cs_ref[...] = jnp.where(lane == i - t + HEAD_DIM * h, c, cs_ref[...])
                c_ref[h] = c + inc[:, 0:1]

        stages = [scores, suffix_sums, weights]

        def saturated():
            return jnp.max(c_ref[...]) < -SAT_LIMIT

        def note(used):
            cs_ref[...] = jnp.where(lane == LANES - 1, jnp.asarray(used).astype(F32), cs_ref[...])

        _sweep(i + 1, stages, note, first_special=True, saturated=saturated, extra_head=1)
        y_ref[...] = acc[...].astype(BF16)

    return pl.pallas_call(
        body, name=name, grid=(hp, nb),
        in_specs=[pl.BlockSpec((T, LANES), lambda p, i: (i, p)),
                  pl.BlockSpec((S, LANES), lambda p, i: (0, p)),
                  pl.BlockSpec((S, LANES), lambda p, i: (0, p))],
        out_specs=[pl.BlockSpec((T, LANES), lambda p, i: (i, p)),
                   pl.BlockSpec((None, T, LANES), lambda p, i: (p, i, 0))],
        out_shape=[jax.ShapeDtypeStruct((S, nsb), BF16), jax.ShapeDtypeStruct((hp, S, LANES), F32)],
        scratch_shapes=[pltpu.VMEM((2, T, 1), F32), pltpu.VMEM((T, LANES), F32),
                        pltpu.VMEM((N_SLOTS, 2, T, T), F32), pltpu.VMEM((N_SLOTS, 2, T, T), F32)],
        compiler_params=_cparams(),
    )(q, k, v)


def _attn_bwd(q, k, v, dy, col0, carry, name):
    S, nsb = q.shape
    T = ATT_TILE
    hp = nsb // LANES
    nb = S // T

    def body(q_ref, k_ref, v_ref, dy_ref, cs_ref, dq_ref, dk_ref, dv_ref, e_ref, acc,
             z_st, da_st, b_st, inc_st, a_st, e_st, p_st):
        i = pl.program_id(1)

        @pl.when(i == 0)
        def _():
            dk_ref[...] = jnp.zeros_like(dk_ref)
            dv_ref[...] = jnp.zeros_like(dv_ref)

        masks, lane = _head_masks()
        qv = q_ref[...]
        dyb = dy_ref[...].astype(BF16)
        qm = [jnp.where(m, qv, jnp.zeros_like(qv)) for m in masks]
        dym = [jnp.where(m, dyb, jnp.zeros_like(dyb)) for m in masks]
        neg_suffix, prefix, causal = _tile_consts(T)
        e_ref[...] = jnp.zeros_like(e_ref)
        acc[...] = jnp.zeros_like(acc)

        def blk(ref, j):
            return ref[pl.ds(pl.multiple_of(j * T, T), T), :]

        used = jnp.max(jnp.where(lane == LANES - 1, cs_ref[...], 0.0)).astype(jnp.int32)
        n = jnp.clip(used, 1, i + 1)
        first = i + 1 - n

        def scores(t, slot, diag):
            kj = blk(k_ref, first + t)
            vj = blk(v_ref, first + t)
            for h in range(2):
                z_st[slot, h] = jnp.minimum(_dot_nt(qm[h], kj), Z_CLAMP)
                da_st[slot, h] = _dot_nt(dym[h], vj)

        def suffix_sums(t, slot, diag):
            for h in range(2):
                u = jnp.exp(z_st[slot, h])
                w = 1.0 + u
                b_st[slot, h] = u / w
                sp = jnp.log(w)
                if diag:
                    sp = jnp.where(causal, sp, 0.0)
                inc_st[slot, h] = _dot(sp.astype(BF16), neg_suffix)

        def probs(t, slot, diag):
            csv = cs_ref[...]
            for h in range(2):
                c = jnp.sum(jnp.where(lane == first + t + HEAD_DIM * h, csv, 0.0), axis=-1, keepdims=True)
                a = jnp.exp(z_st[slot, h] + inc_st[slot, h] + c)
                if diag:
                    a = jnp.where(causal, a, 0.0)
                a_st[slot, h] = a.astype(BF16)
                e = a * da_st[slot, h]
                e_st[slot, h] = e
                p_st[slot, h] = _dot(e.astype(BF16), prefix)

        def grads(t, slot, diag):
            kj = blk(k_ref, first + t)
            off = pl.multiple_of((first + t) * T, T)
            for h in range(2):
                p = p_st[slot, h]
                dz = e_st[slot, h] - b_st[slot, h] * (e_ref[h] + p)
                if diag:
                    dz = jnp.where(causal, dz, 0.0)
                dzb = dz.astype(BF16)
                acc[...] += jnp.where(masks[h], _dot(dzb, kj), 0.0)
                dk_ref[pl.ds(off, T), :] += _dot_tn(dzb, qm[h])
                dv_ref[pl.ds(off, T), :] += _dot_tn(a_st[slot, h], dym[h])
                e_ref[h] += p[:, T - 1:T]

        stages = [scores, suffix_sums, probs, grads]

        _sweep(n, stages, lambda done: None, last_special=True)
        dq_ref[...] = acc[...]

    return pl.pallas_call(
        body, name=name, grid=(hp, nb),
        in_specs=[pl.BlockSpec((T, LANES), lambda p, i: (i, p)),
                  pl.BlockSpec((S, LANES), lambda p, i: (0, p)),
                  pl.BlockSpec((S, LANES), lambda p, i: (0, p)),
                  pl.BlockSpec((T, LANES), lambda p, i: (i, col0 + p)),
                  pl.BlockSpec((None, T, LANES), lambda p, i: (p, i, 0))],
        out_specs=[pl.BlockSpec((T, LANES), lambda p, i: (i, p)),
                   pl.BlockSpec((S, LANES), lambda p, i: (0, p)),
                   pl.BlockSpec((S, LANES), lambda p, i: (0, p))],
        out_shape=[jax.ShapeDtypeStruct((S, nsb), F32)] * 3,
        scratch_shapes=[pltpu.VMEM((2, T, 1), F32), pltpu.VMEM((T, LANES), F32)]
        + [pltpu.VMEM((N_SLOTS, 2, T, T), dt) for dt in (F32, F32, F32, F32, BF16, F32, F32)],
        compiler_params=_cparams(),
    )(q, k, v, dy, carry)


def _ple(x, p, tgt, gain, wpg, wpp, name):
    S, D = x.shape
    P = p.shape[1]
    ns, _, nc = wpp.shape
    tm = _rows(S, 256)

    def body(x_ref, p_ref, t_ref, g_ref, wpg_ref, wpp_ref, dx_ref, h_ref, du_ref, dpp_ref, loss_ref, dg_ref):
        @pl.when(pl.program_id(0) == 0)
        def _():
            loss_ref[...] = jnp.zeros_like(loss_ref)
            dg_ref[...] = jnp.zeros_like(dg_ref)

        x_ = x_ref[...]
        g = g_ref[...]
        h, xn, r = _rms(x_, g)
        hb = h.astype(BF16)
        h_ref[...] = hb
        gate = jax.nn.sigmoid(_dot(hb, wpg_ref[...]))
        pb = p_ref[...].astype(BF16)
        pp = jnp.concatenate([_dot(pb, wpp_ref[n]) for n in range(ns)], axis=1)
        err = (x_ + gate * pp) - t_ref[...]
        loss_ref[...] += (0.5 / D) * jnp.sum(err * err)
        dy = err * (1.0 / D)
        du = ((dy * pp) * (gate * (1.0 - gate))).astype(BF16)
        du_ref[...] = du
        dpp_ref[...] = (dy * gate).astype(BF16)
        dx, dg = _rms_bwd(_dot_nt(du, wpg_ref[...]), xn, r, g)
        dx_ref[...] = dy + dx
        dg_ref[...] += jnp.broadcast_to(dg, dg_ref.shape)

    row = pl.BlockSpec((tm, D), lambda i: (i, 0))
    return pl.pallas_call(
        body, name=name, grid=(S // tm,),
        in_specs=[row, pl.BlockSpec((tm, P), lambda i: (i, 0)), row,
                  pl.BlockSpec((1, D), lambda i: (0, 0)),
                  pl.BlockSpec((D, D), lambda i: (0, 0)),
                  pl.BlockSpec((ns, P, nc), lambda i: (0, 0, 0))],
        out_specs=[row, row, row, row,
                   pl.BlockSpec((8, LANES), lambda i: (0, 0)),
                   pl.BlockSpec((8, D), lambda i: (0, 0))],
        out_shape=[jax.ShapeDtypeStruct((S, D), F32)] + [jax.ShapeDtypeStruct((S, D), BF16)] * 3
        + [jax.ShapeDtypeStruct((8, LANES), F32), jax.ShapeDtypeStruct((8, D), F32)],
        compiler_params=_cparams(),
    )(x, p, tgt, gain, wpg, wpp)


def _elementwise(fn, ins, n_out, name):
    R, C = ins[0].shape
    tr = _rows(R, 512)

    def body(*refs):
        outs = fn(*[r[...] for r in refs[:len(ins)]])
        for o_ref, o in zip(refs[len(ins):], outs):
            o_ref[...] = o

    spec = pl.BlockSpec((tr, C), lambda i: (i, 0))
    return pl.pallas_call(
        body, name=name, grid=(R // tr,), in_specs=[spec] * len(ins), out_specs=[spec] * n_out,
        out_shape=[jax.ShapeDtypeStruct((R, C), F32)] * n_out, compiler_params=_cparams(),
    )(*ins)


def _adamw(w, g, m, v):
    m = ADAM_B1 * m + (1.0 - ADAM_B1) * g
    v = ADAM_B2 * v + (1.0 - ADAM_B2) * jnp.square(g)
    m_hat = m / (1.0 - ADAM_B1 ** ADAM_STEP)
    v_hat = v / (1.0 - ADAM_B2 ** ADAM_STEP)
    delta = -ADAM_LR * (m_hat / (jnp.sqrt(v_hat) + ADAM_EPS) + ADAM_WD * w)
    return delta, m, v


def _place():
    x, y, c = lax.axis_index("x"), lax.axis_index("y"), lax.axis_index("c")
    chips = [(1 - x, y), (x, 1 - y), (1 - x, 1 - y)]
    return x, y, c, chips


def _half(ref, c, axis_rows):
    n = ref.shape[-2]
    start = pl.multiple_of(c * (n // 2), 8)
    idx = (slice(None),) * (len(ref.shape) - 2) + (pl.ds(start, n // 2), slice(None))
    return ref.at[idx]


def _gather_weights(shards, small, name):
    n = len(shards)

    def body(*refs):
        ins, small_in = refs[:n], refs[n]
        outs, small_out = refs[n + 1:2 * n + 1], refs[2 * n + 1]
        lsem, lrsem, ssem, rsem, sm_s, sm_r = refs[2 * n + 2:]
        x, y, c, chips = _place()
        j = 2 * x + y
        sib = (x, y, 1 - c)

        local = [pltpu.make_async_remote_copy(
            src_ref=ins[a], dst_ref=outs[a].at[j], send_sem=lsem.at[a], recv_sem=lrsem.at[a],
            device_id=sib, device_id_type=MESH) for a in range(n)]
        for cp in local:
            cp.start()
        small_out[j] = small_in[...]
        small_cp = [pltpu.make_async_remote_copy(
            src_ref=small_in, dst_ref=small_out.at[j], send_sem=sm_s.at[k], recv_sem=sm_r.at[k],
            device_id=(*chip, c), device_id_type=MESH) for k, chip in enumerate(chips)]
        for cp in small_cp:
            cp.start()

        def ici(a, k, chip, jj, dev):
            return pltpu.make_async_remote_copy(
                src_ref=_half(ins[a], c, True) if dev is not None else _half(outs[a].at[jj], c, True),
                dst_ref=_half(outs[a].at[jj], c, True),
                send_sem=ssem.at[a, k], recv_sem=rsem.at[a, k],
                device_id=dev if dev is not None else (*chip, c), device_id_type=MESH)

        first = []
        for a in range(n):
            for k, chip in enumerate(chips):
                cp = ici(a, k, chip, j, (*chip, c))
                cp.start()
                first.append(cp)
        passed = []
        for a in range(n):
            for k, chip in enumerate(chips):
                jj = 2 * chip[0] + chip[1]
                ici(a, k, chip, jj, None).wait_recv()
                fw = pltpu.make_async_remote_copy(
                    src_ref=_half(outs[a].at[jj], c, True), dst_ref=_half(outs[a].at[jj], c, True),
                    send_sem=ssem.at[a, 3 + k], recv_sem=rsem.at[a, 3 + k], device_id=sib, device_id_type=MESH)
                fw.start()
                passed.append(fw)
        for a in range(n):
            for k, chip in enumerate(chips):
                jj = 2 * chip[0] + chip[1]
                pltpu.make_async_remote_copy(
                    src_ref=_half(outs[a].at[jj], 1 - c, True), dst_ref=_half(outs[a].at[jj], 1 - c, True),
                    send_sem=ssem.at[a, 3 + k], recv_sem=rsem.at[a, 3 + k], device_id=sib,
                    device_id_type=MESH).wait_recv()
        for cp in small_cp:
            cp.wait()
        for cp in first + passed:
            cp.wait_send()
        for cp in local:
            cp.wait()

    return pl.pallas_call(
        body, name=name,
        in_specs=[HBM] * n + [VMEM_WHOLE],
        out_specs=[HBM] * n + [VMEM_WHOLE],
        out_shape=[jax.ShapeDtypeStruct((N_SHARDS,) + s.shape, s.dtype) for s in shards]
        + [jax.ShapeDtypeStruct((N_SHARDS,) + small.shape, small.dtype)],
        scratch_shapes=[pltpu.SemaphoreType.DMA((n,)), pltpu.SemaphoreType.DMA((n,)),
                        pltpu.SemaphoreType.DMA((n, 6)), pltpu.SemaphoreType.DMA((n, 6)),
                        pltpu.SemaphoreType.DMA((3,)), pltpu.SemaphoreType.DMA((3,))],
    )(*shards, small)


def _swap_halves(grads, small, name):
    n = len(grads)

    def body(*refs):
        ins, small_in = refs[:n], refs[n]
        outs, small_out = refs[n + 1:2 * n + 1], refs[2 * n + 1]
        buf, ssem, rsem, sm_s, sm_r = refs[2 * n + 2:]
        x, y, c, _ = _place()
        me = 4 * x + 2 * y + c
        sib = (x, y, 1 - c)
        cps = [pltpu.make_async_remote_copy(
            src_ref=_half(ins[a], 1 - c, True), dst_ref=outs[a], send_sem=ssem.at[a], recv_sem=rsem.at[a],
            device_id=sib, device_id_type=MESH) for a in range(n)]
        for cp in cps:
            cp.start()
        buf[me] = small_in[...]
        peers = [(fx, fy, fc) for fx in (0, 1) for fy in (0, 1) for fc in (0, 1)][1:]
        sm = []
        for k, (fx, fy, fc) in enumerate(peers):
            dev = (1 - x if fx else x, 1 - y if fy else y, 1 - c if fc else c)
            cp = pltpu.make_async_remote_copy(
                src_ref=small_in, dst_ref=buf.at[me], send_sem=sm_s.at[k], recv_sem=sm_r.at[k],
                device_id=dev, device_id_type=MESH)
            cp.start()
            sm.append(cp)
        for cp in sm:
            cp.wait()
        tot = buf[0]
        for d in range(1, N_DEV):
            tot = tot + buf[d]
        small_out[...] = tot
        for cp in cps:
            cp.wait()

    return pl.pallas_call(
        body, name=name,
        in_specs=[HBM] * n + [VMEM_WHOLE],
        out_specs=[HBM] * n + [VMEM_WHOLE],
        out_shape=[jax.ShapeDtypeStruct((g.shape[0], g.shape[1] // 2, g.shape[2]), F32) for g in grads]
        + [jax.ShapeDtypeStruct(small.shape, F32)],
        scratch_shapes=[pltpu.VMEM((N_DEV,) + small.shape, F32),
                        pltpu.SemaphoreType.DMA((n,)), pltpu.SemaphoreType.DMA((n,)),
                        pltpu.SemaphoreType.DMA((N_DEV - 1,)), pltpu.SemaphoreType.DMA((N_DEV - 1,))],
    )(*grads, small)


def _swap_grad_halves(grads, name):
    n = len(grads)

    def body(*refs):
        ins, outs = refs[:n], refs[n:2 * n]
        ssem, rsem = refs[2 * n:]
        x, y, c, _ = _place()
        cps = [pltpu.make_async_remote_copy(
            src_ref=_half(ins[a], 1 - c, True), dst_ref=outs[a], send_sem=ssem.at[a], recv_sem=rsem.at[a],
            device_id=(x, y, 1 - c), device_id_type=MESH) for a in range(n)]
        for cp in cps:
            cp.start()
        for cp in cps:
            cp.wait()

    return pl.pallas_call(
        body, name=name, in_specs=[HBM] * n, out_specs=[HBM] * n,
        out_shape=[jax.ShapeDtypeStruct((g.shape[0], g.shape[1] // 2, g.shape[2]), F32) for g in grads],
        scratch_shapes=[pltpu.SemaphoreType.DMA((n,)), pltpu.SemaphoreType.DMA((n,))],
    )(*grads)


def _chip_sum(g, recv, core, name):
    ns, R, C = g.shape
    r2 = R // 2
    tr = _rows(r2, 512)
    nrb = r2 // tr

    def body(core_ref, g_ref, r_ref, o_ref, ob_ref):
        s = g_ref[...] + r_ref[...]
        o_ref[...] = s
        ob_ref[...] = s.astype(BF16)

    out = pl.BlockSpec((None, tr, C), lambda s, i, cr: (s, i, 0))
    return pl.pallas_call(
        body, name=name,
        grid_spec=pltpu.PrefetchScalarGridSpec(
            num_scalar_prefetch=1, grid=(ns, nrb),
            in_specs=[pl.BlockSpec((None, tr, C), lambda s, i, cr: (s, cr[0] * nrb + i, 0)), out],
            out_specs=[out, out]),
        out_shape=[jax.ShapeDtypeStruct((ns, r2, C), F32), jax.ShapeDtypeStruct((ns, r2, C), BF16)],
        compiler_params=_cparams(),
    )(core, g, recv)


def _shard_sum(csum, got, place, name):
    _, r2, C = csum.shape
    tr = _rows(r2, 512)
    nrb = r2 // tr

    def body(place_ref, c_ref, g0_ref, g1_ref, g2_ref, o_ref):
        o_ref[...] = ((c_ref[...] + g0_ref[...].astype(F32)) + g1_ref[...].astype(F32)) + g2_ref[...].astype(F32)

    def got_spec(k):
        return pl.BlockSpec((None, tr, C), lambda i, pr: (k, i, 0))

    return pl.pallas_call(
        body, name=name,
        grid_spec=pltpu.PrefetchScalarGridSpec(
            num_scalar_prefetch=1, grid=(nrb,),
            in_specs=[pl.BlockSpec((None, tr, C), lambda i, pr: (pr[0], i, 0)), got_spec(0), got_spec(1), got_spec(2)],
            out_specs=pl.BlockSpec((tr, C), lambda i, pr: (pr[1] * nrb + i, 0))),
        out_shape=jax.ShapeDtypeStruct((2 * r2, C), F32),
        compiler_params=_cparams(),
    )(place, csum, got, got, got)


def _scatter_chip_sums(csums, name):
    n = len(csums)

    def body(*refs):
        ins, outs = refs[:n], refs[n:2 * n]
        ssem, rsem = refs[2 * n:]
        x, y, c, chips = _place()
        cps = []
        for a in range(n):
            for k, chip in enumerate(chips):
                jj = 2 * chip[0] + chip[1]
                cp = pltpu.make_async_remote_copy(
                    src_ref=ins[a].at[jj], dst_ref=outs[a].at[k], send_sem=ssem.at[a, k], recv_sem=rsem.at[a, k],
                    device_id=(*chip, c), device_id_type=MESH)
                cp.start()
                cps.append(cp)
        for cp in cps:
            cp.wait()

    return pl.pallas_call(
        body, name=name, in_specs=[HBM] * n, out_specs=[HBM] * n,
        out_shape=[jax.ShapeDtypeStruct((3,) + g.shape[1:], g.dtype) for g in csums],
        scratch_shapes=[pltpu.SemaphoreType.DMA((n, 3)), pltpu.SemaphoreType.DMA((n, 3))],
    )(*csums)


def _gather_ride(shards):
    n = len(shards)

    def make(ins, outs, sems):
        lsem, lrsem, ssem, rsem = sems
        x, y, c, chips = _place()
        j = 2 * x + y
        cps = [pltpu.make_async_remote_copy(
            src_ref=ins[a], dst_ref=outs[a].at[j], send_sem=lsem.at[a], recv_sem=lrsem.at[a],
            device_id=(x, y, 1 - c), device_id_type=MESH) for a in range(n)]
        for a in range(n):
            for k, chip in enumerate(chips):
                cps.append(pltpu.make_async_remote_copy(
                    src_ref=_half(ins[a], c, True), dst_ref=_half(outs[a].at[j], c, True),
                    send_sem=ssem.at[a, k], recv_sem=rsem.at[a, k], device_id=(*chip, c), device_id_type=MESH))
        return cps

    return _Ride(shards, [jax.ShapeDtypeStruct((N_SHARDS,) + s.shape, s.dtype) for s in shards],
                 [pltpu.SemaphoreType.DMA((n,)), pltpu.SemaphoreType.DMA((n,)),
                  pltpu.SemaphoreType.DMA((n, 3)), pltpu.SemaphoreType.DMA((n, 3))], make)


def _forward_halves(gathered, name):
    n = len(gathered)

    def body(*refs):
        outs = refs[n:2 * n]
        ssem, rsem = refs[2 * n:]
        x, y, c, chips = _place()
        cps = []
        for a in range(n):
            for k, chip in enumerate(chips):
                part = _half(outs[a].at[2 * chip[0] + chip[1]], c, True)
                cps.append(pltpu.make_async_remote_copy(
                    src_ref=part, dst_ref=part, send_sem=ssem.at[a, k], recv_sem=rsem.at[a, k],
                    device_id=(x, y, 1 - c), device_id_type=MESH))
        for cp in cps:
            cp.start()
        for cp in cps:
            cp.wait()

    return pl.pallas_call(
        body, name=name, in_specs=[HBM] * n, out_specs=[HBM] * n,
        out_shape=[jax.ShapeDtypeStruct(g.shape, g.dtype) for g in gathered],
        input_output_aliases={a: a for a in range(n)},
        scratch_shapes=[pltpu.SemaphoreType.DMA((n, 3)), pltpu.SemaphoreType.DMA((n, 3))],
    )(*gathered)


def _scatter_ride(csums):
    n = len(csums)

    def make(ins, outs, sems):
        ssem, rsem = sems
        x, y, c, chips = _place()
        return [pltpu.make_async_remote_copy(
            src_ref=ins[a].at[2 * chip[0] + chip[1]], dst_ref=outs[a].at[k], send_sem=ssem.at[a, k],
            recv_sem=rsem.at[a, k], device_id=(*chip, c), device_id_type=MESH)
            for a in range(n) for k, chip in enumerate(chips)]

    return _Ride(csums, [jax.ShapeDtypeStruct((3,) + g.shape[1:], g.dtype) for g in csums],
                 [pltpu.SemaphoreType.DMA((n, 3)), pltpu.SemaphoreType.DMA((n, 3))], make)


def _join_halves(fulls, name):
    n = len(fulls)

    def body(*refs):
        outs = refs[n:2 * n]
        ssem, rsem = refs[2 * n:]
        x, y, c, _ = _place()
        cps = [pltpu.make_async_remote_copy(
            src_ref=_half(outs[a], c, True), dst_ref=_half(outs[a], c, True), send_sem=ssem.at[a],
            recv_sem=rsem.at[a], device_id=(x, y, 1 - c), device_id_type=MESH) for a in range(n)]
        for cp in cps:
            cp.start()
        for cp in cps:
            cp.wait()

    return pl.pallas_call(
        body, name=name, in_specs=[HBM] * n, out_specs=[HBM] * n,
        out_shape=[jax.ShapeDtypeStruct(f.shape, F32) for f in fulls],
        input_output_aliases={a: a for a in range(n)},
        scratch_shapes=[pltpu.SemaphoreType.DMA((n,))] * 2,
    )(*fulls)


def _pad_rows(a, rows, cols):
    return jnp.pad(a, ((0, rows - a.shape[0]), (0, cols - a.shape[1])))


def kernel(x, p, ffn1_norm, ffn1_w_gate, ffn1_w_up, ffn1_w_down, mix_norm, w_in, conv_w, conv_b, q_norm, k_norm, w_out, ffn2_norm, ffn2_w_gate, ffn2_w_up, ffn2_w_down, ple_norm, ple_w_gate, ple_w_proj, loss_target, m_ffn1_norm, m_ffn1_w_gate, m_ffn1_w_up, m_ffn1_w_down, m_mix_norm, m_w_in, m_conv_w, m_conv_b, m_q_norm, m_k_norm, m_w_out, m_ffn2_norm, m_ffn2_w_gate, m_ffn2_w_up, m_ffn2_w_down, m_ple_norm, m_ple_w_gate, m_ple_w_proj, v_ffn1_norm, v_ffn1_w_gate, v_ffn1_w_up, v_ffn1_w_down, v_mix_norm, v_w_in, v_conv_w, v_conv_b, v_q_norm, v_k_norm, v_w_out, v_ffn2_norm, v_ffn2_w_gate, v_ffn2_w_up, v_ffn2_w_down, v_ple_norm, v_ple_w_gate, v_ple_w_proj):
    big = dict(ffn1_w_gate=ffn1_w_gate, ffn1_w_up=ffn1_w_up, ffn1_w_down=ffn1_w_down, w_in=w_in, w_out=w_out,
               ffn2_w_gate=ffn2_w_gate, ffn2_w_up=ffn2_w_up, ffn2_w_down=ffn2_w_down,
               ple_w_gate=ple_w_gate, ple_w_proj=ple_w_proj)
    big_m = dict(ffn1_w_gate=m_ffn1_w_gate, ffn1_w_up=m_ffn1_w_up, ffn1_w_down=m_ffn1_w_down, w_in=m_w_in,
                 w_out=m_w_out, ffn2_w_gate=m_ffn2_w_gate, ffn2_w_up=m_ffn2_w_up, ffn2_w_down=m_ffn2_w_down,
                 ple_w_gate=m_ple_w_gate, ple_w_proj=m_ple_w_proj)
    big_v = dict(ffn1_w_gate=v_ffn1_w_gate, ffn1_w_up=v_ffn1_w_up, ffn1_w_down=v_ffn1_w_down, w_in=v_w_in,
                 w_out=v_w_out, ffn2_w_gate=v_ffn2_w_gate, ffn2_w_up=v_ffn2_w_up, ffn2_w_down=v_ffn2_w_down,
                 ple_w_gate=v_ple_w_gate, ple_w_proj=v_ple_w_proj)
    names = list(big)
    xs = x[0]
    ps = p[0, 0]
    tgt = loss_target[0]
    S, D = xs.shape
    nconv = conv_b.shape[1]
    nsb = D - nconv
    cwl = conv_w.shape[2]
    jchip = 2 * lax.axis_index("x") + lax.axis_index("y")
    core = lax.axis_index("c")

    early, late = names[:3], names[3:]
    assert all(k.startswith("ffn1") for k in early)
    shards = {k: big[k][0].astype(BF16) for k in names}
    gathered = _gather_weights([shards[k] for k in early], _pad_rows(conv_w[0], 8, LANES), "gather_weights")
    W = dict(zip(early, gathered[:-1]))
    cw_full = jnp.transpose(gathered[-1][:, :, :cwl], (1, 0, 2)).reshape(8, N_SHARDS * cwl)
    qg = jnp.tile(q_norm, (1, CONV_COLS // HEAD_DIM))
    kg = jnp.tile(k_norm, (1, CONV_COLS // HEAD_DIM))
    n_units = nconv // CONV_COLS

    x1, h1, a1, b1, *landed = _ffn_fwd(xs, ffn1_norm, W["ffn1_w_gate"], W["ffn1_w_up"], W["ffn1_w_down"], "ffn1_fwd",
                                        ride=_gather_ride([shards[k] for k in late]))
    W.update(zip(late, _forward_halves(landed, "gather_forward")))
    wout_full = W["w_out"].reshape(-1, D)
    wpg_full = W["ple_w_gate"].reshape(-1, D)
    proj, h2 = _norm_proj(x1, mix_norm, W["w_in"], "mix_in_proj")
    y_conv = _conv_fwd(proj, cw_full, conv_b, nconv, "conv_fwd")
    qs = _qk_norm(proj, qg, 3 * n_units, nsb, HEAD_DIM ** -0.5, "q_norm_fwd")
    kh = _qk_norm(proj, kg, 4 * n_units, nsb, 1.0, "k_norm_fwd")
    vb = proj[:, 3 * nconv + 2 * nsb:].astype(BF16)
    y_sb, carry = _attn_fwd(qs, kh, vb, "attn_fwd")
    ycat = jnp.concatenate([y_conv, y_sb], axis=1)
    x2 = _out_proj(ycat, wout_full, x1, "mix_out_proj")
    x3, h3, a3, b3 = _ffn_fwd(x2, ffn2_norm, W["ffn2_w_gate"], W["ffn2_w_up"], W["ffn2_w_down"], "ffn2_fwd")

    dx3, h4, du4, dpp, loss_blk, dg_ple = _ple(x3, ps, tgt, ple_norm, wpg_full, W["ple_w_proj"], "ple_loss")
    G = {}
    tk = _rows(S, 2048)
    nk = S // tk
    kd = wpg_full.shape[0] // N_SHARDS
    G["ple_w_gate"] = _tn_matmul(
        h4, du4, pl.BlockSpec((tk, kd), lambda m, k: (k, m)), pl.BlockSpec((tk, D), lambda m, k: (k, 0)),
        (N_SHARDS, kd, D), pl.BlockSpec((None, kd, D), lambda m, k: (m, 0, 0)), (N_SHARDS, nk), "ple_w_gate_grad")
    P = ps.shape[1]
    npp = D // N_SHARDS
    G["ple_w_proj"] = _tn_matmul(
        ps, dpp, pl.BlockSpec((tk, P), lambda m, k: (k, 0)), pl.BlockSpec((tk, npp), lambda m, k: (k, m)),
        (N_SHARDS, P, npp), pl.BlockSpec((None, P, npp), lambda m, k: (m, 0, 0)), (N_SHARDS, nk), "ple_w_proj_grad")

    def ffn_grads(pre, h, s, da, db, dy):
        fs = s.shape[2]
        hs = pl.BlockSpec((tk, D), lambda m, k: (k, 0))
        ss = pl.BlockSpec((None, tk, fs), lambda m, k: (m, k, 0))
        G[pre + "_w_gate"] = _tn_matmul(h, da, hs, ss, (N_SHARDS, D, fs),
                                        pl.BlockSpec((None, D, fs), lambda m, k: (m, 0, 0)), (N_SHARDS, nk), pre + "_w_gate_grad")
        G[pre + "_w_up"] = _tn_matmul(h, db, hs, ss, (N_SHARDS, D, fs),
                                      pl.BlockSpec((None, D, fs), lambda m, k: (m, 0, 0)), (N_SHARDS, nk), pre + "_w_up_grad")
        G[pre + "_w_down"] = _tn_matmul(s, dy, ss, hs, (N_SHARDS, fs, D),
                                        pl.BlockSpec((None, fs, D), lambda m, k: (m, 0, 0)), (N_SHARDS, nk), pre + "_w_down_grad")

    s3, da3, db3, dy3 = _ffn_bwd_act(dx3, a3, b3, W["ffn2_w_down"], "ffn2_bwd_act")
    dx2, dg_ffn2 = _ffn_bwd_in(da3, db3, W["ffn2_w_gate"], W["ffn2_w_up"], dx3, x2, ffn2_norm, "ffn2_bwd_in")
    ffn_grads("ffn2", h3, s3, da3, db3, dy3)

    dycat = _out_proj_bwd(dx2, wout_full, "mix_out_proj_bwd")
    ko = wout_full.shape[0] // N_SHARDS
    G["w_out"] = _tn_matmul(
        ycat, dx2, pl.BlockSpec((tk, ko), lambda m, k: (k, m)), pl.BlockSpec((tk, D), lambda m, k: (k, 0)),
        (N_SHARDS, ko, D), pl.BlockSpec((None, ko, D), lambda m, k: (m, 0, 0)), (N_SHARDS, nk), "w_out_grad")
    db_, dc_, du_, dwb = _conv_bwd(proj, dycat, cw_full, conv_b, nconv, "conv_bwd")
    dqs, dkh, dv = _attn_bwd(qs, kh, vb, dycat, nconv // LANES, carry, "attn_bwd")
    dq, dg_q = _qk_norm_bwd(proj, dqs, qg, 3 * n_units, nsb, HEAD_DIM ** -0.5, "q_norm_bwd")
    dk, dg_k = _qk_norm_bwd(proj, dkh, kg, 4 * n_units, nsb, 1.0, "k_norm_bwd")
    dproj = jnp.concatenate([db_, dc_, du_, dq, dk, dv.astype(BF16)], axis=1)
    nin = W["w_in"].shape[2]
    G["w_in"] = _tn_matmul(
        h2, dproj, pl.BlockSpec((tk, D), lambda m, k: (k, 0)), pl.BlockSpec((tk, nin), lambda m, k: (k, m)),
        (N_SHARDS, D, nin), pl.BlockSpec((None, D, nin), lambda m, k: (m, 0, 0)), (N_SHARDS, nk), "w_in_grad")
    dx1, dg_mix = _norm_proj_bwd(dproj, W["w_in"], dx2, x1, mix_norm, "mix_in_proj_bwd")

    core_arr = jnp.reshape(core, (1,)).astype(jnp.int32)
    place = jnp.stack([jchip, core]).astype(jnp.int32)

    def chip_sums(group, recv):
        return zip(*[_chip_sum(G[k], r, core_arr, f"chip_sum_{k}") for k, r in zip(group, recv)])

    cs_late, csb_late = chip_sums(late, _swap_grad_halves([G[k] for k in late], "grad_swap_late"))
    s1, da1, db1, dy1, *got_late = _ffn_bwd_act(dx1, a1, b1, W["ffn1_w_down"], "ffn1_bwd_act",
                                                  ride=_scatter_ride(list(csb_late)))
    dx0, dg_ffn1 = _ffn_bwd_in(da1, db1, W["ffn1_w_gate"], W["ffn1_w_up"], dx1, xs, ffn1_norm, "ffn1_bwd_in")
    ffn_grads("ffn1", h1, s1, da1, db1, dy1)

    assert D >= nconv and D % LANES == 0
    fold = lambda t: t[0].reshape(-1, HEAD_DIM).sum(axis=0)[None, :]
    small_rows = [dg_ffn1[0:1], dg_mix[0:1], dg_ffn2[0:1], dg_ple[0:1],
                  _pad_rows(dwb[3:4], 1, D), _pad_rows(dwb[0:3], 3, D),
                  _pad_rows(fold(dg_q), 1, D), _pad_rows(fold(dg_k), 1, D), _pad_rows(loss_blk[0:1, 0:1], 1, D)]
    small = _pad_rows(jnp.concatenate(small_rows, axis=0), SMALL_ROWS, D)

    swapped = _swap_halves([G[k] for k in early], small, "grad_swap_halves")
    small_sum = swapped[-1]
    cs_early, csb_early = chip_sums(early, swapped[:-1])
    got_early = _scatter_chip_sums(list(csb_early), "grad_scatter")
    full = _join_halves([_shard_sum(cs, gt, place, f"shard_sum_{k}")
                         for k, cs, gt in zip(names, list(cs_early) + list(cs_late), list(got_early) + got_late)],
                        "grad_join_halves")

    out_g, out_d, out_m, out_v = {}, {}, {}, {}
    for a, k in enumerate(names):
        shp = big[k].shape
        g2 = full[a]
        d_, m_, v_ = _elementwise(_adamw, [big[k][0].reshape(g2.shape), g2, big_m[k][0].reshape(g2.shape),
                                           big_v[k][0].reshape(g2.shape)], 3, f"adamw_{k}")
        out_g[k], out_d[k], out_m[k], out_v[k] = (t.reshape(shp) for t in (g2, d_, m_, v_))

    sm_names = ["ffn1_norm", "mix_norm", "ffn2_norm", "ple_norm", "conv_b", "conv_w", "q_norm", "k_norm"]
    sm_w = dict(ffn1_norm=ffn1_norm, mix_norm=mix_norm, ffn2_norm=ffn2_norm, ple_norm=ple_norm, conv_b=conv_b,
                conv_w=conv_w[0], q_norm=q_norm, k_norm=k_norm)
    sm_m = dict(ffn1_norm=m_ffn1_norm, mix_norm=m_mix_norm, ffn2_norm=m_ffn2_norm, ple_norm=m_ple_norm,
                conv_b=m_conv_b, conv_w=m_conv_w[0], q_norm=m_q_norm, k_norm=m_k_norm)
    sm_v = dict(ffn1_norm=v_ffn1_norm, mix_norm=v_mix_norm, ffn2_norm=v_ffn2_norm, ple_norm=v_ple_norm,
                conv_b=v_conv_b, conv_w=v_conv_w[0], q_norm=v_q_norm, k_norm=v_k_norm)
    sm_g = dict(ffn1_norm=small_sum[0:1], mix_norm=small_sum[1:2], ffn2_norm=small_sum[2:3], ple_norm=small_sum[3:4],
                conv_b=small_sum[4:5, :nconv],
                conv_w=lax.dynamic_slice_in_dim(small_sum[5:8, :nconv], jchip * cwl, cwl, axis=1),
                q_norm=small_sum[8:9, :HEAD_DIM], k_norm=small_sum[9:10, :HEAD_DIM])
    loss = small_sum[10, 0]
    pack = lambda d: _pad_rows(jnp.concatenate([_pad_rows(d[k], d[k].shape[0], D) for k in sm_names], axis=0), SMALL_ROWS, D)
    sd, smm, svv = _elementwise(_adamw, [pack(sm_w), pack(sm_g), pack(sm_m), pack(sm_v)], 3, "adamw_small")
    row = 0
    for k in sm_names:
        r_, c_ = sm_w[k].shape
        shp = (1, r_, c_) if k == "conv_w" else (r_, c_)
        out_g[k] = sm_g[k].reshape(shp)
        out_d[k], out_m[k], out_v[k] = (t[row:row + r_, :c_].reshape(shp) for t in (sd, smm, svv))
        row += r_

    order = ["ffn1_norm", "ffn1_w_gate", "ffn1_w_up", "ffn1_w_down", "mix_norm", "w_in", "conv_w", "conv_b",
             "q_norm", "k_norm", "w_out", "ffn2_norm", "ffn2_w_gate", "ffn2_w_up", "ffn2_w_down", "ple_norm",
             "ple_w_gate", "ple_w_proj"]
    return (loss, dx0[None], *[out_g[k] for k in order], *[out_d[k] for k in order],
            *[out_m[k] for k in order], *[out_v[k] for k in order])
```

```python
import jax
import jax.numpy as jnp
from jax import lax
from jax.experimental import pallas as pl
from jax.experimental.pallas import tpu as pltpu

F32 = jnp.float32
BF16 = jnp.bfloat16
MESH = pl.DeviceIdType.MESH

EPS = 1e-6
HEAD_DIM = 64
LANES = 128
FFN_RES = 0.5
ADAM_LR = 0.001
ADAM_B1 = 0.9
ADAM_B2 = 0.999
ADAM_EPS = 1e-08
ADAM_WD = 0.01
ADAM_STEP = 10
N_SHARDS = 4
N_DEV = 8
ATT_TILE = 256
VMEM_LIMIT = 52 * 1024 * 1024
SMALL_ROWS = 16
HBM = pl.BlockSpec(memory_space=pltpu.HBM)
VMEM_WHOLE = pl.BlockSpec(memory_space=pltpu.VMEM)


def _cparams(**kw):
    return pltpu.CompilerParams(vmem_limit_bytes=VMEM_LIMIT, **kw)


def _dot(a, b):
    return jnp.dot(a, b, preferred_element_type=F32)


def _dot_nt(a, b):
    return lax.dot_general(a, b, (((1,), (1,)), ((), ())), preferred_element_type=F32)


def _dot_tn(a, b):
    return lax.dot_general(a, b, (((0,), (0,)), ((), ())), preferred_element_type=F32)


def _split_dot(x, m):
    hi = x.astype(BF16)
    lo = (x - hi.astype(F32)).astype(BF16)
    return _dot(hi, m) + _dot(lo, m)


def _rms(x, g):
    r = lax.rsqrt(jnp.mean(x * x, axis=-1, keepdims=True) + EPS)
    xn = x * r
    return xn * g, xn, r


def _rms_bwd(dh, xn, r, g):
    dxn = dh * g
    dx = r * (dxn - xn * jnp.mean(dxn * xn, axis=-1, keepdims=True))
    return dx, jnp.sum(dh * xn, axis=0, keepdims=True)


def _rows(n, cap=512):
    for t in (2048, 1024, 512, 448, 384, 352, 256, 192, 176, 128, 96, 88, 64, 48, 32, 16, 8):
        if t <= cap and n % t == 0:
            return t
    raise ValueError(f"no row tile for {n}")


class _Ride:
    def __init__(self, ins, out_shapes, sems, make):
        self.ins, self.out_shapes, self.sems, self.make = list(ins), list(out_shapes), list(sems), make

    def split(self, refs, n_in, n_out, n_scratch):
        ni, no = len(self.ins), len(self.out_shapes)
        ins, rin = refs[:n_in], refs[n_in:n_in + ni]
        outs = refs[n_in + ni:n_in + ni + n_out]
        rout = refs[n_in + ni + n_out:n_in + ni + n_out + no]
        rest = refs[n_in + ni + n_out + no:]
        return ins, outs, rest[:n_scratch], lambda: self.make(rin, rout, rest[n_scratch:])


_NO_RIDE = _Ride([], [], [], lambda i, o, s: [])


def _ride_along(copies, first, last):
    @pl.when(first)
    def _():
        for cp in copies():
            cp.start()

    @pl.when(last)
    def _():
        for cp in copies():
            cp.wait()


def _ffn_fwd(x, gain, wg, wu, wd, name, ride=_NO_RIDE):
    S, D = x.shape
    ns, _, fs = wg.shape
    tm = _rows(S, 512)
    ni = S // tm

    def body(*refs):
        (x_ref, g_ref, wg_ref, wu_ref, wd_ref), (xo_ref, h_ref, a_ref, b_ref), (hs, acc), copies = ride.split(refs, 5, 4, 2)
        j = pl.program_id(1)
        _ride_along(copies, (pl.program_id(0) == 0) & (j == 0), (pl.program_id(0) == ni - 1) & (j == ns - 1))

        @pl.when(j == 0)
        def _():
            h, _, _ = _rms(x_ref[...], g_ref[...])
            hb = h.astype(BF16)
            hs[...] = hb
            h_ref[...] = hb
            acc[...] = jnp.zeros_like(acc)

        hb = hs[...]
        a = _dot(hb, wg_ref[...])
        b = _dot(hb, wu_ref[...])
        a_ref[...] = a.astype(BF16)
        b_ref[...] = b.astype(BF16)
        s = (a * jax.nn.sigmoid(a)) * b
        acc[...] += _dot(s.astype(BF16), wd_ref[...])

        @pl.when(j == ns - 1)
        def _():
            xo_ref[...] = x_ref[...] + FFN_RES * acc[...]

    return pl.pallas_call(
        body, name=name, grid=(ni, ns),
        in_specs=[
            pl.BlockSpec((tm, D), lambda i, j: (i, 0)),
            pl.BlockSpec((1, D), lambda i, j: (0, 0)),
            pl.BlockSpec((None, D, fs), lambda i, j: (j, 0, 0)),
            pl.BlockSpec((None, D, fs), lambda i, j: (j, 0, 0)),
            pl.BlockSpec((None, fs, D), lambda i, j: (j, 0, 0)),
        ] + [HBM] * len(ride.ins),
        out_specs=[
            pl.BlockSpec((tm, D), lambda i, j: (i, 0)),
            pl.BlockSpec((tm, D), lambda i, j: (i, 0)),
            pl.BlockSpec((None, tm, fs), lambda i, j: (j, i, 0)),
            pl.BlockSpec((None, tm, fs), lambda i, j: (j, i, 0)),
        ] + [HBM] * len(ride.out_shapes),
        out_shape=[
            jax.ShapeDtypeStruct((S, D), F32),
            jax.ShapeDtypeStruct((S, D), BF16),
            jax.ShapeDtypeStruct((ns, S, fs), BF16),
            jax.ShapeDtypeStruct((ns, S, fs), BF16),
        ] + ride.out_shapes,
        scratch_shapes=[pltpu.VMEM((tm, D), BF16), pltpu.VMEM((tm, D), F32)] + ride.sems,
        compiler_params=_cparams(),
    )(x, gain, wg, wu, wd, *ride.ins)


def _ffn_bwd_act(dxo, a, b, wd, name, ride=_NO_RIDE):
    S, D = dxo.shape
    ns, fs, _ = wd.shape
    tm = _rows(S, 512)
    ni = S // tm

    def body(*refs):
        (dxo_ref, a_ref, b_ref, wd_ref), (s_ref, da_ref, db_ref, dy_ref), (dys,), copies = ride.split(refs, 4, 4, 1)
        j = pl.program_id(1)
        _ride_along(copies, (pl.program_id(0) == 0) & (j == 0), (pl.program_id(0) == ni - 1) & (j == ns - 1))

        @pl.when(j == 0)
        def _():
            dy = (FFN_RES * dxo_ref[...]).astype(BF16)
            dys[...] = dy
            dy_ref[...] = dy

        av = a_ref[...].astype(F32)
        bv = b_ref[...].astype(F32)
        ds = _dot_nt(dys[...], wd_ref[...])
        sig = jax.nn.sigmoid(av)
        sl = av * sig
        s_ref[...] = (sl * bv).astype(BF16)
        da_ref[...] = (ds * bv * (sig * (1.0 + av * (1.0 - sig)))).astype(BF16)
        db_ref[...] = (ds * sl).astype(BF16)

    act = pl.BlockSpec((None, tm, fs), lambda i, j: (j, i, 0))
    row = pl.BlockSpec((tm, D), lambda i, j: (i, 0))
    return pl.pallas_call(
        body, name=name, grid=(ni, ns),
        in_specs=[row, act, act, pl.BlockSpec((None, fs, D), lambda i, j: (j, 0, 0))] + [HBM] * len(ride.ins),
        out_specs=[act, act, act, row] + [HBM] * len(ride.out_shapes),
        out_shape=[jax.ShapeDtypeStruct((ns, S, fs), BF16)] * 3 + [jax.ShapeDtypeStruct((S, D), BF16)]
        + ride.out_shapes,
        scratch_shapes=[pltpu.VMEM((tm, D), BF16)] + ride.sems,
        compiler_params=_cparams(),
    )(dxo, a, b, wd, *ride.ins)


def _ffn_bwd_in(da, db, wg, wu, dres, x, gain, name, ride=_NO_RIDE):
    S, D = x.shape
    ns, _, fs = wg.shape
    tm = _rows(S, 512)
    ni = S // tm

    def body(*refs):
        (da_ref, db_ref, wg_ref, wu_ref, dres_ref, x_ref, g_ref), (dx_ref, dg_ref), (acc,), copies = ride.split(refs, 7, 2, 1)
        i = pl.program_id(0)
        j = pl.program_id(1)
        _ride_along(copies, (i == 0) & (j == 0), (i == ni - 1) & (j == ns - 1))

        @pl.when((i == 0) & (j == 0))
        def _():
            dg_ref[...] = jnp.zeros_like(dg_ref)

        @pl.when(j == 0)
        def _():
            acc[...] = jnp.zeros_like(acc)

        acc[...] += _dot_nt(da_ref[...], wg_ref[...]) + _dot_nt(db_ref[...], wu_ref[...])

        @pl.when(j == ns - 1)
        def _():
            g = g_ref[...]
            _, xn, r = _rms(x_ref[...], g)
            dx, dg = _rms_bwd(acc[...], xn, r, g)
            dx_ref[...] = dres_ref[...] + dx
            dg_ref[...] += jnp.broadcast_to(dg, dg_ref.shape)

    act = pl.BlockSpec((None, tm, fs), lambda i, j: (j, i, 0))
    row = pl.BlockSpec((tm, D), lambda i, j: (i, 0))
    wsp = pl.BlockSpec((None, D, fs), lambda i, j: (j, 0, 0))
    return pl.pallas_call(
        body, name=name, grid=(ni, ns),
        in_specs=[act, act, wsp, wsp, row, row, pl.BlockSpec((1, D), lambda i, j: (0, 0))] + [HBM] * len(ride.ins),
        out_specs=[row, pl.BlockSpec((8, D), lambda i, j: (0, 0))] + [HBM] * len(ride.out_shapes),
        out_shape=[jax.ShapeDtypeStruct((S, D), F32), jax.ShapeDtypeStruct((8, D), F32)] + ride.out_shapes,
        scratch_shapes=[pltpu.VMEM((tm, D), F32)] + ride.sems,
        compiler_params=_cparams(),
    )(da, db, wg, wu, dres, x, gain, *ride.ins)


def _tn_matmul(a, b, a_spec, b_spec, o_shape, o_spec, grid, name):
    kaxis = len(grid) - 1

    def body(a_ref, b_ref, o_ref):
        @pl.when(pl.program_id(kaxis) == 0)
        def _():
            o_ref[...] = jnp.zeros_like(o_ref)

        o_ref[...] += _dot_tn(a_ref[...].astype(BF16), b_ref[...].astype(BF16))

    return pl.pallas_call(
        body, name=name, grid=grid, in_specs=[a_spec, b_spec], out_specs=o_spec,
        out_shape=jax.ShapeDtypeStruct(o_shape, F32), compiler_params=_cparams(),
    )(a, b)


def _norm_proj(x, gain, w, name):
    S, D = x.shape
    ns, _, n = w.shape
    tm = _rows(S, 512)

    def body(x_ref, g_ref, w_ref, o_ref, h_ref, hs):
        @pl.when(pl.program_id(1) == 0)
        def _():
            h, _, _ = _rms(x_ref[...], g_ref[...])
            hb = h.astype(BF16)
            hs[...] = hb
            h_ref[...] = hb

        o_ref[...] = _dot(hs[...], w_ref[...])

    return pl.pallas_call(
        body, name=name, grid=(S // tm, ns),
        in_specs=[
            pl.BlockSpec((tm, D), lambda i, j: (i, 0)),
            pl.BlockSpec((1, D), lambda i, j: (0, 0)),
            pl.BlockSpec((None, D, n), lambda i, j: (j, 0, 0)),
        ],
        out_specs=[
            pl.BlockSpec((tm, n), lambda i, j: (i, j)),
            pl.BlockSpec((tm, D), lambda i, j: (i, 0)),
        ],
        out_shape=[jax.ShapeDtypeStruct((S, ns * n), F32), jax.ShapeDtypeStruct((S, D), BF16)],
        scratch_shapes=[pltpu.VMEM((tm, D), BF16)],
        compiler_params=_cparams(),
    )(x, gain, w)


def _norm_proj_bwd(dproj, w, dres, x, gain, name, ride=_NO_RIDE):
    S, D = x.shape
    ns, _, n = w.shape
    tm = _rows(S, 512)
    ni = S // tm

    def body(*refs):
        (dp_ref, w_ref, dres_ref, x_ref, g_ref), (dx_ref, dg_ref), (acc,), copies = ride.split(refs, 5, 2, 1)
        i = pl.program_id(0)
        j = pl.program_id(1)
        _ride_along(copies, (i == 0) & (j == 0), (i == ni - 1) & (j == ns - 1))

        @pl.when((i == 0) & (j == 0))
        def _():
            dg_ref[...] = jnp.zeros_like(dg_ref)

        @pl.when(j == 0)
        def _():
            acc[...] = jnp.zeros_like(acc)

        acc[...] += _dot_nt(dp_ref[...], w_ref[...])

        @pl.when(j == ns - 1)
        def _():
            g = g_ref[...]
            _, xn, r = _rms(x_ref[...], g)
            dx, dg = _rms_bwd(acc[...], xn, r, g)
            dx_ref[...] = dres_ref[...] + dx
            dg_ref[...] += jnp.broadcast_to(dg, dg_ref.shape)

    return pl.pallas_call(
        body, name=name, grid=(ni, ns),
        in_specs=[
            pl.BlockSpec((tm, n), lambda i, j: (i, j)),
            pl.BlockSpec((None, D, n), lambda i, j: (j, 0, 0)),
            pl.BlockSpec((tm, D), lambda i, j: (i, 0)),
            pl.BlockSpec((tm, D), lambda i, j: (i, 0)),
            pl.BlockSpec((1, D), lambda i, j: (0, 0)),
        ] + [HBM] * len(ride.ins),
        out_specs=[
            pl.BlockSpec((tm, D), lambda i, j: (i, 0)),
            pl.BlockSpec((8, D), lambda i, j: (0, 0)),
        ] + [HBM] * len(ride.out_shapes),
        out_shape=[jax.ShapeDtypeStruct((S, D), F32), jax.ShapeDtypeStruct((8, D), F32)] + ride.out_shapes,
        scratch_shapes=[pltpu.VMEM((tm, D), F32)] + ride.sems,
        compiler_params=_cparams(),
    )(dproj, w, dres, x, gain, *ride.ins)


def _out_proj(ycat, w, res, name):
    S, K = ycat.shape
    D = w.shape[1]
    tm = _rows(S, 512)

    def body(y_ref, w_ref, r_ref, o_ref):
        o_ref[...] = r_ref[...] + _dot(y_ref[...], w_ref[...])

    return pl.pallas_call(
        body, name=name, grid=(S // tm,),
        in_specs=[
            pl.BlockSpec((tm, K), lambda i: (i, 0)),
            pl.BlockSpec((K, D), lambda i: (0, 0)),
            pl.BlockSpec((tm, D), lambda i: (i, 0)),
        ],
        out_specs=pl.BlockSpec((tm, D), lambda i: (i, 0)),
        out_shape=jax.ShapeDtypeStruct((S, D), F32),
        compiler_params=_cparams(),
    )(ycat, w, res)


def _out_proj_bwd(dx, w, name):
    S, D = dx.shape
    K = w.shape[0]
    tm = _rows(S, 512)

    def body(d_ref, w_ref, o_ref):
        o_ref[...] = _dot_nt(d_ref[...].astype(BF16), w_ref[...])

    return pl.pallas_call(
        body, name=name, grid=(S // tm,),
        in_specs=[pl.BlockSpec((tm, D), lambda i: (i, 0)), pl.BlockSpec((K, D), lambda i: (0, 0))],
        out_specs=pl.BlockSpec((tm, K), lambda i: (i, 0)),
        out_shape=jax.ShapeDtypeStruct((S, K), F32),
        compiler_params=_cparams(),
    )(dx, w)


CONV_COLS = 256


def _shift_down(z, halo, k, row):
    out = pltpu.roll(z, k, 0)
    for n in range(k):
        out = jnp.where(row == n, halo[8 - k + n:8 - k + n + 1, :], out)
    return out


def _shift_up(g, halo, k, row, ts):
    out = pltpu.roll(g, ts - k, 0)
    for n in range(k):
        out = jnp.where(row == ts - k + n, halo[n:n + 1, :], out)
    return out


def _conv_fwd(proj, cw, cb, nconv, name):
    S = proj.shape[0]
    ncb = nconv // CONV_COLS
    ts = _rows(S, 512)
    hb = ts // 8

    def body(b_ref, c_ref, u_ref, ch_ref, uh_ref, w_ref, bias_ref, o_ref):
        i = pl.program_id(1)
        z = c_ref[...] * u_ref[...]
        halo = jnp.where(i > 0, ch_ref[...] * uh_ref[...], 0.0)
        row = lax.broadcasted_iota(jnp.int32, z.shape, 0)
        w = w_ref[...]
        yc = w[0:1, :] * _shift_down(z, halo, 2, row) + w[1:2, :] * _shift_down(z, halo, 1, row) + w[2:3, :] * z
        o_ref[...] = (b_ref[...] * (yc + bias_ref[...])).astype(BF16)

    def blk(unit):
        return pl.BlockSpec((ts, CONV_COLS), lambda cbi, i: (i, unit * ncb + cbi))

    def prev(unit):
        return pl.BlockSpec((8, CONV_COLS), lambda cbi, i: (jnp.maximum(i * hb - 1, 0), unit * ncb + cbi))

    return pl.pallas_call(
        body, name=name, grid=(ncb, S // ts),
        in_specs=[blk(0), blk(1), blk(2), prev(1), prev(2),
                  pl.BlockSpec((8, CONV_COLS), lambda cbi, i: (0, cbi)),
                  pl.BlockSpec((1, CONV_COLS), lambda cbi, i: (0, cbi))],
        out_specs=pl.BlockSpec((ts, CONV_COLS), lambda cbi, i: (i, cbi)),
        out_shape=jax.ShapeDtypeStruct((S, nconv), BF16),
        compiler_params=_cparams(),
    )(proj, proj, proj, proj, proj, cw, cb)


def _conv_bwd(proj, dy, cw, cb, nconv, name):
    S = proj.shape[0]
    ncb = nconv // CONV_COLS
    ts = _rows(S, 512)
    hb = ts // 8
    nblk = S // ts

    def body(b_ref, c_ref, u_ref, dy_ref, ch_ref, uh_ref, bn_ref, dyn_ref, w_ref, bias_ref,
             db_ref, dc_ref, du_ref, dw_ref):
        i = pl.program_id(1)

        @pl.when(i == 0)
        def _():
            dw_ref[...] = jnp.zeros_like(dw_ref)

        c = c_ref[...]
        u = u_ref[...]
        bg = b_ref[...]
        dy_ = dy_ref[...]
        z = c * u
        halo = jnp.where(i > 0, ch_ref[...] * uh_ref[...], 0.0)
        row = lax.broadcasted_iota(jnp.int32, z.shape, 0)
        w = w_ref[...]
        z2 = _shift_down(z, halo, 2, row)
        z1 = _shift_down(z, halo, 1, row)
        yc = w[0:1, :] * z2 + w[1:2, :] * z1 + w[2:3, :] * z
        db_ref[...] = (dy_ * (yc + bias_ref[...])).astype(BF16)
        g = dy_ * bg
        gnext = jnp.where(i < nblk - 1, dyn_ref[...] * bn_ref[...], 0.0)
        dz = w[2:3, :] * g + w[1:2, :] * _shift_up(g, gnext, 1, row, ts) + w[0:1, :] * _shift_up(g, gnext, 2, row, ts)
        dc_ref[...] = (dz * u).astype(BF16)
        du_ref[...] = (dz * c).astype(BF16)
        r8 = lax.broadcasted_iota(jnp.int32, (8, CONV_COLS), 0)
        sums = [jnp.sum(g * z2, axis=0, keepdims=True), jnp.sum(g * z1, axis=0, keepdims=True),
                jnp.sum(g * z, axis=0, keepdims=True), jnp.sum(g, axis=0, keepdims=True)]
        upd = jnp.zeros((8, CONV_COLS), F32)
        for n, sv in enumerate(sums):
            upd = jnp.where(r8 == n, sv, upd)
        dw_ref[...] += upd

    def blk(unit):
        return pl.BlockSpec((ts, CONV_COLS), lambda cbi, i: (i, unit * ncb + cbi))

    def prev(unit):
        return pl.BlockSpec((8, CONV_COLS), lambda cbi, i: (jnp.maximum(i * hb - 1, 0), unit * ncb + cbi))

    def nxt(unit):
        return pl.BlockSpec((8, CONV_COLS), lambda cbi, i: (jnp.minimum((i + 1) * hb, S // 8 - 1), unit * ncb + cbi))

    o = pl.BlockSpec((ts, CONV_COLS), lambda cbi, i: (i, cbi))
    return pl.pallas_call(
        body, name=name, grid=(ncb, nblk),
        in_specs=[blk(0), blk(1), blk(2), blk(0), prev(1), prev(2), nxt(0), nxt(0),
                  pl.BlockSpec((8, CONV_COLS), lambda cbi, i: (0, cbi)),
                  pl.BlockSpec((1, CONV_COLS), lambda cbi, i: (0, cbi))],
        out_specs=[o, o, o, pl.BlockSpec((8, CONV_COLS), lambda cbi, i: (0, cbi))],
        out_shape=[jax.ShapeDtypeStruct((S, nconv), BF16)] * 3 + [jax.ShapeDtypeStruct((8, nconv), F32)],
        compiler_params=_cparams(),
    )(proj, proj, proj, dy, proj, proj, proj, dy, cw, cb)


def _group_ones(n):
    r = lax.broadcasted_iota(jnp.int32, (n, n), 0) // HEAD_DIM
    c = lax.broadcasted_iota(jnp.int32, (n, n), 1) // HEAD_DIM
    return jnp.where(r == c, 1.0, 0.0).astype(BF16)


def _qk_norm(proj, gain_t, unit0, nsb, scale, name):
    S = proj.shape[0]
    nb = nsb // CONV_COLS
    ts = _rows(S, 512)

    def body(x_ref, g_ref, o_ref):
        x = x_ref[...]
        ss = _split_dot(x * x, _group_ones(CONV_COLS))
        r = lax.rsqrt(ss * (1.0 / HEAD_DIM) + EPS)
        o_ref[...] = ((x * r) * g_ref[...] * scale).astype(BF16)

    return pl.pallas_call(
        body, name=name, grid=(nb, S // ts),
        in_specs=[pl.BlockSpec((ts, CONV_COLS), lambda u, i: (i, unit0 + u)),
                  pl.BlockSpec((1, CONV_COLS), lambda u, i: (0, 0))],
        out_specs=pl.BlockSpec((ts, CONV_COLS), lambda u, i: (i, u)),
        out_shape=jax.ShapeDtypeStruct((S, nsb), BF16),
        compiler_params=_cparams(),
    )(proj, gain_t)


def _qk_norm_bwd(proj, dout, gain_t, unit0, nsb, scale, name):
    S = proj.shape[0]
    nb = nsb // CONV_COLS
    ts = _rows(S, 512)

    def body(x_ref, d_ref, g_ref, dx_ref, dg_ref):
        @pl.when(pl.program_id(1) == 0)
        def _():
            dg_ref[...] = jnp.zeros_like(dg_ref)

        x = x_ref[...]
        g = g_ref[...]
        ones = _group_ones(CONV_COLS)
        ss = _split_dot(x * x, ones)
        r = lax.rsqrt(ss * (1.0 / HEAD_DIM) + EPS)
        xn = x * r
        dh = d_ref[...] * scale
        dxn = dh * g
        m = _split_dot(dxn * xn, ones) * (1.0 / HEAD_DIM)
        dx_ref[...] = (r * (dxn - xn * m)).astype(BF16)
        dg_ref[...] += jnp.broadcast_to(jnp.sum(dh * xn, axis=0, keepdims=True), dg_ref.shape)

    return pl.pallas_call(
        body, name=name, grid=(nb, S // ts),
        in_specs=[pl.BlockSpec((ts, CONV_COLS), lambda u, i: (i, unit0 + u)),
                  pl.BlockSpec((ts, CONV_COLS), lambda u, i: (i, u)),
                  pl.BlockSpec((1, CONV_COLS), lambda u, i: (0, 0))],
        out_specs=[pl.BlockSpec((ts, CONV_COLS), lambda u, i: (i, u)),
                   pl.BlockSpec((8, CONV_COLS), lambda u, i: (0, u))],
        out_shape=[jax.ShapeDtypeStruct((S, nsb), BF16), jax.ShapeDtypeStruct((8, nsb), F32)],
        compiler_params=_cparams(),
    )(proj, dout, gain_t)


Z_CLAMP = 80.0
N_SLOTS = 3
SAT_LIMIT = 120.0


def _head_masks():
    lane = lax.broadcasted_iota(jnp.int32, (1, LANES), 1)
    return [lane < HEAD_DIM, lane >= HEAD_DIM], lane


def _tile_consts(T):
    r_i = lax.broadcasted_iota(jnp.int32, (T, T), 0)
    c_i = lax.broadcasted_iota(jnp.int32, (T, T), 1)
    neg_suffix = jnp.where(r_i >= c_i, -1.0, 0.0).astype(BF16)
    prefix = jnp.where(r_i <= c_i, 1.0, 0.0).astype(BF16)
    return neg_suffix, prefix, c_i < r_i


def _pipeline(n, stages, first_special=False, last_special=False, saturated=None, extra_head=0):
    depth = len(stages)
    head = (depth if first_special else depth - 1) + extra_head
    off = 0 if last_special else 1
    for m in range(head):
        for k in reversed(range(min(m, depth - 1) + 1)):
            stages[k](m - k, (m - k) % N_SLOTS, first_special and m == k)

    def trip(m, u):
        for k in reversed(range(depth)):
            stages[k](m - k, (head + u - k) % N_SLOTS, False)

    def group(g, carry):
        for u in range(N_SLOTS):
            trip(head + g * N_SLOTS + u, u)
        return carry

    count = n - 1 + off - head
    full = count // N_SLOTS
    if saturated is None:
        lax.fori_loop(0, full, group, 0)
        go_on, done = True, n
    else:
        def more(state):
            return (state[0] < full) & (state[1] == 0)

        def step(state):
            group(state[0], 0)
            return state[0] + 1, saturated().astype(jnp.int32)

        groups, stop = lax.while_loop(more, step, (jnp.int32(0), saturated().astype(jnp.int32)))
        go_on = stop == 0
        done = jnp.where(go_on, n, head - depth + 1 + N_SLOTS * groups)
    for r in range(N_SLOTS):
        @pl.when((count - full * N_SLOTS == r) & go_on)
        def _(r=r):
            for u in range(r):
                trip(head + full * N_SLOTS + u, u)
            for e in range(depth - off):
                for k in reversed(range(e + off, depth)):
                    t = n - 1 - (k - e - off)
                    stages[k](t, (head + r + e - k) % N_SLOTS, last_special and k == e + off)
    return done


def _sweep(n, stages, finish, first_special=False, last_special=False, saturated=None, extra_head=0):
    depth = len(stages)
    least = (depth if first_special else depth - 1) + extra_head + (1 if last_special else 0)
    for short in range(1, least):
        @pl.when(n == short)
        def _(short=short):
            for m in range(short + depth - 1):
                for k in reversed(range(depth)):
                    t = m - k
                    if 0 <= t < short:
                        stages[k](t, t % N_SLOTS, (first_special and t == 0) or (last_special and t == short - 1))
            finish(short)

    @pl.when(n >= least)
    def _():
        finish(_pipeline(n, stages, first_special, last_special, saturated, extra_head))


def _attn_fwd(q, k, v, name):
    S, nsb = q.shape
    T = ATT_TILE
    hp = nsb // LANES
    nb = S // T
    assert nb <= HEAD_DIM

    def body(q_ref, k_ref, v_ref, y_ref, cs_ref, c_ref, acc, z_st, inc_st):
        i = pl.program_id(1)
        masks, lane = _head_masks()
        qv = q_ref[...]
        qm = [jnp.where(m, qv, jnp.zeros_like(qv)) for m in masks]
        neg_suffix, _, causal = _tile_consts(T)
        c_ref[...] = jnp.zeros_like(c_ref)
        acc[...] = jnp.zeros_like(acc)
        cs_ref[...] = jnp.zeros_like(cs_ref)

        def blk(ref, j):
            return ref[pl.ds(pl.multiple_of(j * T, T), T), :]

        def scores(t, slot, diag):
            kj = blk(k_ref, i - t)
            for h in range(2):
                z_st[slot, h] = jnp.minimum(_dot_nt(qm[h], kj), Z_CLAMP)

        def suffix_sums(t, slot, diag):
            for h in range(2):
                sp = jnp.log(1.0 + jnp.exp(z_st[slot, h]))
                if diag:
                    sp = jnp.where(causal, sp, 0.0)
                inc_st[slot, h] = _dot(sp.astype(BF16), neg_suffix)

        def weights(t, slot, diag):
            vj = blk(v_ref, i - t)
            for h in range(2):
                inc = inc_st[slot, h]
                c = c_ref[h]
                a = jnp.exp(z_st[slot, h] + inc + c)
                if diag:
                    a = jnp.where(causal, a, 0.0)
                upd = _dot(a.astype(BF16), vj)
                acc[...] += jnp.where(masks[h], upd, 0.0)
                cs_ref[...] = jnp.where(lane == i - t + HEAD_DIM * h, c, cs_ref[...])
                c_ref[h] = c + inc[:, 0:1]

        stages = [scores, suffix_sums, weights]

        def saturated():
            return jnp.max(c_ref[...]) < -SAT_LIMIT

        def note(used):
            cs_ref[...] = jnp.where(lane == LANES - 1, jnp.asarray(used).astype(F32), cs_ref[...])

        _sweep(i + 1, stages, note, first_special=True, saturated=saturated, extra_head=1)
        y_ref[...] = acc[...].astype(BF16)

    return pl.pallas_call(
        body, name=name, grid=(hp, nb),
        in_specs=[pl.BlockSpec((T, LANES), lambda p, i: (i, p)),
                  pl.BlockSpec((S, LANES), lambda p, i: (0, p)),
                  pl.BlockSpec((S, LANES), lambda p, i: (0, p))],
        out_specs=[pl.BlockSpec((T, LANES), lambda p, i: (i, p)),
                   pl.BlockSpec((None, T, LANES), lambda p, i: (p, i, 0))],
        out_shape=[jax.ShapeDtypeStruct((S, nsb), BF16), jax.ShapeDtypeStruct((hp, S, LANES), F32)],
        scratch_shapes=[pltpu.VMEM((2, T, 1), F32), pltpu.VMEM((T, LANES), F32),
                        pltpu.VMEM((N_SLOTS, 2, T, T), F32), pltpu.VMEM((N_SLOTS, 2, T, T), F32)],
        compiler_params=_cparams(),
    )(q, k, v)


def _attn_bwd(q, k, v, dy, col0, carry, name):
    S, nsb = q.shape
    T = ATT_TILE
    hp = nsb // LANES
    nb = S // T

    def body(q_ref, k_ref, v_ref, dy_ref, cs_ref, dq_ref, dk_ref, dv_ref, e_ref, acc,
             z_st, da_st, b_st, inc_st, a_st, e_st, p_st):
        i = pl.program_id(1)

        @pl.when(i == 0)
        def _():
            dk_ref[...] = jnp.zeros_like(dk_ref)
            dv_ref[...] = jnp.zeros_like(dv_ref)

        masks, lane = _head_masks()
        qv = q_ref[...]
        dyb = dy_ref[...].astype(BF16)
        qm = [jnp.where(m, qv, jnp.zeros_like(qv)) for m in masks]
        dym = [jnp.where(m, dyb, jnp.zeros_like(dyb)) for m in masks]
        neg_suffix, prefix, causal = _tile_consts(T)
        e_ref[...] = jnp.zeros_like(e_ref)
        acc[...] = jnp.zeros_like(acc)

        def blk(ref, j):
            return ref[pl.ds(pl.multiple_of(j * T, T), T), :]

        used = jnp.max(jnp.where(lane == LANES - 1, cs_ref[...], 0.0)).astype(jnp.int32)
        n = jnp.clip(used, 1, i + 1)
        first = i + 1 - n

        def scores(t, slot, diag):
            kj = blk(k_ref, first + t)
            vj = blk(v_ref, first + t)
            for h in range(2):
                z_st[slot, h] = jnp.minimum(_dot_nt(qm[h], kj), Z_CLAMP)
                da_st[slot, h] = _dot_nt(dym[h], vj)

        def suffix_sums(t, slot, diag):
            for h in range(2):
                u = jnp.exp(z_st[slot, h])
                w = 1.0 + u
                b_st[slot, h] = u / w
                sp = jnp.log(w)
                if diag:
                    sp = jnp.where(causal, sp, 0.0)
                inc_st[slot, h] = _dot(sp.astype(BF16), neg_suffix)

        def probs(t, slot, diag):
            csv = cs_ref[...]
            for h in range(2):
                c = jnp.sum(jnp.where(lane == first + t + HEAD_DIM * h, csv, 0.0), axis=-1, keepdims=True)
                a = jnp.exp(z_st[slot, h] + inc_st[slot, h] + c)
                if diag:
                    a = jnp.where(causal, a, 0.0)
                a_st[slot, h] = a.astype(BF16)
                e = a * da_st[slot, h]
                e_st[slot, h] = e
                p_st[slot, h] = _dot(e.astype(BF16), prefix)

        def grads(t, slot, diag):
            kj = blk(k_ref, first + t)
            off = pl.multiple_of((first + t) * T, T)
            for h in range(2):
                p = p_st[slot, h]
                dz = e_st[slot, h] - b_st[slot, h] * (e_ref[h] + p)
                if diag:
                    dz = jnp.where(causal, dz, 0.0)
                dzb = dz.astype(BF16)
                acc[...] += jnp.where(masks[h], _dot(dzb, kj), 0.0)
                dk_ref[pl.ds(off, T), :] += _dot_tn(dzb, qm[h])
                dv_ref[pl.ds(off, T), :] += _dot_tn(a_st[slot, h], dym[h])
                e_ref[h] += p[:, T - 1:T]

        stages = [scores, suffix_sums, probs, grads]

        _sweep(n, stages, lambda done: None, last_special=True)
        dq_ref[...] = acc[...]

    return pl.pallas_call(
        body, name=name, grid=(hp, nb),
        in_specs=[pl.BlockSpec((T, LANES), lambda p, i: (i, p)),
                  pl.BlockSpec((S, LANES), lambda p, i: (0, p)),
                  pl.BlockSpec((S, LANES), lambda p, i: (0, p)),
                  pl.BlockSpec((T, LANES), lambda p, i: (i, col0 + p)),
                  pl.BlockSpec((None, T, LANES), lambda p, i: (p, i, 0))],
        out_specs=[pl.BlockSpec((T, LANES), lambda p, i: (i, p)),
                   pl.BlockSpec((S, LANES), lambda p, i: (0, p)),
                   pl.BlockSpec((S, LANES), lambda p, i: (0, p))],
        out_shape=[jax.ShapeDtypeStruct((S, nsb), F32)] * 3,
        scratch_shapes=[pltpu.VMEM((2, T, 1), F32), pltpu.VMEM((T, LANES), F32)]
        + [pltpu.VMEM((N_SLOTS, 2, T, T), dt) for dt in (F32, F32, F32, F32, BF16, F32, F32)],
        compiler_params=_cparams(),
    )(q, k, v, dy, carry)


def _ple(x, p, tgt, gain, wpg, wpp, name):
    S, D = x.shape
    P = p.shape[1]
    ns, _, nc = wpp.shape
    tm = _rows(S, 256)

    def body(x_ref, p_ref, t_ref, g_ref, wpg_ref, wpp_ref, dx_ref, h_ref, du_ref, dpp_ref, loss_ref, dg_ref):
        @pl.when(pl.program_id(0) == 0)
        def _():
            loss_ref[...] = jnp.zeros_like(loss_ref)
            dg_ref[...] = jnp.zeros_like(dg_ref)

        x_ = x_ref[...]
        g = g_ref[...]
        h, xn, r = _rms(x_, g)
        hb = h.astype(BF16)
        h_ref[...] = hb
        gate = jax.nn.sigmoid(_dot(hb, wpg_ref[...]))
        pb = p_ref[...].astype(BF16)
        pp = jnp.concatenate([_dot(pb, wpp_ref[n]) for n in range(ns)], axis=1)
        err = (x_ + gate * pp) - t_ref[...]
        loss_ref[...] += (0.5 / D) * jnp.sum(err * err)
        dy = err * (1.0 / D)
        du = ((dy * pp) * (gate * (1.0 - gate))).astype(BF16)
        du_ref[...] = du
        dpp_ref[...] = (dy * gate).astype(BF16)
        dx, dg = _rms_bwd(_dot_nt(du, wpg_ref[...]), xn, r, g)
        dx_ref[...] = dy + dx
        dg_ref[...] += jnp.broadcast_to(dg, dg_ref.shape)

    row = pl.BlockSpec((tm, D), lambda i: (i, 0))
    return pl.pallas_call(
        body, name=name, grid=(S // tm,),
        in_specs=[row, pl.BlockSpec((tm, P), lambda i: (i, 0)), row,
                  pl.BlockSpec((1, D), lambda i: (0, 0)),
                  pl.BlockSpec((D, D), lambda i: (0, 0)),
                  pl.BlockSpec((ns, P, nc), lambda i: (0, 0, 0))],
        out_specs=[row, row, row, row,
                   pl.BlockSpec((8, LANES), lambda i: (0, 0)),
                   pl.BlockSpec((8, D), lambda i: (0, 0))],
        out_shape=[jax.ShapeDtypeStruct((S, D), F32)] + [jax.ShapeDtypeStruct((S, D), BF16)] * 3
        + [jax.ShapeDtypeStruct((8, LANES), F32), jax.ShapeDtypeStruct((8, D), F32)],
        compiler_params=_cparams(),
    )(x, p, tgt, gain, wpg, wpp)


def _elementwise(fn, ins, n_out, name):
    R, C = ins[0].shape
    tr = _rows(R, 512)

    def body(*refs):
        outs = fn(*[r[...] for r in refs[:len(ins)]])
        for o_ref, o in zip(refs[len(ins):], outs):
            o_ref[...] = o

    spec = pl.BlockSpec((tr, C), lambda i: (i, 0))
    return pl.pallas_call(
        body, name=name, grid=(R // tr,), in_specs=[spec] * len(ins), out_specs=[spec] * n_out,
        out_shape=[jax.ShapeDtypeStruct((R, C), F32)] * n_out, compiler_params=_cparams(),
    )(*ins)


def _adamw(w, g, m, v):
    m = ADAM_B1 * m + (1.0 - ADAM_B1) * g
    v = ADAM_B2 * v + (1.0 - ADAM_B2) * jnp.square(g)
    m_hat = m / (1.0 - ADAM_B1 ** ADAM_STEP)
    v_hat = v / (1.0 - ADAM_B2 ** ADAM_STEP)
    delta = -ADAM_LR * (m_hat / (jnp.sqrt(v_hat) + ADAM_EPS) + ADAM_WD * w)
    return delta, m, v


def _place():
    x, y, c = lax.axis_index("x"), lax.axis_index("y"), lax.axis_index("c")
    chips = [(1 - x, y), (x, 1 - y), (1 - x, 1 - y)]
    return x, y, c, chips


def _half(ref, c, axis_rows):
    n = ref.shape[-2]
    start = pl.multiple_of(c * (n // 2), 8)
    idx = (slice(None),) * (len(ref.shape) - 2) + (pl.ds(start, n // 2), slice(None))
    return ref.at[idx]


def _gather_weights(shards, small, name):
    n = len(shards)

    def body(*refs):
        ins, small_in = refs[:n], refs[n]
        outs, small_out = refs[n + 1:2 * n + 1], refs[2 * n + 1]
        lsem, lrsem, ssem, rsem, sm_s, sm_r = refs[2 * n + 2:]
        x, y, c, chips = _place()
        j = 2 * x + y
        sib = (x, y, 1 - c)

        local = [pltpu.make_async_remote_copy(
            src_ref=ins[a], dst_ref=outs[a].at[j], send_sem=lsem.at[a], recv_sem=lrsem.at[a],
            device_id=sib, device_id_type=MESH) for a in range(n)]
        for cp in local:
            cp.start()
        small_out[j] = small_in[...]
        small_cp = [pltpu.make_async_remote_copy(
            src_ref=small_in, dst_ref=small_out.at[j], send_sem=sm_s.at[k], recv_sem=sm_r.at[k],
            device_id=(*chip, c), device_id_type=MESH) for k, chip in enumerate(chips)]
        for cp in small_cp:
            cp.start()

        def ici(a, k, chip, jj, dev):
            return pltpu.make_async_remote_copy(
                src_ref=_half(ins[a], c, True) if dev is not None else _half(outs[a].at[jj], c, True),
                dst_ref=_half(outs[a].at[jj], c, True),
                send_sem=ssem.at[a, k], recv_sem=rsem.at[a, k],
                device_id=dev if dev is not None else (*chip, c), device_id_type=MESH)

        first = []
        for a in range(n):
            for k, chip in enumerate(chips):
                cp = ici(a, k, chip, j, (*chip, c))
                cp.start()
                first.append(cp)
        passed = []
        for a in range(n):
            for k, chip in enumerate(chips):
                jj = 2 * chip[0] + chip[1]
                ici(a, k, chip, jj, None).wait_recv()
                fw = pltpu.make_async_remote_copy(
                    src_ref=_half(outs[a].at[jj], c, True), dst_ref=_half(outs[a].at[jj], c, True),
                    send_sem=ssem.at[a, 3 + k], recv_sem=rsem.at[a, 3 + k], device_id=sib, device_id_type=MESH)
                fw.start()
                passed.append(fw)
        for a in range(n):
            for k, chip in enumerate(chips):
                jj = 2 * chip[0] + chip[1]
                pltpu.make_async_remote_copy(
                    src_ref=_half(outs[a].at[jj], 1 - c, True), dst_ref=_half(outs[a].at[jj], 1 - c, True),
                    send_sem=ssem.at[a, 3 + k], recv_sem=rsem.at[a, 3 + k], device_id=sib,
                    device_id_type=MESH).wait_recv()
        for cp in small_cp:
            cp.wait()
        for cp in first + passed:
            cp.wait_send()
        for cp in local:
            cp.wait()

    return pl.pallas_call(
        body, name=name,
        in_specs=[HBM] * n + [VMEM_WHOLE],
        out_specs=[HBM] * n + [VMEM_WHOLE],
        out_shape=[jax.ShapeDtypeStruct((N_SHARDS,) + s.shape, s.dtype) for s in shards]
        + [jax.ShapeDtypeStruct((N_SHARDS,) + small.shape, small.dtype)],
        scratch_shapes=[pltpu.SemaphoreType.DMA((n,)), pltpu.SemaphoreType.DMA((n,)),
                        pltpu.SemaphoreType.DMA((n, 6)), pltpu.SemaphoreType.DMA((n, 6)),
                        pltpu.SemaphoreType.DMA((3,)), pltpu.SemaphoreType.DMA((3,))],
    )(*shards, small)


def _sum_small(small, name):
    def body(small_in, small_out, buf, sm_s, sm_r):
        x, y, c, _ = _place()
        me = 4 * x + 2 * y + c
        buf[me] = small_in[...]
        peers = [(fx, fy, fc) for fx in (0, 1) for fy in (0, 1) for fc in (0, 1)][1:]
        sm = []
        for k, (fx, fy, fc) in enumerate(peers):
            dev = (1 - x if fx else x, 1 - y if fy else y, 1 - c if fc else c)
            cp = pltpu.make_async_remote_copy(
                src_ref=small_in, dst_ref=buf.at[me], send_sem=sm_s.at[k], recv_sem=sm_r.at[k],
                device_id=dev, device_id_type=MESH)
            cp.start()
            sm.append(cp)
        for cp in sm:
            cp.wait()
        tot = buf[0]
        for d in range(1, N_DEV):
            tot = tot + buf[d]
        small_out[...] = tot

    return pl.pallas_call(
        body, name=name, in_specs=[VMEM_WHOLE], out_specs=VMEM_WHOLE,
        out_shape=jax.ShapeDtypeStruct(small.shape, F32),
        scratch_shapes=[pltpu.VMEM((N_DEV,) + small.shape, F32),
                        pltpu.SemaphoreType.DMA((N_DEV - 1,)), pltpu.SemaphoreType.DMA((N_DEV - 1,))],
    )(small)


def _swap_ride(grads):
    n = len(grads)

    def make(ins, outs, sems):
        ssem, rsem = sems
        x, y, c, _ = _place()
        return [pltpu.make_async_remote_copy(
            src_ref=_half(ins[a], 1 - c, True), dst_ref=outs[a], send_sem=ssem.at[a], recv_sem=rsem.at[a],
            device_id=(x, y, 1 - c), device_id_type=MESH) for a in range(n)]

    return _Ride(grads, [jax.ShapeDtypeStruct((g.shape[0], g.shape[1] // 2, g.shape[2]), F32) for g in grads],
                 [pltpu.SemaphoreType.DMA((n,)), pltpu.SemaphoreType.DMA((n,))], make)


def _swap_grad_halves(grads, name):
    n = len(grads)

    def body(*refs):
        ins, outs = refs[:n], refs[n:2 * n]
        ssem, rsem = refs[2 * n:]
        x, y, c, _ = _place()
        cps = [pltpu.make_async_remote_copy(
            src_ref=_half(ins[a], 1 - c, True), dst_ref=outs[a], send_sem=ssem.at[a], recv_sem=rsem.at[a],
            device_id=(x, y, 1 - c), device_id_type=MESH) for a in range(n)]
        for cp in cps:
            cp.start()
        for cp in cps:
            cp.wait()

    return pl.pallas_call(
        body, name=name, in_specs=[HBM] * n, out_specs=[HBM] * n,
        out_shape=[jax.ShapeDtypeStruct((g.shape[0], g.shape[1] // 2, g.shape[2]), F32) for g in grads],
        scratch_shapes=[pltpu.SemaphoreType.DMA((n,)), pltpu.SemaphoreType.DMA((n,))],
    )(*grads)


def _chip_sum(g, recv, core, name):
    ns, R, C = g.shape
    r2 = R // 2
    tr = _rows(r2, 512)
    nrb = r2 // tr

    def body(core_ref, g_ref, r_ref, o_ref, ob_ref):
        s = g_ref[...] + r_ref[...]
        o_ref[...] = s
        ob_ref[...] = s.astype(BF16)

    out = pl.BlockSpec((None, tr, C), lambda s, i, cr: (s, i, 0))
    return pl.pallas_call(
        body, name=name,
        grid_spec=pltpu.PrefetchScalarGridSpec(
            num_scalar_prefetch=1, grid=(ns, nrb),
            in_specs=[pl.BlockSpec((None, tr, C), lambda s, i, cr: (s, cr[0] * nrb + i, 0)), out],
            out_specs=[out, out]),
        out_shape=[jax.ShapeDtypeStruct((ns, r2, C), F32), jax.ShapeDtypeStruct((ns, r2, C), BF16)],
        compiler_params=_cparams(),
    )(core, g, recv)


def _shard_sum(csum, got, place, name):
    _, r2, C = csum.shape
    tr = _rows(r2, 512)
    nrb = r2 // tr

    def body(place_ref, c_ref, g0_ref, g1_ref, g2_ref, o_ref):
        o_ref[...] = ((c_ref[...] + g0_ref[...].astype(F32)) + g1_ref[...].astype(F32)) + g2_ref[...].astype(F32)

    def got_spec(k):
        return pl.BlockSpec((None, tr, C), lambda i, pr: (k, i, 0))

    return pl.pallas_call(
        body, name=name,
        grid_spec=pltpu.PrefetchScalarGridSpec(
            num_scalar_prefetch=1, grid=(nrb,),
            in_specs=[pl.BlockSpec((None, tr, C), lambda i, pr: (pr[0], i, 0)), got_spec(0), got_spec(1), got_spec(2)],
            out_specs=pl.BlockSpec((tr, C), lambda i, pr: (pr[1] * nrb + i, 0))),
        out_shape=jax.ShapeDtypeStruct((2 * r2, C), F32),
        compiler_params=_cparams(),
    )(place, csum, got, got, got)


def _gather_ride(shards):
    n = len(shards)

    def make(ins, outs, sems):
        lsem, lrsem, ssem, rsem = sems
        x, y, c, chips = _place()
        j = 2 * x + y
        cps = [pltpu.make_async_remote_copy(
            src_ref=ins[a], dst_ref=outs[a].at[j], send_sem=lsem.at[a], recv_sem=lrsem.at[a],
            device_id=(x, y, 1 - c), device_id_type=MESH) for a in range(n)]
        for a in range(n):
            for k, chip in enumerate(chips):
                cps.append(pltpu.make_async_remote_copy(
                    src_ref=_half(ins[a], c, True), dst_ref=_half(outs[a].at[j], c, True),
                    send_sem=ssem.at[a, k], recv_sem=rsem.at[a, k], device_id=(*chip, c), device_id_type=MESH))
        return cps

    return _Ride(shards, [jax.ShapeDtypeStruct((N_SHARDS,) + s.shape, s.dtype) for s in shards],
                 [pltpu.SemaphoreType.DMA((n,)), pltpu.SemaphoreType.DMA((n,)),
                  pltpu.SemaphoreType.DMA((n, 3)), pltpu.SemaphoreType.DMA((n, 3))], make)


def _forward_halves(gathered, name):
    n = len(gathered)

    def body(*refs):
        outs = refs[n:2 * n]
        ssem, rsem = refs[2 * n:]
        x, y, c, chips = _place()
        cps = []
        for a in range(n):
            for k, chip in enumerate(chips):
                part = _half(outs[a].at[2 * chip[0] + chip[1]], c, True)
                cps.append(pltpu.make_async_remote_copy(
                    src_ref=part, dst_ref=part, send_sem=ssem.at[a, k], recv_sem=rsem.at[a, k],
                    device_id=(x, y, 1 - c), device_id_type=MESH))
        for cp in cps:
            cp.start()
        for cp in cps:
            cp.wait()

    return pl.pallas_call(
        body, name=name, in_specs=[HBM] * n, out_specs=[HBM] * n,
        out_shape=[jax.ShapeDtypeStruct(g.shape, g.dtype) for g in gathered],
        input_output_aliases={a: a for a in range(n)},
        scratch_shapes=[pltpu.SemaphoreType.DMA((n, 3)), pltpu.SemaphoreType.DMA((n, 3))],
    )(*gathered)


def _scatter_ride(csums):
    n = len(csums)

    def make(ins, outs, sems):
        ssem, rsem = sems
        x, y, c, chips = _place()
        return [pltpu.make_async_remote_copy(
            src_ref=ins[a].at[2 * chip[0] + chip[1]], dst_ref=outs[a].at[k], send_sem=ssem.at[a, k],
            recv_sem=rsem.at[a, k], device_id=(*chip, c), device_id_type=MESH)
            for a in range(n) for k, chip in enumerate(chips)]

    return _Ride(csums, [jax.ShapeDtypeStruct((3,) + g.shape[1:], g.dtype) for g in csums],
                 [pltpu.SemaphoreType.DMA((n, 3)), pltpu.SemaphoreType.DMA((n, 3))], make)


def _join_halves(fulls, name):
    n = len(fulls)

    def body(*refs):
        outs = refs[n:2 * n]
        ssem, rsem = refs[2 * n:]
        x, y, c, _ = _place()
        cps = [pltpu.make_async_remote_copy(
            src_ref=_half(outs[a], c, True), dst_ref=_half(outs[a], c, True), send_sem=ssem.at[a],
            recv_sem=rsem.at[a], device_id=(x, y, 1 - c), device_id_type=MESH) for a in range(n)]
        for cp in cps:
            cp.start()
        for cp in cps:
            cp.wait()

    return pl.pallas_call(
        body, name=name, in_specs=[HBM] * n, out_specs=[HBM] * n,
        out_shape=[jax.ShapeDtypeStruct(f.shape, F32) for f in fulls],
        input_output_aliases={a: a for a in range(n)},
        scratch_shapes=[pltpu.SemaphoreType.DMA((n,))] * 2,
    )(*fulls)


def _pad_rows(a, rows, cols):
    return jnp.pad(a, ((0, rows - a.shape[0]), (0, cols - a.shape[1])))


def kernel(x, p, ffn1_norm, ffn1_w_gate, ffn1_w_up, ffn1_w_down, mix_norm, w_in, conv_w, conv_b, q_norm, k_norm, w_out, ffn2_norm, ffn2_w_gate, ffn2_w_up, ffn2_w_down, ple_norm, ple_w_gate, ple_w_proj, loss_target, m_ffn1_norm, m_ffn1_w_gate, m_ffn1_w_up, m_ffn1_w_down, m_mix_norm, m_w_in, m_conv_w, m_conv_b, m_q_norm, m_k_norm, m_w_out, m_ffn2_norm, m_ffn2_w_gate, m_ffn2_w_up, m_ffn2_w_down, m_ple_norm, m_ple_w_gate, m_ple_w_proj, v_ffn1_norm, v_ffn1_w_gate, v_ffn1_w_up, v_ffn1_w_down, v_mix_norm, v_w_in, v_conv_w, v_conv_b, v_q_norm, v_k_norm, v_w_out, v_ffn2_norm, v_ffn2_w_gate, v_ffn2_w_up, v_ffn2_w_down, v_ple_norm, v_ple_w_gate, v_ple_w_proj):
    big = dict(ffn1_w_gate=ffn1_w_gate, ffn1_w_up=ffn1_w_up, ffn1_w_down=ffn1_w_down, w_in=w_in, w_out=w_out,
               ffn2_w_gate=ffn2_w_gate, ffn2_w_up=ffn2_w_up, ffn2_w_down=ffn2_w_down,
               ple_w_gate=ple_w_gate, ple_w_proj=ple_w_proj)
    big_m = dict(ffn1_w_gate=m_ffn1_w_gate, ffn1_w_up=m_ffn1_w_up, ffn1_w_down=m_ffn1_w_down, w_in=m_w_in,
                 w_out=m_w_out, ffn2_w_gate=m_ffn2_w_gate, ffn2_w_up=m_ffn2_w_up, ffn2_w_down=m_ffn2_w_down,
                 ple_w_gate=m_ple_w_gate, ple_w_proj=m_ple_w_proj)
    big_v = dict(ffn1_w_gate=v_ffn1_w_gate, ffn1_w_up=v_ffn1_w_up, ffn1_w_down=v_ffn1_w_down, w_in=v_w_in,
                 w_out=v_w_out, ffn2_w_gate=v_ffn2_w_gate, ffn2_w_up=v_ffn2_w_up, ffn2_w_down=v_ffn2_w_down,
                 ple_w_gate=v_ple_w_gate, ple_w_proj=v_ple_w_proj)
    names = list(big)
    xs = x[0]
    ps = p[0, 0]
    tgt = loss_target[0]
    S, D = xs.shape
    nconv = conv_b.shape[1]
    nsb = D - nconv
    cwl = conv_w.shape[2]
    jchip = 2 * lax.axis_index("x") + lax.axis_index("y")
    core = lax.axis_index("c")

    early, late = names[:3], names[3:]
    assert all(k.startswith("ffn1") for k in early)
    shards = {k: big[k][0].astype(BF16) for k in names}
    gathered = _gather_weights([shards[k] for k in early], _pad_rows(conv_w[0], 8, LANES), "gather_weights")
    W = dict(zip(early, gathered[:-1]))
    cw_full = jnp.transpose(gathered[-1][:, :, :cwl], (1, 0, 2)).reshape(8, N_SHARDS * cwl)
    qg = jnp.tile(q_norm, (1, CONV_COLS // HEAD_DIM))
    kg = jnp.tile(k_norm, (1, CONV_COLS // HEAD_DIM))
    n_units = nconv // CONV_COLS

    x1, h1, a1, b1, *landed = _ffn_fwd(xs, ffn1_norm, W["ffn1_w_gate"], W["ffn1_w_up"], W["ffn1_w_down"], "ffn1_fwd",
                                        ride=_gather_ride([shards[k] for k in late]))
    W.update(zip(late, _forward_halves(landed, "gather_forward")))
    wout_full = W["w_out"].reshape(-1, D)
    wpg_full = W["ple_w_gate"].reshape(-1, D)
    proj, h2 = _norm_proj(x1, mix_norm, W["w_in"], "mix_in_proj")
    y_conv = _conv_fwd(proj, cw_full, conv_b, nconv, "conv_fwd")
    qs = _qk_norm(proj, qg, 3 * n_units, nsb, HEAD_DIM ** -0.5, "q_norm_fwd")
    kh = _qk_norm(proj, kg, 4 * n_units, nsb, 1.0, "k_norm_fwd")
    vb = proj[:, 3 * nconv + 2 * nsb:].astype(BF16)
    y_sb, carry = _attn_fwd(qs, kh, vb, "attn_fwd")
    ycat = jnp.concatenate([y_conv, y_sb], axis=1)
    x2 = _out_proj(ycat, wout_full, x1, "mix_out_proj")
    x3, h3, a3, b3 = _ffn_fwd(x2, ffn2_norm, W["ffn2_w_gate"], W["ffn2_w_up"], W["ffn2_w_down"], "ffn2_fwd")

    dx3, h4, du4, dpp, loss_blk, dg_ple = _ple(x3, ps, tgt, ple_norm, wpg_full, W["ple_w_proj"], "ple_loss")
    G = {}
    tk = _rows(S, 2048)
    nk = S // tk
    kd = wpg_full.shape[0] // N_SHARDS
    G["ple_w_gate"] = _tn_matmul(
        h4, du4, pl.BlockSpec((tk, kd), lambda m, k: (k, m)), pl.BlockSpec((tk, D), lambda m, k: (k, 0)),
        (N_SHARDS, kd, D), pl.BlockSpec((None, kd, D), lambda m, k: (m, 0, 0)), (N_SHARDS, nk), "ple_w_gate_grad")
    P = ps.shape[1]
    npp = D // N_SHARDS
    G["ple_w_proj"] = _tn_matmul(
        ps, dpp, pl.BlockSpec((tk, P), lambda m, k: (k, 0)), pl.BlockSpec((tk, npp), lambda m, k: (k, m)),
        (N_SHARDS, P, npp), pl.BlockSpec((None, P, npp), lambda m, k: (m, 0, 0)), (N_SHARDS, nk), "ple_w_proj_grad")

    def ffn_grads(pre, h, s, da, db, dy):
        fs = s.shape[2]
        hs = pl.BlockSpec((tk, D), lambda m, k: (k, 0))
        ss = pl.BlockSpec((None, tk, fs), lambda m, k: (m, k, 0))
        G[pre + "_w_gate"] = _tn_matmul(h, da, hs, ss, (N_SHARDS, D, fs),
                                        pl.BlockSpec((None, D, fs), lambda m, k: (m, 0, 0)), (N_SHARDS, nk), pre + "_w_gate_grad")
        G[pre + "_w_up"] = _tn_matmul(h, db, hs, ss, (N_SHARDS, D, fs),
                                      pl.BlockSpec((None, D, fs), lambda m, k: (m, 0, 0)), (N_SHARDS, nk), pre + "_w_up_grad")
        G[pre + "_w_down"] = _tn_matmul(s, dy, ss, hs, (N_SHARDS, fs, D),
                                        pl.BlockSpec((None, fs, D), lambda m, k: (m, 0, 0)), (N_SHARDS, nk), pre + "_w_down_grad")

    s3, da3, db3, dy3 = _ffn_bwd_act(dx3, a3, b3, W["ffn2_w_down"], "ffn2_bwd_act")
    dx2, dg_ffn2 = _ffn_bwd_in(da3, db3, W["ffn2_w_gate"], W["ffn2_w_up"], dx3, x2, ffn2_norm, "ffn2_bwd_in")
    ffn_grads("ffn2", h3, s3, da3, db3, dy3)

    dycat = _out_proj_bwd(dx2, wout_full, "mix_out_proj_bwd")
    ko = wout_full.shape[0] // N_SHARDS
    G["w_out"] = _tn_matmul(
        ycat, dx2, pl.BlockSpec((tk, ko), lambda m, k: (k, m)), pl.BlockSpec((tk, D), lambda m, k: (k, 0)),
        (N_SHARDS, ko, D), pl.BlockSpec((None, ko, D), lambda m, k: (m, 0, 0)), (N_SHARDS, nk), "w_out_grad")
    db_, dc_, du_, dwb = _conv_bwd(proj, dycat, cw_full, conv_b, nconv, "conv_bwd")
    dqs, dkh, dv = _attn_bwd(qs, kh, vb, dycat, nconv // LANES, carry, "attn_bwd")
    dq, dg_q = _qk_norm_bwd(proj, dqs, qg, 3 * n_units, nsb, HEAD_DIM ** -0.5, "q_norm_bwd")
    dk, dg_k = _qk_norm_bwd(proj, dkh, kg, 4 * n_units, nsb, 1.0, "k_norm_bwd")
    dproj = jnp.concatenate([db_, dc_, du_, dq, dk, dv.astype(BF16)], axis=1)
    nin = W["w_in"].shape[2]
    G["w_in"] = _tn_matmul(
        h2, dproj, pl.BlockSpec((tk, D), lambda m, k: (k, 0)), pl.BlockSpec((tk, nin), lambda m, k: (k, m)),
        (N_SHARDS, D, nin), pl.BlockSpec((None, D, nin), lambda m, k: (m, 0, 0)), (N_SHARDS, nk), "w_in_grad")
    dx1, dg_mix, *recv_late = _norm_proj_bwd(dproj, W["w_in"], dx2, x1, mix_norm, "mix_in_proj_bwd",
                                             ride=_swap_ride([G[k] for k in late]))
    core_arr = jnp.reshape(core, (1,)).astype(jnp.int32)
    place = jnp.stack([jchip, core]).astype(jnp.int32)

    def chip_sums(group, recv):
        return zip(*[_chip_sum(G[k], r, core_arr, f"chip_sum_{k}") for k, r in zip(group, recv)])

    cs_late, csb_late = chip_sums(late, recv_late)
    s1, da1, db1, dy1, *got_late = _ffn_bwd_act(dx1, a1, b1, W["ffn1_w_down"], "ffn1_bwd_act",
                                                  ride=_scatter_ride(list(csb_late)))
    ffn_grads("ffn1", h1, s1, da1, db1, dy1)
    cs_early, csb_early = chip_sums(early, _swap_grad_halves([G[k] for k in early], "grad_swap_halves"))
    dx0, dg_ffn1, *got_early = _ffn_bwd_in(da1, db1, W["ffn1_w_gate"], W["ffn1_w_up"], dx1, xs, ffn1_norm,
                                           "ffn1_bwd_in", ride=_scatter_ride(list(csb_early)))

    assert D >= nconv and D % LANES == 0
    fold = lambda t: t[0].reshape(-1, HEAD_DIM).sum(axis=0)[None, :]
    small_rows = [dg_ffn1[0:1], dg_mix[0:1], dg_ffn2[0:1], dg_ple[0:1],
                  _pad_rows(dwb[3:4], 1, D), _pad_rows(dwb[0:3], 3, D),
                  _pad_rows(fold(dg_q), 1, D), _pad_rows(fold(dg_k), 1, D), _pad_rows(loss_blk[0:1, 0:1], 1, D)]
    small = _pad_rows(jnp.concatenate(small_rows, axis=0), SMALL_ROWS, D)

    small_sum = _sum_small(small, "small_sum")

    full = _join_halves([_shard_sum(cs, gt, place, f"shard_sum_{k}")
                         for k, cs, gt in zip(names, list(cs_early) + list(cs_late), got_early + got_late)],
                        "grad_join_halves")

    out_g, out_d, out_m, out_v = {}, {}, {}, {}
    for a, k in enumerate(names):
        shp = big[k].shape
        g2 = full[a]
        d_, m_, v_ = _elementwise(_adamw, [big[k][0].reshape(g2.shape), g2, big_m[k][0].reshape(g2.shape),
                                           big_v[k][0].reshape(g2.shape)], 3, f"adamw_{k}")
        out_g[k], out_d[k], out_m[k], out_v[k] = (t.reshape(shp) for t in (g2, d_, m_, v_))

    sm_names = ["ffn1_norm", "mix_norm", "ffn2_norm", "ple_norm", "conv_b", "conv_w", "q_norm", "k_norm"]
    sm_w = dict(ffn1_norm=ffn1_norm, mix_norm=mix_norm, ffn2_norm=ffn2_norm, ple_norm=ple_norm, conv_b=conv_b,
                conv_w=conv_w[0], q_norm=q_norm, k_norm=k_norm)
    sm_m = dict(ffn1_norm=m_ffn1_norm, mix_norm=m_mix_norm, ffn2_norm=m_ffn2_norm, ple_norm=m_ple_norm,
                conv_b=m_conv_b, conv_w=m_conv_w[0], q_norm=m_q_norm, k_norm=m_k_norm)
    sm_v = dict(ffn1_norm=v_ffn1_norm, mix_norm=v_mix_norm, ffn2_norm=v_ffn2_norm, ple_norm=v_ple_norm,
                conv_b=v_conv_b, conv_w=v_conv_w[0], q_norm=v_q_norm, k_norm=v_k_norm)
    sm_g = dict(ffn1_norm=small_sum[0:1], mix_norm=small_sum[1:2], ffn2_norm=small_sum[2:3], ple_norm=small_sum[3:4],
                conv_b=small_sum[4:5, :nconv],
                conv_w=lax.dynamic_slice_in_dim(small_sum[5:8, :nconv], jchip * cwl, cwl, axis=1),
                q_norm=small_sum[8:9, :HEAD_DIM], k_norm=small_sum[9:10, :HEAD_DIM])
    loss = small_sum[10, 0]
    pack = lambda d: _pad_rows(jnp.concatenate([_pad_rows(d[k], d[k].shape[0], D) for k in sm_names], axis=0), SMALL_ROWS, D)
    sd, smm, svv = _elementwise(_adamw, [pack(sm_w), pack(sm_g), pack(sm_m), pack(sm_v)], 3, "adamw_small")
    row = 0
    for k in sm_names:
        r_, c_ = sm_w[k].shape
        shp = (1, r_, c_) if k == "conv_w" else (r_, c_)
        out_g[k] = sm_g[k].reshape(shp)
        out_d[k], out_m[k], out_v[k] = (t[row:row + r_, :c_].reshape(shp) for t in (sd, smm, svv))
        row += r_

    order = ["ffn1_norm", "ffn1_w_gate", "ffn1_w_up", "ffn1_w_down", "mix_norm", "w_in", "conv_w", "conv_b",
             "q_norm", "k_norm", "w_out", "ffn2_norm", "ffn2_w_gate", "ffn2_w_up", "ffn2_w_down", "ple_norm",
             "ple_w_gate", "ple_w_proj"]
    return (loss, dx0[None], *[out_g[k] for k in order], *[out_d[k] for k in order],
            *[out_m[k] for k in order], *[out_v[k] for k in order])
```

```python
import jax
import jax.numpy as jnp
from jax import lax
from jax.experimental import pallas as pl
from jax.experimental.pallas import tpu as pltpu

F32 = jnp.float32
BF16 = jnp.bfloat16
MESH = pl.DeviceIdType.MESH

EPS = 1e-6
HEAD_DIM = 64
LANES = 128
FFN_RES = 0.5
ADAM_LR = 0.001
ADAM_B1 = 0.9
ADAM_B2 = 0.999
ADAM_EPS = 1e-08
ADAM_WD = 0.01
ADAM_STEP = 10
N_SHARDS = 4
N_DEV = 8
ATT_TILE = 256
VMEM_LIMIT = 52 * 1024 * 1024
SMALL_ROWS = 16
HBM = pl.BlockSpec(memory_space=pltpu.HBM)
VMEM_WHOLE = pl.BlockSpec(memory_space=pltpu.VMEM)


def _cparams(**kw):
    return pltpu.CompilerParams(vmem_limit_bytes=VMEM_LIMIT, **kw)


def _dot(a, b):
    return jnp.dot(a, b, preferred_element_type=F32)


def _dot_nt(a, b):
    return lax.dot_general(a, b, (((1,), (1,)), ((), ())), preferred_element_type=F32)


def _dot_tn(a, b):
    return lax.dot_general(a, b, (((0,), (0,)), ((), ())), preferred_element_type=F32)


def _split_dot(x, m):
    hi = x.astype(BF16)
    lo = (x - hi.astype(F32)).astype(BF16)
    return _dot(hi, m) + _dot(lo, m)


def _rms(x, g):
    r = lax.rsqrt(jnp.mean(x * x, axis=-1, keepdims=True) + EPS)
    xn = x * r
    return xn * g, xn, r


def _rms_bwd(dh, xn, r, g):
    dxn = dh * g
    dx = r * (dxn - xn * jnp.mean(dxn * xn, axis=-1, keepdims=True))
    return dx, jnp.sum(dh * xn, axis=0, keepdims=True)


def _rows(n, cap=512):
    for t in (2048, 1024, 512, 448, 384, 352, 256, 192, 176, 128, 96, 88, 64, 48, 32, 16, 8):
        if t <= cap and n % t == 0:
            return t
    raise ValueError(f"no row tile for {n}")


class _Ride:
    def __init__(self, ins, out_shapes, sems, make):
        self.ins, self.out_shapes, self.sems, self.make = list(ins), list(out_shapes), list(sems), make

    def split(self, refs, n_in, n_out, n_scratch):
        ni, no = len(self.ins), len(self.out_shapes)
        ins, rin = refs[:n_in], refs[n_in:n_in + ni]
        outs = refs[n_in + ni:n_in + ni + n_out]
        rout = refs[n_in + ni + n_out:n_in + ni + n_out + no]
        rest = refs[n_in + ni + n_out + no:]
        return ins, outs, rest[:n_scratch], lambda: self.make(rin, rout, rest[n_scratch:])


_NO_RIDE = _Ride([], [], [], lambda i, o, s: [])


def _ride_along(copies, first, last):
    @pl.when(first)
    def _():
        for cp in copies():
            cp.start()

    @pl.when(last)
    def _():
        for cp in copies():
            cp.wait()


def _ffn_fwd(x, gain, wg, wu, wd, name, ride=_NO_RIDE):
    S, D = x.shape
    ns, _, fs = wg.shape
    tm = _rows(S, 512)
    ni = S // tm

    def body(*refs):
        (x_ref, g_ref, wg_ref, wu_ref, wd_ref), (xo_ref, h_ref, a_ref, b_ref), (hs, acc), copies = ride.split(refs, 5, 4, 2)
        j = pl.program_id(1)
        _ride_along(copies, (pl.program_id(0) == 0) & (j == 0), (pl.program_id(0) == ni - 1) & (j == ns - 1))

        @pl.when(j == 0)
        def _():
            h, _, _ = _rms(x_ref[...], g_ref[...])
            hb = h.astype(BF16)
            hs[...] = hb
            h_ref[...] = hb
            acc[...] = jnp.zeros_like(acc)

        hb = hs[...]
        a = _dot(hb, wg_ref[...])
        b = _dot(hb, wu_ref[...])
        a_ref[...] = a.astype(BF16)
        b_ref[...] = b.astype(BF16)
        s = (a * jax.nn.sigmoid(a)) * b
        acc[...] += _dot(s.astype(BF16), wd_ref[...])

        @pl.when(j == ns - 1)
        def _():
            xo_ref[...] = x_ref[...] + FFN_RES * acc[...]

    return pl.pallas_call(
        body, name=name, grid=(ni, ns),
        in_specs=[
            pl.BlockSpec((tm, D), lambda i, j: (i, 0)),
            pl.BlockSpec((1, D), lambda i, j: (0, 0)),
            pl.BlockSpec((None, D, fs), lambda i, j: (j, 0, 0)),
            pl.BlockSpec((None, D, fs), lambda i, j: (j, 0, 0)),
            pl.BlockSpec((None, fs, D), lambda i, j: (j, 0, 0)),
        ] + [HBM] * len(ride.ins),
        out_specs=[
            pl.BlockSpec((tm, D), lambda i, j: (i, 0)),
            pl.BlockSpec((tm, D), lambda i, j: (i, 0)),
            pl.BlockSpec((None, tm, fs), lambda i, j: (j, i, 0)),
            pl.BlockSpec((None, tm, fs), lambda i, j: (j, i, 0)),
        ] + [HBM] * len(ride.out_shapes),
        out_shape=[
            jax.ShapeDtypeStruct((S, D), F32),
            jax.ShapeDtypeStruct((S, D), BF16),
            jax.ShapeDtypeStruct((ns, S, fs), BF16),
            jax.ShapeDtypeStruct((ns, S, fs), BF16),
        ] + ride.out_shapes,
        scratch_shapes=[pltpu.VMEM((tm, D), BF16), pltpu.VMEM((tm, D), F32)] + ride.sems,
        compiler_params=_cparams(),
    )(x, gain, wg, wu, wd, *ride.ins)


def _ffn_bwd_act(dxo, a, b, wd, name, ride=_NO_RIDE):
    S, D = dxo.shape
    ns, fs, _ = wd.shape
    tm = _rows(S, 512)
    ni = S // tm

    def body(*refs):
        (dxo_ref, a_ref, b_ref, wd_ref), (s_ref, da_ref, db_ref, dy_ref), (dys,), copies = ride.split(refs, 4, 4, 1)
        j = pl.program_id(1)
        _ride_along(copies, (pl.program_id(0) == 0) & (j == 0), (pl.program_id(0) == ni - 1) & (j == ns - 1))

        @pl.when(j == 0)
        def _():
            dy = (FFN_RES * dxo_ref[...]).astype(BF16)
            dys[...] = dy
            dy_ref[...] = dy

        av = a_ref[...].astype(F32)
        bv = b_ref[...].astype(F32)
        ds = _dot_nt(dys[...], wd_ref[...])
        sig = jax.nn.sigmoid(av)
        sl = av * sig
        s_ref[...] = (sl * bv).astype(BF16)
        da_ref[...] = (ds * bv * (sig * (1.0 + av * (1.0 - sig)))).astype(BF16)
        db_ref[...] = (ds * sl).astype(BF16)

    act = pl.BlockSpec((None, tm, fs), lambda i, j: (j, i, 0))
    row = pl.BlockSpec((tm, D), lambda i, j: (i, 0))
    return pl.pallas_call(
        body, name=name, grid=(ni, ns),
        in_specs=[row, act, act, pl.BlockSpec((None, fs, D), lambda i, j: (j, 0, 0))] + [HBM] * len(ride.ins),
        out_specs=[act, act, act, row] + [HBM] * len(ride.out_shapes),
        out_shape=[jax.ShapeDtypeStruct((ns, S, fs), BF16)] * 3 + [jax.ShapeDtypeStruct((S, D), BF16)]
        + ride.out_shapes,
        scratch_shapes=[pltpu.VMEM((tm, D), BF16)] + ride.sems,
        compiler_params=_cparams(),
    )(dxo, a, b, wd, *ride.ins)


def _ffn_bwd_in(da, db, wg, wu, dres, x, gain, name, ride=_NO_RIDE):
    S, D = x.shape
    ns, _, fs = wg.shape
    tm = _rows(S, 512)
    ni = S // tm

    def body(*refs):
        (da_ref, db_ref, wg_ref, wu_ref, dres_ref, x_ref, g_ref), (dx_ref, dg_ref), (acc,), copies = ride.split(refs, 7, 2, 1)
        i = pl.program_id(0)
        j = pl.program_id(1)
        _ride_along(copies, (i == 0) & (j == 0), (i == ni - 1) & (j == ns - 1))

        @pl.when((i == 0) & (j == 0))
        def _():
            dg_ref[...] = jnp.zeros_like(dg_ref)

        @pl.when(j == 0)
        def _():
            acc[...] = jnp.zeros_like(acc)

        acc[...] += _dot_nt(da_ref[...], wg_ref[...]) + _dot_nt(db_ref[...], wu_ref[...])

        @pl.when(j == ns - 1)
        def _():
            g = g_ref[...]
            _, xn, r = _rms(x_ref[...], g)
            dx, dg = _rms_bwd(acc[...], xn, r, g)
            dx_ref[...] = dres_ref[...] + dx
            dg_ref[...] += jnp.broadcast_to(dg, dg_ref.shape)

    act = pl.BlockSpec((None, tm, fs), lambda i, j: (j, i, 0))
    row = pl.BlockSpec((tm, D), lambda i, j: (i, 0))
    wsp = pl.BlockSpec((None, D, fs), lambda i, j: (j, 0, 0))
    return pl.pallas_call(
        body, name=name, grid=(ni, ns),
        in_specs=[act, act, wsp, wsp, row, row, pl.BlockSpec((1, D), lambda i, j: (0, 0))] + [HBM] * len(ride.ins),
        out_specs=[row, pl.BlockSpec((8, D), lambda i, j: (0, 0))] + [HBM] * len(ride.out_shapes),
        out_shape=[jax.ShapeDtypeStruct((S, D), F32), jax.ShapeDtypeStruct((8, D), F32)] + ride.out_shapes,
        scratch_shapes=[pltpu.VMEM((tm, D), F32)] + ride.sems,
        compiler_params=_cparams(),
    )(da, db, wg, wu, dres, x, gain, *ride.ins)


def _tn_matmul(a, b, a_spec, b_spec, o_shape, o_spec, grid, name):
    kaxis = len(grid) - 1

    def body(a_ref, b_ref, o_ref):
        @pl.when(pl.program_id(kaxis) == 0)
        def _():
            o_ref[...] = jnp.zeros_like(o_ref)

        o_ref[...] += _dot_tn(a_ref[...].astype(BF16), b_ref[...].astype(BF16))

    return pl.pallas_call(
        body, name=name, grid=grid, in_specs=[a_spec, b_spec], out_specs=o_spec,
        out_shape=jax.ShapeDtypeStruct(o_shape, F32), compiler_params=_cparams(),
    )(a, b)


def _norm_proj(x, gain, w, name):
    S, D = x.shape
    ns, _, n = w.shape
    tm = _rows(S, 512)

    def body(x_ref, g_ref, w_ref, o_ref, h_ref, hs):
        @pl.when(pl.program_id(1) == 0)
        def _():
            h, _, _ = _rms(x_ref[...], g_ref[...])
            hb = h.astype(BF16)
            hs[...] = hb
            h_ref[...] = hb

        o_ref[...] = _dot(hs[...], w_ref[...])

    return pl.pallas_call(
        body, name=name, grid=(S // tm, ns),
        in_specs=[
            pl.BlockSpec((tm, D), lambda i, j: (i, 0)),
            pl.BlockSpec((1, D), lambda i, j: (0, 0)),
            pl.BlockSpec((None, D, n), lambda i, j: (j, 0, 0)),
        ],
        out_specs=[
            pl.BlockSpec((tm, n), lambda i, j: (i, j)),
            pl.BlockSpec((tm, D), lambda i, j: (i, 0)),
        ],
        out_shape=[jax.ShapeDtypeStruct((S, ns * n), F32), jax.ShapeDtypeStruct((S, D), BF16)],
        scratch_shapes=[pltpu.VMEM((tm, D), BF16)],
        compiler_params=_cparams(),
    )(x, gain, w)


def _norm_proj_bwd(dproj, w, dres, x, gain, name, ride=_NO_RIDE):
    S, D = x.shape
    ns, _, n = w.shape
    tm = _rows(S, 512)
    ni = S // tm

    def body(*refs):
        (dp_ref, w_ref, dres_ref, x_ref, g_ref), (dx_ref, dg_ref), (acc,), copies = ride.split(refs, 5, 2, 1)
        i = pl.program_id(0)
        j = pl.program_id(1)
        _ride_along(copies, (i == 0) & (j == 0), (i == ni - 1) & (j == ns - 1))

        @pl.when((i == 0) & (j == 0))
        def _():
            dg_ref[...] = jnp.zeros_like(dg_ref)

        @pl.when(j == 0)
        def _():
            acc[...] = jnp.zeros_like(acc)

        acc[...] += _dot_nt(dp_ref[...], w_ref[...])

        @pl.when(j == ns - 1)
        def _():
            g = g_ref[...]
            _, xn, r = _rms(x_ref[...], g)
            dx, dg = _rms_bwd(acc[...], xn, r, g)
            dx_ref[...] = dres_ref[...] + dx
            dg_ref[...] += jnp.broadcast_to(dg, dg_ref.shape)

    return pl.pallas_call(
        body, name=name, grid=(ni, ns),
        in_specs=[
            pl.BlockSpec((tm, n), lambda i, j: (i, j)),
            pl.BlockSpec((None, D, n), lambda i, j: (j, 0, 0)),
            pl.BlockSpec((tm, D), lambda i, j: (i, 0)),
            pl.BlockSpec((tm, D), lambda i, j: (i, 0)),
            pl.BlockSpec((1, D), lambda i, j: (0, 0)),
        ] + [HBM] * len(ride.ins),
        out_specs=[
            pl.BlockSpec((tm, D), lambda i, j: (i, 0)),
            pl.BlockSpec((8, D), lambda i, j: (0, 0)),
        ] + [HBM] * len(ride.out_shapes),
        out_shape=[jax.ShapeDtypeStruct((S, D), F32), jax.ShapeDtypeStruct((8, D), F32)] + ride.out_shapes,
        scratch_shapes=[pltpu.VMEM((tm, D), F32)] + ride.sems,
        compiler_params=_cparams(),
    )(dproj, w, dres, x, gain, *ride.ins)


def _out_proj(ycat, w, res, name):
    S, K = ycat.shape
    D = w.shape[1]
    tm = _rows(S, 512)

    def body(y_ref, w_ref, r_ref, o_ref):
        o_ref[...] = r_ref[...] + _dot(y_ref[...], w_ref[...])

    return pl.pallas_call(
        body, name=name, grid=(S // tm,),
        in_specs=[
            pl.BlockSpec((tm, K), lambda i: (i, 0)),
            pl.BlockSpec((K, D), lambda i: (0, 0)),
            pl.BlockSpec((tm, D), lambda i: (i, 0)),
        ],
        out_specs=pl.BlockSpec((tm, D), lambda i: (i, 0)),
        out_shape=jax.ShapeDtypeStruct((S, D), F32),
        compiler_params=_cparams(),
    )(ycat, w, res)


def _out_proj_bwd(dx, w, name):
    S, D = dx.shape
    K = w.shape[0]
    tm = _rows(S, 512)

    def body(d_ref, w_ref, o_ref):
        o_ref[...] = _dot_nt(d_ref[...].astype(BF16), w_ref[...])

    return pl.pallas_call(
        body, name=name, grid=(S // tm,),
        in_specs=[pl.BlockSpec((tm, D), lambda i: (i, 0)), pl.BlockSpec((K, D), lambda i: (0, 0))],
        out_specs=pl.BlockSpec((tm, K), lambda i: (i, 0)),
        out_shape=jax.ShapeDtypeStruct((S, K), F32),
        compiler_params=_cparams(),
    )(dx, w)


CONV_COLS = 256


def _shift_down(z, halo, k, row):
    out = pltpu.roll(z, k, 0)
    for n in range(k):
        out = jnp.where(row == n, halo[8 - k + n:8 - k + n + 1, :], out)
    return out


def _shift_up(g, halo, k, row, ts):
    out = pltpu.roll(g, ts - k, 0)
    for n in range(k):
        out = jnp.where(row == ts - k + n, halo[n:n + 1, :], out)
    return out


def _conv_fwd(proj, cw, cb, nconv, name):
    S = proj.shape[0]
    ncb = nconv // CONV_COLS
    ts = _rows(S, 512)
    hb = ts // 8

    def body(b_ref, c_ref, u_ref, ch_ref, uh_ref, w_ref, bias_ref, o_ref):
        i = pl.program_id(1)
        z = c_ref[...] * u_ref[...]
        halo = jnp.where(i > 0, ch_ref[...] * uh_ref[...], 0.0)
        row = lax.broadcasted_iota(jnp.int32, z.shape, 0)
        w = w_ref[...]
        yc = w[0:1, :] * _shift_down(z, halo, 2, row) + w[1:2, :] * _shift_down(z, halo, 1, row) + w[2:3, :] * z
        o_ref[...] = (b_ref[...] * (yc + bias_ref[...])).astype(BF16)

    def blk(unit):
        return pl.BlockSpec((ts, CONV_COLS), lambda cbi, i: (i, unit * ncb + cbi))

    def prev(unit):
        return pl.BlockSpec((8, CONV_COLS), lambda cbi, i: (jnp.maximum(i * hb - 1, 0), unit * ncb + cbi))

    return pl.pallas_call(
        body, name=name, grid=(ncb, S // ts),
        in_specs=[blk(0), blk(1), blk(2), prev(1), prev(2),
                  pl.BlockSpec((8, CONV_COLS), lambda cbi, i: (0, cbi)),
                  pl.BlockSpec((1, CONV_COLS), lambda cbi, i: (0, cbi))],
        out_specs=pl.BlockSpec((ts, CONV_COLS), lambda cbi, i: (i, cbi)),
        out_shape=jax.ShapeDtypeStruct((S, nconv), BF16),
        compiler_params=_cparams(),
    )(proj, proj, proj, proj, proj, cw, cb)


def _conv_bwd(proj, dy, cw, cb, nconv, name):
    S = proj.shape[0]
    ncb = nconv // CONV_COLS
    ts = _rows(S, 512)
    hb = ts // 8
    nblk = S // ts

    def body(b_ref, c_ref, u_ref, dy_ref, ch_ref, uh_ref, bn_ref, dyn_ref, w_ref, bias_ref,
             db_ref, dc_ref, du_ref, dw_ref):
        i = pl.program_id(1)

        @pl.when(i == 0)
        def _():
            dw_ref[...] = jnp.zeros_like(dw_ref)

        c = c_ref[...]
        u = u_ref[...]
        bg = b_ref[...]
        dy_ = dy_ref[...]
        z = c * u
        halo = jnp.where(i > 0, ch_ref[...] * uh_ref[...], 0.0)
        row = lax.broadcasted_iota(jnp.int32, z.shape, 0)
        w = w_ref[...]
        z2 = _shift_down(z, halo, 2, row)
        z1 = _shift_down(z, halo, 1, row)
        yc = w[0:1, :] * z2 + w[1:2, :] * z1 + w[2:3, :] * z
        db_ref[...] = (dy_ * (yc + bias_ref[...])).astype(BF16)
        g = dy_ * bg
        gnext = jnp.where(i < nblk - 1, dyn_ref[...] * bn_ref[...], 0.0)
        dz = w[2:3, :] * g + w[1:2, :] * _shift_up(g, gnext, 1, row, ts) + w[0:1, :] * _shift_up(g, gnext, 2, row, ts)
        dc_ref[...] = (dz * u).astype(BF16)
        du_ref[...] = (dz * c).astype(BF16)
        r8 = lax.broadcasted_iota(jnp.int32, (8, CONV_COLS), 0)
        sums = [jnp.sum(g * z2, axis=0, keepdims=True), jnp.sum(g * z1, axis=0, keepdims=True),
                jnp.sum(g * z, axis=0, keepdims=True), jnp.sum(g, axis=0, keepdims=True)]
        upd = jnp.zeros((8, CONV_COLS), F32)
        for n, sv in enumerate(sums):
            upd = jnp.where(r8 == n, sv, upd)
        dw_ref[...] += upd

    def blk(unit):
        return pl.BlockSpec((ts, CONV_COLS), lambda cbi, i: (i, unit * ncb + cbi))

    def prev(unit):
        return pl.BlockSpec((8, CONV_COLS), lambda cbi, i: (jnp.maximum(i * hb - 1, 0), unit * ncb + cbi))

    def nxt(unit):
        return pl.BlockSpec((8, CONV_COLS), lambda cbi, i: (jnp.minimum((i + 1) * hb, S // 8 - 1), unit * ncb + cbi))

    o = pl.BlockSpec((ts, CONV_COLS), lambda cbi, i: (i, cbi))
    return pl.pallas_call(
        body, name=name, grid=(ncb, nblk),
        in_specs=[blk(0), blk(1), blk(2), blk(0), prev(1), prev(2), nxt(0), nxt(0),
                  pl.BlockSpec((8, CONV_COLS), lambda cbi, i: (0, cbi)),
                  pl.BlockSpec((1, CONV_COLS), lambda cbi, i: (0, cbi))],
        out_specs=[o, o, o, pl.BlockSpec((8, CONV_COLS), lambda cbi, i: (0, cbi))],
        out_shape=[jax.ShapeDtypeStruct((S, nconv), BF16)] * 3 + [jax.ShapeDtypeStruct((8, nconv), F32)],
        compiler_params=_cparams(),
    )(proj, proj, proj, dy, proj, proj, proj, dy, cw, cb)


def _group_ones(n):
    r = lax.broadcasted_iota(jnp.int32, (n, n), 0) // HEAD_DIM
    c = lax.broadcasted_iota(jnp.int32, (n, n), 1) // HEAD_DIM
    return jnp.where(r == c, 1.0, 0.0).astype(BF16)


def _qk_norm(proj, gain_t, unit0, nsb, scale, name):
    S = proj.shape[0]
    nb = nsb // CONV_COLS
    ts = _rows(S, 512)

    def body(x_ref, g_ref, o_ref):
        x = x_ref[...]
        ss = _split_dot(x * x, _group_ones(CONV_COLS))
        r = lax.rsqrt(ss * (1.0 / HEAD_DIM) + EPS)
        o_ref[...] = ((x * r) * g_ref[...] * scale).astype(BF16)

    return pl.pallas_call(
        body, name=name, grid=(nb, S // ts),
        in_specs=[pl.BlockSpec((ts, CONV_COLS), lambda u, i: (i, unit0 + u)),
                  pl.BlockSpec((1, CONV_COLS), lambda u, i: (0, 0))],
        out_specs=pl.BlockSpec((ts, CONV_COLS), lambda u, i: (i, u)),
        out_shape=jax.ShapeDtypeStruct((S, nsb), BF16),
        compiler_params=_cparams(),
    )(proj, gain_t)


def _qk_norm_bwd(proj, dout, gain_t, unit0, nsb, scale, name):
    S = proj.shape[0]
    nb = nsb // CONV_COLS
    ts = _rows(S, 512)

    def body(x_ref, d_ref, g_ref, dx_ref, dg_ref):
        @pl.when(pl.program_id(1) == 0)
        def _():
            dg_ref[...] = jnp.zeros_like(dg_ref)

        x = x_ref[...]
        g = g_ref[...]
        ones = _group_ones(CONV_COLS)
        ss = _split_dot(x * x, ones)
        r = lax.rsqrt(ss * (1.0 / HEAD_DIM) + EPS)
        xn = x * r
        dh = d_ref[...] * scale
        dxn = dh * g
        m = _split_dot(dxn * xn, ones) * (1.0 / HEAD_DIM)
        dx_ref[...] = (r * (dxn - xn * m)).astype(BF16)
        dg_ref[...] += jnp.broadcast_to(jnp.sum(dh * xn, axis=0, keepdims=True), dg_ref.shape)

    return pl.pallas_call(
        body, name=name, grid=(nb, S // ts),
        in_specs=[pl.BlockSpec((ts, CONV_COLS), lambda u, i: (i, unit0 + u)),
                  pl.BlockSpec((ts, CONV_COLS), lambda u, i: (i, u)),
                  pl.BlockSpec((1, CONV_COLS), lambda u, i: (0, 0))],
        out_specs=[pl.BlockSpec((ts, CONV_COLS), lambda u, i: (i, u)),
                   pl.BlockSpec((8, CONV_COLS), lambda u, i: (0, u))],
        out_shape=[jax.ShapeDtypeStruct((S, nsb), BF16), jax.ShapeDtypeStruct((8, nsb), F32)],
        compiler_params=_cparams(),
    )(proj, dout, gain_t)


Z_CLAMP = 80.0
N_SLOTS = 3
SAT_LIMIT = 120.0


def _head_masks():
    lane = lax.broadcasted_iota(jnp.int32, (1, LANES), 1)
    return [lane < HEAD_DIM, lane >= HEAD_DIM], lane


def _tile_consts(T):
    r_i = lax.broadcasted_iota(jnp.int32, (T, T), 0)
    c_i = lax.broadcasted_iota(jnp.int32, (T, T), 1)
    neg_suffix = jnp.where(r_i >= c_i, -1.0, 0.0).astype(BF16)
    prefix = jnp.where(r_i <= c_i, 1.0, 0.0).astype(BF16)
    return neg_suffix, prefix, c_i < r_i


def _pipeline(n, stages, first_special=False, last_special=False, saturated=None, extra_head=0):
    depth = len(stages)
    head = (depth if first_special else depth - 1) + extra_head
    off = 0 if last_special else 1
    for m in range(head):
        for k in reversed(range(min(m, depth - 1) + 1)):
            stages[k](m - k, (m - k) % N_SLOTS, first_special and m == k)

    def trip(m, u):
        for k in reversed(range(depth)):
            stages[k](m - k, (head + u - k) % N_SLOTS, False)

    def group(g, carry):
        for u in range(N_SLOTS):
            trip(head + g * N_SLOTS + u, u)
        return carry

    count = n - 1 + off - head
    full = count // N_SLOTS
    if saturated is None:
        lax.fori_loop(0, full, group, 0)
        go_on, done = True, n
    else:
        def more(state):
            return (state[0] < full) & (state[1] == 0)

        def step(state):
            group(state[0], 0)
            return state[0] + 1, saturated().astype(jnp.int32)

        groups, stop = lax.while_loop(more, step, (jnp.int32(0), saturated().astype(jnp.int32)))
        go_on = stop == 0
        done = jnp.where(go_on, n, head - depth + 1 + N_SLOTS * groups)
    for r in range(N_SLOTS):
        @pl.when((count - full * N_SLOTS == r) & go_on)
        def _(r=r):
            for u in range(r):
                trip(head + full * N_SLOTS + u, u)
            for e in range(depth - off):
                for k in reversed(range(e + off, depth)):
                    t = n - 1 - (k - e - off)
                    stages[k](t, (head + r + e - k) % N_SLOTS, last_special and k == e + off)
    return done


def _sweep(n, stages, finish, first_special=False, last_special=False, saturated=None, extra_head=0):
    depth = len(stages)
    least = (depth if first_special else depth - 1) + extra_head + (1 if last_special else 0)
    for short in range(1, least):
        @pl.when(n == short)
        def _(short=short):
            for m in range(short + depth - 1):
                for k in reversed(range(depth)):
                    t = m - k
                    if 0 <= t < short:
                        stages[k](t, t % N_SLOTS, (first_special and t == 0) or (last_special and t == short - 1))
            finish(short)

    @pl.when(n >= least)
    def _():
        finish(_pipeline(n, stages, first_special, last_special, saturated, extra_head))


def _attn_fwd(q, k, v, name):
    S, nsb = q.shape
    T = ATT_TILE
    hp = nsb // LANES
    nb = S // T
    assert nb <= HEAD_DIM

    def body(q_ref, k_ref, v_ref, y_ref, cs_ref, c_ref, acc, z_st, inc_st):
        i = pl.program_id(1)
        masks, lane = _head_masks()
        qv = q_ref[...]
        qm = [jnp.where(m, qv, jnp.zeros_like(qv)) for m in masks]
        neg_suffix, _, causal = _tile_consts(T)
        c_ref[...] = jnp.zeros_like(c_ref)
        acc[...] = jnp.zeros_like(acc)
        cs_ref[...] = jnp.zeros_like(cs_ref)

        def blk(ref, j):
            return ref[pl.ds(pl.multiple_of(j * T, T), T), :]

        def scores(t, slot, diag):
            kj = blk(k_ref, i - t)
            for h in range(2):
                z_st[slot, h] = jnp.minimum(_dot_nt(qm[h], kj), Z_CLAMP)

        def suffix_sums(t, slot, diag):
            for h in range(2):
                sp = jnp.log(1.0 + jnp.exp(z_st[slot, h]))
                if diag:
                    sp = jnp.where(causal, sp, 0.0)
                inc_st[slot, h] = _dot(sp.astype(BF16), neg_suffix)

        def weights(t, slot, diag):
            vj = blk(v_ref, i - t)
            for h in range(2):
                inc = inc_st[slot, h]
                c = c_ref[h]
                a = jnp.exp(z_st[slot, h] + inc + c)
                if diag:
                    a = jnp.where(causal, a, 0.0)
                upd = _dot(a.astype(BF16), vj)
                acc[...] += jnp.where(masks[h], upd, 0.0)
                cs_ref[...] = jnp.where(lane == i - t + HEAD_DIM * h, c, cs_ref[...])
                c_ref[h] = c + inc[:, 0:1]

        stages = [scores, suffix_sums, weights]

        def saturated():
            return jnp.max(c_ref[...]) < -SAT_LIMIT

        def note(used):
            cs_ref[...] = jnp.where(lane == LANES - 1, jnp.asarray(used).astype(F32), cs_ref[...])

        _sweep(i + 1, stages, note, first_special=True, saturated=saturated, extra_head=1)
        y_ref[...] = acc[...].astype(BF16)

    return pl.pallas_call(
        body, name=name, grid=(hp, nb),
        in_specs=[pl.BlockSpec((T, LANES), lambda p, i: (i, p)),
                  pl.BlockSpec((S, LANES), lambda p, i: (0, p)),
                  pl.BlockSpec((S, LANES), lambda p, i: (0, p))],
        out_specs=[pl.BlockSpec((T, LANES), lambda p, i: (i, p)),
                   pl.BlockSpec((None, T, LANES), lambda p, i: (p, i, 0))],
        out_shape=[jax.ShapeDtypeStruct((S, nsb), BF16), jax.ShapeDtypeStruct((hp, S, LANES), F32)],
        scratch_shapes=[pltpu.VMEM((2, T, 1), F32), pltpu.VMEM((T, LANES), F32),
                        pltpu.VMEM((N_SLOTS, 2, T, T), F32), pltpu.VMEM((N_SLOTS, 2, T, T), F32)],
        compiler_params=_cparams(),
    )(q, k, v)


def _attn_bwd(q, k, v, dy, col0, carry, name):
    S, nsb = q.shape
    T = ATT_TILE
    hp = nsb // LANES
    nb = S // T

    def body(q_ref, k_ref, v_ref, dy_ref, cs_ref, dq_ref, dk_ref, dv_ref, e_ref, acc,
             z_st, da_st, b_st, inc_st, a_st, e_st, p_st):
        i = pl.program_id(1)

        @pl.when(i == 0)
        def _():
            dk_ref[...] = jnp.zeros_like(dk_ref)
            dv_ref[...] = jnp.zeros_like(dv_ref)

        masks, lane = _head_masks()
        qv = q_ref[...]
        dyb = dy_ref[...].astype(BF16)
        qm = [jnp.where(m, qv, jnp.zeros_like(qv)) for m in masks]
        dym = [jnp.where(m, dyb, jnp.zeros_like(dyb)) for m in masks]
        neg_suffix, prefix, causal = _tile_consts(T)
        e_ref[...] = jnp.zeros_like(e_ref)
        acc[...] = jnp.zeros_like(acc)

        def blk(ref, j):
            return ref[pl.ds(pl.multiple_of(j * T, T), T), :]

        used = jnp.max(jnp.where(lane == LANES - 1, cs_ref[...], 0.0)).astype(jnp.int32)
        n = jnp.clip(used, 1, i + 1)
        first = i + 1 - n

        def scores(t, slot, diag):
            kj = blk(k_ref, first + t)
            vj = blk(v_ref, first + t)
            for h in range(2):
                z_st[slot, h] = jnp.minimum(_dot_nt(qm[h], kj), Z_CLAMP)
                da_st[slot, h] = _dot_nt(dym[h], vj)

        def suffix_sums(t, slot, diag):
            for h in range(2):
                u = jnp.exp(z_st[slot, h])
                w = 1.0 + u
                b_st[slot, h] = u / w
                sp = jnp.log(w)
                if diag:
                    sp = jnp.where(causal, sp, 0.0)
                inc_st[slot, h] = _dot(sp.astype(BF16), neg_suffix)

        def probs(t, slot, diag):
            csv = cs_ref[...]
            for h in range(2):
                c = jnp.sum(jnp.where(lane == first + t + HEAD_DIM * h, csv, 0.0), axis=-1, keepdims=True)
                a = jnp.exp(z_st[slot, h] + inc_st[slot, h] + c)
                if diag:
                    a = jnp.where(causal, a, 0.0)
                a_st[slot, h] = a.astype(BF16)
                e = a * da_st[slot, h]
                e_st[slot, h] = e
                p_st[slot, h] = _dot(e.astype(BF16), prefix)

        def grads(t, slot, diag):
            kj = blk(k_ref, first + t)
            off = pl.multiple_of((first + t) * T, T)
            for h in range(2):
                p = p_st[slot, h]
                dz = e_st[slot, h] - b_st[slot, h] * (e_ref[h] + p)
                if diag:
                    dz = jnp.where(causal, dz, 0.0)
                dzb = dz.astype(BF16)
                acc[...] += jnp.where(masks[h], _dot(dzb, kj), 0.0)
                dk_ref[pl.ds(off, T), :] += _dot_tn(dzb, qm[h])
                dv_ref[pl.ds(off, T), :] += _dot_tn(a_st[slot, h], dym[h])
                e_ref[h] += p[:, T - 1:T]

        stages = [scores, suffix_sums, probs, grads]

        _sweep(n, stages, lambda done: None, last_special=True)
        dq_ref[...] = acc[...]

    return pl.pallas_call(
        body, name=name, grid=(hp, nb),
        in_specs=[pl.BlockSpec((T, LANES), lambda p, i: (i, p)),
                  pl.BlockSpec((S, LANES), lambda p, i: (0, p)),
                  pl.BlockSpec((S, LANES), lambda p, i: (0, p)),
                  pl.BlockSpec((T, LANES), lambda p, i: (i, col0 + p)),
                  pl.BlockSpec((None, T, LANES), lambda p, i: (p, i, 0))],
        out_specs=[pl.BlockSpec((T, LANES), lambda p, i: (i, p)),
                   pl.BlockSpec((S, LANES), lambda p, i: (0, p)),
                   pl.BlockSpec((S, LANES), lambda p, i: (0, p))],
        out_shape=[jax.ShapeDtypeStruct((S, nsb), F32)] * 3,
        scratch_shapes=[pltpu.VMEM((2, T, 1), F32), pltpu.VMEM((T, LANES), F32)]
        + [pltpu.VMEM((N_SLOTS, 2, T, T), dt) for dt in (F32, F32, F32, F32, BF16, F32, F32)],
        compiler_params=_cparams(),
    )(q, k, v, dy, carry)


def _ple(x, p, tgt, gain, wpg, wpp, name):
    S, D = x.shape
    P = p.shape[1]
    ns, _, nc = wpp.shape
    tm = _rows(S, 256)

    def body(x_ref, p_ref, t_ref, g_ref, wpg_ref, wpp_ref, dx_ref, h_ref, du_ref, dpp_ref, loss_ref, dg_ref):
        @pl.when(pl.program_id(0) == 0)
        def _():
            loss_ref[...] = jnp.zeros_like(loss_ref)
            dg_ref[...] = jnp.zeros_like(dg_ref)

        x_ = x_ref[...]
        g = g_ref[...]
        h, xn, r = _rms(x_, g)
        hb = h.astype(BF16)
        h_ref[...] = hb
        gate = jax.nn.sigmoid(_dot(hb, wpg_ref[...]))
        pb = p_ref[...].astype(BF16)
        pp = jnp.concatenate([_dot(pb, wpp_ref[n]) for n in range(ns)], axis=1)
        err = (x_ + gate * pp) - t_ref[...]
        loss_ref[...] += (0.5 / D) * jnp.sum(err * err)
        dy = err * (1.0 / D)
        du = ((dy * pp) * (gate * (1.0 - gate))).astype(BF16)
        du_ref[...] = du
        dpp_ref[...] = (dy * gate).astype(BF16)
        dx, dg = _rms_bwd(_dot_nt(du, wpg_ref[...]), xn, r, g)
        dx_ref[...] = dy + dx
        dg_ref[...] += jnp.broadcast_to(dg, dg_ref.shape)

    row = pl.BlockSpec((tm, D), lambda i: (i, 0))
    return pl.pallas_call(
        body, name=name, grid=(S // tm,),
        in_specs=[row, pl.BlockSpec((tm, P), lambda i: (i, 0)), row,
                  pl.BlockSpec((1, D), lambda i: (0, 0)),
                  pl.BlockSpec((D, D), lambda i: (0, 0)),
                  pl.BlockSpec((ns, P, nc), lambda i: (0, 0, 0))],
        out_specs=[row, row, row, row,
                   pl.BlockSpec((8, LANES), lambda i: (0, 0)),
                   pl.BlockSpec((8, D), lambda i: (0, 0))],
        out_shape=[jax.ShapeDtypeStruct((S, D), F32)] + [jax.ShapeDtypeStruct((S, D), BF16)] * 3
        + [jax.ShapeDtypeStruct((8, LANES), F32), jax.ShapeDtypeStruct((8, D), F32)],
        compiler_params=_cparams(),
    )(x, p, tgt, gain, wpg, wpp)


def _elementwise(fn, ins, n_out, name):
    R, C = ins[0].shape
    tr = _rows(R, 512)

    def body(*refs):
        outs = fn(*[r[...] for r in refs[:len(ins)]])
        for o_ref, o in zip(refs[len(ins):], outs):
            o_ref[...] = o

    spec = pl.BlockSpec((tr, C), lambda i: (i, 0))
    return pl.pallas_call(
        body, name=name, grid=(R // tr,), in_specs=[spec] * len(ins), out_specs=[spec] * n_out,
        out_shape=[jax.ShapeDtypeStruct((R, C), F32)] * n_out, compiler_params=_cparams(),
    )(*ins)


def _adamw(w, g, m, v):
    m = ADAM_B1 * m + (1.0 - ADAM_B1) * g
    v = ADAM_B2 * v + (1.0 - ADAM_B2) * jnp.square(g)
    m_hat = m / (1.0 - ADAM_B1 ** ADAM_STEP)
    v_hat = v / (1.0 - ADAM_B2 ** ADAM_STEP)
    delta = -ADAM_LR * (m_hat / (jnp.sqrt(v_hat) + ADAM_EPS) + ADAM_WD * w)
    return delta, m, v


def _place():
    x, y, c = lax.axis_index("x"), lax.axis_index("y"), lax.axis_index("c")
    chips = [(1 - x, y), (x, 1 - y), (1 - x, 1 - y)]
    return x, y, c, chips


def _half(ref, c, axis_rows):
    n = ref.shape[-2]
    start = pl.multiple_of(c * (n // 2), 8)
    idx = (slice(None),) * (len(ref.shape) - 2) + (pl.ds(start, n // 2), slice(None))
    return ref.at[idx]


def _gather_weights(shards, small, name):
    n = len(shards)

    def body(*refs):
        ins, small_in = refs[:n], refs[n]
        outs, small_out = refs[n + 1:2 * n + 1], refs[2 * n + 1]
        lsem, lrsem, ssem, rsem, sm_s, sm_r = refs[2 * n + 2:]
        x, y, c, chips = _place()
        j = 2 * x + y
        sib = (x, y, 1 - c)

        local = [pltpu.make_async_remote_copy(
            src_ref=ins[a], dst_ref=outs[a].at[j], send_sem=lsem.at[a], recv_sem=lrsem.at[a],
            device_id=sib, device_id_type=MESH) for a in range(n)]
        for cp in local:
            cp.start()
        small_out[j] = small_in[...]
        small_cp = [pltpu.make_async_remote_copy(
            src_ref=small_in, dst_ref=small_out.at[j], send_sem=sm_s.at[k], recv_sem=sm_r.at[k],
            device_id=(*chip, c), device_id_type=MESH) for k, chip in enumerate(chips)]
        for cp in small_cp:
            cp.start()

        def ici(a, k, chip, jj, dev):
            return pltpu.make_async_remote_copy(
                src_ref=_half(ins[a], c, True) if dev is not None else _half(outs[a].at[jj], c, True),
                dst_ref=_half(outs[a].at[jj], c, True),
                send_sem=ssem.at[a, k], recv_sem=rsem.at[a, k],
                device_id=dev if dev is not None else (*chip, c), device_id_type=MESH)

        first = []
        for a in range(n):
            for k, chip in enumerate(chips):
                cp = ici(a, k, chip, j, (*chip, c))
                cp.start()
                first.append(cp)
        passed = []
        for a in range(n):
            for k, chip in enumerate(chips):
                jj = 2 * chip[0] + chip[1]
                ici(a, k, chip, jj, None).wait_recv()
                fw = pltpu.make_async_remote_copy(
                    src_ref=_half(outs[a].at[jj], c, True), dst_ref=_half(outs[a].at[jj], c, True),
                    send_sem=ssem.at[a, 3 + k], recv_sem=rsem.at[a, 3 + k], device_id=sib, device_id_type=MESH)
                fw.start()
                passed.append(fw)
        for a in range(n):
            for k, chip in enumerate(chips):
                jj = 2 * chip[0] + chip[1]
                pltpu.make_async_remote_copy(
                    src_ref=_half(outs[a].at[jj], 1 - c, True), dst_ref=_half(outs[a].at[jj], 1 - c, True),
                    send_sem=ssem.at[a, 3 + k], recv_sem=rsem.at[a, 3 + k], device_id=sib,
                    device_id_type=MESH).wait_recv()
        for cp in small_cp:
            cp.wait()
        for cp in first + passed:
            cp.wait_send()
        for cp in local:
            cp.wait()

    return pl.pallas_call(
        body, name=name,
        in_specs=[HBM] * n + [VMEM_WHOLE],
        out_specs=[HBM] * n + [VMEM_WHOLE],
        out_shape=[jax.ShapeDtypeStruct((N_SHARDS,) + s.shape, s.dtype) for s in shards]
        + [jax.ShapeDtypeStruct((N_SHARDS,) + small.shape, small.dtype)],
        scratch_shapes=[pltpu.SemaphoreType.DMA((n,)), pltpu.SemaphoreType.DMA((n,)),
                        pltpu.SemaphoreType.DMA((n, 6)), pltpu.SemaphoreType.DMA((n, 6)),
                        pltpu.SemaphoreType.DMA((3,)), pltpu.SemaphoreType.DMA((3,))],
    )(*shards, small)


def _sum_small(small, name):
    def body(small_in, small_out, buf, sm_s, sm_r):
        x, y, c, _ = _place()
        me = 4 * x + 2 * y + c
        buf[me] = small_in[...]
        peers = [(fx, fy, fc) for fx in (0, 1) for fy in (0, 1) for fc in (0, 1)][1:]
        sm = []
        for k, (fx, fy, fc) in enumerate(peers):
            dev = (1 - x if fx else x, 1 - y if fy else y, 1 - c if fc else c)
            cp = pltpu.make_async_remote_copy(
                src_ref=small_in, dst_ref=buf.at[me], send_sem=sm_s.at[k], recv_sem=sm_r.at[k],
                device_id=dev, device_id_type=MESH)
            cp.start()
            sm.append(cp)
        for cp in sm:
            cp.wait()
        tot = buf[0]
        for d in range(1, N_DEV):
            tot = tot + buf[d]
        small_out[...] = tot

    return pl.pallas_call(
        body, name=name, in_specs=[VMEM_WHOLE], out_specs=VMEM_WHOLE,
        out_shape=jax.ShapeDtypeStruct(small.shape, F32),
        scratch_shapes=[pltpu.VMEM((N_DEV,) + small.shape, F32),
                        pltpu.SemaphoreType.DMA((N_DEV - 1,)), pltpu.SemaphoreType.DMA((N_DEV - 1,))],
    )(small)


def _swap_ride(grads):
    n = len(grads)

    def make(ins, outs, sems):
        ssem, rsem = sems
        x, y, c, _ = _place()
        return [pltpu.make_async_remote_copy(
            src_ref=_half(ins[a], 1 - c, True), dst_ref=outs[a], send_sem=ssem.at[a], recv_sem=rsem.at[a],
            device_id=(x, y, 1 - c), device_id_type=MESH) for a in range(n)]

    return _Ride(grads, [jax.ShapeDtypeStruct((g.shape[0], g.shape[1] // 2, g.shape[2]), F32) for g in grads],
                 [pltpu.SemaphoreType.DMA((n,)), pltpu.SemaphoreType.DMA((n,))], make)


def _swap_grad_halves(grads, name):
    n = len(grads)

    def body(*refs):
        ins, outs = refs[:n], refs[n:2 * n]
        ssem, rsem = refs[2 * n:]
        x, y, c, _ = _place()
        cps = [pltpu.make_async_remote_copy(
            src_ref=_half(ins[a], 1 - c, True), dst_ref=outs[a], send_sem=ssem.at[a], recv_sem=rsem.at[a],
            device_id=(x, y, 1 - c), device_id_type=MESH) for a in range(n)]
        for cp in cps:
            cp.start()
        for cp in cps:
            cp.wait()

    return pl.pallas_call(
        body, name=name, in_specs=[HBM] * n, out_specs=[HBM] * n,
        out_shape=[jax.ShapeDtypeStruct((g.shape[0], g.shape[1] // 2, g.shape[2]), F32) for g in grads],
        scratch_shapes=[pltpu.SemaphoreType.DMA((n,)), pltpu.SemaphoreType.DMA((n,))],
    )(*grads)


def _chip_sum(g, recv, core, name):
    ns, R, C = g.shape
    r2 = R // 2
    tr = _rows(r2, 512)
    nrb = r2 // tr

    def body(core_ref, g_ref, r_ref, o_ref, ob_ref):
        s = g_ref[...] + r_ref[...]
        o_ref[...] = s
        ob_ref[...] = s.astype(BF16)

    out = pl.BlockSpec((None, tr, C), lambda s, i, cr: (s, i, 0))
    return pl.pallas_call(
        body, name=name,
        grid_spec=pltpu.PrefetchScalarGridSpec(
            num_scalar_prefetch=1, grid=(ns, nrb),
            in_specs=[pl.BlockSpec((None, tr, C), lambda s, i, cr: (s, cr[0] * nrb + i, 0)), out],
            out_specs=[out, out]),
        out_shape=[jax.ShapeDtypeStruct((ns, r2, C), F32), jax.ShapeDtypeStruct((ns, r2, C), BF16)],
        compiler_params=_cparams(),
    )(core, g, recv)


def _shard_sum(csum, got, place, name):
    _, r2, C = csum.shape
    tr = _rows(r2, 512)
    nrb = r2 // tr

    def body(place_ref, c_ref, g0_ref, g1_ref, g2_ref, o_ref):
        o_ref[...] = ((c_ref[...] + g0_ref[...].astype(F32)) + g1_ref[...].astype(F32)) + g2_ref[...].astype(F32)

    def got_spec(k):
        return pl.BlockSpec((None, tr, C), lambda i, pr: (k, i, 0))

    return pl.pallas_call(
        body, name=name,
        grid_spec=pltpu.PrefetchScalarGridSpec(
            num_scalar_prefetch=1, grid=(nrb,),
            in_specs=[pl.BlockSpec((None, tr, C), lambda i, pr: (pr[0], i, 0)), got_spec(0), got_spec(1), got_spec(2)],
            out_specs=pl.BlockSpec((tr, C), lambda i, pr: (pr[1] * nrb + i, 0))),
        out_shape=jax.ShapeDtypeStruct((2 * r2, C), F32),
        compiler_params=_cparams(),
    )(place, csum, got, got, got)


def _gather_ride(shards):
    n = len(shards)

    def make(ins, outs, sems):
        lsem, lrsem, ssem, rsem = sems
        x, y, c, chips = _place()
        j = 2 * x + y
        cps = [pltpu.make_async_remote_copy(
            src_ref=ins[a], dst_ref=outs[a].at[j], send_sem=lsem.at[a], recv_sem=lrsem.at[a],
            device_id=(x, y, 1 - c), device_id_type=MESH) for a in range(n)]
        for a in range(n):
            for k, chip in enumerate(chips):
                cps.append(pltpu.make_async_remote_copy(
                    src_ref=_half(ins[a], c, True), dst_ref=_half(outs[a].at[j], c, True),
                    send_sem=ssem.at[a, k], recv_sem=rsem.at[a, k], device_id=(*chip, c), device_id_type=MESH))
        return cps

    return _Ride(shards, [jax.ShapeDtypeStruct((N_SHARDS,) + s.shape, s.dtype) for s in shards],
                 [pltpu.SemaphoreType.DMA((n,)), pltpu.SemaphoreType.DMA((n,)),
                  pltpu.SemaphoreType.DMA((n, 3)), pltpu.SemaphoreType.DMA((n, 3))], make)


def _forward_halves(gathered, name):
    n = len(gathered)

    def body(*refs):
        outs = refs[n:2 * n]
        ssem, rsem = refs[2 * n:]
        x, y, c, chips = _place()
        cps = []
        for a in range(n):
            for k, chip in enumerate(chips):
                part = _half(outs[a].at[2 * chip[0] + chip[1]], c, True)
                cps.append(pltpu.make_async_remote_copy(
                    src_ref=part, dst_ref=part, send_sem=ssem.at[a, k], recv_sem=rsem.at[a, k],
                    device_id=(x, y, 1 - c), device_id_type=MESH))
        for cp in cps:
            cp.start()
        for cp in cps:
            cp.wait()

    return pl.pallas_call(
        body, name=name, in_specs=[HBM] * n, out_specs=[HBM] * n,
        out_shape=[jax.ShapeDtypeStruct(g.shape, g.dtype) for g in gathered],
        input_output_aliases={a: a for a in range(n)},
        scratch_shapes=[pltpu.SemaphoreType.DMA((n, 3)), pltpu.SemaphoreType.DMA((n, 3))],
    )(*gathered)


def _scatter_ride(csums):
    n = len(csums)

    def make(ins, outs, sems):
        ssem, rsem = sems
        x, y, c, chips = _place()
        return [pltpu.make_async_remote_copy(
            src_ref=ins[a].at[2 * chip[0] + chip[1]], dst_ref=outs[a].at[k], send_sem=ssem.at[a, k],
            recv_sem=rsem.at[a, k], device_id=(*chip, c), device_id_type=MESH)
            for a in range(n) for k, chip in enumerate(chips)]

    return _Ride(csums, [jax.ShapeDtypeStruct((3,) + g.shape[1:], g.dtype) for g in csums],
                 [pltpu.SemaphoreType.DMA((n, 3)), pltpu.SemaphoreType.DMA((n, 3))], make)


def _join_halves(fulls, name):
    n = len(fulls)

    def body(*refs):
        outs = refs[n:2 * n]
        ssem, rsem = refs[2 * n:]
        x, y, c, _ = _place()
        cps = [pltpu.make_async_remote_copy(
            src_ref=_half(outs[a], c, True), dst_ref=_half(outs[a], c, True), send_sem=ssem.at[a],
            recv_sem=rsem.at[a], device_id=(x, y, 1 - c), device_id_type=MESH) for a in range(n)]
        for cp in cps:
            cp.start()
        for cp in cps:
            cp.wait()

    return pl.pallas_call(
        body, name=name, in_specs=[HBM] * n, out_specs=[HBM] * n,
        out_shape=[jax.ShapeDtypeStruct(f.shape, F32) for f in fulls],
        input_output_aliases={a: a for a in range(n)},
        scratch_shapes=[pltpu.SemaphoreType.DMA((n,))] * 2,
    )(*fulls)


def _pad_rows(a, rows, cols):
    return jnp.pad(a, ((0, rows - a.shape[0]), (0, cols - a.shape[1])))


def kernel(x, p, ffn1_norm, ffn1_w_gate, ffn1_w_up, ffn1_w_down, mix_norm, w_in, conv_w, conv_b, q_norm, k_norm, w_out, ffn2_norm, ffn2_w_gate, ffn2_w_up, ffn2_w_down, ple_norm, ple_w_gate, ple_w_proj, loss_target, m_ffn1_norm, m_ffn1_w_gate, m_ffn1_w_up, m_ffn1_w_down, m_mix_norm, m_w_in, m_conv_w, m_conv_b, m_q_norm, m_k_norm, m_w_out, m_ffn2_norm, m_ffn2_w_gate, m_ffn2_w_up, m_ffn2_w_down, m_ple_norm, m_ple_w_gate, m_ple_w_proj, v_ffn1_norm, v_ffn1_w_gate, v_ffn1_w_up, v_ffn1_w_down, v_mix_norm, v_w_in, v_conv_w, v_conv_b, v_q_norm, v_k_norm, v_w_out, v_ffn2_norm, v_ffn2_w_gate, v_ffn2_w_up, v_ffn2_w_down, v_ple_norm, v_ple_w_gate, v_ple_w_proj):
    big = dict(ffn1_w_gate=ffn1_w_gate, ffn1_w_up=ffn1_w_up, ffn1_w_down=ffn1_w_down, w_in=w_in, w_out=w_out,
               ffn2_w_gate=ffn2_w_gate, ffn2_w_up=ffn2_w_up, ffn2_w_down=ffn2_w_down,
               ple_w_gate=ple_w_gate, ple_w_proj=ple_w_proj)
    big_m = dict(ffn1_w_gate=m_ffn1_w_gate, ffn1_w_up=m_ffn1_w_up, ffn1_w_down=m_ffn1_w_down, w_in=m_w_in,
                 w_out=m_w_out, ffn2_w_gate=m_ffn2_w_gate, ffn2_w_up=m_ffn2_w_up, ffn2_w_down=m_ffn2_w_down,
                 ple_w_gate=m_ple_w_gate, ple_w_proj=m_ple_w_proj)
    big_v = dict(ffn1_w_gate=v_ffn1_w_gate, ffn1_w_up=v_ffn1_w_up, ffn1_w_down=v_ffn1_w_down, w_in=v_w_in,
                 w_out=v_w_out, ffn2_w_gate=v_ffn2_w_gate, ffn2_w_up=v_ffn2_w_up, ffn2_w_down=v_ffn2_w_down,
                 ple_w_gate=v_ple_w_gate, ple_w_proj=v_ple_w_proj)
    names = list(big)
    xs = x[0]
    ps = p[0, 0]
    tgt = loss_target[0]
    S, D = xs.shape
    nconv = conv_b.shape[1]
    nsb = D - nconv
    cwl = conv_w.shape[2]
    jchip = 2 * lax.axis_index("x") + lax.axis_index("y")
    core = lax.axis_index("c")

    early, late = names[:3], names[3:]
    assert all(k.startswith("ffn1") for k in early)
    shards = {k: big[k][0].astype(BF16) for k in names}
    gathered = _gather_weights([shards[k] for k in early], _pad_rows(conv_w[0], 8, LANES), "gather_weights")
    W = dict(zip(early, gathered[:-1]))
    cw_full = jnp.transpose(gathered[-1][:, :, :cwl], (1, 0, 2)).reshape(8, N_SHARDS * cwl)
    qg = jnp.tile(q_norm, (1, CONV_COLS // HEAD_DIM))
    kg = jnp.tile(k_norm, (1, CONV_COLS // HEAD_DIM))
    n_units = nconv // CONV_COLS

    x1, h1, a1, b1, *landed = _ffn_fwd(xs, ffn1_norm, W["ffn1_w_gate"], W["ffn1_w_up"], W["ffn1_w_down"], "ffn1_fwd",
                                        ride=_gather_ride([shards[k] for k in late]))
    W.update(zip(late, _forward_halves(landed, "gather_forward")))
    wout_full = W["w_out"].reshape(-1, D)
    wpg_full = W["ple_w_gate"].reshape(-1, D)
    proj, h2 = _norm_proj(x1, mix_norm, W["w_in"], "mix_in_proj")
    y_conv = _conv_fwd(proj, cw_full, conv_b, nconv, "conv_fwd")
    qs = _qk_norm(proj, qg, 3 * n_units, nsb, HEAD_DIM ** -0.5, "q_norm_fwd")
    kh = _qk_norm(proj, kg, 4 * n_units, nsb, 1.0, "k_norm_fwd")
    vb = proj[:, 3 * nconv + 2 * nsb:].astype(BF16)
    y_sb, carry = _attn_fwd(qs, kh, vb, "attn_fwd")
    ycat = jnp.concatenate([y_conv, y_sb], axis=1)
    x2 = _out_proj(ycat, wout_full, x1, "mix_out_proj")
    x3, h3, a3, b3 = _ffn_fwd(x2, ffn2_norm, W["ffn2_w_gate"], W["ffn2_w_up"], W["ffn2_w_down"], "ffn2_fwd")

    dx3, h4, du4, dpp, loss_blk, dg_ple = _ple(x3, ps, tgt, ple_norm, wpg_full, W["ple_w_proj"], "ple_loss")
    G = {}
    tk = _rows(S, 2048)
    nk = S // tk
    kd = wpg_full.shape[0] // N_SHARDS
    G["ple_w_gate"] = _tn_matmul(
        h4, du4, pl.BlockSpec((tk, kd), lambda m, k: (k, m)), pl.BlockSpec((tk, D), lambda m, k: (k, 0)),
        (N_SHARDS, kd, D), pl.BlockSpec((None, kd, D), lambda m, k: (m, 0, 0)), (N_SHARDS, nk), "ple_w_gate_grad")
    P = ps.shape[1]
    npp = D // N_SHARDS
    G["ple_w_proj"] = _tn_matmul(
        ps, dpp, pl.BlockSpec((tk, P), lambda m, k: (k, 0)), pl.BlockSpec((tk, npp), lambda m, k: (k, m)),
        (N_SHARDS, P, npp), pl.BlockSpec((None, P, npp), lambda m, k: (m, 0, 0)), (N_SHARDS, nk), "ple_w_proj_grad")

    def ffn_grads(pre, h, s, da, db, dy):
        fs = s.shape[2]
        hs = pl.BlockSpec((tk, D), lambda m, k: (k, 0))
        ss = pl.BlockSpec((None, tk, fs), lambda m, k: (m, k, 0))
        for leaf, lhs, rhs in (("_w_gate", da, h), ("_w_up", db, h), ("_w_down", s, dy)):
            G[pre + leaf] = _tn_matmul(lhs, rhs, ss, hs, (N_SHARDS, fs, D),
                                       pl.BlockSpec((None, fs, D), lambda m, k: (m, 0, 0)), (N_SHARDS, nk), pre + leaf + "_grad")

    s3, da3, db3, dy3 = _ffn_bwd_act(dx3, a3, b3, W["ffn2_w_down"], "ffn2_bwd_act")
    dx2, dg_ffn2 = _ffn_bwd_in(da3, db3, W["ffn2_w_gate"], W["ffn2_w_up"], dx3, x2, ffn2_norm, "ffn2_bwd_in")
    ffn_grads("ffn2", h3, s3, da3, db3, dy3)

    dycat = _out_proj_bwd(dx2, wout_full, "mix_out_proj_bwd")
    ko = wout_full.shape[0] // N_SHARDS
    G["w_out"] = _tn_matmul(
        ycat, dx2, pl.BlockSpec((tk, ko), lambda m, k: (k, m)), pl.BlockSpec((tk, D), lambda m, k: (k, 0)),
        (N_SHARDS, ko, D), pl.BlockSpec((None, ko, D), lambda m, k: (m, 0, 0)), (N_SHARDS, nk), "w_out_grad")
    db_, dc_, du_, dwb = _conv_bwd(proj, dycat, cw_full, conv_b, nconv, "conv_bwd")
    dqs, dkh, dv = _attn_bwd(qs, kh, vb, dycat, nconv // LANES, carry, "attn_bwd")
    dq, dg_q = _qk_norm_bwd(proj, dqs, qg, 3 * n_units, nsb, HEAD_DIM ** -0.5, "q_norm_bwd")
    dk, dg_k = _qk_norm_bwd(proj, dkh, kg, 4 * n_units, nsb, 1.0, "k_norm_bwd")
    dproj = jnp.concatenate([db_, dc_, du_, dq, dk, dv.astype(BF16)], axis=1)
    nin = W["w_in"].shape[2]
    G["w_in"] = _tn_matmul(
        h2, dproj, pl.BlockSpec((tk, D), lambda m, k: (k, 0)), pl.BlockSpec((tk, nin), lambda m, k: (k, m)),
        (N_SHARDS, D, nin), pl.BlockSpec((None, D, nin), lambda m, k: (m, 0, 0)), (N_SHARDS, nk), "w_in_grad")
    dx1, dg_mix, *recv_late = _norm_proj_bwd(dproj, W["w_in"], dx2, x1, mix_norm, "mix_in_proj_bwd",
                                             ride=_swap_ride([G[k] for k in late]))
    core_arr = jnp.reshape(core, (1,)).astype(jnp.int32)
    place = jnp.stack([jchip, core]).astype(jnp.int32)

    def chip_sums(group, recv):
        return zip(*[_chip_sum(G[k], r, core_arr, f"chip_sum_{k}") for k, r in zip(group, recv)])

    cs_late, csb_late = chip_sums(late, recv_late)
    s1, da1, db1, dy1, *got_late = _ffn_bwd_act(dx1, a1, b1, W["ffn1_w_down"], "ffn1_bwd_act",
                                                  ride=_scatter_ride(list(csb_late)))
    ffn_grads("ffn1", h1, s1, da1, db1, dy1)
    cs_early, csb_early = chip_sums(early, _swap_grad_halves([G[k] for k in early], "grad_swap_halves"))
    dx0, dg_ffn1, *got_early = _ffn_bwd_in(da1, db1, W["ffn1_w_gate"], W["ffn1_w_up"], dx1, xs, ffn1_norm,
                                           "ffn1_bwd_in", ride=_scatter_ride(list(csb_early)))

    assert D >= nconv and D % LANES == 0
    fold = lambda t: t[0].reshape(-1, HEAD_DIM).sum(axis=0)[None, :]
    small_rows = [dg_ffn1[0:1], dg_mix[0:1], dg_ffn2[0:1], dg_ple[0:1],
                  _pad_rows(dwb[3:4], 1, D), _pad_rows(dwb[0:3], 3, D),
                  _pad_rows(fold(dg_q), 1, D), _pad_rows(fold(dg_k), 1, D), _pad_rows(loss_blk[0:1, 0:1], 1, D)]
    small = _pad_rows(jnp.concatenate(small_rows, axis=0), SMALL_ROWS, D)

    small_sum = _sum_small(small, "small_sum")

    full = _join_halves([_shard_sum(cs, gt, place, f"shard_sum_{k}")
                         for k, cs, gt in zip(names, list(cs_early) + list(cs_late), got_early + got_late)],
                        "grad_join_halves")

    out_g, out_d, out_m, out_v = {}, {}, {}, {}
    for a, k in enumerate(names):
        shp = big[k].shape
        g2 = full[a]
        flipped = k.endswith(("_w_gate", "_w_up")) and k.startswith("ffn")
        view = (lambda t: jnp.transpose(t[0])) if flipped else (lambda t: t[0].reshape(g2.shape))
        back = (lambda t: jnp.transpose(t).reshape(shp)) if flipped else (lambda t: t.reshape(shp))
        d_, m_, v_ = _elementwise(_adamw, [view(big[k]), g2, view(big_m[k]), view(big_v[k])], 3, f"adamw_{k}")
        out_g[k], out_d[k], out_m[k], out_v[k] = (back(t) for t in (g2, d_, m_, v_))

    sm_names = ["ffn1_norm", "mix_norm", "ffn2_norm", "ple_norm", "conv_b", "conv_w", "q_norm", "k_norm"]
    sm_w = dict(ffn1_norm=ffn1_norm, mix_norm=mix_norm, ffn2_norm=ffn2_norm, ple_norm=ple_norm, conv_b=conv_b,
                conv_w=conv_w[0], q_norm=q_norm, k_norm=k_norm)
    sm_m = dict(ffn1_norm=m_ffn1_norm, mix_norm=m_mix_norm, ffn2_norm=m_ffn2_norm, ple_norm=m_ple_norm,
                conv_b=m_conv_b, conv_w=m_conv_w[0], q_norm=m_q_norm, k_norm=m_k_norm)
    sm_v = dict(ffn1_norm=v_ffn1_norm, mix_norm=v_mix_norm, ffn2_norm=v_ffn2_norm, ple_norm=v_ple_norm,
                conv_b=v_conv_b, conv_w=v_conv_w[0], q_norm=v_q_norm, k_norm=v_k_norm)
    sm_g = dict(ffn1_norm=small_sum[0:1], mix_norm=small_sum[1:2], ffn2_norm=small_sum[2:3], ple_norm=small_sum[3:4],
                conv_b=small_sum[4:5, :nconv],
                conv_w=lax.dynamic_slice_in_dim(small_sum[5:8, :nconv], jchip * cwl, cwl, axis=1),
                q_norm=small_sum[8:9, :HEAD_DIM], k_norm=small_sum[9:10, :HEAD_DIM])
    loss = small_sum[10, 0]
    pack = lambda d: _pad_rows(jnp.concatenate([_pad_rows(d[k], d[k].shape[0], D) for k in sm_names], axis=0), SMALL_ROWS, D)
    sd, smm, svv = _elementwise(_adamw, [pack(sm_w), pack(sm_g), pack(sm_m), pack(sm_v)], 3, "adamw_small")
    row = 0
    for k in sm_names:
        r_, c_ = sm_w[k].shape
        shp = (1, r_, c_) if k == "conv_w" else (r_, c_)
        out_g[k] = sm_g[k].reshape(shp)
        out_d[k], out_m[k], out_v[k] = (t[row:row + r_, :c_].reshape(shp) for t in (sd, smm, svv))
        row += r_

    order = ["ffn1_norm", "ffn1_w_gate", "ffn1_w_up", "ffn1_w_down", "mix_norm", "w_in", "conv_w", "conv_b",
             "q_norm", "k_norm", "w_out", "ffn2_norm", "ffn2_w_gate", "ffn2_w_up", "ffn2_w_down", "ple_norm",
             "ple_w_gate", "ple_w_proj"]
    return (loss, dx0[None], *[out_g[k] for k in order], *[out_d[k] for k in order],
            *[out_m[k] for k in order], *[out_v[k] for k in order])
```

```python
import jax
import jax.numpy as jnp
from jax import lax
from jax.experimental import pallas as pl
from jax.experimental.pallas import tpu as pltpu

F32 = jnp.float32
BF16 = jnp.bfloat16
MESH = pl.DeviceIdType.MESH

EPS = 1e-6
HEAD_DIM = 64
LANES = 128
FFN_RES = 0.5
ADAM_LR = 0.001
ADAM_B1 = 0.9
ADAM_B2 = 0.999
ADAM_EPS = 1e-08
ADAM_WD = 0.01
ADAM_STEP = 10
N_SHARDS = 4
N_DEV = 8
ATT_TILE = 256
VMEM_LIMIT = 52 * 1024 * 1024
SMALL_ROWS = 16
HBM = pl.BlockSpec(memory_space=pltpu.HBM)
VMEM_WHOLE = pl.BlockSpec(memory_space=pltpu.VMEM)


def _cparams(**kw):
    return pltpu.CompilerParams(vmem_limit_bytes=VMEM_LIMIT, **kw)


def _dot(a, b):
    return jnp.dot(a, b, preferred_element_type=F32)


def _dot_nt(a, b):
    return lax.dot_general(a, b, (((1,), (1,)), ((), ())), preferred_element_type=F32)


def _dot_tn(a, b):
    return lax.dot_general(a, b, (((0,), (0,)), ((), ())), preferred_element_type=F32)


def _sum_dot(x, m):
    return _dot(x.astype(BF16), m)


def _rms(x, g):
    r = lax.rsqrt(jnp.mean(x * x, axis=-1, keepdims=True) + EPS)
    xn = x * r
    return xn * g, xn, r


def _rms_bwd(dh, xn, r, g):
    dxn = dh * g
    dx = r * (dxn - xn * jnp.mean(dxn * xn, axis=-1, keepdims=True))
    return dx, jnp.sum(dh * xn, axis=0, keepdims=True)


def _rows(n, cap=512):
    for t in (2048, 1024, 512, 448, 384, 352, 256, 192, 176, 128, 96, 88, 64, 48, 32, 16, 8):
        if t <= cap and n % t == 0:
            return t
    raise ValueError(f"no row tile for {n}")


class _Ride:
    def __init__(self, ins, out_shapes, sems, make):
        self.ins, self.out_shapes, self.sems, self.make = list(ins), list(out_shapes), list(sems), make

    def split(self, refs, n_in, n_out, n_scratch):
        ni, no = len(self.ins), len(self.out_shapes)
        ins, rin = refs[:n_in], refs[n_in:n_in + ni]
        outs = refs[n_in + ni:n_in + ni + n_out]
        rout = refs[n_in + ni + n_out:n_in + ni + n_out + no]
        rest = refs[n_in + ni + n_out + no:]
        return ins, outs, rest[:n_scratch], lambda: self.make(rin, rout, rest[n_scratch:])


_NO_RIDE = _Ride([], [], [], lambda i, o, s: [])


def _ride_along(copies, first, last):
    @pl.when(first)
    def _():
        for cp in copies():
            cp.start()

    @pl.when(last)
    def _():
        for cp in copies():
            cp.wait()


def _ffn_fwd(x, gain, wg, wu, wd, name, ride=_NO_RIDE):
    S, D = x.shape
    ns, _, fs = wg.shape
    tm = _rows(S, 512)
    ni = S // tm

    def body(*refs):
        (x_ref, g_ref, wg_ref, wu_ref, wd_ref), (xo_ref, h_ref, a_ref, b_ref), (hs, acc), copies = ride.split(refs, 5, 4, 2)
        j = pl.program_id(1)
        _ride_along(copies, (pl.program_id(0) == 0) & (j == 0), (pl.program_id(0) == ni - 1) & (j == ns - 1))

        @pl.when(j == 0)
        def _():
            h, _, _ = _rms(x_ref[...], g_ref[...])
            hb = h.astype(BF16)
            hs[...] = hb
            h_ref[...] = hb
            acc[...] = jnp.zeros_like(acc)

        hb = hs[...]
        a = _dot(hb, wg_ref[...])
        b = _dot(hb, wu_ref[...])
        a_ref[...] = a.astype(BF16)
        b_ref[...] = b.astype(BF16)
        s = (a * jax.nn.sigmoid(a)) * b
        acc[...] += _dot(s.astype(BF16), wd_ref[...])

        @pl.when(j == ns - 1)
        def _():
            xo_ref[...] = x_ref[...] + FFN_RES * acc[...]

    return pl.pallas_call(
        body, name=name, grid=(ni, ns),
        in_specs=[
            pl.BlockSpec((tm, D), lambda i, j: (i, 0)),
            pl.BlockSpec((1, D), lambda i, j: (0, 0)),
            pl.BlockSpec((None, D, fs), lambda i, j: (j, 0, 0)),
            pl.BlockSpec((None, D, fs), lambda i, j: (j, 0, 0)),
            pl.BlockSpec((None, fs, D), lambda i, j: (j, 0, 0)),
        ] + [HBM] * len(ride.ins),
        out_specs=[
            pl.BlockSpec((tm, D), lambda i, j: (i, 0)),
            pl.BlockSpec((tm, D), lambda i, j: (i, 0)),
            pl.BlockSpec((None, tm, fs), lambda i, j: (j, i, 0)),
            pl.BlockSpec((None, tm, fs), lambda i, j: (j, i, 0)),
        ] + [HBM] * len(ride.out_shapes),
        out_shape=[
            jax.ShapeDtypeStruct((S, D), F32),
            jax.ShapeDtypeStruct((S, D), BF16),
            jax.ShapeDtypeStruct((ns, S, fs), BF16),
            jax.ShapeDtypeStruct((ns, S, fs), BF16),
        ] + ride.out_shapes,
        scratch_shapes=[pltpu.VMEM((tm, D), BF16), pltpu.VMEM((tm, D), F32)] + ride.sems,
        compiler_params=_cparams(),
    )(x, gain, wg, wu, wd, *ride.ins)


def _ffn_bwd_act(dxo, a, b, wd, name, ride=_NO_RIDE):
    S, D = dxo.shape
    ns, fs, _ = wd.shape
    tm = _rows(S, 512)
    ni = S // tm

    def body(*refs):
        (dxo_ref, a_ref, b_ref, wd_ref), (s_ref, da_ref, db_ref, dy_ref), (dys,), copies = ride.split(refs, 4, 4, 1)
        j = pl.program_id(1)
        _ride_along(copies, (pl.program_id(0) == 0) & (j == 0), (pl.program_id(0) == ni - 1) & (j == ns - 1))

        @pl.when(j == 0)
        def _():
            dy = (FFN_RES * dxo_ref[...]).astype(BF16)
            dys[...] = dy
            dy_ref[...] = dy

        av = a_ref[...].astype(F32)
        bv = b_ref[...].astype(F32)
        ds = _dot_nt(dys[...], wd_ref[...])
        sig = jax.nn.sigmoid(av)
        sl = av * sig
        s_ref[...] = (sl * bv).astype(BF16)
        da_ref[...] = (ds * bv * (sig * (1.0 + av * (1.0 - sig)))).astype(BF16)
        db_ref[...] = (ds * sl).astype(BF16)

    act = pl.BlockSpec((None, tm, fs), lambda i, j: (j, i, 0))
    row = pl.BlockSpec((tm, D), lambda i, j: (i, 0))
    return pl.pallas_call(
        body, name=name, grid=(ni, ns),
        in_specs=[row, act, act, pl.BlockSpec((None, fs, D), lambda i, j: (j, 0, 0))] + [HBM] * len(ride.ins),
        out_specs=[act, act, act, row] + [HBM] * len(ride.out_shapes),
        out_shape=[jax.ShapeDtypeStruct((ns, S, fs), BF16)] * 3 + [jax.ShapeDtypeStruct((S, D), BF16)]
        + ride.out_shapes,
        scratch_shapes=[pltpu.VMEM((tm, D), BF16)] + ride.sems,
        compiler_params=_cparams(),
    )(dxo, a, b, wd, *ride.ins)


def _ffn_bwd_in(da, db, wg, wu, dres, x, gain, name, ride=_NO_RIDE):
    S, D = x.shape
    ns, _, fs = wg.shape
    tm = _rows(S, 512)
    ni = S // tm

    def body(*refs):
        (da_ref, db_ref, wg_ref, wu_ref, dres_ref, x_ref, g_ref), (dx_ref, dg_ref), (acc,), copies = ride.split(refs, 7, 2, 1)
        i = pl.program_id(0)
        j = pl.program_id(1)
        _ride_along(copies, (i == 0) & (j == 0), (i == ni - 1) & (j == ns - 1))

        @pl.when((i == 0) & (j == 0))
        def _():
            dg_ref[...] = jnp.zeros_like(dg_ref)

        @pl.when(j == 0)
        def _():
            acc[...] = jnp.zeros_like(acc)

        acc[...] += _dot_nt(da_ref[...], wg_ref[...]) + _dot_nt(db_ref[...], wu_ref[...])

        @pl.when(j == ns - 1)
        def _():
            g = g_ref[...]
            _, xn, r = _rms(x_ref[...], g)
            dx, dg = _rms_bwd(acc[...], xn, r, g)
            dx_ref[...] = dres_ref[...] + dx
            dg_ref[...] += jnp.broadcast_to(dg, dg_ref.shape)

    act = pl.BlockSpec((None, tm, fs), lambda i, j: (j, i, 0))
    row = pl.BlockSpec((tm, D), lambda i, j: (i, 0))
    wsp = pl.BlockSpec((None, D, fs), lambda i, j: (j, 0, 0))
    return pl.pallas_call(
        body, name=name, grid=(ni, ns),
        in_specs=[act, act, wsp, wsp, row, row, pl.BlockSpec((1, D), lambda i, j: (0, 0))] + [HBM] * len(ride.ins),
        out_specs=[row, pl.BlockSpec((8, D), lambda i, j: (0, 0))] + [HBM] * len(ride.out_shapes),
        out_shape=[jax.ShapeDtypeStruct((S, D), F32), jax.ShapeDtypeStruct((8, D), F32)] + ride.out_shapes,
        scratch_shapes=[pltpu.VMEM((tm, D), F32)] + ride.sems,
        compiler_params=_cparams(),
    )(da, db, wg, wu, dres, x, gain, *ride.ins)


def _tn_matmul(a, b, a_spec, b_spec, o_shape, o_spec, grid, name):
    kaxis = len(grid) - 1

    def body(a_ref, b_ref, o_ref):
        @pl.when(pl.program_id(kaxis) == 0)
        def _():
            o_ref[...] = jnp.zeros_like(o_ref)

        o_ref[...] += _dot_tn(a_ref[...].astype(BF16), b_ref[...].astype(BF16))

    return pl.pallas_call(
        body, name=name, grid=grid, in_specs=[a_spec, b_spec], out_specs=o_spec,
        out_shape=jax.ShapeDtypeStruct(o_shape, F32), compiler_params=_cparams(),
    )(a, b)


def _norm_proj(x, gain, w, name):
    S, D = x.shape
    ns, _, n = w.shape
    tm = _rows(S, 1024)

    def body(x_ref, g_ref, w_ref, o_ref, h_ref, hs):
        @pl.when(pl.program_id(1) == 0)
        def _():
            h, _, _ = _rms(x_ref[...], g_ref[...])
            hb = h.astype(BF16)
            hs[...] = hb
            h_ref[...] = hb

        o_ref[...] = _dot(hs[...], w_ref[...])

    return pl.pallas_call(
        body, name=name, grid=(S // tm, ns),
        in_specs=[
            pl.BlockSpec((tm, D), lambda i, j: (i, 0)),
            pl.BlockSpec((1, D), lambda i, j: (0, 0)),
            pl.BlockSpec((None, D, n), lambda i, j: (j, 0, 0)),
        ],
        out_specs=[
            pl.BlockSpec((tm, n), lambda i, j: (i, j)),
            pl.BlockSpec((tm, D), lambda i, j: (i, 0)),
        ],
        out_shape=[jax.ShapeDtypeStruct((S, ns * n), F32), jax.ShapeDtypeStruct((S, D), BF16)],
        scratch_shapes=[pltpu.VMEM((tm, D), BF16)],
        compiler_params=_cparams(),
    )(x, gain, w)


def _norm_proj_bwd(dproj, w, dres, x, gain, name, ride=_NO_RIDE):
    S, D = x.shape
    ns, _, n = w.shape
    tm = _rows(S, 512)
    ni = S // tm

    def body(*refs):
        (dp_ref, w_ref, dres_ref, x_ref, g_ref), (dx_ref, dg_ref), (acc,), copies = ride.split(refs, 5, 2, 1)
        i = pl.program_id(0)
        j = pl.program_id(1)
        _ride_along(copies, (i == 0) & (j == 0), (i == ni - 1) & (j == ns - 1))

        @pl.when((i == 0) & (j == 0))
        def _():
            dg_ref[...] = jnp.zeros_like(dg_ref)

        @pl.when(j == 0)
        def _():
            acc[...] = jnp.zeros_like(acc)

        acc[...] += _dot_nt(dp_ref[...], w_ref[...])

        @pl.when(j == ns - 1)
        def _():
            g = g_ref[...]
            _, xn, r = _rms(x_ref[...], g)
            dx, dg = _rms_bwd(acc[...], xn, r, g)
            dx_ref[...] = dres_ref[...] + dx
            dg_ref[...] += jnp.broadcast_to(dg, dg_ref.shape)

    return pl.pallas_call(
        body, name=name, grid=(ni, ns),
        in_specs=[
            pl.BlockSpec((tm, n), lambda i, j: (i, j)),
            pl.BlockSpec((None, D, n), lambda i, j: (j, 0, 0)),
            pl.BlockSpec((tm, D), lambda i, j: (i, 0)),
            pl.BlockSpec((tm, D), lambda i, j: (i, 0)),
            pl.BlockSpec((1, D), lambda i, j: (0, 0)),
        ] + [HBM] * len(ride.ins),
        out_specs=[
            pl.BlockSpec((tm, D), lambda i, j: (i, 0)),
            pl.BlockSpec((8, D), lambda i, j: (0, 0)),
        ] + [HBM] * len(ride.out_shapes),
        out_shape=[jax.ShapeDtypeStruct((S, D), F32), jax.ShapeDtypeStruct((8, D), F32)] + ride.out_shapes,
        scratch_shapes=[pltpu.VMEM((tm, D), F32)] + ride.sems,
        compiler_params=_cparams(),
    )(dproj, w, dres, x, gain, *ride.ins)


def _out_proj(ycat, w, res, name):
    S, K = ycat.shape
    D = w.shape[1]
    tm = _rows(S, 512)

    def body(y_ref, w_ref, r_ref, o_ref):
        o_ref[...] = r_ref[...] + _dot(y_ref[...], w_ref[...])

    return pl.pallas_call(
        body, name=name, grid=(S // tm,),
        in_specs=[
            pl.BlockSpec((tm, K), lambda i: (i, 0)),
            pl.BlockSpec((K, D), lambda i: (0, 0)),
            pl.BlockSpec((tm, D), lambda i: (i, 0)),
        ],
        out_specs=pl.BlockSpec((tm, D), lambda i: (i, 0)),
        out_shape=jax.ShapeDtypeStruct((S, D), F32),
        compiler_params=_cparams(),
    )(ycat, w, res)


def _out_proj_bwd(dx, w, name):
    S, D = dx.shape
    K = w.shape[0]
    tm = _rows(S, 512)

    def body(d_ref, w_ref, o_ref):
        o_ref[...] = _dot_nt(d_ref[...].astype(BF16), w_ref[...])

    return pl.pallas_call(
        body, name=name, grid=(S // tm,),
        in_specs=[pl.BlockSpec((tm, D), lambda i: (i, 0)), pl.BlockSpec((K, D), lambda i: (0, 0))],
        out_specs=pl.BlockSpec((tm, K), lambda i: (i, 0)),
        out_shape=jax.ShapeDtypeStruct((S, K), F32),
        compiler_params=_cparams(),
    )(dx, w)


CONV_COLS = 256


def _shift_down(z, halo, k, row):
    out = pltpu.roll(z, k, 0)
    for n in range(k):
        out = jnp.where(row == n, halo[8 - k + n:8 - k + n + 1, :], out)
    return out


def _shift_up(g, halo, k, row, ts):
    out = pltpu.roll(g, ts - k, 0)
    for n in range(k):
        out = jnp.where(row == ts - k + n, halo[n:n + 1, :], out)
    return out


def _conv_fwd(proj, cw, cb, nconv, name):
    S = proj.shape[0]
    ncb = nconv // CONV_COLS
    ts = _rows(S, 512)
    hb = ts // 8

    def body(b_ref, c_ref, u_ref, ch_ref, uh_ref, w_ref, bias_ref, o_ref):
        i = pl.program_id(1)
        z = c_ref[...] * u_ref[...]
        halo = jnp.where(i > 0, ch_ref[...] * uh_ref[...], 0.0)
        row = lax.broadcasted_iota(jnp.int32, z.shape, 0)
        w = w_ref[...]
        yc = w[0:1, :] * _shift_down(z, halo, 2, row) + w[1:2, :] * _shift_down(z, halo, 1, row) + w[2:3, :] * z
        o_ref[...] = (b_ref[...] * (yc + bias_ref[...])).astype(BF16)

    def blk(unit):
        return pl.BlockSpec((ts, CONV_COLS), lambda cbi, i: (i, unit * ncb + cbi))

    def prev(unit):
        return pl.BlockSpec((8, CONV_COLS), lambda cbi, i: (jnp.maximum(i * hb - 1, 0), unit * ncb + cbi))

    return pl.pallas_call(
        body, name=name, grid=(ncb, S // ts),
        in_specs=[blk(0), blk(1), blk(2), prev(1), prev(2),
                  pl.BlockSpec((8, CONV_COLS), lambda cbi, i: (0, cbi)),
                  pl.BlockSpec((1, CONV_COLS), lambda cbi, i: (0, cbi))],
        out_specs=pl.BlockSpec((ts, CONV_COLS), lambda cbi, i: (i, cbi)),
        out_shape=jax.ShapeDtypeStruct((S, nconv), BF16),
        compiler_params=_cparams(),
    )(proj, proj, proj, proj, proj, cw, cb)


def _conv_bwd(proj, dy, cw, cb, nconv, name):
    S = proj.shape[0]
    ncb = nconv // CONV_COLS
    ts = _rows(S, 512)
    hb = ts // 8
    nblk = S // ts

    def body(b_ref, c_ref, u_ref, dy_ref, ch_ref, uh_ref, bn_ref, dyn_ref, w_ref, bias_ref,
             db_ref, dc_ref, du_ref, dw_ref):
        i = pl.program_id(1)

        @pl.when(i == 0)
        def _():
            dw_ref[...] = jnp.zeros_like(dw_ref)

        c = c_ref[...]
        u = u_ref[...]
        bg = b_ref[...]
        dy_ = dy_ref[...]
        z = c * u
        halo = jnp.where(i > 0, ch_ref[...] * uh_ref[...], 0.0)
        row = lax.broadcasted_iota(jnp.int32, z.shape, 0)
        w = w_ref[...]
        z2 = _shift_down(z, halo, 2, row)
        z1 = _shift_down(z, halo, 1, row)
        yc = w[0:1, :] * z2 + w[1:2, :] * z1 + w[2:3, :] * z
        db_ref[...] = (dy_ * (yc + bias_ref[...])).astype(BF16)
        g = dy_ * bg
        gnext = jnp.where(i < nblk - 1, dyn_ref[...] * bn_ref[...], 0.0)
        dz = w[2:3, :] * g + w[1:2, :] * _shift_up(g, gnext, 1, row, ts) + w[0:1, :] * _shift_up(g, gnext, 2, row, ts)
        dc_ref[...] = (dz * u).astype(BF16)
        du_ref[...] = (dz * c).astype(BF16)
        r8 = lax.broadcasted_iota(jnp.int32, (8, CONV_COLS), 0)
        sums = [jnp.sum(g * z2, axis=0, keepdims=True), jnp.sum(g * z1, axis=0, keepdims=True),
                jnp.sum(g * z, axis=0, keepdims=True), jnp.sum(g, axis=0, keepdims=True)]
        upd = jnp.zeros((8, CONV_COLS), F32)
        for n, sv in enumerate(sums):
            upd = jnp.where(r8 == n, sv, upd)
        dw_ref[...] += upd

    def blk(unit):
        return pl.BlockSpec((ts, CONV_COLS), lambda cbi, i: (i, unit * ncb + cbi))

    def prev(unit):
        return pl.BlockSpec((8, CONV_COLS), lambda cbi, i: (jnp.maximum(i * hb - 1, 0), unit * ncb + cbi))

    def nxt(unit):
        return pl.BlockSpec((8, CONV_COLS), lambda cbi, i: (jnp.minimum((i + 1) * hb, S // 8 - 1), unit * ncb + cbi))

    o = pl.BlockSpec((ts, CONV_COLS), lambda cbi, i: (i, cbi))
    return pl.pallas_call(
        body, name=name, grid=(ncb, nblk),
        in_specs=[blk(0), blk(1), blk(2), blk(0), prev(1), prev(2), nxt(0), nxt(0),
                  pl.BlockSpec((8, CONV_COLS), lambda cbi, i: (0, cbi)),
                  pl.BlockSpec((1, CONV_COLS), lambda cbi, i: (0, cbi))],
        out_specs=[o, o, o, pl.BlockSpec((8, CONV_COLS), lambda cbi, i: (0, cbi))],
        out_shape=[jax.ShapeDtypeStruct((S, nconv), BF16)] * 3 + [jax.ShapeDtypeStruct((8, nconv), F32)],
        compiler_params=_cparams(),
    )(proj, proj, proj, dy, proj, proj, proj, dy, cw, cb)


def _group_ones(n):
    r = lax.broadcasted_iota(jnp.int32, (n, n), 0) // HEAD_DIM
    c = lax.broadcasted_iota(jnp.int32, (n, n), 1) // HEAD_DIM
    return jnp.where(r == c, 1.0, 0.0).astype(BF16)


def _qk_norm(proj, gain_t, unit0, nsb, scale, name):
    S = proj.shape[0]
    nb = nsb // CONV_COLS
    ts = _rows(S, 512)

    def body(x_ref, g_ref, o_ref):
        x = x_ref[...]
        ss = _sum_dot(x * x, _group_ones(CONV_COLS))
        r = lax.rsqrt(ss * (1.0 / HEAD_DIM) + EPS)
        o_ref[...] = ((x * r) * g_ref[...] * scale).astype(BF16)

    return pl.pallas_call(
        body, name=name, grid=(nb, S // ts),
        in_specs=[pl.BlockSpec((ts, CONV_COLS), lambda u, i: (i, unit0 + u)),
                  pl.BlockSpec((1, CONV_COLS), lambda u, i: (0, 0))],
        out_specs=pl.BlockSpec((ts, CONV_COLS), lambda u, i: (i, u)),
        out_shape=jax.ShapeDtypeStruct((S, nsb), BF16),
        compiler_params=_cparams(),
    )(proj, gain_t)


def _qk_norm_bwd(proj, dout, gain_t, unit0, nsb, scale, name):
    S = proj.shape[0]
    nb = nsb // CONV_COLS
    ts = _rows(S, 512)

    def body(x_ref, d_ref, g_ref, dx_ref, dg_ref):
        @pl.when(pl.program_id(1) == 0)
        def _():
            dg_ref[...] = jnp.zeros_like(dg_ref)

        x = x_ref[...]
        g = g_ref[...]
        ones = _group_ones(CONV_COLS)
        ss = _sum_dot(x * x, ones)
        r = lax.rsqrt(ss * (1.0 / HEAD_DIM) + EPS)
        xn = x * r
        dh = d_ref[...] * scale
        dxn = dh * g
        m = _sum_dot(dxn * xn, ones) * (1.0 / HEAD_DIM)
        dx_ref[...] = (r * (dxn - xn * m)).astype(BF16)
        dg_ref[...] += jnp.broadcast_to(jnp.sum(dh * xn, axis=0, keepdims=True), dg_ref.shape)

    return pl.pallas_call(
        body, name=name, grid=(nb, S // ts),
        in_specs=[pl.BlockSpec((ts, CONV_COLS), lambda u, i: (i, unit0 + u)),
                  pl.BlockSpec((ts, CONV_COLS), lambda u, i: (i, u)),
                  pl.BlockSpec((1, CONV_COLS), lambda u, i: (0, 0))],
        out_specs=[pl.BlockSpec((ts, CONV_COLS), lambda u, i: (i, u)),
                   pl.BlockSpec((8, CONV_COLS), lambda u, i: (0, u))],
        out_shape=[jax.ShapeDtypeStruct((S, nsb), BF16), jax.ShapeDtypeStruct((8, nsb), F32)],
        compiler_params=_cparams(),
    )(proj, dout, gain_t)


Z_CLAMP = 80.0
N_SLOTS = 3
SAT_LIMIT = 120.0


def _head_masks():
    lane = lax.broadcasted_iota(jnp.int32, (1, LANES), 1)
    return [lane < HEAD_DIM, lane >= HEAD_DIM], lane


def _tile_consts(T):
    r_i = lax.broadcasted_iota(jnp.int32, (T, T), 0)
    c_i = lax.broadcasted_iota(jnp.int32, (T, T), 1)
    neg_suffix = jnp.where(r_i >= c_i, -1.0, 0.0).astype(BF16)
    prefix = jnp.where(r_i <= c_i, 1.0, 0.0).astype(BF16)
    return neg_suffix, prefix, c_i < r_i


def _pipeline(n, stages, first_special=False, last_special=False, saturated=None, extra_head=0):
    depth = len(stages)
    head = (depth if first_special else depth - 1) + extra_head
    off = 0 if last_special else 1
    for m in range(head):
        for k in reversed(range(min(m, depth - 1) + 1)):
            stages[k](m - k, (m - k) % N_SLOTS, first_special and m == k)

    def trip(m, u):
        for k in reversed(range(depth)):
            stages[k](m - k, (head + u - k) % N_SLOTS, False)

    def group(g, carry):
        for u in range(N_SLOTS):
            trip(head + g * N_SLOTS + u, u)
        return carry

    count = n - 1 + off - head
    full = count // N_SLOTS
    if saturated is None:
        lax.fori_loop(0, full, group, 0)
        go_on, done = True, n
    else:
        def more(state):
            return (state[0] < full) & (state[1] == 0)

        def step(state):
            group(state[0], 0)
            return state[0] + 1, saturated().astype(jnp.int32)

        groups, stop = lax.while_loop(more, step, (jnp.int32(0), saturated().astype(jnp.int32)))
        go_on = stop == 0
        done = jnp.where(go_on, n, head - depth + 1 + N_SLOTS * groups)
    for r in range(N_SLOTS):
        @pl.when((count - full * N_SLOTS == r) & go_on)
        def _(r=r):
            for u in range(r):
                trip(head + full * N_SLOTS + u, u)
            for e in range(depth - off):
                for k in reversed(range(e + off, depth)):
                    t = n - 1 - (k - e - off)
                    stages[k](t, (head + r + e - k) % N_SLOTS, last_special and k == e + off)
    return done


def _sweep(n, stages, finish, first_special=False, last_special=False, saturated=None, extra_head=0):
    depth = len(stages)
    least = (depth if first_special else depth - 1) + extra_head + (1 if last_special else 0)
    for short in range(1, least):
        @pl.when(n == short)
        def _(short=short):
            for m in range(short + depth - 1):
                for k in reversed(range(depth)):
                    t = m - k
                    if 0 <= t < short:
                        stages[k](t, t % N_SLOTS, (first_special and t == 0) or (last_special and t == short - 1))
            finish(short)

    @pl.when(n >= least)
    def _():
        finish(_pipeline(n, stages, first_special, last_special, saturated, extra_head))


def _attn_fwd(q, k, v, name):
    S, nsb = q.shape
    T = ATT_TILE
    hp = nsb // LANES
    nb = S // T
    assert nb <= HEAD_DIM

    def body(q_ref, k_ref, v_ref, y_ref, cs_ref, c_ref, acc, z_st, inc_st):
        i = pl.program_id(1)
        masks, lane = _head_masks()
        qv = q_ref[...]
        qm = [jnp.where(m, qv, jnp.zeros_like(qv)) for m in masks]
        neg_suffix, _, causal = _tile_consts(T)
        c_ref[...] = jnp.zeros_like(c_ref)
        acc[...] = jnp.zeros_like(acc)
        cs_ref[...] = jnp.zeros_like(cs_ref)

        def blk(ref, j):
            return ref[pl.ds(pl.multiple_of(j * T, T), T), :]

        def scores(t, slot, diag):
            kj = blk(k_ref, i - t)
            for h in range(2):
                z_st[slot, h] = jnp.minimum(_dot_nt(qm[h], kj), Z_CLAMP)

        def suffix_sums(t, slot, diag):
            for h in range(2):
                sp = jnp.log(1.0 + jnp.exp(z_st[slot, h]))
                if diag:
                    sp = jnp.where(causal, sp, 0.0)
                inc_st[slot, h] = _dot(sp.astype(BF16), neg_suffix)

        def weights(t, slot, diag):
            vj = blk(v_ref, i - t)
            for h in range(2):
                inc = inc_st[slot, h]
                c = c_ref[h]
                a = jnp.exp(z_st[slot, h] + inc + c)
                if diag:
                    a = jnp.where(causal, a, 0.0)
                upd = _dot(a.astype(BF16), vj)
                acc[...] += jnp.where(masks[h], upd, 0.0)
                cs_ref[...] = jnp.where(lane == i - t + HEAD_DIM * h, c, cs_ref[...])
                c_ref[h] = c + inc[:, 0:1]

        stages = [scores, suffix_sums, weights]

        def saturated():
            return jnp.max(c_ref[...]) < -SAT_LIMIT

        def note(used):
            cs_ref[...] = jnp.where(lane == LANES - 1, jnp.asarray(used).astype(F32), cs_ref[...])

        _sweep(i + 1, stages, note, first_special=True, saturated=saturated, extra_head=1)
        y_ref[...] = acc[...].astype(BF16)

    return pl.pallas_call(
        body, name=name, grid=(hp, nb),
        in_specs=[pl.BlockSpec((T, LANES), lambda p, i: (i, p)),
                  pl.BlockSpec((S, LANES), lambda p, i: (0, p)),
                  pl.BlockSpec((S, LANES), lambda p, i: (0, p))],
        out_specs=[pl.BlockSpec((T, LANES), lambda p, i: (i, p)),
                   pl.BlockSpec((None, T, LANES), lambda p, i: (p, i, 0))],
        out_shape=[jax.ShapeDtypeStruct((S, nsb), BF16), jax.ShapeDtypeStruct((hp, S, LANES), F32)],
        scratch_shapes=[pltpu.VMEM((2, T, 1), F32), pltpu.VMEM((T, LANES), F32),
                        pltpu.VMEM((N_SLOTS, 2, T, T), F32), pltpu.VMEM((N_SLOTS, 2, T, T), F32)],
        compiler_params=_cparams(),
    )(q, k, v)


def _attn_bwd(q, k, v, dy, col0, carry, name):
    S, nsb = q.shape
    T = ATT_TILE
    hp = nsb // LANES
    nb = S // T

    def body(q_ref, k_ref, v_ref, dy_ref, cs_ref, dq_ref, dk_ref, dv_ref, e_ref, acc,
             z_st, da_st, b_st, inc_st, a_st, e_st, p_st):
        i = pl.program_id(1)

        @pl.when(i == 0)
        def _():
            dk_ref[...] = jnp.zeros_like(dk_ref)
            dv_ref[...] = jnp.zeros_like(dv_ref)

        masks, lane = _head_masks()
        qv = q_ref[...]
        dyb = dy_ref[...].astype(BF16)
        qm = [jnp.where(m, qv, jnp.zeros_like(qv)) for m in masks]
        dym = [jnp.where(m, dyb, jnp.zeros_like(dyb)) for m in masks]
        neg_suffix, prefix, causal = _tile_consts(T)
        e_ref[...] = jnp.zeros_like(e_ref)
        acc[...] = jnp.zeros_like(acc)

        def blk(ref, j):
            return ref[pl.ds(pl.multiple_of(j * T, T), T), :]

        used = jnp.max(jnp.where(lane == LANES - 1, cs_ref[...], 0.0)).astype(jnp.int32)
        n = jnp.clip(used, 1, i + 1)
        first = i + 1 - n

        def scores(t, slot, diag):
            kj = blk(k_ref, first + t)
            vj = blk(v_ref, first + t)
            for h in range(2):
                z_st[slot, h] = jnp.minimum(_dot_nt(qm[h], kj), Z_CLAMP)
                da_st[slot, h] = _dot_nt(dym[h], vj)

        def suffix_sums(t, slot, diag):
            for h in range(2):
                u = jnp.exp(z_st[slot, h])
                w = 1.0 + u
                b_st[slot, h] = u / w
                sp = jnp.log(w)
                if diag:
                    sp = jnp.where(causal, sp, 0.0)
                inc_st[slot, h] = _dot(sp.astype(BF16), neg_suffix)

        def probs(t, slot, diag):
            csv = cs_ref[...]
            for h in range(2):
                c = jnp.sum(jnp.where(lane == first + t + HEAD_DIM * h, csv, 0.0), axis=-1, keepdims=True)
                a = jnp.exp(z_st[slot, h] + inc_st[slot, h] + c)
                if diag:
                    a = jnp.where(causal, a, 0.0)
                a_st[slot, h] = a.astype(BF16)
                e = a * da_st[slot, h]
                e_st[slot, h] = e
                p_st[slot, h] = _dot(e.astype(BF16), prefix)

        def grads(t, slot, diag):
            kj = blk(k_ref, first + t)
            off = pl.multiple_of((first + t) * T, T)
            for h in range(2):
                p = p_st[slot, h]
                dz = e_st[slot, h] - b_st[slot, h] * (e_ref[h] + p)
                if diag:
                    dz = jnp.where(causal, dz, 0.0)
                dzb = dz.astype(BF16)
                acc[...] += jnp.where(masks[h], _dot(dzb, kj), 0.0)
                dk_ref[pl.ds(off, T), :] += _dot_tn(dzb, qm[h])
                dv_ref[pl.ds(off, T), :] += _dot_tn(a_st[slot, h], dym[h])
                e_ref[h] += p[:, T - 1:T]

        stages = [scores, suffix_sums, probs, grads]

        _sweep(n, stages, lambda done: None, last_special=True)
        dq_ref[...] = acc[...]

    return pl.pallas_call(
        body, name=name, grid=(hp, nb),
        in_specs=[pl.BlockSpec((T, LANES), lambda p, i: (i, p)),
                  pl.BlockSpec((S, LANES), lambda p, i: (0, p)),
                  pl.BlockSpec((S, LANES), lambda p, i: (0, p)),
                  pl.BlockSpec((T, LANES), lambda p, i: (i, col0 + p)),
                  pl.BlockSpec((None, T, LANES), lambda p, i: (p, i, 0))],
        out_specs=[pl.BlockSpec((T, LANES), lambda p, i: (i, p)),
                   pl.BlockSpec((S, LANES), lambda p, i: (0, p)),
                   pl.BlockSpec((S, LANES), lambda p, i: (0, p))],
        out_shape=[jax.ShapeDtypeStruct((S, nsb), F32)] * 3,
        scratch_shapes=[pltpu.VMEM((2, T, 1), F32), pltpu.VMEM((T, LANES), F32)]
        + [pltpu.VMEM((N_SLOTS, 2, T, T), dt) for dt in (F32, F32, F32, F32, BF16, F32, F32)],
        compiler_params=_cparams(),
    )(q, k, v, dy, carry)


def _ple(x, p, tgt, gain, wpg, wpp, name):
    S, D = x.shape
    P = p.shape[1]
    ns, _, nc = wpp.shape
    tm = _rows(S, 256)

    def body(x_ref, p_ref, t_ref, g_ref, wpg_ref, wpp_ref, dx_ref, h_ref, du_ref, dpp_ref, loss_ref, dg_ref):
        @pl.when(pl.program_id(0) == 0)
        def _():
            loss_ref[...] = jnp.zeros_like(loss_ref)
            dg_ref[...] = jnp.zeros_like(dg_ref)

        x_ = x_ref[...]
        g = g_ref[...]
        h, xn, r = _rms(x_, g)
        hb = h.astype(BF16)
        h_ref[...] = hb
        gate = jax.nn.sigmoid(_dot(hb, wpg_ref[...]))
        pb = p_ref[...].astype(BF16)
        pp = jnp.concatenate([_dot(pb, wpp_ref[n]) for n in range(ns)], axis=1)
        err = (x_ + gate * pp) - t_ref[...]
        loss_ref[...] += (0.5 / D) * jnp.sum(err * err)
        dy = err * (1.0 / D)
        du = ((dy * pp) * (gate * (1.0 - gate))).astype(BF16)
        du_ref[...] = du
        dpp_ref[...] = (dy * gate).astype(BF16)
        dx, dg = _rms_bwd(_dot_nt(du, wpg_ref[...]), xn, r, g)
        dx_ref[...] = dy + dx
        dg_ref[...] += jnp.broadcast_to(dg, dg_ref.shape)

    row = pl.BlockSpec((tm, D), lambda i: (i, 0))
    return pl.pallas_call(
        body, name=name, grid=(S // tm,),
        in_specs=[row, pl.BlockSpec((tm, P), lambda i: (i, 0)), row,
                  pl.BlockSpec((1, D), lambda i: (0, 0)),
                  pl.BlockSpec((D, D), lambda i: (0, 0)),
                  pl.BlockSpec((ns, P, nc), lambda i: (0, 0, 0))],
        out_specs=[row, row, row, row,
                   pl.BlockSpec((8, LANES), lambda i: (0, 0)),
                   pl.BlockSpec((8, D), lambda i: (0, 0))],
        out_shape=[jax.ShapeDtypeStruct((S, D), F32)] + [jax.ShapeDtypeStruct((S, D), BF16)] * 3
        + [jax.ShapeDtypeStruct((8, LANES), F32), jax.ShapeDtypeStruct((8, D), F32)],
        compiler_params=_cparams(),
    )(x, p, tgt, gain, wpg, wpp)


def _elementwise(fn, ins, n_out, name):
    R, C = ins[0].shape
    tr = _rows(R, 512)

    def body(*refs):
        outs = fn(*[r[...] for r in refs[:len(ins)]])
        for o_ref, o in zip(refs[len(ins):], outs):
            o_ref[...] = o

    spec = pl.BlockSpec((tr, C), lambda i: (i, 0))
    return pl.pallas_call(
        body, name=name, grid=(R // tr,), in_specs=[spec] * len(ins), out_specs=[spec] * n_out,
        out_shape=[jax.ShapeDtypeStruct((R, C), F32)] * n_out, compiler_params=_cparams(),
    )(*ins)


def _adamw(w, g, m, v):
    m = ADAM_B1 * m + (1.0 - ADAM_B1) * g
    v = ADAM_B2 * v + (1.0 - ADAM_B2) * jnp.square(g)
    m_hat = m / (1.0 - ADAM_B1 ** ADAM_STEP)
    v_hat = v / (1.0 - ADAM_B2 ** ADAM_STEP)
    delta = -ADAM_LR * (m_hat / (jnp.sqrt(v_hat) + ADAM_EPS) + ADAM_WD * w)
    return delta, m, v


def _place():
    x, y, c = lax.axis_index("x"), lax.axis_index("y"), lax.axis_index("c")
    chips = [(1 - x, y), (x, 1 - y), (1 - x, 1 - y)]
    return x, y, c, chips


def _half(ref, c, axis_rows):
    n = ref.shape[-2]
    start = pl.multiple_of(c * (n // 2), 8)
    idx = (slice(None),) * (len(ref.shape) - 2) + (pl.ds(start, n // 2), slice(None))
    return ref.at[idx]


def _gather_weights(shards, small, name):
    n = len(shards)

    def body(*refs):
        ins, small_in = refs[:n], refs[n]
        outs, small_out = refs[n + 1:2 * n + 1], refs[2 * n + 1]
        lsem, lrsem, ssem, rsem, sm_s, sm_r = refs[2 * n + 2:]
        x, y, c, chips = _place()
        j = 2 * x + y
        sib = (x, y, 1 - c)

        local = [pltpu.make_async_remote_copy(
            src_ref=ins[a], dst_ref=outs[a].at[j], send_sem=lsem.at[a], recv_sem=lrsem.at[a],
            device_id=sib, device_id_type=MESH) for a in range(n)]
        for cp in local:
            cp.start()
        small_out[j] = small_in[...]
        small_cp = [pltpu.make_async_remote_copy(
            src_ref=small_in, dst_ref=small_out.at[j], send_sem=sm_s.at[k], recv_sem=sm_r.at[k],
            device_id=(*chip, c), device_id_type=MESH) for k, chip in enumerate(chips)]
        for cp in small_cp:
            cp.start()

        def ici(a, k, chip, jj, dev):
            return pltpu.make_async_remote_copy(
                src_ref=_half(ins[a], c, True) if dev is not None else _half(outs[a].at[jj], c, True),
                dst_ref=_half(outs[a].at[jj], c, True),
                send_sem=ssem.at[a, k], recv_sem=rsem.at[a, k],
                device_id=dev if dev is not None else (*chip, c), device_id_type=MESH)

        first = []
        for a in range(n):
            for k, chip in enumerate(chips):
                cp = ici(a, k, chip, j, (*chip, c))
                cp.start()
                first.append(cp)
        passed = []
        for a in range(n):
            for k, chip in enumerate(chips):
                jj = 2 * chip[0] + chip[1]
                ici(a, k, chip, jj, None).wait_recv()
                fw = pltpu.make_async_remote_copy(
                    src_ref=_half(outs[a].at[jj], c, True), dst_ref=_half(outs[a].at[jj], c, True),
                    send_sem=ssem.at[a, 3 + k], recv_sem=rsem.at[a, 3 + k], device_id=sib, device_id_type=MESH)
                fw.start()
                passed.append(fw)
        for a in range(n):
            for k, chip in enumerate(chips):
                jj = 2 * chip[0] + chip[1]
                pltpu.make_async_remote_copy(
                    src_ref=_half(outs[a].at[jj], 1 - c, True), dst_ref=_half(outs[a].at[jj], 1 - c, True),
                    send_sem=ssem.at[a, 3 + k], recv_sem=rsem.at[a, 3 + k], device_id=sib,
                    device_id_type=MESH).wait_recv()
        for cp in small_cp:
            cp.wait()
        for cp in first + passed:
            cp.wait_send()
        for cp in local:
            cp.wait()

    return pl.pallas_call(
        body, name=name,
        in_specs=[HBM] * n + [VMEM_WHOLE],
        out_specs=[HBM] * n + [VMEM_WHOLE],
        out_shape=[jax.ShapeDtypeStruct((N_SHARDS,) + s.shape, s.dtype) for s in shards]
        + [jax.ShapeDtypeStruct((N_SHARDS,) + small.shape, small.dtype)],
        scratch_shapes=[pltpu.SemaphoreType.DMA((n,)), pltpu.SemaphoreType.DMA((n,)),
                        pltpu.SemaphoreType.DMA((n, 6)), pltpu.SemaphoreType.DMA((n, 6)),
                        pltpu.SemaphoreType.DMA((3,)), pltpu.SemaphoreType.DMA((3,))],
    )(*shards, small)


def _sum_small(small, name):
    def body(small_in, small_out, buf, sm_s, sm_r):
        x, y, c, _ = _place()
        me = 4 * x + 2 * y + c
        buf[me] = small_in[...]
        peers = [(fx, fy, fc) for fx in (0, 1) for fy in (0, 1) for fc in (0, 1)][1:]
        sm = []
        for k, (fx, fy, fc) in enumerate(peers):
            dev = (1 - x if fx else x, 1 - y if fy else y, 1 - c if fc else c)
            cp = pltpu.make_async_remote_copy(
                src_ref=small_in, dst_ref=buf.at[me], send_sem=sm_s.at[k], recv_sem=sm_r.at[k],
                device_id=dev, device_id_type=MESH)
            cp.start()
            sm.append(cp)
        for cp in sm:
            cp.wait()
        tot = buf[0]
        for d in range(1, N_DEV):
            tot = tot + buf[d]
        small_out[...] = tot

    return pl.pallas_call(
        body, name=name, in_specs=[VMEM_WHOLE], out_specs=VMEM_WHOLE,
        out_shape=jax.ShapeDtypeStruct(small.shape, F32),
        scratch_shapes=[pltpu.VMEM((N_DEV,) + small.shape, F32),
                        pltpu.SemaphoreType.DMA((N_DEV - 1,)), pltpu.SemaphoreType.DMA((N_DEV - 1,))],
    )(small)


def _swap_ride(grads):
    n = len(grads)

    def make(ins, outs, sems):
        ssem, rsem = sems
        x, y, c, _ = _place()
        return [pltpu.make_async_remote_copy(
            src_ref=_half(ins[a], 1 - c, True), dst_ref=outs[a], send_sem=ssem.at[a], recv_sem=rsem.at[a],
            device_id=(x, y, 1 - c), device_id_type=MESH) for a in range(n)]

    return _Ride(grads, [jax.ShapeDtypeStruct((g.shape[0], g.shape[1] // 2, g.shape[2]), F32) for g in grads],
                 [pltpu.SemaphoreType.DMA((n,)), pltpu.SemaphoreType.DMA((n,))], make)


def _swap_grad_halves(grads, name):
    n = len(grads)

    def body(*refs):
        ins, outs = refs[:n], refs[n:2 * n]
        ssem, rsem = refs[2 * n:]
        x, y, c, _ = _place()
        cps = [pltpu.make_async_remote_copy(
            src_ref=_half(ins[a], 1 - c, True), dst_ref=outs[a], send_sem=ssem.at[a], recv_sem=rsem.at[a],
            device_id=(x, y, 1 - c), device_id_type=MESH) for a in range(n)]
        for cp in cps:
            cp.start()
        for cp in cps:
            cp.wait()

    return pl.pallas_call(
        body, name=name, in_specs=[HBM] * n, out_specs=[HBM] * n,
        out_shape=[jax.ShapeDtypeStruct((g.shape[0], g.shape[1] // 2, g.shape[2]), F32) for g in grads],
        scratch_shapes=[pltpu.SemaphoreType.DMA((n,)), pltpu.SemaphoreType.DMA((n,))],
    )(*grads)


def _chip_sum(g, recv, core, name):
    ns, R, C = g.shape
    r2 = R // 2
    tr = _rows(r2, 512)
    nrb = r2 // tr

    def body(core_ref, g_ref, r_ref, o_ref, ob_ref):
        s = g_ref[...] + r_ref[...]
        o_ref[...] = s
        ob_ref[...] = s.astype(BF16)

    out = pl.BlockSpec((None, tr, C), lambda s, i, cr: (s, i, 0))
    return pl.pallas_call(
        body, name=name,
        grid_spec=pltpu.PrefetchScalarGridSpec(
            num_scalar_prefetch=1, grid=(ns, nrb),
            in_specs=[pl.BlockSpec((None, tr, C), lambda s, i, cr: (s, cr[0] * nrb + i, 0)), out],
            out_specs=[out, out]),
        out_shape=[jax.ShapeDtypeStruct((ns, r2, C), F32), jax.ShapeDtypeStruct((ns, r2, C), BF16)],
        compiler_params=_cparams(),
    )(core, g, recv)


def _shard_sum(csum, got, place, name):
    _, r2, C = csum.shape
    tr = _rows(r2, 512)
    nrb = r2 // tr

    def body(place_ref, c_ref, g0_ref, g1_ref, g2_ref, o_ref):
        o_ref[...] = ((c_ref[...] + g0_ref[...].astype(F32)) + g1_ref[...].astype(F32)) + g2_ref[...].astype(F32)

    def got_spec(k):
        return pl.BlockSpec((None, tr, C), lambda i, pr: (k, i, 0))

    return pl.pallas_call(
        body, name=name,
        grid_spec=pltpu.PrefetchScalarGridSpec(
            num_scalar_prefetch=1, grid=(nrb,),
            in_specs=[pl.BlockSpec((None, tr, C), lambda i, pr: (pr[0], i, 0)), got_spec(0), got_spec(1), got_spec(2)],
            out_specs=pl.BlockSpec((tr, C), lambda i, pr: (pr[1] * nrb + i, 0))),
        out_shape=jax.ShapeDtypeStruct((2 * r2, C), F32),
        compiler_params=_cparams(),
    )(place, csum, got, got, got)


def _gather_ride(shards):
    n = len(shards)

    def make(ins, outs, sems):
        lsem, lrsem, ssem, rsem = sems
        x, y, c, chips = _place()
        j = 2 * x + y
        cps = [pltpu.make_async_remote_copy(
            src_ref=ins[a], dst_ref=outs[a].at[j], send_sem=lsem.at[a], recv_sem=lrsem.at[a],
            device_id=(x, y, 1 - c), device_id_type=MESH) for a in range(n)]
        for a in range(n):
            for k, chip in enumerate(chips):
                cps.append(pltpu.make_async_remote_copy(
                    src_ref=_half(ins[a], c, True), dst_ref=_half(outs[a].at[j], c, True),
                    send_sem=ssem.at[a, k], recv_sem=rsem.at[a, k], device_id=(*chip, c), device_id_type=MESH))
        return cps

    return _Ride(shards, [jax.ShapeDtypeStruct((N_SHARDS,) + s.shape, s.dtype) for s in shards],
                 [pltpu.SemaphoreType.DMA((n,)), pltpu.SemaphoreType.DMA((n,)),
                  pltpu.SemaphoreType.DMA((n, 3)), pltpu.SemaphoreType.DMA((n, 3))], make)


def _forward_halves(gathered, name):
    n = len(gathered)

    def body(*refs):
        outs = refs[n:2 * n]
        ssem, rsem = refs[2 * n:]
        x, y, c, chips = _place()
        cps = []
        for a in range(n):
            for k, chip in enumerate(chips):
                part = _half(outs[a].at[2 * chip[0] + chip[1]], c, True)
                cps.append(pltpu.make_async_remote_copy(
                    src_ref=part, dst_ref=part, send_sem=ssem.at[a, k], recv_sem=rsem.at[a, k],
                    device_id=(x, y, 1 - c), device_id_type=MESH))
        for cp in cps:
            cp.start()
        for cp in cps:
            cp.wait()

    return pl.pallas_call(
        body, name=name, in_specs=[HBM] * n, out_specs=[HBM] * n,
        out_shape=[jax.ShapeDtypeStruct(g.shape, g.dtype) for g in gathered],
        input_output_aliases={a: a for a in range(n)},
        scratch_shapes=[pltpu.SemaphoreType.DMA((n, 3)), pltpu.SemaphoreType.DMA((n, 3))],
    )(*gathered)


def _scatter_ride(csums):
    n = len(csums)

    def make(ins, outs, sems):
        ssem, rsem = sems
        x, y, c, chips = _place()
        return [pltpu.make_async_remote_copy(
            src_ref=ins[a].at[2 * chip[0] + chip[1]], dst_ref=outs[a].at[k], send_sem=ssem.at[a, k],
            recv_sem=rsem.at[a, k], device_id=(*chip, c), device_id_type=MESH)
            for a in range(n) for k, chip in enumerate(chips)]

    return _Ride(csums, [jax.ShapeDtypeStruct((3,) + g.shape[1:], g.dtype) for g in csums],
                 [pltpu.SemaphoreType.DMA((n, 3)), pltpu.SemaphoreType.DMA((n, 3))], make)


def _join_halves(fulls, name):
    n = len(fulls)

    def body(*refs):
        outs = refs[n:2 * n]
        ssem, rsem = refs[2 * n:]
        x, y, c, _ = _place()
        cps = [pltpu.make_async_remote_copy(
            src_ref=_half(outs[a], c, True), dst_ref=_half(outs[a], c, True), send_sem=ssem.at[a],
            recv_sem=rsem.at[a], device_id=(x, y, 1 - c), device_id_type=MESH) for a in range(n)]
        for cp in cps:
            cp.start()
        for cp in cps:
            cp.wait()

    return pl.pallas_call(
        body, name=name, in_specs=[HBM] * n, out_specs=[HBM] * n,
        out_shape=[jax.ShapeDtypeStruct(f.shape, F32) for f in fulls],
        input_output_aliases={a: a for a in range(n)},
        scratch_shapes=[pltpu.SemaphoreType.DMA((n,))] * 2,
    )(*fulls)


def _pad_rows(a, rows, cols):
    return jnp.pad(a, ((0, rows - a.shape[0]), (0, cols - a.shape[1])))


def kernel(x, p, ffn1_norm, ffn1_w_gate, ffn1_w_up, ffn1_w_down, mix_norm, w_in, conv_w, conv_b, q_norm, k_norm, w_out, ffn2_norm, ffn2_w_gate, ffn2_w_up, ffn2_w_down, ple_norm, ple_w_gate, ple_w_proj, loss_target, m_ffn1_norm, m_ffn1_w_gate, m_ffn1_w_up, m_ffn1_w_down, m_mix_norm, m_w_in, m_conv_w, m_conv_b, m_q_norm, m_k_norm, m_w_out, m_ffn2_norm, m_ffn2_w_gate, m_ffn2_w_up, m_ffn2_w_down, m_ple_norm, m_ple_w_gate, m_ple_w_proj, v_ffn1_norm, v_ffn1_w_gate, v_ffn1_w_up, v_ffn1_w_down, v_mix_norm, v_w_in, v_conv_w, v_conv_b, v_q_norm, v_k_norm, v_w_out, v_ffn2_norm, v_ffn2_w_gate, v_ffn2_w_up, v_ffn2_w_down, v_ple_norm, v_ple_w_gate, v_ple_w_proj):
    big = dict(ffn1_w_gate=ffn1_w_gate, ffn1_w_up=ffn1_w_up, ffn1_w_down=ffn1_w_down, w_in=w_in, w_out=w_out,
               ffn2_w_gate=ffn2_w_gate, ffn2_w_up=ffn2_w_up, ffn2_w_down=ffn2_w_down,
               ple_w_gate=ple_w_gate, ple_w_proj=ple_w_proj)
    big_m = dict(ffn1_w_gate=m_ffn1_w_gate, ffn1_w_up=m_ffn1_w_up, ffn1_w_down=m_ffn1_w_down, w_in=m_w_in,
                 w_out=m_w_out, ffn2_w_gate=m_ffn2_w_gate, ffn2_w_up=m_ffn2_w_up, ffn2_w_down=m_ffn2_w_down,
                 ple_w_gate=m_ple_w_gate, ple_w_proj=m_ple_w_proj)
    big_v = dict(ffn1_w_gate=v_ffn1_w_gate, ffn1_w_up=v_ffn1_w_up, ffn1_w_down=v_ffn1_w_down, w_in=v_w_in,
                 w_out=v_w_out, ffn2_w_gate=v_ffn2_w_gate, ffn2_w_up=v_ffn2_w_up, ffn2_w_down=v_ffn2_w_down,
                 ple_w_gate=v_ple_w_gate, ple_w_proj=v_ple_w_proj)
    names = list(big)
    xs = x[0]
    ps = p[0, 0]
    tgt = loss_target[0]
    S, D = xs.shape
    nconv = conv_b.shape[1]
    nsb = D - nconv
    cwl = conv_w.shape[2]
    jchip = 2 * lax.axis_index("x") + lax.axis_index("y")
    core = lax.axis_index("c")

    early, late = names[:3], names[3:]
    assert all(k.startswith("ffn1") for k in early)
    shards = {k: big[k][0].astype(BF16) for k in names}
    gathered = _gather_weights([shards[k] for k in early], _pad_rows(conv_w[0], 8, LANES), "gather_weights")
    W = dict(zip(early, gathered[:-1]))
    cw_full = jnp.transpose(gathered[-1][:, :, :cwl], (1, 0, 2)).reshape(8, N_SHARDS * cwl)
    qg = jnp.tile(q_norm, (1, CONV_COLS // HEAD_DIM))
    kg = jnp.tile(k_norm, (1, CONV_COLS // HEAD_DIM))
    n_units = nconv // CONV_COLS

    x1, h1, a1, b1, *landed = _ffn_fwd(xs, ffn1_norm, W["ffn1_w_gate"], W["ffn1_w_up"], W["ffn1_w_down"], "ffn1_fwd",
                                        ride=_gather_ride([shards[k] for k in late]))
    W.update(zip(late, _forward_halves(landed, "gather_forward")))
    wout_full = W["w_out"].reshape(-1, D)
    wpg_full = W["ple_w_gate"].reshape(-1, D)
    proj, h2 = _norm_proj(x1, mix_norm, W["w_in"], "mix_in_proj")
    y_conv = _conv_fwd(proj, cw_full, conv_b, nconv, "conv_fwd")
    qs = _qk_norm(proj, qg, 3 * n_units, nsb, HEAD_DIM ** -0.5, "q_norm_fwd")
    kh = _qk_norm(proj, kg, 4 * n_units, nsb, 1.0, "k_norm_fwd")
    vb = proj[:, 3 * nconv + 2 * nsb:].astype(BF16)
    y_sb, carry = _attn_fwd(qs, kh, vb, "attn_fwd")
    ycat = jnp.concatenate([y_conv, y_sb], axis=1)
    x2 = _out_proj(ycat, wout_full, x1, "mix_out_proj")
    x3, h3, a3, b3 = _ffn_fwd(x2, ffn2_norm, W["ffn2_w_gate"], W["ffn2_w_up"], W["ffn2_w_down"], "ffn2_fwd")

    dx3, h4, du4, dpp, loss_blk, dg_ple = _ple(x3, ps, tgt, ple_norm, wpg_full, W["ple_w_proj"], "ple_loss")
    G = {}
    tk = _rows(S, 2048)
    nk = S // tk
    kd = wpg_full.shape[0] // N_SHARDS
    G["ple_w_gate"] = _tn_matmul(
        h4, du4, pl.BlockSpec((tk, kd), lambda m, k: (k, m)), pl.BlockSpec((tk, D), lambda m, k: (k, 0)),
        (N_SHARDS, kd, D), pl.BlockSpec((None, kd, D), lambda m, k: (m, 0, 0)), (N_SHARDS, nk), "ple_w_gate_grad")
    P = ps.shape[1]
    npp = D // N_SHARDS
    G["ple_w_proj"] = _tn_matmul(
        ps, dpp, pl.BlockSpec((tk, P), lambda m, k: (k, 0)), pl.BlockSpec((tk, npp), lambda m, k: (k, m)),
        (N_SHARDS, P, npp), pl.BlockSpec((None, P, npp), lambda m, k: (m, 0, 0)), (N_SHARDS, nk), "ple_w_proj_grad")

    def ffn_grads(pre, h, s, da, db, dy):
        fs = s.shape[2]
        hs = pl.BlockSpec((tk, D), lambda m, k: (k, 0))
        ss = pl.BlockSpec((None, tk, fs), lambda m, k: (m, k, 0))
        for leaf, lhs, rhs in (("_w_gate", da, h), ("_w_up", db, h), ("_w_down", s, dy)):
            G[pre + leaf] = _tn_matmul(lhs, rhs, ss, hs, (N_SHARDS, fs, D),
                                       pl.BlockSpec((None, fs, D), lambda m, k: (m, 0, 0)), (N_SHARDS, nk), pre + leaf + "_grad")

    s3, da3, db3, dy3 = _ffn_bwd_act(dx3, a3, b3, W["ffn2_w_down"], "ffn2_bwd_act")
    dx2, dg_ffn2 = _ffn_bwd_in(da3, db3, W["ffn2_w_gate"], W["ffn2_w_up"], dx3, x2, ffn2_norm, "ffn2_bwd_in")
    ffn_grads("ffn2", h3, s3, da3, db3, dy3)

    dycat = _out_proj_bwd(dx2, wout_full, "mix_out_proj_bwd")
    ko = wout_full.shape[0] // N_SHARDS
    G["w_out"] = _tn_matmul(
        ycat, dx2, pl.BlockSpec((tk, ko), lambda m, k: (k, m)), pl.BlockSpec((tk, D), lambda m, k: (k, 0)),
        (N_SHARDS, ko, D), pl.BlockSpec((None, ko, D), lambda m, k: (m, 0, 0)), (N_SHARDS, nk), "w_out_grad")
    db_, dc_, du_, dwb = _conv_bwd(proj, dycat, cw_full, conv_b, nconv, "conv_bwd")
    dqs, dkh, dv = _attn_bwd(qs, kh, vb, dycat, nconv // LANES, carry, "attn_bwd")
    dq, dg_q = _qk_norm_bwd(proj, dqs, qg, 3 * n_units, nsb, HEAD_DIM ** -0.5, "q_norm_bwd")
    dk, dg_k = _qk_norm_bwd(proj, dkh, kg, 4 * n_units, nsb, 1.0, "k_norm_bwd")
    dproj = jnp.concatenate([db_, dc_, du_, dq, dk, dv.astype(BF16)], axis=1)
    nin = W["w_in"].shape[2]
    G["w_in"] = _tn_matmul(
        h2, dproj, pl.BlockSpec((tk, D), lambda m, k: (k, 0)), pl.BlockSpec((tk, nin), lambda m, k: (k, m)),
        (N_SHARDS, D, nin), pl.BlockSpec((None, D, nin), lambda m, k: (m, 0, 0)), (N_SHARDS, nk), "w_in_grad")
    dx1, dg_mix, *recv_late = _norm_proj_bwd(dproj, W["w_in"], dx2, x1, mix_norm, "mix_in_proj_bwd",
                                             ride=_swap_ride([G[k] for k in late]))
    core_arr = jnp.reshape(core, (1,)).astype(jnp.int32)
    place = jnp.stack([jchip, core]).astype(jnp.int32)

    def chip_sums(group, recv):
        return zip(*[_chip_sum(G[k], r, core_arr, f"chip_sum_{k}") for k, r in zip(group, recv)])

    cs_late, csb_late = chip_sums(late, recv_late)
    s1, da1, db1, dy1, *got_late = _ffn_bwd_act(dx1, a1, b1, W["ffn1_w_down"], "ffn1_bwd_act",
                                                  ride=_scatter_ride(list(csb_late)))
    ffn_grads("ffn1", h1, s1, da1, db1, dy1)
    cs_early, csb_early = chip_sums(early, _swap_grad_halves([G[k] for k in early], "grad_swap_halves"))
    dx0, dg_ffn1, *got_early = _ffn_bwd_in(da1, db1, W["ffn1_w_gate"], W["ffn1_w_up"], dx1, xs, ffn1_norm,
                                           "ffn1_bwd_in", ride=_scatter_ride(list(csb_early)))

    assert D >= nconv and D % LANES == 0
    fold = lambda t: t[0].reshape(-1, HEAD_DIM).sum(axis=0)[None, :]
    small_rows = [dg_ffn1[0:1], dg_mix[0:1], dg_ffn2[0:1], dg_ple[0:1],
                  _pad_rows(dwb[3:4], 1, D), _pad_rows(dwb[0:3], 3, D),
                  _pad_rows(fold(dg_q), 1, D), _pad_rows(fold(dg_k), 1, D), _pad_rows(loss_blk[0:1, 0:1], 1, D)]
    small = _pad_rows(jnp.concatenate(small_rows, axis=0), SMALL_ROWS, D)

    small_sum = _sum_small(small, "small_sum")

    full = _join_halves([_shard_sum(cs, gt, place, f"shard_sum_{k}")
                         for k, cs, gt in zip(names, list(cs_early) + list(cs_late), got_early + got_late)],
                        "grad_join_halves")

    out_g, out_d, out_m, out_v = {}, {}, {}, {}
    for a, k in enumerate(names):
        shp = big[k].shape
        g2 = full[a]
        flipped = k.endswith(("_w_gate", "_w_up")) and k.startswith("ffn")
        view = (lambda t: jnp.transpose(t[0])) if flipped else (lambda t: t[0].reshape(g2.shape))
        back = (lambda t: jnp.transpose(t).reshape(shp)) if flipped else (lambda t: t.reshape(shp))
        d_, m_, v_ = _elementwise(_adamw, [view(big[k]), g2, view(big_m[k]), view(big_v[k])], 3, f"adamw_{k}")
        out_g[k], out_d[k], out_m[k], out_v[k] = (back(t) for t in (g2, d_, m_, v_))

    sm_names = ["ffn1_norm", "mix_norm", "ffn2_norm", "ple_norm", "conv_b", "conv_w", "q_norm", "k_norm"]
    sm_w = dict(ffn1_norm=ffn1_norm, mix_norm=mix_norm, ffn2_norm=ffn2_norm, ple_norm=ple_norm, conv_b=conv_b,
                conv_w=conv_w[0], q_norm=q_norm, k_norm=k_norm)
    sm_m = dict(ffn1_norm=m_ffn1_norm, mix_norm=m_mix_norm, ffn2_norm=m_ffn2_norm, ple_norm=m_ple_norm,
                conv_b=m_conv_b, conv_w=m_conv_w[0], q_norm=m_q_norm, k_norm=m_k_norm)
    sm_v = dict(ffn1_norm=v_ffn1_norm, mix_norm=v_mix_norm, ffn2_norm=v_ffn2_norm, ple_norm=v_ple_norm,
                conv_b=v_conv_b, conv_w=v_conv_w[0], q_norm=v_q_norm, k_norm=v_k_norm)
    sm_g = dict(ffn1_norm=small_sum[0:1], mix_norm=small_sum[1:2], ffn2_norm=small_sum[2:3], ple_norm=small_sum[3:4],
                conv_b=small_sum[4:5, :nconv],
                conv_w=lax.dynamic_slice_in_dim(small_sum[5:8, :nconv], jchip * cwl, cwl, axis=1),
                q_norm=small_sum[8:9, :HEAD_DIM], k_norm=small_sum[9:10, :HEAD_DIM])
    loss = small_sum[10, 0]
    pack = lambda d: _pad_rows(jnp.concatenate([_pad_rows(d[k], d[k].shape[0], D) for k in sm_names], axis=0), SMALL_ROWS, D)
    sd, smm, svv = _elementwise(_adamw, [pack(sm_w), pack(sm_g), pack(sm_m), pack(sm_v)], 3, "adamw_small")
    row = 0
    for k in sm_names:
        r_, c_ = sm_w[k].shape
        shp = (1, r_, c_) if k == "conv_w" else (r_, c_)
        out_g[k] = sm_g[k].reshape(shp)
        out_d[k], out_m[k], out_v[k] = (t[row:row + r_, :c_].reshape(shp) for t in (sd, smm, svv))
        row += r_

    order = ["ffn1_norm", "ffn1_w_gate", "ffn1_w_up", "ffn1_w_down", "mix_norm", "w_in", "conv_w", "conv_b",
             "q_norm", "k_norm", "w_out", "ffn2_norm", "ffn2_w_gate", "ffn2_w_up", "ffn2_w_down", "ple_norm",
             "ple_w_gate", "ple_w_proj"]
    return (loss, dx0[None], *[out_g[k] for k in order], *[out_d[k] for k in order],
            *[out_m[k] for k in order], *[out_v[k] for k in order])
```

```python
import jax
import jax.numpy as jnp
from jax import lax
from jax.experimental import pallas as pl
from jax.experimental.pallas import tpu as pltpu

F32 = jnp.float32
BF16 = jnp.bfloat16
MESH = pl.DeviceIdType.MESH

EPS = 1e-6
HEAD_DIM = 64
LANES = 128
FFN_RES = 0.5
ADAM_LR = 0.001
ADAM_B1 = 0.9
ADAM_B2 = 0.999
ADAM_EPS = 1e-08
ADAM_WD = 0.01
ADAM_STEP = 10
N_SHARDS = 4
N_DEV = 8
ATT_TILE = 256
VMEM_LIMIT = 52 * 1024 * 1024
SMALL_ROWS = 16
HBM = pl.BlockSpec(memory_space=pltpu.HBM)
VMEM_WHOLE = pl.BlockSpec(memory_space=pltpu.VMEM)


def _cparams(**kw):
    return pltpu.CompilerParams(vmem_limit_bytes=VMEM_LIMIT, **kw)


def _dot(a, b):
    return jnp.dot(a, b, preferred_element_type=F32)


def _dot_nt(a, b):
    return lax.dot_general(a, b, (((1,), (1,)), ((), ())), preferred_element_type=F32)


def _dot_tn(a, b):
    return lax.dot_general(a, b, (((0,), (0,)), ((), ())), preferred_element_type=F32)


def _sum_dot(x, m):
    return _dot(x.astype(BF16), m)


def _rms(x, g):
    r = lax.rsqrt(jnp.mean(x * x, axis=-1, keepdims=True) + EPS)
    xn = x * r
    return xn * g, xn, r


def _rms_bwd(dh, xn, r, g):
    dxn = dh * g
    dx = r * (dxn - xn * jnp.mean(dxn * xn, axis=-1, keepdims=True))
    return dx, jnp.sum(dh * xn, axis=0, keepdims=True)


def _rows(n, cap=512):
    for t in (2048, 1024, 512, 448, 384, 352, 256, 192, 176, 128, 96, 88, 64, 48, 32, 16, 8):
        if t <= cap and n % t == 0:
            return t
    raise ValueError(f"no row tile for {n}")


class _Ride:
    def __init__(self, ins, out_shapes, sems, make):
        self.ins, self.out_shapes, self.sems, self.make = list(ins), list(out_shapes), list(sems), make

    def split(self, refs, n_in, n_out, n_scratch):
        ni, no = len(self.ins), len(self.out_shapes)
        ins, rin = refs[:n_in], refs[n_in:n_in + ni]
        outs = refs[n_in + ni:n_in + ni + n_out]
        rout = refs[n_in + ni + n_out:n_in + ni + n_out + no]
        rest = refs[n_in + ni + n_out + no:]
        return ins, outs, rest[:n_scratch], lambda: self.make(rin, rout, rest[n_scratch:])


_NO_RIDE = _Ride([], [], [], lambda i, o, s: [])


def _ride_along(copies, first, last):
    @pl.when(first)
    def _():
        for cp in copies():
            cp.start()

    @pl.when(last)
    def _():
        for cp in copies():
            cp.wait()


def _ffn_fwd(x, gain, wg, wu, wd, name, ride=_NO_RIDE):
    S, D = x.shape
    ns, _, fs = wg.shape
    tm = _rows(S, 512)
    ni = S // tm

    def body(*refs):
        (x_ref, g_ref, wg_ref, wu_ref, wd_ref), (xo_ref, h_ref, a_ref, b_ref), (hs, acc), copies = ride.split(refs, 5, 4, 2)
        j = pl.program_id(1)
        _ride_along(copies, (pl.program_id(0) == 0) & (j == 0), (pl.program_id(0) == ni - 1) & (j == ns - 1))

        @pl.when(j == 0)
        def _():
            h, _, _ = _rms(x_ref[...], g_ref[...])
            hb = h.astype(BF16)
            hs[...] = hb
            h_ref[...] = hb
            acc[...] = jnp.zeros_like(acc)

        hb = hs[...]
        a = _dot(hb, wg_ref[...])
        b = _dot(hb, wu_ref[...])
        a_ref[...] = a.astype(BF16)
        b_ref[...] = b.astype(BF16)
        s = (a * jax.nn.sigmoid(a)) * b
        acc[...] += _dot(s.astype(BF16), wd_ref[...])

        @pl.when(j == ns - 1)
        def _():
            xo_ref[...] = x_ref[...] + FFN_RES * acc[...]

    return pl.pallas_call(
        body, name=name, grid=(ni, ns),
        in_specs=[
            pl.BlockSpec((tm, D), lambda i, j: (i, 0)),
            pl.BlockSpec((1, D), lambda i, j: (0, 0)),
            pl.BlockSpec((None, D, fs), lambda i, j: (j, 0, 0)),
            pl.BlockSpec((None, D, fs), lambda i, j: (j, 0, 0)),
            pl.BlockSpec((None, fs, D), lambda i, j: (j, 0, 0)),
        ] + [HBM] * len(ride.ins),
        out_specs=[
            pl.BlockSpec((tm, D), lambda i, j: (i, 0)),
            pl.BlockSpec((tm, D), lambda i, j: (i, 0)),
            pl.BlockSpec((None, tm, fs), lambda i, j: (j, i, 0)),
            pl.BlockSpec((None, tm, fs), lambda i, j: (j, i, 0)),
        ] + [HBM] * len(ride.out_shapes),
        out_shape=[
            jax.ShapeDtypeStruct((S, D), F32),
            jax.ShapeDtypeStruct((S, D), BF16),
            jax.ShapeDtypeStruct((ns, S, fs), BF16),
            jax.ShapeDtypeStruct((ns, S, fs), BF16),
        ] + ride.out_shapes,
        scratch_shapes=[pltpu.VMEM((tm, D), BF16), pltpu.VMEM((tm, D), F32)] + ride.sems,
        compiler_params=_cparams(),
    )(x, gain, wg, wu, wd, *ride.ins)


def _ffn_bwd_act(dxo, a, b, wd, name, ride=_NO_RIDE):
    S, D = dxo.shape
    ns, fs, _ = wd.shape
    tm = _rows(S, 512)
    ni = S // tm

    def body(*refs):
        (dxo_ref, a_ref, b_ref, wd_ref), (s_ref, da_ref, db_ref, dy_ref), (dys,), copies = ride.split(refs, 4, 4, 1)
        j = pl.program_id(1)
        _ride_along(copies, (pl.program_id(0) == 0) & (j == 0), (pl.program_id(0) == ni - 1) & (j == ns - 1))

        @pl.when(j == 0)
        def _():
            dy = (FFN_RES * dxo_ref[...]).astype(BF16)
            dys[...] = dy
            dy_ref[...] = dy

        av = a_ref[...].astype(F32)
        bv = b_ref[...].astype(F32)
        ds = _dot_nt(dys[...], wd_ref[...])
        sig = jax.nn.sigmoid(av)
        sl = av * sig
        s_ref[...] = (sl * bv).astype(BF16)
        da_ref[...] = (ds * bv * (sig * (1.0 + av * (1.0 - sig)))).astype(BF16)
        db_ref[...] = (ds * sl).astype(BF16)

    act = pl.BlockSpec((None, tm, fs), lambda i, j: (j, i, 0))
    row = pl.BlockSpec((tm, D), lambda i, j: (i, 0))
    return pl.pallas_call(
        body, name=name, grid=(ni, ns),
        in_specs=[row, act, act, pl.BlockSpec((None, fs, D), lambda i, j: (j, 0, 0))] + [HBM] * len(ride.ins),
        out_specs=[act, act, act, row] + [HBM] * len(ride.out_shapes),
        out_shape=[jax.ShapeDtypeStruct((ns, S, fs), BF16)] * 3 + [jax.ShapeDtypeStruct((S, D), BF16)]
        + ride.out_shapes,
        scratch_shapes=[pltpu.VMEM((tm, D), BF16)] + ride.sems,
        compiler_params=_cparams(),
    )(dxo, a, b, wd, *ride.ins)


def _ffn_bwd_in(da, db, wg, wu, dres, x, gain, name, ride=_NO_RIDE):
    S, D = x.shape
    ns, _, fs = wg.shape
    tm = _rows(S, 512)
    ni = S // tm

    def body(*refs):
        (da_ref, db_ref, wg_ref, wu_ref, dres_ref, x_ref, g_ref), (dx_ref, dg_ref), (acc,), copies = ride.split(refs, 7, 2, 1)
        i = pl.program_id(0)
        j = pl.program_id(1)
        _ride_along(copies, (i == 0) & (j == 0), (i == ni - 1) & (j == ns - 1))

        @pl.when((i == 0) & (j == 0))
        def _():
            dg_ref[...] = jnp.zeros_like(dg_ref)

        @pl.when(j == 0)
        def _():
            acc[...] = jnp.zeros_like(acc)

        acc[...] += _dot_nt(da_ref[...], wg_ref[...]) + _dot_nt(db_ref[...], wu_ref[...])

        @pl.when(j == ns - 1)
        def _():
            g = g_ref[...]
            _, xn, r = _rms(x_ref[...], g)
            dx, dg = _rms_bwd(acc[...], xn, r, g)
            dx_ref[...] = dres_ref[...] + dx
            dg_ref[...] += jnp.broadcast_to(dg, dg_ref.shape)

    act = pl.BlockSpec((None, tm, fs), lambda i, j: (j, i, 0))
    row = pl.BlockSpec((tm, D), lambda i, j: (i, 0))
    wsp = pl.BlockSpec((None, D, fs), lambda i, j: (j, 0, 0))
    return pl.pallas_call(
        body, name=name, grid=(ni, ns),
        in_specs=[act, act, wsp, wsp, row, row, pl.BlockSpec((1, D), lambda i, j: (0, 0))] + [HBM] * len(ride.ins),
        out_specs=[row, pl.BlockSpec((8, D), lambda i, j: (0, 0))] + [HBM] * len(ride.out_shapes),
        out_shape=[jax.ShapeDtypeStruct((S, D), F32), jax.ShapeDtypeStruct((8, D), F32)] + ride.out_shapes,
        scratch_shapes=[pltpu.VMEM((tm, D), F32)] + ride.sems,
        compiler_params=_cparams(),
    )(da, db, wg, wu, dres, x, gain, *ride.ins)


def _tn_matmul(a, b, a_spec, b_spec, o_shape, o_spec, grid, name):
    kaxis = len(grid) - 1

    def body(a_ref, b_ref, o_ref):
        @pl.when(pl.program_id(kaxis) == 0)
        def _():
            o_ref[...] = jnp.zeros_like(o_ref)

        o_ref[...] += _dot_tn(a_ref[...].astype(BF16), b_ref[...].astype(BF16))

    return pl.pallas_call(
        body, name=name, grid=grid, in_specs=[a_spec, b_spec], out_specs=o_spec,
        out_shape=jax.ShapeDtypeStruct(o_shape, F32), compiler_params=_cparams(),
    )(a, b)


def _norm_proj(x, gain, w, name):
    S, D = x.shape
    ns, _, n = w.shape
    tm = _rows(S, 1024)

    def body(x_ref, g_ref, w_ref, o_ref, h_ref, hs):
        @pl.when(pl.program_id(1) == 0)
        def _():
            h, _, _ = _rms(x_ref[...], g_ref[...])
            hb = h.astype(BF16)
            hs[...] = hb
            h_ref[...] = hb

        o_ref[...] = _dot(hs[...], w_ref[...])

    return pl.pallas_call(
        body, name=name, grid=(S // tm, ns),
        in_specs=[
            pl.BlockSpec((tm, D), lambda i, j: (i, 0)),
            pl.BlockSpec((1, D), lambda i, j: (0, 0)),
            pl.BlockSpec((None, D, n), lambda i, j: (j, 0, 0)),
        ],
        out_specs=[
            pl.BlockSpec((tm, n), lambda i, j: (i, j)),
            pl.BlockSpec((tm, D), lambda i, j: (i, 0)),
        ],
        out_shape=[jax.ShapeDtypeStruct((S, ns * n), F32), jax.ShapeDtypeStruct((S, D), BF16)],
        scratch_shapes=[pltpu.VMEM((tm, D), BF16)],
        compiler_params=_cparams(),
    )(x, gain, w)


def _norm_proj_bwd(dproj, w, dres, x, gain, name, ride=_NO_RIDE):
    S, D = x.shape
    ns, _, n = w.shape
    tm = _rows(S, 1024)
    ni = S // tm

    def body(*refs):
        (dp_ref, w_ref, dres_ref, x_ref, g_ref), (dx_ref, dg_ref), (acc,), copies = ride.split(refs, 5, 2, 1)
        i = pl.program_id(0)
        j = pl.program_id(1)
        _ride_along(copies, (i == 0) & (j == 0), (i == ni - 1) & (j == ns - 1))

        @pl.when((i == 0) & (j == 0))
        def _():
            dg_ref[...] = jnp.zeros_like(dg_ref)

        @pl.when(j == 0)
        def _():
            acc[...] = jnp.zeros_like(acc)

        acc[...] += _dot_nt(dp_ref[...], w_ref[...])

        @pl.when(j == ns - 1)
        def _():
            g = g_ref[...]
            _, xn, r = _rms(x_ref[...], g)
            dx, dg = _rms_bwd(acc[...], xn, r, g)
            dx_ref[...] = dres_ref[...] + dx
            dg_ref[...] += jnp.broadcast_to(dg, dg_ref.shape)

    return pl.pallas_call(
        body, name=name, grid=(ni, ns),
        in_specs=[
            pl.BlockSpec((tm, n), lambda i, j: (i, j)),
            pl.BlockSpec((None, D, n), lambda i, j: (j, 0, 0)),
            pl.BlockSpec((tm, D), lambda i, j: (i, 0)),
            pl.BlockSpec((tm, D), lambda i, j: (i, 0)),
            pl.BlockSpec((1, D), lambda i, j: (0, 0)),
        ] + [HBM] * len(ride.ins),
        out_specs=[
            pl.BlockSpec((tm, D), lambda i, j: (i, 0)),
            pl.BlockSpec((8, D), lambda i, j: (0, 0)),
        ] + [HBM] * len(ride.out_shapes),
        out_shape=[jax.ShapeDtypeStruct((S, D), F32), jax.ShapeDtypeStruct((8, D), F32)] + ride.out_shapes,
        scratch_shapes=[pltpu.VMEM((tm, D), F32)] + ride.sems,
        compiler_params=_cparams(),
    )(dproj, w, dres, x, gain, *ride.ins)


def _out_proj(ycat, w, res, name):
    S, K = ycat.shape
    D = w.shape[1]
    tm = _rows(S, 1024)

    def body(y_ref, w_ref, r_ref, o_ref):
        o_ref[...] = r_ref[...] + _dot(y_ref[...], w_ref[...])

    return pl.pallas_call(
        body, name=name, grid=(S // tm,),
        in_specs=[
            pl.BlockSpec((tm, K), lambda i: (i, 0)),
            pl.BlockSpec((K, D), lambda i: (0, 0)),
            pl.BlockSpec((tm, D), lambda i: (i, 0)),
        ],
        out_specs=pl.BlockSpec((tm, D), lambda i: (i, 0)),
        out_shape=jax.ShapeDtypeStruct((S, D), F32),
        compiler_params=_cparams(),
    )(ycat, w, res)


def _out_proj_bwd(dx, w, name):
    S, D = dx.shape
    K = w.shape[0]
    tm = _rows(S, 1024)

    def body(d_ref, w_ref, o_ref):
        o_ref[...] = _dot_nt(d_ref[...].astype(BF16), w_ref[...])

    return pl.pallas_call(
        body, name=name, grid=(S // tm,),
        in_specs=[pl.BlockSpec((tm, D), lambda i: (i, 0)), pl.BlockSpec((K, D), lambda i: (0, 0))],
        out_specs=pl.BlockSpec((tm, K), lambda i: (i, 0)),
        out_shape=jax.ShapeDtypeStruct((S, K), F32),
        compiler_params=_cparams(),
    )(dx, w)


CONV_COLS = 256


def _shift_down(z, halo, k, row):
    out = pltpu.roll(z, k, 0)
    for n in range(k):
        out = jnp.where(row == n, halo[8 - k + n:8 - k + n + 1, :], out)
    return out


def _shift_up(g, halo, k, row, ts):
    out = pltpu.roll(g, ts - k, 0)
    for n in range(k):
        out = jnp.where(row == ts - k + n, halo[n:n + 1, :], out)
    return out


def _conv_fwd(proj, cw, cb, nconv, name):
    S = proj.shape[0]
    ncb = nconv // CONV_COLS
    ts = _rows(S, 1024)
    hb = ts // 8

    def body(b_ref, c_ref, u_ref, ch_ref, uh_ref, w_ref, bias_ref, o_ref):
        i = pl.program_id(1)
        z = c_ref[...] * u_ref[...]
        halo = jnp.where(i > 0, ch_ref[...] * uh_ref[...], 0.0)
        row = lax.broadcasted_iota(jnp.int32, z.shape, 0)
        w = w_ref[...]
        yc = w[0:1, :] * _shift_down(z, halo, 2, row) + w[1:2, :] * _shift_down(z, halo, 1, row) + w[2:3, :] * z
        o_ref[...] = (b_ref[...] * (yc + bias_ref[...])).astype(BF16)

    def blk(unit):
        return pl.BlockSpec((ts, CONV_COLS), lambda cbi, i: (i, unit * ncb + cbi))

    def prev(unit):
        return pl.BlockSpec((8, CONV_COLS), lambda cbi, i: (jnp.maximum(i * hb - 1, 0), unit * ncb + cbi))

    return pl.pallas_call(
        body, name=name, grid=(ncb, S // ts),
        in_specs=[blk(0), blk(1), blk(2), prev(1), prev(2),
                  pl.BlockSpec((8, CONV_COLS), lambda cbi, i: (0, cbi)),
                  pl.BlockSpec((1, CONV_COLS), lambda cbi, i: (0, cbi))],
        out_specs=pl.BlockSpec((ts, CONV_COLS), lambda cbi, i: (i, cbi)),
        out_shape=jax.ShapeDtypeStruct((S, nconv), BF16),
        compiler_params=_cparams(),
    )(proj, proj, proj, proj, proj, cw, cb)


def _conv_bwd(proj, dy, cw, cb, nconv, name):
    S = proj.shape[0]
    ncb = nconv // CONV_COLS
    ts = _rows(S, 1024)
    hb = ts // 8
    nblk = S // ts

    def body(b_ref, c_ref, u_ref, dy_ref, ch_ref, uh_ref, bn_ref, dyn_ref, w_ref, bias_ref,
             db_ref, dc_ref, du_ref, dw_ref):
        i = pl.program_id(1)

        @pl.when(i == 0)
        def _():
            dw_ref[...] = jnp.zeros_like(dw_ref)

        c = c_ref[...]
        u = u_ref[...]
        bg = b_ref[...]
        dy_ = dy_ref[...]
        z = c * u
        halo = jnp.where(i > 0, ch_ref[...] * uh_ref[...], 0.0)
        row = lax.broadcasted_iota(jnp.int32, z.shape, 0)
        w = w_ref[...]
        z2 = _shift_down(z, halo, 2, row)
        z1 = _shift_down(z, halo, 1, row)
        yc = w[0:1, :] * z2 + w[1:2, :] * z1 + w[2:3, :] * z
        db_ref[...] = (dy_ * (yc + bias_ref[...])).astype(BF16)
        g = dy_ * bg
        gnext = jnp.where(i < nblk - 1, dyn_ref[...] * bn_ref[...], 0.0)
        dz = w[2:3, :] * g + w[1:2, :] * _shift_up(g, gnext, 1, row, ts) + w[0:1, :] * _shift_up(g, gnext, 2, row, ts)
        dc_ref[...] = (dz * u).astype(BF16)
        du_ref[...] = (dz * c).astype(BF16)
        r8 = lax.broadcasted_iota(jnp.int32, (8, CONV_COLS), 0)
        sums = [jnp.sum(g * z2, axis=0, keepdims=True), jnp.sum(g * z1, axis=0, keepdims=True),
                jnp.sum(g * z, axis=0, keepdims=True), jnp.sum(g, axis=0, keepdims=True)]
        upd = jnp.zeros((8, CONV_COLS), F32)
        for n, sv in enumerate(sums):
            upd = jnp.where(r8 == n, sv, upd)
        dw_ref[...] += upd

    def blk(unit):
        return pl.BlockSpec((ts, CONV_COLS), lambda cbi, i: (i, unit * ncb + cbi))

    def prev(unit):
        return pl.BlockSpec((8, CONV_COLS), lambda cbi, i: (jnp.maximum(i * hb - 1, 0), unit * ncb + cbi))

    def nxt(unit):
        return pl.BlockSpec((8, CONV_COLS), lambda cbi, i: (jnp.minimum((i + 1) * hb, S // 8 - 1), unit * ncb + cbi))

    o = pl.BlockSpec((ts, CONV_COLS), lambda cbi, i: (i, cbi))
    return pl.pallas_call(
        body, name=name, grid=(ncb, nblk),
        in_specs=[blk(0), blk(1), blk(2), blk(0), prev(1), prev(2), nxt(0), nxt(0),
                  pl.BlockSpec((8, CONV_COLS), lambda cbi, i: (0, cbi)),
                  pl.BlockSpec((1, CONV_COLS), lambda cbi, i: (0, cbi))],
        out_specs=[o, o, o, pl.BlockSpec((8, CONV_COLS), lambda cbi, i: (0, cbi))],
        out_shape=[jax.ShapeDtypeStruct((S, nconv), BF16)] * 3 + [jax.ShapeDtypeStruct((8, nconv), F32)],
        compiler_params=_cparams(),
    )(proj, proj, proj, dy, proj, proj, proj, dy, cw, cb)


def _group_ones(n):
    r = lax.broadcasted_iota(jnp.int32, (n, n), 0) // HEAD_DIM
    c = lax.broadcasted_iota(jnp.int32, (n, n), 1) // HEAD_DIM
    return jnp.where(r == c, 1.0, 0.0).astype(BF16)


def _qk_norm(proj, gain_t, unit0, nsb, scale, name):
    S = proj.shape[0]
    nb = nsb // CONV_COLS
    ts = _rows(S, 1024)

    def body(x_ref, g_ref, o_ref):
        x = x_ref[...]
        ss = _sum_dot(x * x, _group_ones(CONV_COLS))
        r = lax.rsqrt(ss * (1.0 / HEAD_DIM) + EPS)
        o_ref[...] = ((x * r) * g_ref[...] * scale).astype(BF16)

    return pl.pallas_call(
        body, name=name, grid=(nb, S // ts),
        in_specs=[pl.BlockSpec((ts, CONV_COLS), lambda u, i: (i, unit0 + u)),
                  pl.BlockSpec((1, CONV_COLS), lambda u, i: (0, 0))],
        out_specs=pl.BlockSpec((ts, CONV_COLS), lambda u, i: (i, u)),
        out_shape=jax.ShapeDtypeStruct((S, nsb), BF16),
        compiler_params=_cparams(),
    )(proj, gain_t)


def _qk_norm_bwd(proj, dout, gain_t, unit0, nsb, scale, name):
    S = proj.shape[0]
    nb = nsb // CONV_COLS
    ts = _rows(S, 1024)

    def body(x_ref, d_ref, g_ref, dx_ref, dg_ref):
        @pl.when(pl.program_id(1) == 0)
        def _():
            dg_ref[...] = jnp.zeros_like(dg_ref)

        x = x_ref[...]
        g = g_ref[...]
        ones = _group_ones(CONV_COLS)
        ss = _sum_dot(x * x, ones)
        r = lax.rsqrt(ss * (1.0 / HEAD_DIM) + EPS)
        xn = x * r
        dh = d_ref[...] * scale
        dxn = dh * g
        m = _sum_dot(dxn * xn, ones) * (1.0 / HEAD_DIM)
        dx_ref[...] = (r * (dxn - xn * m)).astype(BF16)
        dg_ref[...] += jnp.broadcast_to(jnp.sum(dh * xn, axis=0, keepdims=True), dg_ref.shape)

    return pl.pallas_call(
        body, name=name, grid=(nb, S // ts),
        in_specs=[pl.BlockSpec((ts, CONV_COLS), lambda u, i: (i, unit0 + u)),
                  pl.BlockSpec((ts, CONV_COLS), lambda u, i: (i, u)),
                  pl.BlockSpec((1, CONV_COLS), lambda u, i: (0, 0))],
        out_specs=[pl.BlockSpec((ts, CONV_COLS), lambda u, i: (i, u)),
                   pl.BlockSpec((8, CONV_COLS), lambda u, i: (0, u))],
        out_shape=[jax.ShapeDtypeStruct((S, nsb), BF16), jax.ShapeDtypeStruct((8, nsb), F32)],
        compiler_params=_cparams(),
    )(proj, dout, gain_t)


Z_CLAMP = 80.0
N_SLOTS = 3
SAT_LIMIT = 120.0


def _head_masks():
    lane = lax.broadcasted_iota(jnp.int32, (1, LANES), 1)
    return [lane < HEAD_DIM, lane >= HEAD_DIM], lane


def _tile_consts(T):
    r_i = lax.broadcasted_iota(jnp.int32, (T, T), 0)
    c_i = lax.broadcasted_iota(jnp.int32, (T, T), 1)
    neg_suffix = jnp.where(r_i >= c_i, -1.0, 0.0).astype(BF16)
    prefix = jnp.where(r_i <= c_i, 1.0, 0.0).astype(BF16)
    return neg_suffix, prefix, c_i < r_i


def _pipeline(n, stages, first_special=False, last_special=False, saturated=None, extra_head=0):
    depth = len(stages)
    head = (depth if first_special else depth - 1) + extra_head
    off = 0 if last_special else 1
    for m in range(head):
        for k in reversed(range(min(m, depth - 1) + 1)):
            stages[k](m - k, (m - k) % N_SLOTS, first_special and m == k)

    def trip(m, u):
        for k in reversed(range(depth)):
            stages[k](m - k, (head + u - k) % N_SLOTS, False)

    def group(g, carry):
        for u in range(N_SLOTS):
            trip(head + g * N_SLOTS + u, u)
        return carry

    count = n - 1 + off - head
    full = count // N_SLOTS
    if saturated is None:
        lax.fori_loop(0, full, group, 0)
        go_on, done = True, n
    else:
        def more(state):
            return (state[0] < full) & (state[1] == 0)

        def step(state):
            group(state[0], 0)
            return state[0] + 1, saturated().astype(jnp.int32)

        groups, stop = lax.while_loop(more, step, (jnp.int32(0), saturated().astype(jnp.int32)))
        go_on = stop == 0
        done = jnp.where(go_on, n, head - depth + 1 + N_SLOTS * groups)
    for r in range(N_SLOTS):
        @pl.when((count - full * N_SLOTS == r) & go_on)
        def _(r=r):
            for u in range(r):
                trip(head + full * N_SLOTS + u, u)
            for e in range(depth - off):
                for k in reversed(range(e + off, depth)):
                    t = n - 1 - (k - e - off)
                    stages[k](t, (head + r + e - k) % N_SLOTS, last_special and k == e + off)
    return done


def _sweep(n, stages, finish, first_special=False, last_special=False, saturated=None, extra_head=0):
    depth = len(stages)
    least = (depth if first_special else depth - 1) + extra_head + (1 if last_special else 0)
    for short in range(1, least):
        @pl.when(n == short)
        def _(short=short):
            for m in range(short + depth - 1):
                for k in reversed(range(depth)):
                    t = m - k
                    if 0 <= t < short:
                        stages[k](t, t % N_SLOTS, (first_special and t == 0) or (last_special and t == short - 1))
            finish(short)

    @pl.when(n >= least)
    def _():
        finish(_pipeline(n, stages, first_special, last_special, saturated, extra_head))


def _attn_fwd(q, k, v, name):
    S, nsb = q.shape
    T = ATT_TILE
    hp = nsb // LANES
    nb = S // T
    assert nb <= HEAD_DIM

    def body(q_ref, k_ref, v_ref, y_ref, cs_ref, c_ref, acc, z_st, inc_st):
        i = pl.program_id(1)
        masks, lane = _head_masks()
        qv = q_ref[...]
        qm = [jnp.where(m, qv, jnp.zeros_like(qv)) for m in masks]
        neg_suffix, _, causal = _tile_consts(T)
        c_ref[...] = jnp.zeros_like(c_ref)
        acc[...] = jnp.zeros_like(acc)
        cs_ref[...] = jnp.zeros_like(cs_ref)

        def blk(ref, j):
            return ref[pl.ds(pl.multiple_of(j * T, T), T), :]

        def scores(t, slot, diag):
            kj = blk(k_ref, i - t)
            for h in range(2):
                z_st[slot, h] = jnp.minimum(_dot_nt(qm[h], kj), Z_CLAMP)

        def suffix_sums(t, slot, diag):
            for h in range(2):
                sp = jnp.log(1.0 + jnp.exp(z_st[slot, h]))
                if diag:
                    sp = jnp.where(causal, sp, 0.0)
                inc_st[slot, h] = _dot(sp.astype(BF16), neg_suffix)

        def weights(t, slot, diag):
            vj = blk(v_ref, i - t)
            for h in range(2):
                inc = inc_st[slot, h]
                c = c_ref[h]
                a = jnp.exp(z_st[slot, h] + inc + c)
                if diag:
                    a = jnp.where(causal, a, 0.0)
                upd = _dot(a.astype(BF16), vj)
                acc[...] += jnp.where(masks[h], upd, 0.0)
                cs_ref[...] = jnp.where(lane == i - t + HEAD_DIM * h, c, cs_ref[...])
                c_ref[h] = c + inc[:, 0:1]

        stages = [scores, suffix_sums, weights]

        def saturated():
            return jnp.max(c_ref[...]) < -SAT_LIMIT

        def note(used):
            cs_ref[...] = jnp.where(lane == LANES - 1, jnp.asarray(used).astype(F32), cs_ref[...])

        _sweep(i + 1, stages, note, first_special=True, saturated=saturated, extra_head=1)
        y_ref[...] = acc[...].astype(BF16)

    return pl.pallas_call(
        body, name=name, grid=(hp, nb),
        in_specs=[pl.BlockSpec((T, LANES), lambda p, i: (i, p)),
                  pl.BlockSpec((S, LANES), lambda p, i: (0, p)),
                  pl.BlockSpec((S, LANES), lambda p, i: (0, p))],
        out_specs=[pl.BlockSpec((T, LANES), lambda p, i: (i, p)),
                   pl.BlockSpec((None, T, LANES), lambda p, i: (p, i, 0))],
        out_shape=[jax.ShapeDtypeStruct((S, nsb), BF16), jax.ShapeDtypeStruct((hp, S, LANES), F32)],
        scratch_shapes=[pltpu.VMEM((2, T, 1), F32), pltpu.VMEM((T, LANES), F32),
                        pltpu.VMEM((N_SLOTS, 2, T, T), F32), pltpu.VMEM((N_SLOTS, 2, T, T), F32)],
        compiler_params=_cparams(),
    )(q, k, v)


def _attn_bwd(q, k, v, dy, col0, carry, name):
    S, nsb = q.shape
    T = ATT_TILE
    hp = nsb // LANES
    nb = S // T

    def body(q_ref, k_ref, v_ref, dy_ref, cs_ref, dq_ref, dk_ref, dv_ref, e_ref, acc,
             z_st, da_st, b_st, inc_st, a_st, e_st, p_st):
        i = pl.program_id(1)

        @pl.when(i == 0)
        def _():
            dk_ref[...] = jnp.zeros_like(dk_ref)
            dv_ref[...] = jnp.zeros_like(dv_ref)

        masks, lane = _head_masks()
        qv = q_ref[...]
        dyb = dy_ref[...].astype(BF16)
        qm = [jnp.where(m, qv, jnp.zeros_like(qv)) for m in masks]
        dym = [jnp.where(m, dyb, jnp.zeros_like(dyb)) for m in masks]
        neg_suffix, prefix, causal = _tile_consts(T)
        e_ref[...] = jnp.zeros_like(e_ref)
        acc[...] = jnp.zeros_like(acc)

        def blk(ref, j):
            return ref[pl.ds(pl.multiple_of(j * T, T), T), :]

        used = jnp.max(jnp.where(lane == LANES - 1, cs_ref[...], 0.0)).astype(jnp.int32)
        n = jnp.clip(used, 1, i + 1)
        first = i + 1 - n

        def scores(t, slot, diag):
            kj = blk(k_ref, first + t)
            vj = blk(v_ref, first + t)
            for h in range(2):
                z_st[slot, h] = jnp.minimum(_dot_nt(qm[h], kj), Z_CLAMP)
                da_st[slot, h] = _dot_nt(dym[h], vj)

        def suffix_sums(t, slot, diag):
            for h in range(2):
                u = jnp.exp(z_st[slot, h])
                w = 1.0 + u
                b_st[slot, h] = u / w
                sp = jnp.log(w)
                if diag:
                    sp = jnp.where(causal, sp, 0.0)
                inc_st[slot, h] = _dot(sp.astype(BF16), neg_suffix)

        def probs(t, slot, diag):
            csv = cs_ref[...]
            for h in range(2):
                c = jnp.sum(jnp.where(lane == first + t + HEAD_DIM * h, csv, 0.0), axis=-1, keepdims=True)
                a = jnp.exp(z_st[slot, h] + inc_st[slot, h] + c)
                if diag:
                    a = jnp.where(causal, a, 0.0)
                a_st[slot, h] = a.astype(BF16)
                e = a * da_st[slot, h]
                e_st[slot, h] = e
                p_st[slot, h] = _dot(e.astype(BF16), prefix)

        def grads(t, slot, diag):
            kj = blk(k_ref, first + t)
            off = pl.multiple_of((first + t) * T, T)
            for h in range(2):
                p = p_st[slot, h]
                dz = e_st[slot, h] - b_st[slot, h] * (e_ref[h] + p)
                if diag:
                    dz = jnp.where(causal, dz, 0.0)
                dzb = dz.astype(BF16)
                acc[...] += jnp.where(masks[h], _dot(dzb, kj), 0.0)
                dk_ref[pl.ds(off, T), :] += _dot_tn(dzb, qm[h])
                dv_ref[pl.ds(off, T), :] += _dot_tn(a_st[slot, h], dym[h])
                e_ref[h] += p[:, T - 1:T]

        stages = [scores, suffix_sums, probs, grads]

        _sweep(n, stages, lambda done: None, last_special=True)
        dq_ref[...] = acc[...]

    return pl.pallas_call(
        body, name=name, grid=(hp, nb),
        in_specs=[pl.BlockSpec((T, LANES), lambda p, i: (i, p)),
                  pl.BlockSpec((S, LANES), lambda p, i: (0, p)),
                  pl.BlockSpec((S, LANES), lambda p, i: (0, p)),
                  pl.BlockSpec((T, LANES), lambda p, i: (i, col0 + p)),
                  pl.BlockSpec((None, T, LANES), lambda p, i: (p, i, 0))],
        out_specs=[pl.BlockSpec((T, LANES), lambda p, i: (i, p)),
                   pl.BlockSpec((S, LANES), lambda p, i: (0, p)),
                   pl.BlockSpec((S, LANES), lambda p, i: (0, p))],
        out_shape=[jax.ShapeDtypeStruct((S, nsb), F32)] * 3,
        scratch_shapes=[pltpu.VMEM((2, T, 1), F32), pltpu.VMEM((T, LANES), F32)]
        + [pltpu.VMEM((N_SLOTS, 2, T, T), dt) for dt in (F32, F32, F32, F32, BF16, F32, F32)],
        compiler_params=_cparams(),
    )(q, k, v, dy, carry)


def _ple(x, p, tgt, gain, wpg, wpp, name):
    S, D = x.shape
    P = p.shape[1]
    ns, _, nc = wpp.shape
    tm = _rows(S, 256)

    def body(x_ref, p_ref, t_ref, g_ref, wpg_ref, wpp_ref, dx_ref, h_ref, du_ref, dpp_ref, loss_ref, dg_ref):
        @pl.when(pl.program_id(0) == 0)
        def _():
            loss_ref[...] = jnp.zeros_like(loss_ref)
            dg_ref[...] = jnp.zeros_like(dg_ref)

        x_ = x_ref[...]
        g = g_ref[...]
        h, xn, r = _rms(x_, g)
        hb = h.astype(BF16)
        h_ref[...] = hb
        gate = jax.nn.sigmoid(_dot(hb, wpg_ref[...]))
        pb = p_ref[...].astype(BF16)
        pp = jnp.concatenate([_dot(pb, wpp_ref[n]) for n in range(ns)], axis=1)
        err = (x_ + gate * pp) - t_ref[...]
        loss_ref[...] += (0.5 / D) * jnp.sum(err * err)
        dy = err * (1.0 / D)
        du = ((dy * pp) * (gate * (1.0 - gate))).astype(BF16)
        du_ref[...] = du
        dpp_ref[...] = (dy * gate).astype(BF16)
        dx, dg = _rms_bwd(_dot_nt(du, wpg_ref[...]), xn, r, g)
        dx_ref[...] = dy + dx
        dg_ref[...] += jnp.broadcast_to(dg, dg_ref.shape)

    row = pl.BlockSpec((tm, D), lambda i: (i, 0))
    return pl.pallas_call(
        body, name=name, grid=(S // tm,),
        in_specs=[row, pl.BlockSpec((tm, P), lambda i: (i, 0)), row,
                  pl.BlockSpec((1, D), lambda i: (0, 0)),
                  pl.BlockSpec((D, D), lambda i: (0, 0)),
                  pl.BlockSpec((ns, P, nc), lambda i: (0, 0, 0))],
        out_specs=[row, row, row, row,
                   pl.BlockSpec((8, LANES), lambda i: (0, 0)),
                   pl.BlockSpec((8, D), lambda i: (0, 0))],
        out_shape=[jax.ShapeDtypeStruct((S, D), F32)] + [jax.ShapeDtypeStruct((S, D), BF16)] * 3
        + [jax.ShapeDtypeStruct((8, LANES), F32), jax.ShapeDtypeStruct((8, D), F32)],
        compiler_params=_cparams(),
    )(x, p, tgt, gain, wpg, wpp)


def _elementwise(fn, ins, n_out, name):
    R, C = ins[0].shape
    tr = _rows(R, 512)

    def body(*refs):
        outs = fn(*[r[...] for r in refs[:len(ins)]])
        for o_ref, o in zip(refs[len(ins):], outs):
            o_ref[...] = o

    spec = pl.BlockSpec((tr, C), lambda i: (i, 0))
    return pl.pallas_call(
        body, name=name, grid=(R // tr,), in_specs=[spec] * len(ins), out_specs=[spec] * n_out,
        out_shape=[jax.ShapeDtypeStruct((R, C), F32)] * n_out, compiler_params=_cparams(),
    )(*ins)


def _adamw(w, g, m, v):
    m = ADAM_B1 * m + (1.0 - ADAM_B1) * g
    v = ADAM_B2 * v + (1.0 - ADAM_B2) * jnp.square(g)
    m_hat = m / (1.0 - ADAM_B1 ** ADAM_STEP)
    v_hat = v / (1.0 - ADAM_B2 ** ADAM_STEP)
    delta = -ADAM_LR * (m_hat / (jnp.sqrt(v_hat) + ADAM_EPS) + ADAM_WD * w)
    return delta, m, v


def _place():
    x, y, c = lax.axis_index("x"), lax.axis_index("y"), lax.axis_index("c")
    chips = [(1 - x, y), (x, 1 - y), (1 - x, 1 - y)]
    return x, y, c, chips


def _half(ref, c, axis_rows):
    n = ref.shape[-2]
    start = pl.multiple_of(c * (n // 2), 8)
    idx = (slice(None),) * (len(ref.shape) - 2) + (pl.ds(start, n // 2), slice(None))
    return ref.at[idx]


def _gather_weights(shards, small, name):
    n = len(shards)

    def body(*refs):
        ins, small_in = refs[:n], refs[n]
        outs, small_out = refs[n + 1:2 * n + 1], refs[2 * n + 1]
        lsem, lrsem, ssem, rsem, sm_s, sm_r = refs[2 * n + 2:]
        x, y, c, chips = _place()
        j = 2 * x + y
        sib = (x, y, 1 - c)

        local = [pltpu.make_async_remote_copy(
            src_ref=ins[a], dst_ref=outs[a].at[j], send_sem=lsem.at[a], recv_sem=lrsem.at[a],
            device_id=sib, device_id_type=MESH) for a in range(n)]
        for cp in local:
            cp.start()
        small_out[j] = small_in[...]
        small_cp = [pltpu.make_async_remote_copy(
            src_ref=small_in, dst_ref=small_out.at[j], send_sem=sm_s.at[k], recv_sem=sm_r.at[k],
            device_id=(*chip, c), device_id_type=MESH) for k, chip in enumerate(chips)]
        for cp in small_cp:
            cp.start()

        def ici(a, k, chip, jj, dev):
            return pltpu.make_async_remote_copy(
                src_ref=_half(ins[a], c, True) if dev is not None else _half(outs[a].at[jj], c, True),
                dst_ref=_half(outs[a].at[jj], c, True),
                send_sem=ssem.at[a, k], recv_sem=rsem.at[a, k],
                device_id=dev if dev is not None else (*chip, c), device_id_type=MESH)

        first = []
        for a in range(n):
            for k, chip in enumerate(chips):
                cp = ici(a, k, chip, j, (*chip, c))
                cp.start()
                first.append(cp)
        passed = []
        for a in range(n):
            for k, chip in enumerate(chips):
                jj = 2 * chip[0] + chip[1]
                ici(a, k, chip, jj, None).wait_recv()
                fw = pltpu.make_async_remote_copy(
                    src_ref=_half(outs[a].at[jj], c, True), dst_ref=_half(outs[a].at[jj], c, True),
                    send_sem=ssem.at[a, 3 + k], recv_sem=rsem.at[a, 3 + k], device_id=sib, device_id_type=MESH)
                fw.start()
                passed.append(fw)
        for a in range(n):
            for k, chip in enumerate(chips):
                jj = 2 * chip[0] + chip[1]
                pltpu.make_async_remote_copy(
                    src_ref=_half(outs[a].at[jj], 1 - c, True), dst_ref=_half(outs[a].at[jj], 1 - c, True),
                    send_sem=ssem.at[a, 3 + k], recv_sem=rsem.at[a, 3 + k], device_id=sib,
                    device_id_type=MESH).wait_recv()
        for cp in small_cp:
            cp.wait()
        for cp in first + passed:
            cp.wait_send()
        for cp in local:
            cp.wait()

    return pl.pallas_call(
        body, name=name,
        in_specs=[HBM] * n + [VMEM_WHOLE],
        out_specs=[HBM] * n + [VMEM_WHOLE],
        out_shape=[jax.ShapeDtypeStruct((N_SHARDS,) + s.shape, s.dtype) for s in shards]
        + [jax.ShapeDtypeStruct((N_SHARDS,) + small.shape, small.dtype)],
        scratch_shapes=[pltpu.SemaphoreType.DMA((n,)), pltpu.SemaphoreType.DMA((n,)),
                        pltpu.SemaphoreType.DMA((n, 6)), pltpu.SemaphoreType.DMA((n, 6)),
                        pltpu.SemaphoreType.DMA((3,)), pltpu.SemaphoreType.DMA((3,))],
    )(*shards, small)


def _sum_small(small, name):
    def body(small_in, small_out, buf, sm_s, sm_r):
        x, y, c, _ = _place()
        me = 4 * x + 2 * y + c
        buf[me] = small_in[...]
        peers = [(fx, fy, fc) for fx in (0, 1) for fy in (0, 1) for fc in (0, 1)][1:]
        sm = []
        for k, (fx, fy, fc) in enumerate(peers):
            dev = (1 - x if fx else x, 1 - y if fy else y, 1 - c if fc else c)
            cp = pltpu.make_async_remote_copy(
                src_ref=small_in, dst_ref=buf.at[me], send_sem=sm_s.at[k], recv_sem=sm_r.at[k],
                device_id=dev, device_id_type=MESH)
            cp.start()
            sm.append(cp)
        for cp in sm:
            cp.wait()
        tot = buf[0]
        for d in range(1, N_DEV):
            tot = tot + buf[d]
        small_out[...] = tot

    return pl.pallas_call(
        body, name=name, in_specs=[VMEM_WHOLE], out_specs=VMEM_WHOLE,
        out_shape=jax.ShapeDtypeStruct(small.shape, F32),
        scratch_shapes=[pltpu.VMEM((N_DEV,) + small.shape, F32),
                        pltpu.SemaphoreType.DMA((N_DEV - 1,)), pltpu.SemaphoreType.DMA((N_DEV - 1,))],
    )(small)


def _swap_ride(grads):
    n = len(grads)

    def make(ins, outs, sems):
        ssem, rsem = sems
        x, y, c, _ = _place()
        return [pltpu.make_async_remote_copy(
            src_ref=_half(ins[a], 1 - c, True), dst_ref=outs[a], send_sem=ssem.at[a], recv_sem=rsem.at[a],
            device_id=(x, y, 1 - c), device_id_type=MESH) for a in range(n)]

    return _Ride(grads, [jax.ShapeDtypeStruct((g.shape[0], g.shape[1] // 2, g.shape[2]), F32) for g in grads],
                 [pltpu.SemaphoreType.DMA((n,)), pltpu.SemaphoreType.DMA((n,))], make)


def _swap_grad_halves(grads, name):
    n = len(grads)

    def body(*refs):
        ins, outs = refs[:n], refs[n:2 * n]
        ssem, rsem = refs[2 * n:]
        x, y, c, _ = _place()
        cps = [pltpu.make_async_remote_copy(
            src_ref=_half(ins[a], 1 - c, True), dst_ref=outs[a], send_sem=ssem.at[a], recv_sem=rsem.at[a],
            device_id=(x, y, 1 - c), device_id_type=MESH) for a in range(n)]
        for cp in cps:
            cp.start()
        for cp in cps:
            cp.wait()

    return pl.pallas_call(
        body, name=name, in_specs=[HBM] * n, out_specs=[HBM] * n,
        out_shape=[jax.ShapeDtypeStruct((g.shape[0], g.shape[1] // 2, g.shape[2]), F32) for g in grads],
        scratch_shapes=[pltpu.SemaphoreType.DMA((n,)), pltpu.SemaphoreType.DMA((n,))],
    )(*grads)


def _chip_sum(g, recv, core, name):
    ns, R, C = g.shape
    r2 = R // 2
    tr = _rows(r2, 512)
    nrb = r2 // tr

    def body(core_ref, g_ref, r_ref, o_ref, ob_ref):
        s = g_ref[...] + r_ref[...]
        o_ref[...] = s
        ob_ref[...] = s.astype(BF16)

    out = pl.BlockSpec((None, tr, C), lambda s, i, cr: (s, i, 0))
    return pl.pallas_call(
        body, name=name,
        grid_spec=pltpu.PrefetchScalarGridSpec(
            num_scalar_prefetch=1, grid=(ns, nrb),
            in_specs=[pl.BlockSpec((None, tr, C), lambda s, i, cr: (s, cr[0] * nrb + i, 0)), out],
            out_specs=[out, out]),
        out_shape=[jax.ShapeDtypeStruct((ns, r2, C), F32), jax.ShapeDtypeStruct((ns, r2, C), BF16)],
        compiler_params=_cparams(),
    )(core, g, recv)


def _shard_sum(csum, got, place, name):
    _, r2, C = csum.shape
    tr = _rows(r2, 512)
    nrb = r2 // tr

    def body(place_ref, c_ref, g0_ref, g1_ref, g2_ref, o_ref):
        o_ref[...] = ((c_ref[...] + g0_ref[...].astype(F32)) + g1_ref[...].astype(F32)) + g2_ref[...].astype(F32)

    def got_spec(k):
        return pl.BlockSpec((None, tr, C), lambda i, pr: (k, i, 0))

    return pl.pallas_call(
        body, name=name,
        grid_spec=pltpu.PrefetchScalarGridSpec(
            num_scalar_prefetch=1, grid=(nrb,),
            in_specs=[pl.BlockSpec((None, tr, C), lambda i, pr: (pr[0], i, 0)), got_spec(0), got_spec(1), got_spec(2)],
            out_specs=pl.BlockSpec((tr, C), lambda i, pr: (pr[1] * nrb + i, 0))),
        out_shape=jax.ShapeDtypeStruct((2 * r2, C), F32),
        compiler_params=_cparams(),
    )(place, csum, got, got, got)


def _gather_ride(shards):
    n = len(shards)

    def make(ins, outs, sems):
        lsem, lrsem, ssem, rsem = sems
        x, y, c, chips = _place()
        j = 2 * x + y
        cps = [pltpu.make_async_remote_copy(
            src_ref=ins[a], dst_ref=outs[a].at[j], send_sem=lsem.at[a], recv_sem=lrsem.at[a],
            device_id=(x, y, 1 - c), device_id_type=MESH) for a in range(n)]
        for a in range(n):
            for k, chip in enumerate(chips):
                cps.append(pltpu.make_async_remote_copy(
                    src_ref=_half(ins[a], c, True), dst_ref=_half(outs[a].at[j], c, True),
                    send_sem=ssem.at[a, k], recv_sem=rsem.at[a, k], device_id=(*chip, c), device_id_type=MESH))
        return cps

    return _Ride(shards, [jax.ShapeDtypeStruct((N_SHARDS,) + s.shape, s.dtype) for s in shards],
                 [pltpu.SemaphoreType.DMA((n,)), pltpu.SemaphoreType.DMA((n,)),
                  pltpu.SemaphoreType.DMA((n, 3)), pltpu.SemaphoreType.DMA((n, 3))], make)


def _forward_halves(gathered, name):
    n = len(gathered)

    def body(*refs):
        outs = refs[n:2 * n]
        ssem, rsem = refs[2 * n:]
        x, y, c, chips = _place()
        cps = []
        for a in range(n):
            for k, chip in enumerate(chips):
                part = _half(outs[a].at[2 * chip[0] + chip[1]], c, True)
                cps.append(pltpu.make_async_remote_copy(
                    src_ref=part, dst_ref=part, send_sem=ssem.at[a, k], recv_sem=rsem.at[a, k],
                    device_id=(x, y, 1 - c), device_id_type=MESH))
        for cp in cps:
            cp.start()
        for cp in cps:
            cp.wait()

    return pl.pallas_call(
        body, name=name, in_specs=[HBM] * n, out_specs=[HBM] * n,
        out_shape=[jax.ShapeDtypeStruct(g.shape, g.dtype) for g in gathered],
        input_output_aliases={a: a for a in range(n)},
        scratch_shapes=[pltpu.SemaphoreType.DMA((n, 3)), pltpu.SemaphoreType.DMA((n, 3))],
    )(*gathered)


def _scatter_ride(csums):
    n = len(csums)

    def make(ins, outs, sems):
        ssem, rsem = sems
        x, y, c, chips = _place()
        return [pltpu.make_async_remote_copy(
            src_ref=ins[a].at[2 * chip[0] + chip[1]], dst_ref=outs[a].at[k], send_sem=ssem.at[a, k],
            recv_sem=rsem.at[a, k], device_id=(*chip, c), device_id_type=MESH)
            for a in range(n) for k, chip in enumerate(chips)]

    return _Ride(csums, [jax.ShapeDtypeStruct((3,) + g.shape[1:], g.dtype) for g in csums],
                 [pltpu.SemaphoreType.DMA((n, 3)), pltpu.SemaphoreType.DMA((n, 3))], make)


def _join_halves(fulls, name):
    n = len(fulls)

    def body(*refs):
        outs = refs[n:2 * n]
        ssem, rsem = refs[2 * n:]
        x, y, c, _ = _place()
        cps = [pltpu.make_async_remote_copy(
            src_ref=_half(outs[a], c, True), dst_ref=_half(outs[a], c, True), send_sem=ssem.at[a],
            recv_sem=rsem.at[a], device_id=(x, y, 1 - c), device_id_type=MESH) for a in range(n)]
        for cp in cps:
            cp.start()
        for cp in cps:
            cp.wait()

    return pl.pallas_call(
        body, name=name, in_specs=[HBM] * n, out_specs=[HBM] * n,
        out_shape=[jax.ShapeDtypeStruct(f.shape, F32) for f in fulls],
        input_output_aliases={a: a for a in range(n)},
        scratch_shapes=[pltpu.SemaphoreType.DMA((n,))] * 2,
    )(*fulls)


def _pad_rows(a, rows, cols):
    return jnp.pad(a, ((0, rows - a.shape[0]), (0, cols - a.shape[1])))


def kernel(x, p, ffn1_norm, ffn1_w_gate, ffn1_w_up, ffn1_w_down, mix_norm, w_in, conv_w, conv_b, q_norm, k_norm, w_out, ffn2_norm, ffn2_w_gate, ffn2_w_up, ffn2_w_down, ple_norm, ple_w_gate, ple_w_proj, loss_target, m_ffn1_norm, m_ffn1_w_gate, m_ffn1_w_up, m_ffn1_w_down, m_mix_norm, m_w_in, m_conv_w, m_conv_b, m_q_norm, m_k_norm, m_w_out, m_ffn2_norm, m_ffn2_w_gate, m_ffn2_w_up, m_ffn2_w_down, m_ple_norm, m_ple_w_gate, m_ple_w_proj, v_ffn1_norm, v_ffn1_w_gate, v_ffn1_w_up, v_ffn1_w_down, v_mix_norm, v_w_in, v_conv_w, v_conv_b, v_q_norm, v_k_norm, v_w_out, v_ffn2_norm, v_ffn2_w_gate, v_ffn2_w_up, v_ffn2_w_down, v_ple_norm, v_ple_w_gate, v_ple_w_proj):
    big = dict(ffn1_w_gate=ffn1_w_gate, ffn1_w_up=ffn1_w_up, ffn1_w_down=ffn1_w_down, w_in=w_in, w_out=w_out,
               ffn2_w_gate=ffn2_w_gate, ffn2_w_up=ffn2_w_up, ffn2_w_down=ffn2_w_down,
               ple_w_gate=ple_w_gate, ple_w_proj=ple_w_proj)
    big_m = dict(ffn1_w_gate=m_ffn1_w_gate, ffn1_w_up=m_ffn1_w_up, ffn1_w_down=m_ffn1_w_down, w_in=m_w_in,
                 w_out=m_w_out, ffn2_w_gate=m_ffn2_w_gate, ffn2_w_up=m_ffn2_w_up, ffn2_w_down=m_ffn2_w_down,
                 ple_w_gate=m_ple_w_gate, ple_w_proj=m_ple_w_proj)
    big_v = dict(ffn1_w_gate=v_ffn1_w_gate, ffn1_w_up=v_ffn1_w_up, ffn1_w_down=v_ffn1_w_down, w_in=v_w_in,
                 w_out=v_w_out, ffn2_w_gate=v_ffn2_w_gate, ffn2_w_up=v_ffn2_w_up, ffn2_w_down=v_ffn2_w_down,
                 ple_w_gate=v_ple_w_gate, ple_w_proj=v_ple_w_proj)
    names = list(big)
    xs = x[0]
    ps = p[0, 0]
    tgt = loss_target[0]
    S, D = xs.shape
    nconv = conv_b.shape[1]
    nsb = D - nconv
    cwl = conv_w.shape[2]
    jchip = 2 * lax.axis_index("x") + lax.axis_index("y")
    core = lax.axis_index("c")

    early, late = names[:3], names[3:]
    assert all(k.startswith("ffn1") for k in early)
    shards = {k: big[k][0].astype(BF16) for k in names}
    gathered = _gather_weights([shards[k] for k in early], _pad_rows(conv_w[0], 8, LANES), "gather_weights")
    W = dict(zip(early, gathered[:-1]))
    cw_full = jnp.transpose(gathered[-1][:, :, :cwl], (1, 0, 2)).reshape(8, N_SHARDS * cwl)
    qg = jnp.tile(q_norm, (1, CONV_COLS // HEAD_DIM))
    kg = jnp.tile(k_norm, (1, CONV_COLS // HEAD_DIM))
    n_units = nconv // CONV_COLS

    x1, h1, a1, b1, *landed = _ffn_fwd(xs, ffn1_norm, W["ffn1_w_gate"], W["ffn1_w_up"], W["ffn1_w_down"], "ffn1_fwd",
                                        ride=_gather_ride([shards[k] for k in late]))
    W.update(zip(late, _forward_halves(landed, "gather_forward")))
    wout_full = W["w_out"].reshape(-1, D)
    wpg_full = W["ple_w_gate"].reshape(-1, D)
    proj, h2 = _norm_proj(x1, mix_norm, W["w_in"], "mix_in_proj")
    y_conv = _conv_fwd(proj, cw_full, conv_b, nconv, "conv_fwd")
    qs = _qk_norm(proj, qg, 3 * n_units, nsb, HEAD_DIM ** -0.5, "q_norm_fwd")
    kh = _qk_norm(proj, kg, 4 * n_units, nsb, 1.0, "k_norm_fwd")
    vb = proj[:, 3 * nconv + 2 * nsb:].astype(BF16)
    y_sb, carry = _attn_fwd(qs, kh, vb, "attn_fwd")
    ycat = jnp.concatenate([y_conv, y_sb], axis=1)
    x2 = _out_proj(ycat, wout_full, x1, "mix_out_proj")
    x3, h3, a3, b3 = _ffn_fwd(x2, ffn2_norm, W["ffn2_w_gate"], W["ffn2_w_up"], W["ffn2_w_down"], "ffn2_fwd")

    dx3, h4, du4, dpp, loss_blk, dg_ple = _ple(x3, ps, tgt, ple_norm, wpg_full, W["ple_w_proj"], "ple_loss")
    G = {}
    tk = _rows(S, 2048)
    nk = S // tk
    kd = wpg_full.shape[0] // N_SHARDS
    G["ple_w_gate"] = _tn_matmul(
        h4, du4, pl.BlockSpec((tk, kd), lambda m, k: (k, m)), pl.BlockSpec((tk, D), lambda m, k: (k, 0)),
        (N_SHARDS, kd, D), pl.BlockSpec((None, kd, D), lambda m, k: (m, 0, 0)), (N_SHARDS, nk), "ple_w_gate_grad")
    P = ps.shape[1]
    npp = D // N_SHARDS
    G["ple_w_proj"] = _tn_matmul(
        ps, dpp, pl.BlockSpec((tk, P), lambda m, k: (k, 0)), pl.BlockSpec((tk, npp), lambda m, k: (k, m)),
        (N_SHARDS, P, npp), pl.BlockSpec((None, P, npp), lambda m, k: (m, 0, 0)), (N_SHARDS, nk), "ple_w_proj_grad")

    def ffn_grads(pre, h, s, da, db, dy):
        fs = s.shape[2]
        hs = pl.BlockSpec((tk, D), lambda m, k: (k, 0))
        ss = pl.BlockSpec((None, tk, fs), lambda m, k: (m, k, 0))
        for leaf, lhs, rhs in (("_w_gate", da, h), ("_w_up", db, h), ("_w_down", s, dy)):
            G[pre + leaf] = _tn_matmul(lhs, rhs, ss, hs, (N_SHARDS, fs, D),
                                       pl.BlockSpec((None, fs, D), lambda m, k: (m, 0, 0)), (N_SHARDS, nk), pre + leaf + "_grad")

    s3, da3, db3, dy3 = _ffn_bwd_act(dx3, a3, b3, W["ffn2_w_down"], "ffn2_bwd_act")
    dx2, dg_ffn2 = _ffn_bwd_in(da3, db3, W["ffn2_w_gate"], W["ffn2_w_up"], dx3, x2, ffn2_norm, "ffn2_bwd_in")
    ffn_grads("ffn2", h3, s3, da3, db3, dy3)

    dycat = _out_proj_bwd(dx2, wout_full, "mix_out_proj_bwd")
    ko = wout_full.shape[0] // N_SHARDS
    G["w_out"] = _tn_matmul(
        ycat, dx2, pl.BlockSpec((tk, ko), lambda m, k: (k, m)), pl.BlockSpec((tk, D), lambda m, k: (k, 0)),
        (N_SHARDS, ko, D), pl.BlockSpec((None, ko, D), lambda m, k: (m, 0, 0)), (N_SHARDS, nk), "w_out_grad")
    db_, dc_, du_, dwb = _conv_bwd(proj, dycat, cw_full, conv_b, nconv, "conv_bwd")
    dqs, dkh, dv = _attn_bwd(qs, kh, vb, dycat, nconv // LANES, carry, "attn_bwd")
    dq, dg_q = _qk_norm_bwd(proj, dqs, qg, 3 * n_units, nsb, HEAD_DIM ** -0.5, "q_norm_bwd")
    dk, dg_k = _qk_norm_bwd(proj, dkh, kg, 4 * n_units, nsb, 1.0, "k_norm_bwd")
    dproj = jnp.concatenate([db_, dc_, du_, dq, dk, dv.astype(BF16)], axis=1)
    nin = W["w_in"].shape[2]
    G["w_in"] = _tn_matmul(
        h2, dproj, pl.BlockSpec((tk, D), lambda m, k: (k, 0)), pl.BlockSpec((tk, nin), lambda m, k: (k, m)),
        (N_SHARDS, D, nin), pl.BlockSpec((None, D, nin), lambda m, k: (m, 0, 0)), (N_SHARDS, nk), "w_in_grad")
    dx1, dg_mix, *recv_late = _norm_proj_bwd(dproj, W["w_in"], dx2, x1, mix_norm, "mix_in_proj_bwd",
                                             ride=_swap_ride([G[k] for k in late]))
    core_arr = jnp.reshape(core, (1,)).astype(jnp.int32)
    place = jnp.stack([jchip, core]).astype(jnp.int32)

    def chip_sums(group, recv):
        return zip(*[_chip_sum(G[k], r, core_arr, f"chip_sum_{k}") for k, r in zip(group, recv)])

    cs_late, csb_late = chip_sums(late, recv_late)
    s1, da1, db1, dy1, *got_late = _ffn_bwd_act(dx1, a1, b1, W["ffn1_w_down"], "ffn1_bwd_act",
                                                  ride=_scatter_ride(list(csb_late)))
    ffn_grads("ffn1", h1, s1, da1, db1, dy1)
    cs_early, csb_early = chip_sums(early, _swap_grad_halves([G[k] for k in early], "grad_swap_halves"))
    dx0, dg_ffn1, *got_early = _ffn_bwd_in(da1, db1, W["ffn1_w_gate"], W["ffn1_w_up"], dx1, xs, ffn1_norm,
                                           "ffn1_bwd_in", ride=_scatter_ride(list(csb_early)))

    assert D >= nconv and D % LANES == 0
    fold = lambda t: t[0].reshape(-1, HEAD_DIM).sum(axis=0)[None, :]
    small_rows = [dg_ffn1[0:1], dg_mix[0:1], dg_ffn2[0:1], dg_ple[0:1],
                  _pad_rows(dwb[3:4], 1, D), _pad_rows(dwb[0:3], 3, D),
                  _pad_rows(fold(dg_q), 1, D), _pad_rows(fold(dg_k), 1, D), _pad_rows(loss_blk[0:1, 0:1], 1, D)]
    small = _pad_rows(jnp.concatenate(small_rows, axis=0), SMALL_ROWS, D)

    small_sum = _sum_small(small, "small_sum")

    full = _join_halves([_shard_sum(cs, gt, place, f"shard_sum_{k}")
                         for k, cs, gt in zip(names, list(cs_early) + list(cs_late), got_early + got_late)],
                        "grad_join_halves")

    out_g, out_d, out_m, out_v = {}, {}, {}, {}
    for a, k in enumerate(names):
        shp = big[k].shape
        g2 = full[a]
        flipped = k.endswith(("_w_gate", "_w_up")) and k.startswith("ffn")
        view = (lambda t: jnp.transpose(t[0])) if flipped else (lambda t: t[0].reshape(g2.shape))
        back = (lambda t: jnp.transpose(t).reshape(shp)) if flipped else (lambda t: t.reshape(shp))
        d_, m_, v_ = _elementwise(_adamw, [view(big[k]), g2, view(big_m[k]), view(big_v[k])], 3, f"adamw_{k}")
        out_g[k], out_d[k], out_m[k], out_v[k] = (back(t) for t in (g2, d_, m_, v_))

    sm_names = ["ffn1_norm", "mix_norm", "ffn2_norm", "ple_norm", "conv_b", "conv_w", "q_norm", "k_norm"]
    sm_w = dict(ffn1_norm=ffn1_norm, mix_norm=mix_norm, ffn2_norm=ffn2_norm, ple_norm=ple_norm, conv_b=conv_b,
                conv_w=conv_w[0], q_norm=q_norm, k_norm=k_norm)
    sm_m = dict(ffn1_norm=m_ffn1_norm, mix_norm=m_mix_norm, ffn2_norm=m_ffn2_norm, ple_norm=m_ple_norm,
                conv_b=m_conv_b, conv_w=m_conv_w[0], q_norm=m_q_norm, k_norm=m_k_norm)
    sm_v = dict(ffn1_norm=v_ffn1_norm, mix_norm=v_mix_norm, ffn2_norm=v_ffn2_norm, ple_norm=v_ple_norm,
                conv_b=v_conv_b, conv_w=v_conv_w[0], q_norm=v_q_norm, k_norm=v_k_norm)
    sm_g = dict(ffn1_norm=small_sum[0:1], mix_norm=small_sum[1:2], ffn2_norm=small_sum[2:3], ple_norm=small_sum[3:4],
                conv_b=small_sum[4:5, :nconv],
                conv_w=lax.dynamic_slice_in_dim(small_sum[5:8, :nconv], jchip * cwl, cwl, axis=1),
                q_norm=small_sum[8:9, :HEAD_DIM], k_norm=small_sum[9:10, :HEAD_DIM])
    loss = small_sum[10, 0]
    pack = lambda d: _pad_rows(jnp.concatenate([_pad_rows(d[k], d[k].shape[0], D) for k in sm_names], axis=0), SMALL_ROWS, D)
    sd, smm, svv = _elementwise(_adamw, [pack(sm_w), pack(sm_g), pack(sm_m), pack(sm_v)], 3, "adamw_small")
    row = 0
    for k in sm_names:
        r_, c_ = sm_w[k].shape
        shp = (1, r_, c_) if k == "conv_w" else (r_, c_)
        out_g[k] = sm_g[k].reshape(shp)
        out_d[k], out_m[k], out_v[k] = (t[row:row + r_, :c_].reshape(shp) for t in (sd, smm, svv))
        row += r_

    order = ["ffn1_norm", "ffn1_w_gate", "ffn1_w_up", "ffn1_w_down", "mix_norm", "w_in", "conv_w", "conv_b",
             "q_norm", "k_norm", "w_out", "ffn2_norm", "ffn2_w_gate", "ffn2_w_up", "ffn2_w_down", "ple_norm",
             "ple_w_gate", "ple_w_proj"]
    return (loss, dx0[None], *[out_g[k] for k in order], *[out_d[k] for k in order],
            *[out_m[k] for k in order], *[out_v[k] for k in order])
```

```python
import jax
import jax.numpy as jnp
from jax import lax
from jax.experimental import pallas as pl
from jax.experimental.pallas import tpu as pltpu

F32 = jnp.float32
BF16 = jnp.bfloat16
MESH = pl.DeviceIdType.MESH

EPS = 1e-6
HEAD_DIM = 64
LANES = 128
FFN_RES = 0.5
ADAM_LR = 0.001
ADAM_B1 = 0.9
ADAM_B2 = 0.999
ADAM_EPS = 1e-08
ADAM_WD = 0.01
ADAM_STEP = 10
N_SHARDS = 4
N_DEV = 8
ATT_TILE = 256
VMEM_LIMIT = 52 * 1024 * 1024
SMALL_ROWS = 16
HBM = pl.BlockSpec(memory_space=pltpu.HBM)
VMEM_WHOLE = pl.BlockSpec(memory_space=pltpu.VMEM)


def _cparams(**kw):
    return pltpu.CompilerParams(vmem_limit_bytes=VMEM_LIMIT, **kw)


def _dot(a, b):
    return jnp.dot(a, b, preferred_element_type=F32)


def _dot_nt(a, b):
    return lax.dot_general(a, b, (((1,), (1,)), ((), ())), preferred_element_type=F32)


def _dot_tn(a, b):
    return lax.dot_general(a, b, (((0,), (0,)), ((), ())), preferred_element_type=F32)


def _sum_dot(x, m):
    return _dot(x.astype(BF16), m)


def _rms(x, g):
    r = lax.rsqrt(jnp.mean(x * x, axis=-1, keepdims=True) + EPS)
    xn = x * r
    return xn * g, xn, r


def _rms_bwd(dh, xn, r, g):
    dxn = dh * g
    dx = r * (dxn - xn * jnp.mean(dxn * xn, axis=-1, keepdims=True))
    return dx, jnp.sum(dh * xn, axis=0, keepdims=True)


def _rows(n, cap=512):
    for t in (4096, 2048, 1024, 512, 448, 384, 352, 256, 192, 176, 128, 96, 88, 64, 48, 32, 16, 8):
        if t <= cap and n % t == 0:
            return t
    raise ValueError(f"no row tile for {n}")


class _Ride:
    def __init__(self, ins, out_shapes, sems, make):
        self.ins, self.out_shapes, self.sems, self.make = list(ins), list(out_shapes), list(sems), make

    def split(self, refs, n_in, n_out, n_scratch):
        ni, no = len(self.ins), len(self.out_shapes)
        ins, rin = refs[:n_in], refs[n_in:n_in + ni]
        outs = refs[n_in + ni:n_in + ni + n_out]
        rout = refs[n_in + ni + n_out:n_in + ni + n_out + no]
        rest = refs[n_in + ni + n_out + no:]
        return ins, outs, rest[:n_scratch], lambda: self.make(rin, rout, rest[n_scratch:])


_NO_RIDE = _Ride([], [], [], lambda i, o, s: [])


def _ride_along(copies, first, last):
    @pl.when(first)
    def _():
        for cp in copies():
            cp.start()

    @pl.when(last)
    def _():
        for cp in copies():
            cp.wait()


def _ffn_fwd(x, gain, wg, wu, wd, name, ride=_NO_RIDE):
    S, D = x.shape
    ns, _, fs = wg.shape
    tm = _rows(S, 512)
    ni = S // tm

    def body(*refs):
        (x_ref, g_ref, wg_ref, wu_ref, wd_ref), (xo_ref, h_ref, a_ref, b_ref), (hs, acc), copies = ride.split(refs, 5, 4, 2)
        j = pl.program_id(1)
        _ride_along(copies, (pl.program_id(0) == 0) & (j == 0), (pl.program_id(0) == ni - 1) & (j == ns - 1))

        @pl.when(j == 0)
        def _():
            h, _, _ = _rms(x_ref[...], g_ref[...])
            hb = h.astype(BF16)
            hs[...] = hb
            h_ref[...] = hb
            acc[...] = jnp.zeros_like(acc)

        hb = hs[...]
        a = _dot(hb, wg_ref[...])
        b = _dot(hb, wu_ref[...])
        a_ref[...] = a.astype(BF16)
        b_ref[...] = b.astype(BF16)
        s = (a * jax.nn.sigmoid(a)) * b
        acc[...] += _dot(s.astype(BF16), wd_ref[...])

        @pl.when(j == ns - 1)
        def _():
            xo_ref[...] = x_ref[...] + FFN_RES * acc[...]

    return pl.pallas_call(
        body, name=name, grid=(ni, ns),
        in_specs=[
            pl.BlockSpec((tm, D), lambda i, j: (i, 0)),
            pl.BlockSpec((1, D), lambda i, j: (0, 0)),
            pl.BlockSpec((None, D, fs), lambda i, j: (j, 0, 0)),
            pl.BlockSpec((None, D, fs), lambda i, j: (j, 0, 0)),
            pl.BlockSpec((None, fs, D), lambda i, j: (j, 0, 0)),
        ] + [HBM] * len(ride.ins),
        out_specs=[
            pl.BlockSpec((tm, D), lambda i, j: (i, 0)),
            pl.BlockSpec((tm, D), lambda i, j: (i, 0)),
            pl.BlockSpec((None, tm, fs), lambda i, j: (j, i, 0)),
            pl.BlockSpec((None, tm, fs), lambda i, j: (j, i, 0)),
        ] + [HBM] * len(ride.out_shapes),
        out_shape=[
            jax.ShapeDtypeStruct((S, D), F32),
            jax.ShapeDtypeStruct((S, D), BF16),
            jax.ShapeDtypeStruct((ns, S, fs), BF16),
            jax.ShapeDtypeStruct((ns, S, fs), BF16),
        ] + ride.out_shapes,
        scratch_shapes=[pltpu.VMEM((tm, D), BF16), pltpu.VMEM((tm, D), F32)] + ride.sems,
        compiler_params=_cparams(),
    )(x, gain, wg, wu, wd, *ride.ins)


def _ffn_bwd_act(dxo, a, b, wd, name, ride=_NO_RIDE):
    S, D = dxo.shape
    ns, fs, _ = wd.shape
    tm = _rows(S, 512)
    ni = S // tm

    def body(*refs):
        (dxo_ref, a_ref, b_ref, wd_ref), (s_ref, da_ref, db_ref, dy_ref), (dys,), copies = ride.split(refs, 4, 4, 1)
        j = pl.program_id(1)
        _ride_along(copies, (pl.program_id(0) == 0) & (j == 0), (pl.program_id(0) == ni - 1) & (j == ns - 1))

        @pl.when(j == 0)
        def _():
            dy = (FFN_RES * dxo_ref[...]).astype(BF16)
            dys[...] = dy
            dy_ref[...] = dy

        av = a_ref[...].astype(F32)
        bv = b_ref[...].astype(F32)
        ds = _dot_nt(dys[...], wd_ref[...])
        sig = jax.nn.sigmoid(av)
        sl = av * sig
        s_ref[...] = (sl * bv).astype(BF16)
        da_ref[...] = (ds * bv * (sig * (1.0 + av * (1.0 - sig)))).astype(BF16)
        db_ref[...] = (ds * sl).astype(BF16)

    act = pl.BlockSpec((None, tm, fs), lambda i, j: (j, i, 0))
    row = pl.BlockSpec((tm, D), lambda i, j: (i, 0))
    return pl.pallas_call(
        body, name=name, grid=(ni, ns),
        in_specs=[row, act, act, pl.BlockSpec((None, fs, D), lambda i, j: (j, 0, 0))] + [HBM] * len(ride.ins),
        out_specs=[act, act, act, row] + [HBM] * len(ride.out_shapes),
        out_shape=[jax.ShapeDtypeStruct((ns, S, fs), BF16)] * 3 + [jax.ShapeDtypeStruct((S, D), BF16)]
        + ride.out_shapes,
        scratch_shapes=[pltpu.VMEM((tm, D), BF16)] + ride.sems,
        compiler_params=_cparams(),
    )(dxo, a, b, wd, *ride.ins)


def _ffn_bwd_in(da, db, wg, wu, dres, x, gain, name, ride=_NO_RIDE):
    S, D = x.shape
    ns, _, fs = wg.shape
    tm = _rows(S, 512)
    ni = S // tm

    def body(*refs):
        (da_ref, db_ref, wg_ref, wu_ref, dres_ref, x_ref, g_ref), (dx_ref, dg_ref), (acc,), copies = ride.split(refs, 7, 2, 1)
        i = pl.program_id(0)
        j = pl.program_id(1)
        _ride_along(copies, (i == 0) & (j == 0), (i == ni - 1) & (j == ns - 1))

        @pl.when((i == 0) & (j == 0))
        def _():
            dg_ref[...] = jnp.zeros_like(dg_ref)

        @pl.when(j == 0)
        def _():
            acc[...] = jnp.zeros_like(acc)

        acc[...] += _dot_nt(da_ref[...], wg_ref[...]) + _dot_nt(db_ref[...], wu_ref[...])

        @pl.when(j == ns - 1)
        def _():
            g = g_ref[...]
            _, xn, r = _rms(x_ref[...], g)
            dx, dg = _rms_bwd(acc[...], xn, r, g)
            dx_ref[...] = dres_ref[...] + dx
            dg_ref[...] += jnp.broadcast_to(dg, dg_ref.shape)

    act = pl.BlockSpec((None, tm, fs), lambda i, j: (j, i, 0))
    row = pl.BlockSpec((tm, D), lambda i, j: (i, 0))
    wsp = pl.BlockSpec((None, D, fs), lambda i, j: (j, 0, 0))
    return pl.pallas_call(
        body, name=name, grid=(ni, ns),
        in_specs=[act, act, wsp, wsp, row, row, pl.BlockSpec((1, D), lambda i, j: (0, 0))] + [HBM] * len(ride.ins),
        out_specs=[row, pl.BlockSpec((8, D), lambda i, j: (0, 0))] + [HBM] * len(ride.out_shapes),
        out_shape=[jax.ShapeDtypeStruct((S, D), F32), jax.ShapeDtypeStruct((8, D), F32)] + ride.out_shapes,
        scratch_shapes=[pltpu.VMEM((tm, D), F32)] + ride.sems,
        compiler_params=_cparams(),
    )(da, db, wg, wu, dres, x, gain, *ride.ins)


def _tn_matmul(a, b, a_spec, b_spec, o_shape, o_spec, grid, name):
    kaxis = len(grid) - 1

    def body(a_ref, b_ref, o_ref):
        @pl.when(pl.program_id(kaxis) == 0)
        def _():
            o_ref[...] = jnp.zeros_like(o_ref)

        o_ref[...] += _dot_tn(a_ref[...].astype(BF16), b_ref[...].astype(BF16))

    return pl.pallas_call(
        body, name=name, grid=grid, in_specs=[a_spec, b_spec], out_specs=o_spec,
        out_shape=jax.ShapeDtypeStruct(o_shape, F32), compiler_params=_cparams(),
    )(a, b)


def _norm_proj(x, gain, w, name):
    S, D = x.shape
    ns, _, n = w.shape
    tm = _rows(S, 1024)

    def body(x_ref, g_ref, w_ref, o_ref, h_ref, hs):
        @pl.when(pl.program_id(1) == 0)
        def _():
            h, _, _ = _rms(x_ref[...], g_ref[...])
            hb = h.astype(BF16)
            hs[...] = hb
            h_ref[...] = hb

        o_ref[...] = _dot(hs[...], w_ref[...])

    return pl.pallas_call(
        body, name=name, grid=(S // tm, ns),
        in_specs=[
            pl.BlockSpec((tm, D), lambda i, j: (i, 0)),
            pl.BlockSpec((1, D), lambda i, j: (0, 0)),
            pl.BlockSpec((None, D, n), lambda i, j: (j, 0, 0)),
        ],
        out_specs=[
            pl.BlockSpec((tm, n), lambda i, j: (i, j)),
            pl.BlockSpec((tm, D), lambda i, j: (i, 0)),
        ],
        out_shape=[jax.ShapeDtypeStruct((S, ns * n), F32), jax.ShapeDtypeStruct((S, D), BF16)],
        scratch_shapes=[pltpu.VMEM((tm, D), BF16)],
        compiler_params=_cparams(),
    )(x, gain, w)


def _norm_proj_bwd(dproj, w, dres, x, gain, name, ride=_NO_RIDE):
    S, D = x.shape
    ns, _, n = w.shape
    tm = _rows(S, 1024)
    ni = S // tm

    def body(*refs):
        (dp_ref, w_ref, dres_ref, x_ref, g_ref), (dx_ref, dg_ref), (acc,), copies = ride.split(refs, 5, 2, 1)
        i = pl.program_id(0)
        j = pl.program_id(1)
        _ride_along(copies, (i == 0) & (j == 0), (i == ni - 1) & (j == ns - 1))

        @pl.when((i == 0) & (j == 0))
        def _():
            dg_ref[...] = jnp.zeros_like(dg_ref)

        @pl.when(j == 0)
        def _():
            acc[...] = jnp.zeros_like(acc)

        acc[...] += _dot_nt(dp_ref[...], w_ref[...])

        @pl.when(j == ns - 1)
        def _():
            g = g_ref[...]
            _, xn, r = _rms(x_ref[...], g)
            dx, dg = _rms_bwd(acc[...], xn, r, g)
            dx_ref[...] = dres_ref[...] + dx
            dg_ref[...] += jnp.broadcast_to(dg, dg_ref.shape)

    return pl.pallas_call(
        body, name=name, grid=(ni, ns),
        in_specs=[
            pl.BlockSpec((tm, n), lambda i, j: (i, j)),
            pl.BlockSpec((None, D, n), lambda i, j: (j, 0, 0)),
            pl.BlockSpec((tm, D), lambda i, j: (i, 0)),
            pl.BlockSpec((tm, D), lambda i, j: (i, 0)),
            pl.BlockSpec((1, D), lambda i, j: (0, 0)),
        ] + [HBM] * len(ride.ins),
        out_specs=[
            pl.BlockSpec((tm, D), lambda i, j: (i, 0)),
            pl.BlockSpec((8, D), lambda i, j: (0, 0)),
        ] + [HBM] * len(ride.out_shapes),
        out_shape=[jax.ShapeDtypeStruct((S, D), F32), jax.ShapeDtypeStruct((8, D), F32)] + ride.out_shapes,
        scratch_shapes=[pltpu.VMEM((tm, D), F32)] + ride.sems,
        compiler_params=_cparams(),
    )(dproj, w, dres, x, gain, *ride.ins)


def _out_proj(ycat, w, res, name):
    S, K = ycat.shape
    D = w.shape[1]
    tm = _rows(S, 1024)

    def body(y_ref, w_ref, r_ref, o_ref):
        o_ref[...] = r_ref[...] + _dot(y_ref[...], w_ref[...])

    return pl.pallas_call(
        body, name=name, grid=(S // tm,),
        in_specs=[
            pl.BlockSpec((tm, K), lambda i: (i, 0)),
            pl.BlockSpec((K, D), lambda i: (0, 0)),
            pl.BlockSpec((tm, D), lambda i: (i, 0)),
        ],
        out_specs=pl.BlockSpec((tm, D), lambda i: (i, 0)),
        out_shape=jax.ShapeDtypeStruct((S, D), F32),
        compiler_params=_cparams(),
    )(ycat, w, res)


def _out_proj_bwd(dx, w, name):
    S, D = dx.shape
    K = w.shape[0]
    tm = _rows(S, 1024)

    def body(d_ref, w_ref, o_ref):
        o_ref[...] = _dot_nt(d_ref[...].astype(BF16), w_ref[...])

    return pl.pallas_call(
        body, name=name, grid=(S // tm,),
        in_specs=[pl.BlockSpec((tm, D), lambda i: (i, 0)), pl.BlockSpec((K, D), lambda i: (0, 0))],
        out_specs=pl.BlockSpec((tm, K), lambda i: (i, 0)),
        out_shape=jax.ShapeDtypeStruct((S, K), F32),
        compiler_params=_cparams(),
    )(dx, w)


CONV_COLS = 256


def _shift_down(z, halo, k, row):
    out = pltpu.roll(z, k, 0)
    for n in range(k):
        out = jnp.where(row == n, halo[8 - k + n:8 - k + n + 1, :], out)
    return out


def _shift_up(g, halo, k, row, ts):
    out = pltpu.roll(g, ts - k, 0)
    for n in range(k):
        out = jnp.where(row == ts - k + n, halo[n:n + 1, :], out)
    return out


def _conv_fwd(proj, cw, cb, nconv, name):
    S = proj.shape[0]
    ncb = nconv // CONV_COLS
    ts = _rows(S, 1024)
    hb = ts // 8

    def body(b_ref, c_ref, u_ref, ch_ref, uh_ref, w_ref, bias_ref, o_ref):
        i = pl.program_id(1)
        z = c_ref[...] * u_ref[...]
        halo = jnp.where(i > 0, ch_ref[...] * uh_ref[...], 0.0)
        row = lax.broadcasted_iota(jnp.int32, z.shape, 0)
        w = w_ref[...]
        yc = w[0:1, :] * _shift_down(z, halo, 2, row) + w[1:2, :] * _shift_down(z, halo, 1, row) + w[2:3, :] * z
        o_ref[...] = (b_ref[...] * (yc + bias_ref[...])).astype(BF16)

    def blk(unit):
        return pl.BlockSpec((ts, CONV_COLS), lambda cbi, i: (i, unit * ncb + cbi))

    def prev(unit):
        return pl.BlockSpec((8, CONV_COLS), lambda cbi, i: (jnp.maximum(i * hb - 1, 0), unit * ncb + cbi))

    return pl.pallas_call(
        body, name=name, grid=(ncb, S // ts),
        in_specs=[blk(0), blk(1), blk(2), prev(1), prev(2),
                  pl.BlockSpec((8, CONV_COLS), lambda cbi, i: (0, cbi)),
                  pl.BlockSpec((1, CONV_COLS), lambda cbi, i: (0, cbi))],
        out_specs=pl.BlockSpec((ts, CONV_COLS), lambda cbi, i: (i, cbi)),
        out_shape=jax.ShapeDtypeStruct((S, nconv), BF16),
        compiler_params=_cparams(),
    )(proj, proj, proj, proj, proj, cw, cb)


def _conv_bwd(proj, dy, cw, cb, nconv, name):
    S = proj.shape[0]
    ncb = nconv // CONV_COLS
    ts = _rows(S, 1024)
    hb = ts // 8
    nblk = S // ts

    def body(b_ref, c_ref, u_ref, dy_ref, ch_ref, uh_ref, bn_ref, dyn_ref, w_ref, bias_ref,
             db_ref, dc_ref, du_ref, dw_ref):
        i = pl.program_id(1)

        @pl.when(i == 0)
        def _():
            dw_ref[...] = jnp.zeros_like(dw_ref)

        c = c_ref[...]
        u = u_ref[...]
        bg = b_ref[...]
        dy_ = dy_ref[...]
        z = c * u
        halo = jnp.where(i > 0, ch_ref[...] * uh_ref[...], 0.0)
        row = lax.broadcasted_iota(jnp.int32, z.shape, 0)
        w = w_ref[...]
        z2 = _shift_down(z, halo, 2, row)
        z1 = _shift_down(z, halo, 1, row)
        yc = w[0:1, :] * z2 + w[1:2, :] * z1 + w[2:3, :] * z
        db_ref[...] = (dy_ * (yc + bias_ref[...])).astype(BF16)
        g = dy_ * bg
        gnext = jnp.where(i < nblk - 1, dyn_ref[...] * bn_ref[...], 0.0)
        dz = w[2:3, :] * g + w[1:2, :] * _shift_up(g, gnext, 1, row, ts) + w[0:1, :] * _shift_up(g, gnext, 2, row, ts)
        dc_ref[...] = (dz * u).astype(BF16)
        du_ref[...] = (dz * c).astype(BF16)
        r8 = lax.broadcasted_iota(jnp.int32, (8, CONV_COLS), 0)
        sums = [jnp.sum(g * z2, axis=0, keepdims=True), jnp.sum(g * z1, axis=0, keepdims=True),
                jnp.sum(g * z, axis=0, keepdims=True), jnp.sum(g, axis=0, keepdims=True)]
        upd = jnp.zeros((8, CONV_COLS), F32)
        for n, sv in enumerate(sums):
            upd = jnp.where(r8 == n, sv, upd)
        dw_ref[...] += upd

    def blk(unit):
        return pl.BlockSpec((ts, CONV_COLS), lambda cbi, i: (i, unit * ncb + cbi))

    def prev(unit):
        return pl.BlockSpec((8, CONV_COLS), lambda cbi, i: (jnp.maximum(i * hb - 1, 0), unit * ncb + cbi))

    def nxt(unit):
        return pl.BlockSpec((8, CONV_COLS), lambda cbi, i: (jnp.minimum((i + 1) * hb, S // 8 - 1), unit * ncb + cbi))

    o = pl.BlockSpec((ts, CONV_COLS), lambda cbi, i: (i, cbi))
    return pl.pallas_call(
        body, name=name, grid=(ncb, nblk),
        in_specs=[blk(0), blk(1), blk(2), blk(0), prev(1), prev(2), nxt(0), nxt(0),
                  pl.BlockSpec((8, CONV_COLS), lambda cbi, i: (0, cbi)),
                  pl.BlockSpec((1, CONV_COLS), lambda cbi, i: (0, cbi))],
        out_specs=[o, o, o, pl.BlockSpec((8, CONV_COLS), lambda cbi, i: (0, cbi))],
        out_shape=[jax.ShapeDtypeStruct((S, nconv), BF16)] * 3 + [jax.ShapeDtypeStruct((8, nconv), F32)],
        compiler_params=_cparams(),
    )(proj, proj, proj, dy, proj, proj, proj, dy, cw, cb)


def _group_ones(n):
    r = lax.broadcasted_iota(jnp.int32, (n, n), 0) // HEAD_DIM
    c = lax.broadcasted_iota(jnp.int32, (n, n), 1) // HEAD_DIM
    return jnp.where(r == c, 1.0, 0.0).astype(BF16)


def _qk_norm(proj, gain_t, unit0, nsb, scale, name):
    S = proj.shape[0]
    nb = nsb // CONV_COLS
    ts = _rows(S, 1024)

    def body(x_ref, g_ref, o_ref):
        x = x_ref[...]
        ss = _sum_dot(x * x, _group_ones(CONV_COLS))
        r = lax.rsqrt(ss * (1.0 / HEAD_DIM) + EPS)
        o_ref[...] = ((x * r) * g_ref[...] * scale).astype(BF16)

    return pl.pallas_call(
        body, name=name, grid=(nb, S // ts),
        in_specs=[pl.BlockSpec((ts, CONV_COLS), lambda u, i: (i, unit0 + u)),
                  pl.BlockSpec((1, CONV_COLS), lambda u, i: (0, 0))],
        out_specs=pl.BlockSpec((ts, CONV_COLS), lambda u, i: (i, u)),
        out_shape=jax.ShapeDtypeStruct((S, nsb), BF16),
        compiler_params=_cparams(),
    )(proj, gain_t)


def _qk_norm_bwd(proj, dout, gain_t, unit0, nsb, scale, name):
    S = proj.shape[0]
    nb = nsb // CONV_COLS
    ts = _rows(S, 1024)

    def body(x_ref, d_ref, g_ref, dx_ref, dg_ref):
        @pl.when(pl.program_id(1) == 0)
        def _():
            dg_ref[...] = jnp.zeros_like(dg_ref)

        x = x_ref[...]
        g = g_ref[...]
        ones = _group_ones(CONV_COLS)
        ss = _sum_dot(x * x, ones)
        r = lax.rsqrt(ss * (1.0 / HEAD_DIM) + EPS)
        xn = x * r
        dh = d_ref[...] * scale
        dxn = dh * g
        m = _sum_dot(dxn * xn, ones) * (1.0 / HEAD_DIM)
        dx_ref[...] = (r * (dxn - xn * m)).astype(BF16)
        dg_ref[...] += jnp.broadcast_to(jnp.sum(dh * xn, axis=0, keepdims=True), dg_ref.shape)

    return pl.pallas_call(
        body, name=name, grid=(nb, S // ts),
        in_specs=[pl.BlockSpec((ts, CONV_COLS), lambda u, i: (i, unit0 + u)),
                  pl.BlockSpec((ts, CONV_COLS), lambda u, i: (i, u)),
                  pl.BlockSpec((1, CONV_COLS), lambda u, i: (0, 0))],
        out_specs=[pl.BlockSpec((ts, CONV_COLS), lambda u, i: (i, u)),
                   pl.BlockSpec((8, CONV_COLS), lambda u, i: (0, u))],
        out_shape=[jax.ShapeDtypeStruct((S, nsb), BF16), jax.ShapeDtypeStruct((8, nsb), F32)],
        compiler_params=_cparams(),
    )(proj, dout, gain_t)


Z_CLAMP = 80.0
N_SLOTS = 3
SAT_LIMIT = 120.0


def _head_masks():
    lane = lax.broadcasted_iota(jnp.int32, (1, LANES), 1)
    return [lane < HEAD_DIM, lane >= HEAD_DIM], lane


def _tile_consts(T):
    r_i = lax.broadcasted_iota(jnp.int32, (T, T), 0)
    c_i = lax.broadcasted_iota(jnp.int32, (T, T), 1)
    neg_suffix = jnp.where(r_i >= c_i, -1.0, 0.0).astype(BF16)
    prefix = jnp.where(r_i <= c_i, 1.0, 0.0).astype(BF16)
    return neg_suffix, prefix, c_i < r_i


def _pipeline(n, stages, first_special=False, last_special=False, saturated=None, extra_head=0):
    depth = len(stages)
    head = (depth if first_special else depth - 1) + extra_head
    off = 0 if last_special else 1
    for m in range(head):
        for k in reversed(range(min(m, depth - 1) + 1)):
            stages[k](m - k, (m - k) % N_SLOTS, first_special and m == k)

    def trip(m, u):
        for k in reversed(range(depth)):
            stages[k](m - k, (head + u - k) % N_SLOTS, False)

    def group(g, carry):
        for u in range(N_SLOTS):
            trip(head + g * N_SLOTS + u, u)
        return carry

    count = n - 1 + off - head
    full = count // N_SLOTS
    if saturated is None:
        lax.fori_loop(0, full, group, 0)
        go_on, done = True, n
    else:
        def more(state):
            return (state[0] < full) & (state[1] == 0)

        def step(state):
            group(state[0], 0)
            return state[0] + 1, saturated().astype(jnp.int32)

        groups, stop = lax.while_loop(more, step, (jnp.int32(0), saturated().astype(jnp.int32)))
        go_on = stop == 0
        done = jnp.where(go_on, n, head - depth + 1 + N_SLOTS * groups)
    for r in range(N_SLOTS):
        @pl.when((count - full * N_SLOTS == r) & go_on)
        def _(r=r):
            for u in range(r):
                trip(head + full * N_SLOTS + u, u)
            for e in range(depth - off):
                for k in reversed(range(e + off, depth)):
                    t = n - 1 - (k - e - off)
                    stages[k](t, (head + r + e - k) % N_SLOTS, last_special and k == e + off)
    return done


def _sweep(n, stages, finish, first_special=False, last_special=False, saturated=None, extra_head=0):
    depth = len(stages)
    least = (depth if first_special else depth - 1) + extra_head + (1 if last_special else 0)
    for short in range(1, least):
        @pl.when(n == short)
        def _(short=short):
            for m in range(short + depth - 1):
                for k in reversed(range(depth)):
                    t = m - k
                    if 0 <= t < short:
                        stages[k](t, t % N_SLOTS, (first_special and t == 0) or (last_special and t == short - 1))
            finish(short)

    @pl.when(n >= least)
    def _():
        finish(_pipeline(n, stages, first_special, last_special, saturated, extra_head))


def _attn_fwd(q, k, v, name):
    S, nsb = q.shape
    T = ATT_TILE
    hp = nsb // LANES
    nb = S // T
    assert nb <= HEAD_DIM

    def body(q_ref, k_ref, v_ref, y_ref, cs_ref, c_ref, acc, z_st, inc_st):
        i = pl.program_id(1)
        masks, lane = _head_masks()
        qv = q_ref[...]
        qm = [jnp.where(m, qv, jnp.zeros_like(qv)) for m in masks]
        neg_suffix, _, causal = _tile_consts(T)
        c_ref[...] = jnp.zeros_like(c_ref)
        acc[...] = jnp.zeros_like(acc)
        cs_ref[...] = jnp.zeros_like(cs_ref)

        def blk(ref, j):
            return ref[pl.ds(pl.multiple_of(j * T, T), T), :]

        def scores(t, slot, diag):
            kj = blk(k_ref, i - t)
            for h in range(2):
                z_st[slot, h] = jnp.minimum(_dot_nt(qm[h], kj), Z_CLAMP)

        def suffix_sums(t, slot, diag):
            for h in range(2):
                sp = jnp.log(1.0 + jnp.exp(z_st[slot, h]))
                if diag:
                    sp = jnp.where(causal, sp, 0.0)
                inc_st[slot, h] = _dot(sp.astype(BF16), neg_suffix)

        def weights(t, slot, diag):
            vj = blk(v_ref, i - t)
            for h in range(2):
                inc = inc_st[slot, h]
                c = c_ref[h]
                a = jnp.exp(z_st[slot, h] + inc + c)
                if diag:
                    a = jnp.where(causal, a, 0.0)
                upd = _dot(a.astype(BF16), vj)
                acc[...] += jnp.where(masks[h], upd, 0.0)
                cs_ref[...] = jnp.where(lane == i - t + HEAD_DIM * h, c, cs_ref[...])
                c_ref[h] = c + inc[:, 0:1]

        stages = [scores, suffix_sums, weights]

        def saturated():
            return jnp.max(c_ref[...]) < -SAT_LIMIT

        def note(used):
            cs_ref[...] = jnp.where(lane == LANES - 1, jnp.asarray(used).astype(F32), cs_ref[...])

        _sweep(i + 1, stages, note, first_special=True, saturated=saturated, extra_head=1)
        y_ref[...] = acc[...].astype(BF16)

    return pl.pallas_call(
        body, name=name, grid=(hp, nb),
        in_specs=[pl.BlockSpec((T, LANES), lambda p, i: (i, p)),
                  pl.BlockSpec((S, LANES), lambda p, i: (0, p)),
                  pl.BlockSpec((S, LANES), lambda p, i: (0, p))],
        out_specs=[pl.BlockSpec((T, LANES), lambda p, i: (i, p)),
                   pl.BlockSpec((None, T, LANES), lambda p, i: (p, i, 0))],
        out_shape=[jax.ShapeDtypeStruct((S, nsb), BF16), jax.ShapeDtypeStruct((hp, S, LANES), F32)],
        scratch_shapes=[pltpu.VMEM((2, T, 1), F32), pltpu.VMEM((T, LANES), F32),
                        pltpu.VMEM((N_SLOTS, 2, T, T), F32), pltpu.VMEM((N_SLOTS, 2, T, T), F32)],
        compiler_params=_cparams(),
    )(q, k, v)


def _attn_bwd(q, k, v, dy, col0, carry, name):
    S, nsb = q.shape
    T = ATT_TILE
    hp = nsb // LANES
    nb = S // T

    def body(q_ref, k_ref, v_ref, dy_ref, cs_ref, dq_ref, dk_ref, dv_ref, e_ref, acc,
             z_st, da_st, b_st, inc_st, a_st, e_st, p_st):
        i = pl.program_id(1)

        @pl.when(i == 0)
        def _():
            dk_ref[...] = jnp.zeros_like(dk_ref)
            dv_ref[...] = jnp.zeros_like(dv_ref)

        masks, lane = _head_masks()
        qv = q_ref[...]
        dyb = dy_ref[...].astype(BF16)
        qm = [jnp.where(m, qv, jnp.zeros_like(qv)) for m in masks]
        dym = [jnp.where(m, dyb, jnp.zeros_like(dyb)) for m in masks]
        neg_suffix, prefix, causal = _tile_consts(T)
        e_ref[...] = jnp.zeros_like(e_ref)
        acc[...] = jnp.zeros_like(acc)

        def blk(ref, j):
            return ref[pl.ds(pl.multiple_of(j * T, T), T), :]

        used = jnp.max(jnp.where(lane == LANES - 1, cs_ref[...], 0.0)).astype(jnp.int32)
        n = jnp.clip(used, 1, i + 1)
        first = i + 1 - n

        def scores(t, slot, diag):
            kj = blk(k_ref, first + t)
            vj = blk(v_ref, first + t)
            for h in range(2):
                z_st[slot, h] = jnp.minimum(_dot_nt(qm[h], kj), Z_CLAMP)
                da_st[slot, h] = _dot_nt(dym[h], vj)

        def suffix_sums(t, slot, diag):
            for h in range(2):
                u = jnp.exp(z_st[slot, h])
                w = 1.0 + u
                b_st[slot, h] = u / w
                sp = jnp.log(w)
                if diag:
                    sp = jnp.where(causal, sp, 0.0)
                inc_st[slot, h] = _dot(sp.astype(BF16), neg_suffix)

        def probs(t, slot, diag):
            csv = cs_ref[...]
            for h in range(2):
                c = jnp.sum(jnp.where(lane == first + t + HEAD_DIM * h, csv, 0.0), axis=-1, keepdims=True)
                a = jnp.exp(z_st[slot, h] + inc_st[slot, h] + c)
                if diag:
                    a = jnp.where(causal, a, 0.0)
                a_st[slot, h] = a.astype(BF16)
                e = a * da_st[slot, h]
                e_st[slot, h] = e
                p_st[slot, h] = _dot(e.astype(BF16), prefix)

        def grads(t, slot, diag):
            kj = blk(k_ref, first + t)
            off = pl.multiple_of((first + t) * T, T)
            for h in range(2):
                p = p_st[slot, h]
                dz = e_st[slot, h] - b_st[slot, h] * (e_ref[h] + p)
                if diag:
                    dz = jnp.where(causal, dz, 0.0)
                dzb = dz.astype(BF16)
                acc[...] += jnp.where(masks[h], _dot(dzb, kj), 0.0)
                dk_ref[pl.ds(off, T), :] += _dot_tn(dzb, qm[h])
                dv_ref[pl.ds(off, T), :] += _dot_tn(a_st[slot, h], dym[h])
                e_ref[h] += p[:, T - 1:T]

        stages = [scores, suffix_sums, probs, grads]

        _sweep(n, stages, lambda done: None, last_special=True)
        dq_ref[...] = acc[...]

    return pl.pallas_call(
        body, name=name, grid=(hp, nb),
        in_specs=[pl.BlockSpec((T, LANES), lambda p, i: (i, p)),
                  pl.BlockSpec((S, LANES), lambda p, i: (0, p)),
                  pl.BlockSpec((S, LANES), lambda p, i: (0, p)),
                  pl.BlockSpec((T, LANES), lambda p, i: (i, col0 + p)),
                  pl.BlockSpec((None, T, LANES), lambda p, i: (p, i, 0))],
        out_specs=[pl.BlockSpec((T, LANES), lambda p, i: (i, p)),
                   pl.BlockSpec((S, LANES), lambda p, i: (0, p)),
                   pl.BlockSpec((S, LANES), lambda p, i: (0, p))],
        out_shape=[jax.ShapeDtypeStruct((S, nsb), F32)] * 3,
        scratch_shapes=[pltpu.VMEM((2, T, 1), F32), pltpu.VMEM((T, LANES), F32)]
        + [pltpu.VMEM((N_SLOTS, 2, T, T), dt) for dt in (F32, F32, F32, F32, BF16, F32, F32)],
        compiler_params=_cparams(),
    )(q, k, v, dy, carry)


def _ple(x, p, tgt, gain, wpg, wpp, name):
    S, D = x.shape
    P = p.shape[1]
    ns, _, nc = wpp.shape
    tm = _rows(S, 256)

    def body(x_ref, p_ref, t_ref, g_ref, wpg_ref, wpp_ref, dx_ref, h_ref, du_ref, dpp_ref, loss_ref, dg_ref):
        @pl.when(pl.program_id(0) == 0)
        def _():
            loss_ref[...] = jnp.zeros_like(loss_ref)
            dg_ref[...] = jnp.zeros_like(dg_ref)

        x_ = x_ref[...]
        g = g_ref[...]
        h, xn, r = _rms(x_, g)
        hb = h.astype(BF16)
        h_ref[...] = hb
        gate = jax.nn.sigmoid(_dot(hb, wpg_ref[...]))
        pb = p_ref[...].astype(BF16)
        pp = jnp.concatenate([_dot(pb, wpp_ref[n]) for n in range(ns)], axis=1)
        err = (x_ + gate * pp) - t_ref[...]
        loss_ref[...] += (0.5 / D) * jnp.sum(err * err)
        dy = err * (1.0 / D)
        du = ((dy * pp) * (gate * (1.0 - gate))).astype(BF16)
        du_ref[...] = du
        dpp_ref[...] = (dy * gate).astype(BF16)
        dx, dg = _rms_bwd(_dot_nt(du, wpg_ref[...]), xn, r, g)
        dx_ref[...] = dy + dx
        dg_ref[...] += jnp.broadcast_to(dg, dg_ref.shape)

    row = pl.BlockSpec((tm, D), lambda i: (i, 0))
    return pl.pallas_call(
        body, name=name, grid=(S // tm,),
        in_specs=[row, pl.BlockSpec((tm, P), lambda i: (i, 0)), row,
                  pl.BlockSpec((1, D), lambda i: (0, 0)),
                  pl.BlockSpec((D, D), lambda i: (0, 0)),
                  pl.BlockSpec((ns, P, nc), lambda i: (0, 0, 0))],
        out_specs=[row, row, row, row,
                   pl.BlockSpec((8, LANES), lambda i: (0, 0)),
                   pl.BlockSpec((8, D), lambda i: (0, 0))],
        out_shape=[jax.ShapeDtypeStruct((S, D), F32)] + [jax.ShapeDtypeStruct((S, D), BF16)] * 3
        + [jax.ShapeDtypeStruct((8, LANES), F32), jax.ShapeDtypeStruct((8, D), F32)],
        compiler_params=_cparams(),
    )(x, p, tgt, gain, wpg, wpp)


def _elementwise(fn, ins, n_out, name):
    R, C = ins[0].shape
    tr = _rows(R, 512)

    def body(*refs):
        outs = fn(*[r[...] for r in refs[:len(ins)]])
        for o_ref, o in zip(refs[len(ins):], outs):
            o_ref[...] = o

    spec = pl.BlockSpec((tr, C), lambda i: (i, 0))
    return pl.pallas_call(
        body, name=name, grid=(R // tr,), in_specs=[spec] * len(ins), out_specs=[spec] * n_out,
        out_shape=[jax.ShapeDtypeStruct((R, C), F32)] * n_out, compiler_params=_cparams(),
    )(*ins)


def _adamw(w, g, m, v):
    m = ADAM_B1 * m + (1.0 - ADAM_B1) * g
    v = ADAM_B2 * v + (1.0 - ADAM_B2) * jnp.square(g)
    m_hat = m / (1.0 - ADAM_B1 ** ADAM_STEP)
    v_hat = v / (1.0 - ADAM_B2 ** ADAM_STEP)
    delta = -ADAM_LR * (m_hat / (jnp.sqrt(v_hat) + ADAM_EPS) + ADAM_WD * w)
    return delta, m, v


def _place():
    x, y, c = lax.axis_index("x"), lax.axis_index("y"), lax.axis_index("c")
    chips = [(1 - x, y), (x, 1 - y), (1 - x, 1 - y)]
    return x, y, c, chips


def _half(ref, c, axis_rows):
    n = ref.shape[-2]
    start = pl.multiple_of(c * (n // 2), 8)
    idx = (slice(None),) * (len(ref.shape) - 2) + (pl.ds(start, n // 2), slice(None))
    return ref.at[idx]


def _gather_weights(shards, small, name):
    n = len(shards)

    def body(*refs):
        ins, small_in = refs[:n], refs[n]
        outs, small_out = refs[n + 1:2 * n + 1], refs[2 * n + 1]
        lsem, lrsem, ssem, rsem, sm_s, sm_r = refs[2 * n + 2:]
        x, y, c, chips = _place()
        j = 2 * x + y
        sib = (x, y, 1 - c)

        local = [pltpu.make_async_remote_copy(
            src_ref=ins[a], dst_ref=outs[a].at[j], send_sem=lsem.at[a], recv_sem=lrsem.at[a],
            device_id=sib, device_id_type=MESH) for a in range(n)]
        for cp in local:
            cp.start()
        small_out[j] = small_in[...]
        small_cp = [pltpu.make_async_remote_copy(
            src_ref=small_in, dst_ref=small_out.at[j], send_sem=sm_s.at[k], recv_sem=sm_r.at[k],
            device_id=(*chip, c), device_id_type=MESH) for k, chip in enumerate(chips)]
        for cp in small_cp:
            cp.start()

        def ici(a, k, chip, jj, dev):
            return pltpu.make_async_remote_copy(
                src_ref=_half(ins[a], c, True) if dev is not None else _half(outs[a].at[jj], c, True),
                dst_ref=_half(outs[a].at[jj], c, True),
                send_sem=ssem.at[a, k], recv_sem=rsem.at[a, k],
                device_id=dev if dev is not None else (*chip, c), device_id_type=MESH)

        first = []
        for a in range(n):
            for k, chip in enumerate(chips):
                cp = ici(a, k, chip, j, (*chip, c))
                cp.start()
                first.append(cp)
        passed = []
        for a in range(n):
            for k, chip in enumerate(chips):
                jj = 2 * chip[0] + chip[1]
                ici(a, k, chip, jj, None).wait_recv()
                fw = pltpu.make_async_remote_copy(
                    src_ref=_half(outs[a].at[jj], c, True), dst_ref=_half(outs[a].at[jj], c, True),
                    send_sem=ssem.at[a, 3 + k], recv_sem=rsem.at[a, 3 + k], device_id=sib, device_id_type=MESH)
                fw.start()
                passed.append(fw)
        for a in range(n):
            for k, chip in enumerate(chips):
                jj = 2 * chip[0] + chip[1]
                pltpu.make_async_remote_copy(
                    src_ref=_half(outs[a].at[jj], 1 - c, True), dst_ref=_half(outs[a].at[jj], 1 - c, True),
                    send_sem=ssem.at[a, 3 + k], recv_sem=rsem.at[a, 3 + k], device_id=sib,
                    device_id_type=MESH).wait_recv()
        for cp in small_cp:
            cp.wait()
        for cp in first + passed:
            cp.wait_send()
        for cp in local:
            cp.wait()

    return pl.pallas_call(
        body, name=name,
        in_specs=[HBM] * n + [VMEM_WHOLE],
        out_specs=[HBM] * n + [VMEM_WHOLE],
        out_shape=[jax.ShapeDtypeStruct((N_SHARDS,) + s.shape, s.dtype) for s in shards]
        + [jax.ShapeDtypeStruct((N_SHARDS,) + small.shape, small.dtype)],
        scratch_shapes=[pltpu.SemaphoreType.DMA((n,)), pltpu.SemaphoreType.DMA((n,)),
                        pltpu.SemaphoreType.DMA((n, 6)), pltpu.SemaphoreType.DMA((n, 6)),
                        pltpu.SemaphoreType.DMA((3,)), pltpu.SemaphoreType.DMA((3,))],
    )(*shards, small)


def _sum_small(small, name):
    def body(small_in, small_out, buf, sm_s, sm_r):
        x, y, c, _ = _place()
        me = 4 * x + 2 * y + c
        buf[me] = small_in[...]
        peers = [(fx, fy, fc) for fx in (0, 1) for fy in (0, 1) for fc in (0, 1)][1:]
        sm = []
        for k, (fx, fy, fc) in enumerate(peers):
            dev = (1 - x if fx else x, 1 - y if fy else y, 1 - c if fc else c)
            cp = pltpu.make_async_remote_copy(
                src_ref=small_in, dst_ref=buf.at[me], send_sem=sm_s.at[k], recv_sem=sm_r.at[k],
                device_id=dev, device_id_type=MESH)
            cp.start()
            sm.append(cp)
        for cp in sm:
            cp.wait()
        tot = buf[0]
        for d in range(1, N_DEV):
            tot = tot + buf[d]
        small_out[...] = tot

    return pl.pallas_call(
        body, name=name, in_specs=[VMEM_WHOLE], out_specs=VMEM_WHOLE,
        out_shape=jax.ShapeDtypeStruct(small.shape, F32),
        scratch_shapes=[pltpu.VMEM((N_DEV,) + small.shape, F32),
                        pltpu.SemaphoreType.DMA((N_DEV - 1,)), pltpu.SemaphoreType.DMA((N_DEV - 1,))],
    )(small)


def _swap_ride(grads):
    n = len(grads)

    def make(ins, outs, sems):
        ssem, rsem = sems
        x, y, c, _ = _place()
        return [pltpu.make_async_remote_copy(
            src_ref=_half(ins[a], 1 - c, True), dst_ref=outs[a], send_sem=ssem.at[a], recv_sem=rsem.at[a],
            device_id=(x, y, 1 - c), device_id_type=MESH) for a in range(n)]

    return _Ride(grads, [jax.ShapeDtypeStruct((g.shape[0], g.shape[1] // 2, g.shape[2]), F32) for g in grads],
                 [pltpu.SemaphoreType.DMA((n,)), pltpu.SemaphoreType.DMA((n,))], make)


def _swap_grad_halves(grads, name):
    n = len(grads)

    def body(*refs):
        ins, outs = refs[:n], refs[n:2 * n]
        ssem, rsem = refs[2 * n:]
        x, y, c, _ = _place()
        cps = [pltpu.make_async_remote_copy(
            src_ref=_half(ins[a], 1 - c, True), dst_ref=outs[a], send_sem=ssem.at[a], recv_sem=rsem.at[a],
            device_id=(x, y, 1 - c), device_id_type=MESH) for a in range(n)]
        for cp in cps:
            cp.start()
        for cp in cps:
            cp.wait()

    return pl.pallas_call(
        body, name=name, in_specs=[HBM] * n, out_specs=[HBM] * n,
        out_shape=[jax.ShapeDtypeStruct((g.shape[0], g.shape[1] // 2, g.shape[2]), F32) for g in grads],
        scratch_shapes=[pltpu.SemaphoreType.DMA((n,)), pltpu.SemaphoreType.DMA((n,))],
    )(*grads)


def _chip_sum(g, recv, core, name):
    ns, R, C = g.shape
    r2 = R // 2
    tr = _rows(r2, 512)
    nrb = r2 // tr

    def body(core_ref, g_ref, r_ref, o_ref, ob_ref):
        s = g_ref[...] + r_ref[...]
        o_ref[...] = s
        ob_ref[...] = s.astype(BF16)

    out = pl.BlockSpec((None, tr, C), lambda s, i, cr: (s, i, 0))
    return pl.pallas_call(
        body, name=name,
        grid_spec=pltpu.PrefetchScalarGridSpec(
            num_scalar_prefetch=1, grid=(ns, nrb),
            in_specs=[pl.BlockSpec((None, tr, C), lambda s, i, cr: (s, cr[0] * nrb + i, 0)), out],
            out_specs=[out, out]),
        out_shape=[jax.ShapeDtypeStruct((ns, r2, C), F32), jax.ShapeDtypeStruct((ns, r2, C), BF16)],
        compiler_params=_cparams(),
    )(core, g, recv)


def _shard_sum(csum, got, place, name):
    _, r2, C = csum.shape
    tr = _rows(r2, 512)
    nrb = r2 // tr

    def body(place_ref, c_ref, g0_ref, g1_ref, g2_ref, o_ref):
        o_ref[...] = ((c_ref[...] + g0_ref[...].astype(F32)) + g1_ref[...].astype(F32)) + g2_ref[...].astype(F32)

    def got_spec(k):
        return pl.BlockSpec((None, tr, C), lambda i, pr: (k, i, 0))

    return pl.pallas_call(
        body, name=name,
        grid_spec=pltpu.PrefetchScalarGridSpec(
            num_scalar_prefetch=1, grid=(nrb,),
            in_specs=[pl.BlockSpec((None, tr, C), lambda i, pr: (pr[0], i, 0)), got_spec(0), got_spec(1), got_spec(2)],
            out_specs=pl.BlockSpec((tr, C), lambda i, pr: (pr[1] * nrb + i, 0))),
        out_shape=jax.ShapeDtypeStruct((2 * r2, C), F32),
        compiler_params=_cparams(),
    )(place, csum, got, got, got)


def _gather_ride(shards):
    n = len(shards)

    def make(ins, outs, sems):
        lsem, lrsem, ssem, rsem = sems
        x, y, c, chips = _place()
        j = 2 * x + y
        cps = [pltpu.make_async_remote_copy(
            src_ref=ins[a], dst_ref=outs[a].at[j], send_sem=lsem.at[a], recv_sem=lrsem.at[a],
            device_id=(x, y, 1 - c), device_id_type=MESH) for a in range(n)]
        for a in range(n):
            for k, chip in enumerate(chips):
                cps.append(pltpu.make_async_remote_copy(
                    src_ref=_half(ins[a], c, True), dst_ref=_half(outs[a].at[j], c, True),
                    send_sem=ssem.at[a, k], recv_sem=rsem.at[a, k], device_id=(*chip, c), device_id_type=MESH))
        return cps

    return _Ride(shards, [jax.ShapeDtypeStruct((N_SHARDS,) + s.shape, s.dtype) for s in shards],
                 [pltpu.SemaphoreType.DMA((n,)), pltpu.SemaphoreType.DMA((n,)),
                  pltpu.SemaphoreType.DMA((n, 3)), pltpu.SemaphoreType.DMA((n, 3))], make)


def _forward_halves(gathered, name):
    n = len(gathered)

    def body(*refs):
        outs = refs[n:2 * n]
        ssem, rsem = refs[2 * n:]
        x, y, c, chips = _place()
        cps = []
        for a in range(n):
            for k, chip in enumerate(chips):
                part = _half(outs[a].at[2 * chip[0] + chip[1]], c, True)
                cps.append(pltpu.make_async_remote_copy(
                    src_ref=part, dst_ref=part, send_sem=ssem.at[a, k], recv_sem=rsem.at[a, k],
                    device_id=(x, y, 1 - c), device_id_type=MESH))
        for cp in cps:
            cp.start()
        for cp in cps:
            cp.wait()

    return pl.pallas_call(
        body, name=name, in_specs=[HBM] * n, out_specs=[HBM] * n,
        out_shape=[jax.ShapeDtypeStruct(g.shape, g.dtype) for g in gathered],
        input_output_aliases={a: a for a in range(n)},
        scratch_shapes=[pltpu.SemaphoreType.DMA((n, 3)), pltpu.SemaphoreType.DMA((n, 3))],
    )(*gathered)


def _scatter_ride(csums):
    n = len(csums)

    def make(ins, outs, sems):
        ssem, rsem = sems
        x, y, c, chips = _place()
        return [pltpu.make_async_remote_copy(
            src_ref=ins[a].at[2 * chip[0] + chip[1]], dst_ref=outs[a].at[k], send_sem=ssem.at[a, k],
            recv_sem=rsem.at[a, k], device_id=(*chip, c), device_id_type=MESH)
            for a in range(n) for k, chip in enumerate(chips)]

    return _Ride(csums, [jax.ShapeDtypeStruct((3,) + g.shape[1:], g.dtype) for g in csums],
                 [pltpu.SemaphoreType.DMA((n, 3)), pltpu.SemaphoreType.DMA((n, 3))], make)


def _join_halves(fulls, name):
    n = len(fulls)

    def body(*refs):
        outs = refs[n:2 * n]
        ssem, rsem = refs[2 * n:]
        x, y, c, _ = _place()
        cps = [pltpu.make_async_remote_copy(
            src_ref=_half(outs[a], c, True), dst_ref=_half(outs[a], c, True), send_sem=ssem.at[a],
            recv_sem=rsem.at[a], device_id=(x, y, 1 - c), device_id_type=MESH) for a in range(n)]
        for cp in cps:
            cp.start()
        for cp in cps:
            cp.wait()

    return pl.pallas_call(
        body, name=name, in_specs=[HBM] * n, out_specs=[HBM] * n,
        out_shape=[jax.ShapeDtypeStruct(f.shape, F32) for f in fulls],
        input_output_aliases={a: a for a in range(n)},
        scratch_shapes=[pltpu.SemaphoreType.DMA((n,))] * 2,
    )(*fulls)


def _pad_rows(a, rows, cols):
    return jnp.pad(a, ((0, rows - a.shape[0]), (0, cols - a.shape[1])))


def kernel(x, p, ffn1_norm, ffn1_w_gate, ffn1_w_up, ffn1_w_down, mix_norm, w_in, conv_w, conv_b, q_norm, k_norm, w_out, ffn2_norm, ffn2_w_gate, ffn2_w_up, ffn2_w_down, ple_norm, ple_w_gate, ple_w_proj, loss_target, m_ffn1_norm, m_ffn1_w_gate, m_ffn1_w_up, m_ffn1_w_down, m_mix_norm, m_w_in, m_conv_w, m_conv_b, m_q_norm, m_k_norm, m_w_out, m_ffn2_norm, m_ffn2_w_gate, m_ffn2_w_up, m_ffn2_w_down, m_ple_norm, m_ple_w_gate, m_ple_w_proj, v_ffn1_norm, v_ffn1_w_gate, v_ffn1_w_up, v_ffn1_w_down, v_mix_norm, v_w_in, v_conv_w, v_conv_b, v_q_norm, v_k_norm, v_w_out, v_ffn2_norm, v_ffn2_w_gate, v_ffn2_w_up, v_ffn2_w_down, v_ple_norm, v_ple_w_gate, v_ple_w_proj):
    big = dict(ffn1_w_gate=ffn1_w_gate, ffn1_w_up=ffn1_w_up, ffn1_w_down=ffn1_w_down, w_in=w_in, w_out=w_out,
               ffn2_w_gate=ffn2_w_gate, ffn2_w_up=ffn2_w_up, ffn2_w_down=ffn2_w_down,
               ple_w_gate=ple_w_gate, ple_w_proj=ple_w_proj)
    big_m = dict(ffn1_w_gate=m_ffn1_w_gate, ffn1_w_up=m_ffn1_w_up, ffn1_w_down=m_ffn1_w_down, w_in=m_w_in,
                 w_out=m_w_out, ffn2_w_gate=m_ffn2_w_gate, ffn2_w_up=m_ffn2_w_up, ffn2_w_down=m_ffn2_w_down,
                 ple_w_gate=m_ple_w_gate, ple_w_proj=m_ple_w_proj)
    big_v = dict(ffn1_w_gate=v_ffn1_w_gate, ffn1_w_up=v_ffn1_w_up, ffn1_w_down=v_ffn1_w_down, w_in=v_w_in,
                 w_out=v_w_out, ffn2_w_gate=v_ffn2_w_gate, ffn2_w_up=v_ffn2_w_up, ffn2_w_down=v_ffn2_w_down,
                 ple_w_gate=v_ple_w_gate, ple_w_proj=v_ple_w_proj)
    names = list(big)
    xs = x[0]
    ps = p[0, 0]
    tgt = loss_target[0]
    S, D = xs.shape
    nconv = conv_b.shape[1]
    nsb = D - nconv
    cwl = conv_w.shape[2]
    jchip = 2 * lax.axis_index("x") + lax.axis_index("y")
    core = lax.axis_index("c")

    early, late = names[:3], names[3:]
    assert all(k.startswith("ffn1") for k in early)
    shards = {k: big[k][0].astype(BF16) for k in names}
    gathered = _gather_weights([shards[k] for k in early], _pad_rows(conv_w[0], 8, LANES), "gather_weights")
    W = dict(zip(early, gathered[:-1]))
    cw_full = jnp.transpose(gathered[-1][:, :, :cwl], (1, 0, 2)).reshape(8, N_SHARDS * cwl)
    qg = jnp.tile(q_norm, (1, CONV_COLS // HEAD_DIM))
    kg = jnp.tile(k_norm, (1, CONV_COLS // HEAD_DIM))
    n_units = nconv // CONV_COLS

    x1, h1, a1, b1, *landed = _ffn_fwd(xs, ffn1_norm, W["ffn1_w_gate"], W["ffn1_w_up"], W["ffn1_w_down"], "ffn1_fwd",
                                        ride=_gather_ride([shards[k] for k in late]))
    W.update(zip(late, _forward_halves(landed, "gather_forward")))
    wout_full = W["w_out"].reshape(-1, D)
    wpg_full = W["ple_w_gate"].reshape(-1, D)
    proj, h2 = _norm_proj(x1, mix_norm, W["w_in"], "mix_in_proj")
    y_conv = _conv_fwd(proj, cw_full, conv_b, nconv, "conv_fwd")
    qs = _qk_norm(proj, qg, 3 * n_units, nsb, HEAD_DIM ** -0.5, "q_norm_fwd")
    kh = _qk_norm(proj, kg, 4 * n_units, nsb, 1.0, "k_norm_fwd")
    vb = proj[:, 3 * nconv + 2 * nsb:].astype(BF16)
    y_sb, carry = _attn_fwd(qs, kh, vb, "attn_fwd")
    ycat = jnp.concatenate([y_conv, y_sb], axis=1)
    x2 = _out_proj(ycat, wout_full, x1, "mix_out_proj")
    x3, h3, a3, b3 = _ffn_fwd(x2, ffn2_norm, W["ffn2_w_gate"], W["ffn2_w_up"], W["ffn2_w_down"], "ffn2_fwd")

    dx3, h4, du4, dpp, loss_blk, dg_ple = _ple(x3, ps, tgt, ple_norm, wpg_full, W["ple_w_proj"], "ple_loss")
    G = {}
    tk = _rows(S, 2048)
    nk = S // tk
    kd = wpg_full.shape[0] // N_SHARDS
    G["ple_w_gate"] = _tn_matmul(
        h4, du4, pl.BlockSpec((tk, kd), lambda m, k: (k, m)), pl.BlockSpec((tk, D), lambda m, k: (k, 0)),
        (N_SHARDS, kd, D), pl.BlockSpec((None, kd, D), lambda m, k: (m, 0, 0)), (N_SHARDS, nk), "ple_w_gate_grad")
    P = ps.shape[1]
    npp = D // N_SHARDS
    G["ple_w_proj"] = _tn_matmul(
        ps, dpp, pl.BlockSpec((tk, P), lambda m, k: (k, 0)), pl.BlockSpec((tk, npp), lambda m, k: (k, m)),
        (N_SHARDS, P, npp), pl.BlockSpec((None, P, npp), lambda m, k: (m, 0, 0)), (N_SHARDS, nk), "ple_w_proj_grad")

    def ffn_grads(pre, h, s, da, db, dy):
        fs = s.shape[2]
        tk, nk = _rows(S, 4096), S // _rows(S, 4096)
        hs = pl.BlockSpec((tk, D), lambda m, k: (k, 0))
        ss = pl.BlockSpec((None, tk, fs), lambda m, k: (m, k, 0))
        for leaf, lhs, rhs in (("_w_gate", da, h), ("_w_up", db, h), ("_w_down", s, dy)):
            G[pre + leaf] = _tn_matmul(lhs, rhs, ss, hs, (N_SHARDS, fs, D),
                                       pl.BlockSpec((None, fs, D), lambda m, k: (m, 0, 0)), (N_SHARDS, nk), pre + leaf + "_grad")

    s3, da3, db3, dy3 = _ffn_bwd_act(dx3, a3, b3, W["ffn2_w_down"], "ffn2_bwd_act")
    dx2, dg_ffn2 = _ffn_bwd_in(da3, db3, W["ffn2_w_gate"], W["ffn2_w_up"], dx3, x2, ffn2_norm, "ffn2_bwd_in")
    ffn_grads("ffn2", h3, s3, da3, db3, dy3)

    dycat = _out_proj_bwd(dx2, wout_full, "mix_out_proj_bwd")
    ko = wout_full.shape[0] // N_SHARDS
    G["w_out"] = _tn_matmul(
        ycat, dx2, pl.BlockSpec((tk, ko), lambda m, k: (k, m)), pl.BlockSpec((tk, D), lambda m, k: (k, 0)),
        (N_SHARDS, ko, D), pl.BlockSpec((None, ko, D), lambda m, k: (m, 0, 0)), (N_SHARDS, nk), "w_out_grad")
    db_, dc_, du_, dwb = _conv_bwd(proj, dycat, cw_full, conv_b, nconv, "conv_bwd")
    dqs, dkh, dv = _attn_bwd(qs, kh, vb, dycat, nconv // LANES, carry, "attn_bwd")
    dq, dg_q = _qk_norm_bwd(proj, dqs, qg, 3 * n_units, nsb, HEAD_DIM ** -0.5, "q_norm_bwd")
    dk, dg_k = _qk_norm_bwd(proj, dkh, kg, 4 * n_units, nsb, 1.0, "k_norm_bwd")
    dproj = jnp.concatenate([db_, dc_, du_, dq, dk, dv.astype(BF16)], axis=1)
    nin = W["w_in"].shape[2]
    G["w_in"] = _tn_matmul(
        h2, dproj, pl.BlockSpec((tk, D), lambda m, k: (k, 0)), pl.BlockSpec((tk, nin), lambda m, k: (k, m)),
        (N_SHARDS, D, nin), pl.BlockSpec((None, D, nin), lambda m, k: (m, 0, 0)), (N_SHARDS, nk), "w_in_grad")
    dx1, dg_mix, *recv_late = _norm_proj_bwd(dproj, W["w_in"], dx2, x1, mix_norm, "mix_in_proj_bwd",
                                             ride=_swap_ride([G[k] for k in late]))
    core_arr = jnp.reshape(core, (1,)).astype(jnp.int32)
    place = jnp.stack([jchip, core]).astype(jnp.int32)

    def chip_sums(group, recv):
        return zip(*[_chip_sum(G[k], r, core_arr, f"chip_sum_{k}") for k, r in zip(group, recv)])

    cs_late, csb_late = chip_sums(late, recv_late)
    s1, da1, db1, dy1, *got_late = _ffn_bwd_act(dx1, a1, b1, W["ffn1_w_down"], "ffn1_bwd_act",
                                                  ride=_scatter_ride(list(csb_late)))
    ffn_grads("ffn1", h1, s1, da1, db1, dy1)
    cs_early, csb_early = chip_sums(early, _swap_grad_halves([G[k] for k in early], "grad_swap_halves"))
    dx0, dg_ffn1, *got_early = _ffn_bwd_in(da1, db1, W["ffn1_w_gate"], W["ffn1_w_up"], dx1, xs, ffn1_norm,
                                           "ffn1_bwd_in", ride=_scatter_ride(list(csb_early)))

    assert D >= nconv and D % LANES == 0
    fold = lambda t: t[0].reshape(-1, HEAD_DIM).sum(axis=0)[None, :]
    small_rows = [dg_ffn1[0:1], dg_mix[0:1], dg_ffn2[0:1], dg_ple[0:1],
                  _pad_rows(dwb[3:4], 1, D), _pad_rows(dwb[0:3], 3, D),
                  _pad_rows(fold(dg_q), 1, D), _pad_rows(fold(dg_k), 1, D), _pad_rows(loss_blk[0:1, 0:1], 1, D)]
    small = _pad_rows(jnp.concatenate(small_rows, axis=0), SMALL_ROWS, D)

    small_sum = _sum_small(small, "small_sum")

    full = _join_halves([_shard_sum(cs, gt, place, f"shard_sum_{k}")
                         for k, cs, gt in zip(names, list(cs_early) + list(cs_late), got_early + got_late)],
                        "grad_join_halves")

    out_g, out_d, out_m, out_v = {}, {}, {}, {}
    for a, k in enumerate(names):
        shp = big[k].shape
        g2 = full[a]
        flipped = k.endswith(("_w_gate", "_w_up")) and k.startswith("ffn")
        view = (lambda t: jnp.transpose(t[0])) if flipped else (lambda t: t[0].reshape(g2.shape))
        back = (lambda t: jnp.transpose(t).reshape(shp)) if flipped else (lambda t: t.reshape(shp))
        d_, m_, v_ = _elementwise(_adamw, [view(big[k]), g2, view(big_m[k]), view(big_v[k])], 3, f"adamw_{k}")
        out_g[k], out_d[k], out_m[k], out_v[k] = (back(t) for t in (g2, d_, m_, v_))

    sm_names = ["ffn1_norm", "mix_norm", "ffn2_norm", "ple_norm", "conv_b", "conv_w", "q_norm", "k_norm"]
    sm_w = dict(ffn1_norm=ffn1_norm, mix_norm=mix_norm, ffn2_norm=ffn2_norm, ple_norm=ple_norm, conv_b=conv_b,
                conv_w=conv_w[0], q_norm=q_norm, k_norm=k_norm)
    sm_m = dict(ffn1_norm=m_ffn1_norm, mix_norm=m_mix_norm, ffn2_norm=m_ffn2_norm, ple_norm=m_ple_norm,
                conv_b=m_conv_b, conv_w=m_conv_w[0], q_norm=m_q_norm, k_norm=m_k_norm)
    sm_v = dict(ffn1_norm=v_ffn1_norm, mix_norm=v_mix_norm, ffn2_norm=v_ffn2_norm, ple_norm=v_ple_norm,
                conv_b=v_conv_b, conv_w=v_conv_w[0], q_norm=v_q_norm, k_norm=v_k_norm)
    sm_g = dict(ffn1_norm=small_sum[0:1], mix_norm=small_sum[1:2], ffn2_norm=small_sum[2:3], ple_norm=small_sum[3:4],
                conv_b=small_sum[4:5, :nconv],
                conv_w=lax.dynamic_slice_in_dim(small_sum[5:8, :nconv], jchip * cwl, cwl, axis=1),
                q_norm=small_sum[8:9, :HEAD_DIM], k_norm=small_sum[9:10, :HEAD_DIM])
    loss = small_sum[10, 0]
    pack = lambda d: _pad_rows(jnp.concatenate([_pad_rows(d[k], d[k].shape[0], D) for k in sm_names], axis=0), SMALL_ROWS, D)
    sd, smm, svv = _elementwise(_adamw, [pack(sm_w), pack(sm_g), pack(sm_m), pack(sm_v)], 3, "adamw_small")
    row = 0
    for k in sm_names:
        r_, c_ = sm_w[k].shape
        shp = (1, r_, c_) if k == "conv_w" else (r_, c_)
        out_g[k] = sm_g[k].reshape(shp)
        out_d[k], out_m[k], out_v[k] = (t[row:row + r_, :c_].reshape(shp) for t in (sd, smm, svv))
        row += r_

    order = ["ffn1_norm", "ffn1_w_gate", "ffn1_w_up", "ffn1_w_down", "mix_norm", "w_in", "conv_w", "conv_b",
             "q_norm", "k_norm", "w_out", "ffn2_norm", "ffn2_w_gate", "ffn2_w_up", "ffn2_w_down", "ple_norm",
             "ple_w_gate", "ple_w_proj"]
    return (loss, dx0[None], *[out_g[k] for k in order], *[out_d[k] for k in order],
            *[out_m[k] for k in order], *[out_v[k] for k in order])
```

```python
import jax
import jax.numpy as jnp
from jax import lax
from jax.experimental import pallas as pl
from jax.experimental.pallas import tpu as pltpu

F32 = jnp.float32
BF16 = jnp.bfloat16
MESH = pl.DeviceIdType.MESH

EPS = 1e-6
HEAD_DIM = 64
LANES = 128
FFN_RES = 0.5
ADAM_LR = 0.001
ADAM_B1 = 0.9
ADAM_B2 = 0.999
ADAM_EPS = 1e-08
ADAM_WD = 0.01
ADAM_STEP = 10
N_SHARDS = 4
N_DEV = 8
ATT_TILE = 256
VMEM_LIMIT = 52 * 1024 * 1024
SMALL_ROWS = 16
HBM = pl.BlockSpec(memory_space=pltpu.HBM)
VMEM_WHOLE = pl.BlockSpec(memory_space=pltpu.VMEM)


def _cparams(**kw):
    return pltpu.CompilerParams(vmem_limit_bytes=VMEM_LIMIT, **kw)


def _dot(a, b):
    return jnp.dot(a, b, preferred_element_type=F32)


def _dot_nt(a, b):
    return lax.dot_general(a, b, (((1,), (1,)), ((), ())), preferred_element_type=F32)


def _dot_tn(a, b):
    return lax.dot_general(a, b, (((0,), (0,)), ((), ())), preferred_element_type=F32)


def _sum_dot(x, m):
    return _dot(x.astype(BF16), m)


def _rms(x, g):
    r = lax.rsqrt(jnp.mean(x * x, axis=-1, keepdims=True) + EPS)
    xn = x * r
    return xn * g, xn, r


def _rms_bwd(dh, xn, r, g):
    dxn = dh * g
    dx = r * (dxn - xn * jnp.mean(dxn * xn, axis=-1, keepdims=True))
    return dx, jnp.sum(dh * xn, axis=0, keepdims=True)


def _rows(n, cap=512):
    for t in (4096, 2048, 1024, 512, 448, 384, 352, 256, 192, 176, 128, 96, 88, 64, 48, 32, 16, 8):
        if t <= cap and n % t == 0:
            return t
    raise ValueError(f"no row tile for {n}")


class _Ride:
    def __init__(self, ins, out_shapes, sems, make):
        self.ins, self.out_shapes, self.sems, self.make = list(ins), list(out_shapes), list(sems), make

    def split(self, refs, n_in, n_out, n_scratch):
        ni, no = len(self.ins), len(self.out_shapes)
        ins, rin = refs[:n_in], refs[n_in:n_in + ni]
        outs = refs[n_in + ni:n_in + ni + n_out]
        rout = refs[n_in + ni + n_out:n_in + ni + n_out + no]
        rest = refs[n_in + ni + n_out + no:]
        return ins, outs, rest[:n_scratch], lambda: self.make(rin, rout, rest[n_scratch:])


_NO_RIDE = _Ride([], [], [], lambda i, o, s: [])


def _ride_along(copies, first, last):
    @pl.when(first)
    def _():
        for cp in copies():
            cp.start()

    @pl.when(last)
    def _():
        for cp in copies():
            cp.wait()


def _ffn_fwd(x, gain, wg, wu, wd, name, ride=_NO_RIDE):
    S, D = x.shape
    ns, _, fs = wg.shape
    tm = _rows(S, 512)
    ni = S // tm

    def body(*refs):
        (x_ref, g_ref, wg_ref, wu_ref, wd_ref), (xo_ref, h_ref, a_ref, b_ref), (hs, acc), copies = ride.split(refs, 5, 4, 2)
        j = pl.program_id(1)
        _ride_along(copies, (pl.program_id(0) == 0) & (j == 0), (pl.program_id(0) == ni - 1) & (j == ns - 1))

        @pl.when(j == 0)
        def _():
            h, _, _ = _rms(x_ref[...], g_ref[...])
            hb = h.astype(BF16)
            hs[...] = hb
            h_ref[...] = hb
            acc[...] = jnp.zeros_like(acc)

        hb = hs[...]
        a = _dot(hb, wg_ref[...])
        b = _dot(hb, wu_ref[...])
        a_ref[...] = a.astype(BF16)
        b_ref[...] = b.astype(BF16)
        s = (a * jax.nn.sigmoid(a)) * b
        acc[...] += _dot(s.astype(BF16), wd_ref[...])

        @pl.when(j == ns - 1)
        def _():
            xo_ref[...] = x_ref[...] + FFN_RES * acc[...]

    return pl.pallas_call(
        body, name=name, grid=(ni, ns),
        in_specs=[
            pl.BlockSpec((tm, D), lambda i, j: (i, 0)),
            pl.BlockSpec((1, D), lambda i, j: (0, 0)),
            pl.BlockSpec((None, D, fs), lambda i, j: (j, 0, 0)),
            pl.BlockSpec((None, D, fs), lambda i, j: (j, 0, 0)),
            pl.BlockSpec((None, fs, D), lambda i, j: (j, 0, 0)),
        ] + [HBM] * len(ride.ins),
        out_specs=[
            pl.BlockSpec((tm, D), lambda i, j: (i, 0)),
            pl.BlockSpec((tm, D), lambda i, j: (i, 0)),
            pl.BlockSpec((None, tm, fs), lambda i, j: (j, i, 0)),
            pl.BlockSpec((None, tm, fs), lambda i, j: (j, i, 0)),
        ] + [HBM] * len(ride.out_shapes),
        out_shape=[
            jax.ShapeDtypeStruct((S, D), F32),
            jax.ShapeDtypeStruct((S, D), BF16),
            jax.ShapeDtypeStruct((ns, S, fs), BF16),
            jax.ShapeDtypeStruct((ns, S, fs), BF16),
        ] + ride.out_shapes,
        scratch_shapes=[pltpu.VMEM((tm, D), BF16), pltpu.VMEM((tm, D), F32)] + ride.sems,
        compiler_params=_cparams(),
    )(x, gain, wg, wu, wd, *ride.ins)


def _ffn_bwd_act(dxo, a, b, wd, name, ride=_NO_RIDE):
    S, D = dxo.shape
    ns, fs, _ = wd.shape
    tm = _rows(S, 512)
    ni = S // tm

    def body(*refs):
        (dxo_ref, a_ref, b_ref, wd_ref), (s_ref, da_ref, db_ref, dy_ref), (dys,), copies = ride.split(refs, 4, 4, 1)
        j = pl.program_id(1)
        _ride_along(copies, (pl.program_id(0) == 0) & (j == 0), (pl.program_id(0) == ni - 1) & (j == ns - 1))

        @pl.when(j == 0)
        def _():
            dy = (FFN_RES * dxo_ref[...]).astype(BF16)
            dys[...] = dy
            dy_ref[...] = dy

        av = a_ref[...].astype(F32)
        bv = b_ref[...].astype(F32)
        ds = _dot_nt(dys[...], wd_ref[...])
        sig = jax.nn.sigmoid(av)
        sl = av * sig
        s_ref[...] = (sl * bv).astype(BF16)
        da_ref[...] = (ds * bv * (sig * (1.0 + av * (1.0 - sig)))).astype(BF16)
        db_ref[...] = (ds * sl).astype(BF16)

    act = pl.BlockSpec((None, tm, fs), lambda i, j: (j, i, 0))
    row = pl.BlockSpec((tm, D), lambda i, j: (i, 0))
    return pl.pallas_call(
        body, name=name, grid=(ni, ns),
        in_specs=[row, act, act, pl.BlockSpec((None, fs, D), lambda i, j: (j, 0, 0))] + [HBM] * len(ride.ins),
        out_specs=[act, act, act, row] + [HBM] * len(ride.out_shapes),
        out_shape=[jax.ShapeDtypeStruct((ns, S, fs), BF16)] * 3 + [jax.ShapeDtypeStruct((S, D), BF16)]
        + ride.out_shapes,
        scratch_shapes=[pltpu.VMEM((tm, D), BF16)] + ride.sems,
        compiler_params=_cparams(),
    )(dxo, a, b, wd, *ride.ins)


def _ffn_bwd_in(da, db, wg, wu, dres, x, gain, name, ride=_NO_RIDE):
    S, D = x.shape
    ns, _, fs = wg.shape
    tm = _rows(S, 512)
    ni = S // tm

    def body(*refs):
        (da_ref, db_ref, wg_ref, wu_ref, dres_ref, x_ref, g_ref), (dx_ref, dg_ref), (acc,), copies = ride.split(refs, 7, 2, 1)
        i = pl.program_id(0)
        j = pl.program_id(1)
        _ride_along(copies, (i == 0) & (j == 0), (i == ni - 1) & (j == ns - 1))

        @pl.when((i == 0) & (j == 0))
        def _():
            dg_ref[...] = jnp.zeros_like(dg_ref)

        @pl.when(j == 0)
        def _():
            acc[...] = jnp.zeros_like(acc)

        acc[...] += _dot_nt(da_ref[...], wg_ref[...]) + _dot_nt(db_ref[...], wu_ref[...])

        @pl.when(j == ns - 1)
        def _():
            g = g_ref[...]
            _, xn, r = _rms(x_ref[...], g)
            dx, dg = _rms_bwd(acc[...], xn, r, g)
            dx_ref[...] = dres_ref[...] + dx
            dg_ref[...] += jnp.broadcast_to(dg, dg_ref.shape)

    act = pl.BlockSpec((None, tm, fs), lambda i, j: (j, i, 0))
    row = pl.BlockSpec((tm, D), lambda i, j: (i, 0))
    wsp = pl.BlockSpec((None, D, fs), lambda i, j: (j, 0, 0))
    return pl.pallas_call(
        body, name=name, grid=(ni, ns),
        in_specs=[act, act, wsp, wsp, row, row, pl.BlockSpec((1, D), lambda i, j: (0, 0))] + [HBM] * len(ride.ins),
        out_specs=[row, pl.BlockSpec((8, D), lambda i, j: (0, 0))] + [HBM] * len(ride.out_shapes),
        out_shape=[jax.ShapeDtypeStruct((S, D), F32), jax.ShapeDtypeStruct((8, D), F32)] + ride.out_shapes,
        scratch_shapes=[pltpu.VMEM((tm, D), F32)] + ride.sems,
        compiler_params=_cparams(),
    )(da, db, wg, wu, dres, x, gain, *ride.ins)


def _tn_matmul(a, b, a_spec, b_spec, o_shape, o_spec, grid, name):
    kaxis = len(grid) - 1

    def body(a_ref, b_ref, o_ref):
        @pl.when(pl.program_id(kaxis) == 0)
        def _():
            o_ref[...] = jnp.zeros_like(o_ref)

        o_ref[...] += _dot_tn(a_ref[...].astype(BF16), b_ref[...].astype(BF16))

    return pl.pallas_call(
        body, name=name, grid=grid, in_specs=[a_spec, b_spec], out_specs=o_spec,
        out_shape=jax.ShapeDtypeStruct(o_shape, F32), compiler_params=_cparams(),
    )(a, b)


def _norm_proj(x, gain, w, name):
    S, D = x.shape
    ns, _, n = w.shape
    tm = _rows(S, 1024)

    def body(x_ref, g_ref, w_ref, o_ref, h_ref, hs):
        @pl.when(pl.program_id(1) == 0)
        def _():
            h, _, _ = _rms(x_ref[...], g_ref[...])
            hb = h.astype(BF16)
            hs[...] = hb
            h_ref[...] = hb

        o_ref[...] = _dot(hs[...], w_ref[...])

    return pl.pallas_call(
        body, name=name, grid=(S // tm, ns),
        in_specs=[
            pl.BlockSpec((tm, D), lambda i, j: (i, 0)),
            pl.BlockSpec((1, D), lambda i, j: (0, 0)),
            pl.BlockSpec((None, D, n), lambda i, j: (j, 0, 0)),
        ],
        out_specs=[
            pl.BlockSpec((tm, n), lambda i, j: (i, j)),
            pl.BlockSpec((tm, D), lambda i, j: (i, 0)),
        ],
        out_shape=[jax.ShapeDtypeStruct((S, ns * n), F32), jax.ShapeDtypeStruct((S, D), BF16)],
        scratch_shapes=[pltpu.VMEM((tm, D), BF16)],
        compiler_params=_cparams(),
    )(x, gain, w)


def _norm_proj_bwd(dproj, w, dres, x, gain, name, ride=_NO_RIDE):
    S, D = x.shape
    ns, _, n = w.shape
    tm = _rows(S, 1024)
    ni = S // tm

    def body(*refs):
        (dp_ref, w_ref, dres_ref, x_ref, g_ref), (dx_ref, dg_ref), (acc,), copies = ride.split(refs, 5, 2, 1)
        i = pl.program_id(0)
        j = pl.program_id(1)
        _ride_along(copies, (i == 0) & (j == 0), (i == ni - 1) & (j == ns - 1))

        @pl.when((i == 0) & (j == 0))
        def _():
            dg_ref[...] = jnp.zeros_like(dg_ref)

        @pl.when(j == 0)
        def _():
            acc[...] = jnp.zeros_like(acc)

        acc[...] += _dot_nt(dp_ref[...], w_ref[...])

        @pl.when(j == ns - 1)
        def _():
            g = g_ref[...]
            _, xn, r = _rms(x_ref[...], g)
            dx, dg = _rms_bwd(acc[...], xn, r, g)
            dx_ref[...] = dres_ref[...] + dx
            dg_ref[...] += jnp.broadcast_to(dg, dg_ref.shape)

    return pl.pallas_call(
        body, name=name, grid=(ni, ns),
        in_specs=[
            pl.BlockSpec((tm, n), lambda i, j: (i, j)),
            pl.BlockSpec((None, D, n), lambda i, j: (j, 0, 0)),
            pl.BlockSpec((tm, D), lambda i, j: (i, 0)),
            pl.BlockSpec((tm, D), lambda i, j: (i, 0)),
            pl.BlockSpec((1, D), lambda i, j: (0, 0)),
        ] + [HBM] * len(ride.ins),
        out_specs=[
            pl.BlockSpec((tm, D), lambda i, j: (i, 0)),
            pl.BlockSpec((8, D), lambda i, j: (0, 0)),
        ] + [HBM] * len(ride.out_shapes),
        out_shape=[jax.ShapeDtypeStruct((S, D), F32), jax.ShapeDtypeStruct((8, D), F32)] + ride.out_shapes,
        scratch_shapes=[pltpu.VMEM((tm, D), F32)] + ride.sems,
        compiler_params=_cparams(),
    )(dproj, w, dres, x, gain, *ride.ins)


def _out_proj(ycat, w, res, name):
    S, K = ycat.shape
    D = w.shape[1]
    tm = _rows(S, 1024)

    def body(y_ref, w_ref, r_ref, o_ref):
        o_ref[...] = r_ref[...] + _dot(y_ref[...], w_ref[...])

    return pl.pallas_call(
        body, name=name, grid=(S // tm,),
        in_specs=[
            pl.BlockSpec((tm, K), lambda i: (i, 0)),
            pl.BlockSpec((K, D), lambda i: (0, 0)),
            pl.BlockSpec((tm, D), lambda i: (i, 0)),
        ],
        out_specs=pl.BlockSpec((tm, D), lambda i: (i, 0)),
        out_shape=jax.ShapeDtypeStruct((S, D), F32),
        compiler_params=_cparams(),
    )(ycat, w, res)


def _out_proj_bwd(dx, w, name):
    S, D = dx.shape
    K = w.shape[0]
    tm = _rows(S, 1024)

    def body(d_ref, w_ref, o_ref):
        o_ref[...] = _dot_nt(d_ref[...].astype(BF16), w_ref[...])

    return pl.pallas_call(
        body, name=name, grid=(S // tm,),
        in_specs=[pl.BlockSpec((tm, D), lambda i: (i, 0)), pl.BlockSpec((K, D), lambda i: (0, 0))],
        out_specs=pl.BlockSpec((tm, K), lambda i: (i, 0)),
        out_shape=jax.ShapeDtypeStruct((S, K), F32),
        compiler_params=_cparams(),
    )(dx, w)


CONV_COLS = 256


def _shift_down(z, halo, k, row):
    out = pltpu.roll(z, k, 0)
    for n in range(k):
        out = jnp.where(row == n, halo[8 - k + n:8 - k + n + 1, :], out)
    return out


def _shift_up(g, halo, k, row, ts):
    out = pltpu.roll(g, ts - k, 0)
    for n in range(k):
        out = jnp.where(row == ts - k + n, halo[n:n + 1, :], out)
    return out


def _conv_fwd(proj, cw, cb, nconv, name):
    S = proj.shape[0]
    ncb = nconv // CONV_COLS
    ts = _rows(S, 1024)
    hb = ts // 8

    def body(b_ref, c_ref, u_ref, ch_ref, uh_ref, w_ref, bias_ref, o_ref):
        i = pl.program_id(1)
        z = c_ref[...] * u_ref[...]
        halo = jnp.where(i > 0, ch_ref[...] * uh_ref[...], 0.0)
        row = lax.broadcasted_iota(jnp.int32, z.shape, 0)
        w = w_ref[...]
        yc = w[0:1, :] * _shift_down(z, halo, 2, row) + w[1:2, :] * _shift_down(z, halo, 1, row) + w[2:3, :] * z
        o_ref[...] = (b_ref[...] * (yc + bias_ref[...])).astype(BF16)

    def blk(unit):
        return pl.BlockSpec((ts, CONV_COLS), lambda cbi, i: (i, unit * ncb + cbi))

    def prev(unit):
        return pl.BlockSpec((8, CONV_COLS), lambda cbi, i: (jnp.maximum(i * hb - 1, 0), unit * ncb + cbi))

    return pl.pallas_call(
        body, name=name, grid=(ncb, S // ts),
        in_specs=[blk(0), blk(1), blk(2), prev(1), prev(2),
                  pl.BlockSpec((8, CONV_COLS), lambda cbi, i: (0, cbi)),
                  pl.BlockSpec((1, CONV_COLS), lambda cbi, i: (0, cbi))],
        out_specs=pl.BlockSpec((ts, CONV_COLS), lambda cbi, i: (i, cbi)),
        out_shape=jax.ShapeDtypeStruct((S, nconv), BF16),
        compiler_params=_cparams(),
    )(proj, proj, proj, proj, proj, cw, cb)


def _conv_bwd(proj, dy, cw, cb, nconv, name):
    S = proj.shape[0]
    ncb = nconv // CONV_COLS
    ts = _rows(S, 1024)
    hb = ts // 8
    nblk = S // ts

    def body(b_ref, c_ref, u_ref, dy_ref, ch_ref, uh_ref, bn_ref, dyn_ref, w_ref, bias_ref,
             db_ref, dc_ref, du_ref, dw_ref):
        i = pl.program_id(1)

        @pl.when(i == 0)
        def _():
            dw_ref[...] = jnp.zeros_like(dw_ref)

        c = c_ref[...]
        u = u_ref[...]
        bg = b_ref[...]
        dy_ = dy_ref[...]
        z = c * u
        halo = jnp.where(i > 0, ch_ref[...] * uh_ref[...], 0.0)
        row = lax.broadcasted_iota(jnp.int32, z.shape, 0)
        w = w_ref[...]
        z2 = _shift_down(z, halo, 2, row)
        z1 = _shift_down(z, halo, 1, row)
        yc = w[0:1, :] * z2 + w[1:2, :] * z1 + w[2:3, :] * z
        db_ref[...] = (dy_ * (yc + bias_ref[...])).astype(BF16)
        g = dy_ * bg
        gnext = jnp.where(i < nblk - 1, dyn_ref[...] * bn_ref[...], 0.0)
        dz = w[2:3, :] * g + w[1:2, :] * _shift_up(g, gnext, 1, row, ts) + w[0:1, :] * _shift_up(g, gnext, 2, row, ts)
        dc_ref[...] = (dz * u).astype(BF16)
        du_ref[...] = (dz * c).astype(BF16)
        r8 = lax.broadcasted_iota(jnp.int32, (8, CONV_COLS), 0)
        sums = [jnp.sum(g * z2, axis=0, keepdims=True), jnp.sum(g * z1, axis=0, keepdims=True),
                jnp.sum(g * z, axis=0, keepdims=True), jnp.sum(g, axis=0, keepdims=True)]
        upd = jnp.zeros((8, CONV_COLS), F32)
        for n, sv in enumerate(sums):
            upd = jnp.where(r8 == n, sv, upd)
        dw_ref[...] += upd

    def blk(unit):
        return pl.BlockSpec((ts, CONV_COLS), lambda cbi, i: (i, unit * ncb + cbi))

    def prev(unit):
        return pl.BlockSpec((8, CONV_COLS), lambda cbi, i: (jnp.maximum(i * hb - 1, 0), unit * ncb + cbi))

    def nxt(unit):
        return pl.BlockSpec((8, CONV_COLS), lambda cbi, i: (jnp.minimum((i + 1) * hb, S // 8 - 1), unit * ncb + cbi))

    o = pl.BlockSpec((ts, CONV_COLS), lambda cbi, i: (i, cbi))
    return pl.pallas_call(
        body, name=name, grid=(ncb, nblk),
        in_specs=[blk(0), blk(1), blk(2), blk(0), prev(1), prev(2), nxt(0), nxt(0),
                  pl.BlockSpec((8, CONV_COLS), lambda cbi, i: (0, cbi)),
                  pl.BlockSpec((1, CONV_COLS), lambda cbi, i: (0, cbi))],
        out_specs=[o, o, o, pl.BlockSpec((8, CONV_COLS), lambda cbi, i: (0, cbi))],
        out_shape=[jax.ShapeDtypeStruct((S, nconv), BF16)] * 3 + [jax.ShapeDtypeStruct((8, nconv), F32)],
        compiler_params=_cparams(),
    )(proj, proj, proj, dy, proj, proj, proj, dy, cw, cb)


def _group_ones(n):
    r = lax.broadcasted_iota(jnp.int32, (n, n), 0) // HEAD_DIM
    c = lax.broadcasted_iota(jnp.int32, (n, n), 1) // HEAD_DIM
    return jnp.where(r == c, 1.0, 0.0).astype(BF16)


def _qk_norm(proj, gain_t, unit0, nsb, scale, name):
    S = proj.shape[0]
    nb = nsb // CONV_COLS
    ts = _rows(S, 1024)

    def body(x_ref, g_ref, o_ref):
        x = x_ref[...]
        ss = _sum_dot(x * x, _group_ones(CONV_COLS))
        r = lax.rsqrt(ss * (1.0 / HEAD_DIM) + EPS)
        o_ref[...] = ((x * r) * g_ref[...] * scale).astype(BF16)

    return pl.pallas_call(
        body, name=name, grid=(nb, S // ts),
        in_specs=[pl.BlockSpec((ts, CONV_COLS), lambda u, i: (i, unit0 + u)),
                  pl.BlockSpec((1, CONV_COLS), lambda u, i: (0, 0))],
        out_specs=pl.BlockSpec((ts, CONV_COLS), lambda u, i: (i, u)),
        out_shape=jax.ShapeDtypeStruct((S, nsb), BF16),
        compiler_params=_cparams(),
    )(proj, gain_t)


def _qk_norm_bwd(proj, dout, gain_t, unit0, nsb, scale, name):
    S = proj.shape[0]
    nb = nsb // CONV_COLS
    ts = _rows(S, 1024)

    def body(x_ref, d_ref, g_ref, dx_ref, dg_ref):
        @pl.when(pl.program_id(1) == 0)
        def _():
            dg_ref[...] = jnp.zeros_like(dg_ref)

        x = x_ref[...]
        g = g_ref[...]
        ones = _group_ones(CONV_COLS)
        ss = _sum_dot(x * x, ones)
        r = lax.rsqrt(ss * (1.0 / HEAD_DIM) + EPS)
        xn = x * r
        dh = d_ref[...] * scale
        dxn = dh * g
        m = _sum_dot(dxn * xn, ones) * (1.0 / HEAD_DIM)
        dx_ref[...] = (r * (dxn - xn * m)).astype(BF16)
        dg_ref[...] += jnp.broadcast_to(jnp.sum(dh * xn, axis=0, keepdims=True), dg_ref.shape)

    return pl.pallas_call(
        body, name=name, grid=(nb, S // ts),
        in_specs=[pl.BlockSpec((ts, CONV_COLS), lambda u, i: (i, unit0 + u)),
                  pl.BlockSpec((ts, CONV_COLS), lambda u, i: (i, u)),
                  pl.BlockSpec((1, CONV_COLS), lambda u, i: (0, 0))],
        out_specs=[pl.BlockSpec((ts, CONV_COLS), lambda u, i: (i, u)),
                   pl.BlockSpec((8, CONV_COLS), lambda u, i: (0, u))],
        out_shape=[jax.ShapeDtypeStruct((S, nsb), BF16), jax.ShapeDtypeStruct((8, nsb), F32)],
        compiler_params=_cparams(),
    )(proj, dout, gain_t)


Z_CLAMP = 80.0
N_SLOTS = 3
SAT_LIMIT = 120.0


def _head_masks():
    lane = lax.broadcasted_iota(jnp.int32, (1, LANES), 1)
    return [lane < HEAD_DIM, lane >= HEAD_DIM], lane


def _tile_consts(T):
    r_i = lax.broadcasted_iota(jnp.int32, (T, T), 0)
    c_i = lax.broadcasted_iota(jnp.int32, (T, T), 1)
    neg_suffix = jnp.where(r_i >= c_i, -1.0, 0.0).astype(BF16)
    prefix = jnp.where(r_i <= c_i, 1.0, 0.0).astype(BF16)
    return neg_suffix, prefix, c_i < r_i


def _pipeline(n, stages, first_special=False, last_special=False, saturated=None, extra_head=0):
    depth = len(stages)
    head = (depth if first_special else depth - 1) + extra_head
    off = 0 if last_special else 1
    for m in range(head):
        for k in reversed(range(min(m, depth - 1) + 1)):
            stages[k](m - k, (m - k) % N_SLOTS, first_special and m == k)

    def trip(m, u):
        for k in reversed(range(depth)):
            stages[k](m - k, (head + u - k) % N_SLOTS, False)

    def group(g, carry):
        for u in range(N_SLOTS):
            trip(head + g * N_SLOTS + u, u)
        return carry

    count = n - 1 + off - head
    full = count // N_SLOTS
    if saturated is None:
        lax.fori_loop(0, full, group, 0)
        go_on, done = True, n
    else:
        def more(state):
            return (state[0] < full) & (state[1] == 0)

        def step(state):
            group(state[0], 0)
            return state[0] + 1, saturated().astype(jnp.int32)

        groups, stop = lax.while_loop(more, step, (jnp.int32(0), saturated().astype(jnp.int32)))
        go_on = stop == 0
        done = jnp.where(go_on, n, head - depth + 1 + N_SLOTS * groups)
    for r in range(N_SLOTS):
        @pl.when((count - full * N_SLOTS == r) & go_on)
        def _(r=r):
            for u in range(r):
                trip(head + full * N_SLOTS + u, u)
            for e in range(depth - off):
                for k in reversed(range(e + off, depth)):
                    t = n - 1 - (k - e - off)
                    stages[k](t, (head + r + e - k) % N_SLOTS, last_special and k == e + off)
    return done


def _sweep(n, stages, finish, first_special=False, last_special=False, saturated=None, extra_head=0):
    depth = len(stages)
    least = (depth if first_special else depth - 1) + extra_head + (1 if last_special else 0)
    for short in range(1, least):
        @pl.when(n == short)
        def _(short=short):
            for m in range(short + depth - 1):
                for k in reversed(range(depth)):
                    t = m - k
                    if 0 <= t < short:
                        stages[k](t, t % N_SLOTS, (first_special and t == 0) or (last_special and t == short - 1))
            finish(short)

    @pl.when(n >= least)
    def _():
        finish(_pipeline(n, stages, first_special, last_special, saturated, extra_head))


def _attn_fwd(q, k, v, name):
    S, nsb = q.shape
    T = ATT_TILE
    hp = nsb // LANES
    nb = S // T
    assert nb <= HEAD_DIM

    def body(q_ref, k_ref, v_ref, y_ref, cs_ref, c_ref, acc, z_st, inc_st):
        i = pl.program_id(1)
        masks, lane = _head_masks()
        qv = q_ref[...]
        qm = [jnp.where(m, qv, jnp.zeros_like(qv)) for m in masks]
        neg_suffix, _, causal = _tile_consts(T)
        c_ref[...] = jnp.zeros_like(c_ref)
        acc[...] = jnp.zeros_like(acc)
        cs_ref[...] = jnp.zeros_like(cs_ref)

        def blk(ref, j):
            return ref[pl.ds(pl.multiple_of(j * T, T), T), :]

        def scores(t, slot, diag):
            kj = blk(k_ref, i - t)
            for h in range(2):
                z_st[slot, h] = jnp.minimum(_dot_nt(qm[h], kj), Z_CLAMP)

        def suffix_sums(t, slot, diag):
            for h in range(2):
                sp = jnp.log(1.0 + jnp.exp(z_st[slot, h]))
                if diag:
                    sp = jnp.where(causal, sp, 0.0)
                inc_st[slot, h] = _dot(sp.astype(BF16), neg_suffix)

        def weights(t, slot, diag):
            vj = blk(v_ref, i - t)
            for h in range(2):
                inc = inc_st[slot, h]
                c = c_ref[h]
                a = jnp.exp(z_st[slot, h] + inc + c)
                if diag:
                    a = jnp.where(causal, a, 0.0)
                upd = _dot(a.astype(BF16), vj)
                acc[...] += jnp.where(masks[h], upd, 0.0)
                cs_ref[...] = jnp.where(lane == i - t + HEAD_DIM * h, c, cs_ref[...])
                c_ref[h] = c + inc[:, 0:1]

        stages = [scores, suffix_sums, weights]

        def saturated():
            return jnp.max(c_ref[...]) < -SAT_LIMIT

        def note(used):
            cs_ref[...] = jnp.where(lane == LANES - 1, jnp.asarray(used).astype(F32), cs_ref[...])

        _sweep(i + 1, stages, note, first_special=True, saturated=saturated, extra_head=1)
        y_ref[...] = acc[...].astype(BF16)

    return pl.pallas_call(
        body, name=name, grid=(hp, nb),
        in_specs=[pl.BlockSpec((T, LANES), lambda p, i: (i, p)),
                  pl.BlockSpec((S, LANES), lambda p, i: (0, p)),
                  pl.BlockSpec((S, LANES), lambda p, i: (0, p))],
        out_specs=[pl.BlockSpec((T, LANES), lambda p, i: (i, p)),
                   pl.BlockSpec((None, T, LANES), lambda p, i: (p, i, 0))],
        out_shape=[jax.ShapeDtypeStruct((S, nsb), BF16), jax.ShapeDtypeStruct((hp, S, LANES), F32)],
        scratch_shapes=[pltpu.VMEM((2, T, 1), F32), pltpu.VMEM((T, LANES), F32),
                        pltpu.VMEM((N_SLOTS, 2, T, T), F32), pltpu.VMEM((N_SLOTS, 2, T, T), F32)],
        compiler_params=_cparams(),
    )(q, k, v)


def _attn_bwd(q, k, v, dy, col0, carry, name):
    S, nsb = q.shape
    T = ATT_TILE
    hp = nsb // LANES
    nb = S // T

    def body(q_ref, k_ref, v_ref, dy_ref, cs_ref, dq_ref, dk_ref, dv_ref, e_ref, acc,
             z_st, da_st, b_st, inc_st, a_st, e_st, p_st):
        i = pl.program_id(1)

        @pl.when(i == 0)
        def _():
            dk_ref[...] = jnp.zeros_like(dk_ref)
            dv_ref[...] = jnp.zeros_like(dv_ref)

        masks, lane = _head_masks()
        qv = q_ref[...]
        dyb = dy_ref[...].astype(BF16)
        qm = [jnp.where(m, qv, jnp.zeros_like(qv)) for m in masks]
        dym = [jnp.where(m, dyb, jnp.zeros_like(dyb)) for m in masks]
        neg_suffix, prefix, causal = _tile_consts(T)
        e_ref[...] = jnp.zeros_like(e_ref)
        acc[...] = jnp.zeros_like(acc)

        def blk(ref, j):
            return ref[pl.ds(pl.multiple_of(j * T, T), T), :]

        used = jnp.max(jnp.where(lane == LANES - 1, cs_ref[...], 0.0)).astype(jnp.int32)
        n = jnp.clip(used, 1, i + 1)
        first = i + 1 - n

        def scores(t, slot, diag):
            kj = blk(k_ref, first + t)
            vj = blk(v_ref, first + t)
            for h in range(2):
                z_st[slot, h] = jnp.minimum(_dot_nt(qm[h], kj), Z_CLAMP)
                da_st[slot, h] = _dot_nt(dym[h], vj)

        def suffix_sums(t, slot, diag):
            for h in range(2):
                u = jnp.exp(z_st[slot, h])
                w = 1.0 + u
                b_st[slot, h] = u / w
                sp = jnp.log(w)
                if diag:
                    sp = jnp.where(causal, sp, 0.0)
                inc_st[slot, h] = _dot(sp.astype(BF16), neg_suffix)

        def probs(t, slot, diag):
            csv = cs_ref[...]
            for h in range(2):
                c = jnp.sum(jnp.where(lane == first + t + HEAD_DIM * h, csv, 0.0), axis=-1, keepdims=True)
                a = jnp.exp(z_st[slot, h] + inc_st[slot, h] + c)
                if diag:
                    a = jnp.where(causal, a, 0.0)
                a_st[slot, h] = a.astype(BF16)
                e = a * da_st[slot, h]
                e_st[slot, h] = e
                p_st[slot, h] = _dot(e.astype(BF16), prefix)

        def grads(t, slot, diag):
            kj = blk(k_ref, first + t)
            off = pl.multiple_of((first + t) * T, T)
            for h in range(2):
                p = p_st[slot, h]
                dz = e_st[slot, h] - b_st[slot, h] * (e_ref[h] + p)
                if diag:
                    dz = jnp.where(causal, dz, 0.0)
                dzb = dz.astype(BF16)
                acc[...] += jnp.where(masks[h], _dot(dzb, kj), 0.0)
                dk_ref[pl.ds(off, T), :] += _dot_tn(dzb, qm[h])
                dv_ref[pl.ds(off, T), :] += _dot_tn(a_st[slot, h], dym[h])
                e_ref[h] += p[:, T - 1:T]

        stages = [scores, suffix_sums, probs, grads]

        _sweep(n, stages, lambda done: None, last_special=True)
        dq_ref[...] = acc[...]

    return pl.pallas_call(
        body, name=name, grid=(hp, nb),
        in_specs=[pl.BlockSpec((T, LANES), lambda p, i: (i, p)),
                  pl.BlockSpec((S, LANES), lambda p, i: (0, p)),
                  pl.BlockSpec((S, LANES), lambda p, i: (0, p)),
                  pl.BlockSpec((T, LANES), lambda p, i: (i, col0 + p)),
                  pl.BlockSpec((None, T, LANES), lambda p, i: (p, i, 0))],
        out_specs=[pl.BlockSpec((T, LANES), lambda p, i: (i, p)),
                   pl.BlockSpec((S, LANES), lambda p, i: (0, p)),
                   pl.BlockSpec((S, LANES), lambda p, i: (0, p))],
        out_shape=[jax.ShapeDtypeStruct((S, nsb), F32)] * 3,
        scratch_shapes=[pltpu.VMEM((2, T, 1), F32), pltpu.VMEM((T, LANES), F32)]
        + [pltpu.VMEM((N_SLOTS, 2, T, T), dt) for dt in (F32, F32, F32, F32, BF16, F32, F32)],
        compiler_params=_cparams(),
    )(q, k, v, dy, carry)


def _ple(x, p, tgt, gain, wpg, wpp, name):
    S, D = x.shape
    P = p.shape[1]
    ns, _, nc = wpp.shape
    tm = _rows(S, 512)

    def body(x_ref, p_ref, t_ref, g_ref, wpg_ref, wpp_ref, dx_ref, h_ref, du_ref, dpp_ref, loss_ref, dg_ref):
        @pl.when(pl.program_id(0) == 0)
        def _():
            loss_ref[...] = jnp.zeros_like(loss_ref)
            dg_ref[...] = jnp.zeros_like(dg_ref)

        x_ = x_ref[...]
        g = g_ref[...]
        h, xn, r = _rms(x_, g)
        hb = h.astype(BF16)
        h_ref[...] = hb
        gate = jax.nn.sigmoid(_dot(hb, wpg_ref[...]))
        pb = p_ref[...].astype(BF16)
        pp = jnp.concatenate([_dot(pb, wpp_ref[n]) for n in range(ns)], axis=1)
        err = (x_ + gate * pp) - t_ref[...]
        loss_ref[...] += (0.5 / D) * jnp.sum(err * err)
        dy = err * (1.0 / D)
        du = ((dy * pp) * (gate * (1.0 - gate))).astype(BF16)
        du_ref[...] = du
        dpp_ref[...] = (dy * gate).astype(BF16)
        dx, dg = _rms_bwd(_dot_nt(du, wpg_ref[...]), xn, r, g)
        dx_ref[...] = dy + dx
        dg_ref[...] += jnp.broadcast_to(dg, dg_ref.shape)

    row = pl.BlockSpec((tm, D), lambda i: (i, 0))
    return pl.pallas_call(
        body, name=name, grid=(S // tm,),
        in_specs=[row, pl.BlockSpec((tm, P), lambda i: (i, 0)), row,
                  pl.BlockSpec((1, D), lambda i: (0, 0)),
                  pl.BlockSpec((D, D), lambda i: (0, 0)),
                  pl.BlockSpec((ns, P, nc), lambda i: (0, 0, 0))],
        out_specs=[row, row, row, row,
                   pl.BlockSpec((8, LANES), lambda i: (0, 0)),
                   pl.BlockSpec((8, D), lambda i: (0, 0))],
        out_shape=[jax.ShapeDtypeStruct((S, D), F32)] + [jax.ShapeDtypeStruct((S, D), BF16)] * 3
        + [jax.ShapeDtypeStruct((8, LANES), F32), jax.ShapeDtypeStruct((8, D), F32)],
        compiler_params=_cparams(),
    )(x, p, tgt, gain, wpg, wpp)


def _elementwise(fn, ins, n_out, name):
    R, C = ins[0].shape
    tr = _rows(R, 512)

    def body(*refs):
        outs = fn(*[r[...] for r in refs[:len(ins)]])
        for o_ref, o in zip(refs[len(ins):], outs):
            o_ref[...] = o

    spec = pl.BlockSpec((tr, C), lambda i: (i, 0))
    return pl.pallas_call(
        body, name=name, grid=(R // tr,), in_specs=[spec] * len(ins), out_specs=[spec] * n_out,
        out_shape=[jax.ShapeDtypeStruct((R, C), F32)] * n_out, compiler_params=_cparams(),
    )(*ins)


def _adamw(w, g, m, v):
    m = ADAM_B1 * m + (1.0 - ADAM_B1) * g
    v = ADAM_B2 * v + (1.0 - ADAM_B2) * jnp.square(g)
    m_hat = m / (1.0 - ADAM_B1 ** ADAM_STEP)
    v_hat = v / (1.0 - ADAM_B2 ** ADAM_STEP)
    delta = -ADAM_LR * (m_hat / (jnp.sqrt(v_hat) + ADAM_EPS) + ADAM_WD * w)
    return delta, m, v


def _place():
    x, y, c = lax.axis_index("x"), lax.axis_index("y"), lax.axis_index("c")
    chips = [(1 - x, y), (x, 1 - y), (1 - x, 1 - y)]
    return x, y, c, chips


def _half(ref, c, axis_rows):
    n = ref.shape[-2]
    start = pl.multiple_of(c * (n // 2), 8)
    idx = (slice(None),) * (len(ref.shape) - 2) + (pl.ds(start, n // 2), slice(None))
    return ref.at[idx]


def _gather_weights(shards, small, name):
    n = len(shards)

    def body(*refs):
        ins, small_in = refs[:n], refs[n]
        outs, small_out = refs[n + 1:2 * n + 1], refs[2 * n + 1]
        lsem, lrsem, ssem, rsem, sm_s, sm_r = refs[2 * n + 2:]
        x, y, c, chips = _place()
        j = 2 * x + y
        sib = (x, y, 1 - c)

        local = [pltpu.make_async_remote_copy(
            src_ref=ins[a], dst_ref=outs[a].at[j], send_sem=lsem.at[a], recv_sem=lrsem.at[a],
            device_id=sib, device_id_type=MESH) for a in range(n)]
        for cp in local:
            cp.start()
        small_out[j] = small_in[...]
        small_cp = [pltpu.make_async_remote_copy(
            src_ref=small_in, dst_ref=small_out.at[j], send_sem=sm_s.at[k], recv_sem=sm_r.at[k],
            device_id=(*chip, c), device_id_type=MESH) for k, chip in enumerate(chips)]
        for cp in small_cp:
            cp.start()

        def ici(a, k, chip, jj, dev):
            return pltpu.make_async_remote_copy(
                src_ref=_half(ins[a], c, True) if dev is not None else _half(outs[a].at[jj], c, True),
                dst_ref=_half(outs[a].at[jj], c, True),
                send_sem=ssem.at[a, k], recv_sem=rsem.at[a, k],
                device_id=dev if dev is not None else (*chip, c), device_id_type=MESH)

        first = []
        for a in range(n):
            for k, chip in enumerate(chips):
                cp = ici(a, k, chip, j, (*chip, c))
                cp.start()
                first.append(cp)
        passed = []
        for a in range(n):
            for k, chip in enumerate(chips):
                jj = 2 * chip[0] + chip[1]
                ici(a, k, chip, jj, None).wait_recv()
                fw = pltpu.make_async_remote_copy(
                    src_ref=_half(outs[a].at[jj], c, True), dst_ref=_half(outs[a].at[jj], c, True),
                    send_sem=ssem.at[a, 3 + k], recv_sem=rsem.at[a, 3 + k], device_id=sib, device_id_type=MESH)
                fw.start()
                passed.append(fw)
        for a in range(n):
            for k, chip in enumerate(chips):
                jj = 2 * chip[0] + chip[1]
                pltpu.make_async_remote_copy(
                    src_ref=_half(outs[a].at[jj], 1 - c, True), dst_ref=_half(outs[a].at[jj], 1 - c, True),
                    send_sem=ssem.at[a, 3 + k], recv_sem=rsem.at[a, 3 + k], device_id=sib,
                    device_id_type=MESH).wait_recv()
        for cp in small_cp:
            cp.wait()
        for cp in first + passed:
            cp.wait_send()
        for cp in local:
            cp.wait()

    return pl.pallas_call(
        body, name=name,
        in_specs=[HBM] * n + [VMEM_WHOLE],
        out_specs=[HBM] * n + [VMEM_WHOLE],
        out_shape=[jax.ShapeDtypeStruct((N_SHARDS,) + s.shape, s.dtype) for s in shards]
        + [jax.ShapeDtypeStruct((N_SHARDS,) + small.shape, small.dtype)],
        scratch_shapes=[pltpu.SemaphoreType.DMA((n,)), pltpu.SemaphoreType.DMA((n,)),
                        pltpu.SemaphoreType.DMA((n, 6)), pltpu.SemaphoreType.DMA((n, 6)),
                        pltpu.SemaphoreType.DMA((3,)), pltpu.SemaphoreType.DMA((3,))],
    )(*shards, small)


def _sum_small(small, name):
    def body(small_in, small_out, buf, sm_s, sm_r):
        x, y, c, _ = _place()
        me = 4 * x + 2 * y + c
        buf[me] = small_in[...]
        peers = [(fx, fy, fc) for fx in (0, 1) for fy in (0, 1) for fc in (0, 1)][1:]
        sm = []
        for k, (fx, fy, fc) in enumerate(peers):
            dev = (1 - x if fx else x, 1 - y if fy else y, 1 - c if fc else c)
            cp = pltpu.make_async_remote_copy(
                src_ref=small_in, dst_ref=buf.at[me], send_sem=sm_s.at[k], recv_sem=sm_r.at[k],
                device_id=dev, device_id_type=MESH)
            cp.start()
            sm.append(cp)
        for cp in sm:
            cp.wait()
        tot = buf[0]
        for d in range(1, N_DEV):
            tot = tot + buf[d]
        small_out[...] = tot

    return pl.pallas_call(
        body, name=name, in_specs=[VMEM_WHOLE], out_specs=VMEM_WHOLE,
        out_shape=jax.ShapeDtypeStruct(small.shape, F32),
        scratch_shapes=[pltpu.VMEM((N_DEV,) + small.shape, F32),
                        pltpu.SemaphoreType.DMA((N_DEV - 1,)), pltpu.SemaphoreType.DMA((N_DEV - 1,))],
    )(small)


def _swap_ride(grads):
    n = len(grads)

    def make(ins, outs, sems):
        ssem, rsem = sems
        x, y, c, _ = _place()
        return [pltpu.make_async_remote_copy(
            src_ref=_half(ins[a], 1 - c, True), dst_ref=outs[a], send_sem=ssem.at[a], recv_sem=rsem.at[a],
            device_id=(x, y, 1 - c), device_id_type=MESH) for a in range(n)]

    return _Ride(grads, [jax.ShapeDtypeStruct((g.shape[0], g.shape[1] // 2, g.shape[2]), F32) for g in grads],
                 [pltpu.SemaphoreType.DMA((n,)), pltpu.SemaphoreType.DMA((n,))], make)


def _swap_grad_halves(grads, name):
    n = len(grads)

    def body(*refs):
        ins, outs = refs[:n], refs[n:2 * n]
        ssem, rsem = refs[2 * n:]
        x, y, c, _ = _place()
        cps = [pltpu.make_async_remote_copy(
            src_ref=_half(ins[a], 1 - c, True), dst_ref=outs[a], send_sem=ssem.at[a], recv_sem=rsem.at[a],
            device_id=(x, y, 1 - c), device_id_type=MESH) for a in range(n)]
        for cp in cps:
            cp.start()
        for cp in cps:
            cp.wait()

    return pl.pallas_call(
        body, name=name, in_specs=[HBM] * n, out_specs=[HBM] * n,
        out_shape=[jax.ShapeDtypeStruct((g.shape[0], g.shape[1] // 2, g.shape[2]), F32) for g in grads],
        scratch_shapes=[pltpu.SemaphoreType.DMA((n,)), pltpu.SemaphoreType.DMA((n,))],
    )(*grads)


def _chip_sum(g, recv, core, name):
    ns, R, C = g.shape
    r2 = R // 2
    tr = _rows(r2, 512)
    nrb = r2 // tr

    def body(core_ref, g_ref, r_ref, o_ref, ob_ref):
        s = g_ref[...] + r_ref[...]
        o_ref[...] = s
        ob_ref[...] = s.astype(BF16)

    out = pl.BlockSpec((None, tr, C), lambda s, i, cr: (s, i, 0))
    return pl.pallas_call(
        body, name=name,
        grid_spec=pltpu.PrefetchScalarGridSpec(
            num_scalar_prefetch=1, grid=(ns, nrb),
            in_specs=[pl.BlockSpec((None, tr, C), lambda s, i, cr: (s, cr[0] * nrb + i, 0)), out],
            out_specs=[out, out]),
        out_shape=[jax.ShapeDtypeStruct((ns, r2, C), F32), jax.ShapeDtypeStruct((ns, r2, C), BF16)],
        compiler_params=_cparams(),
    )(core, g, recv)


def _shard_sum(csum, got, place, name):
    _, r2, C = csum.shape
    tr = _rows(r2, 512)
    nrb = r2 // tr

    def body(place_ref, c_ref, g0_ref, g1_ref, g2_ref, o_ref):
        o_ref[...] = ((c_ref[...] + g0_ref[...].astype(F32)) + g1_ref[...].astype(F32)) + g2_ref[...].astype(F32)

    def got_spec(k):
        return pl.BlockSpec((None, tr, C), lambda i, pr: (k, i, 0))

    return pl.pallas_call(
        body, name=name,
        grid_spec=pltpu.PrefetchScalarGridSpec(
            num_scalar_prefetch=1, grid=(nrb,),
            in_specs=[pl.BlockSpec((None, tr, C), lambda i, pr: (pr[0], i, 0)), got_spec(0), got_spec(1), got_spec(2)],
            out_specs=pl.BlockSpec((tr, C), lambda i, pr: (pr[1] * nrb + i, 0))),
        out_shape=jax.ShapeDtypeStruct((2 * r2, C), F32),
        compiler_params=_cparams(),
    )(place, csum, got, got, got)


def _gather_ride(shards):
    n = len(shards)

    def make(ins, outs, sems):
        lsem, lrsem, ssem, rsem = sems
        x, y, c, chips = _place()
        j = 2 * x + y
        cps = [pltpu.make_async_remote_copy(
            src_ref=ins[a], dst_ref=outs[a].at[j], send_sem=lsem.at[a], recv_sem=lrsem.at[a],
            device_id=(x, y, 1 - c), device_id_type=MESH) for a in range(n)]
        for a in range(n):
            for k, chip in enumerate(chips):
                cps.append(pltpu.make_async_remote_copy(
                    src_ref=_half(ins[a], c, True), dst_ref=_half(outs[a].at[j], c, True),
                    send_sem=ssem.at[a, k], recv_sem=rsem.at[a, k], device_id=(*chip, c), device_id_type=MESH))
        return cps

    return _Ride(shards, [jax.ShapeDtypeStruct((N_SHARDS,) + s.shape, s.dtype) for s in shards],
                 [pltpu.SemaphoreType.DMA((n,)), pltpu.SemaphoreType.DMA((n,)),
                  pltpu.SemaphoreType.DMA((n, 3)), pltpu.SemaphoreType.DMA((n, 3))], make)


def _forward_halves(gathered, name):
    n = len(gathered)

    def body(*refs):
        outs = refs[n:2 * n]
        ssem, rsem = refs[2 * n:]
        x, y, c, chips = _place()
        cps = []
        for a in range(n):
            for k, chip in enumerate(chips):
                part = _half(outs[a].at[2 * chip[0] + chip[1]], c, True)
                cps.append(pltpu.make_async_remote_copy(
                    src_ref=part, dst_ref=part, send_sem=ssem.at[a, k], recv_sem=rsem.at[a, k],
                    device_id=(x, y, 1 - c), device_id_type=MESH))
        for cp in cps:
            cp.start()
        for cp in cps:
            cp.wait()

    return pl.pallas_call(
        body, name=name, in_specs=[HBM] * n, out_specs=[HBM] * n,
        out_shape=[jax.ShapeDtypeStruct(g.shape, g.dtype) for g in gathered],
        input_output_aliases={a: a for a in range(n)},
        scratch_shapes=[pltpu.SemaphoreType.DMA((n, 3)), pltpu.SemaphoreType.DMA((n, 3))],
    )(*gathered)


def _scatter_ride(csums):
    n = len(csums)

    def make(ins, outs, sems):
        ssem, rsem = sems
        x, y, c, chips = _place()
        return [pltpu.make_async_remote_copy(
            src_ref=ins[a].at[2 * chip[0] + chip[1]], dst_ref=outs[a].at[k], send_sem=ssem.at[a, k],
            recv_sem=rsem.at[a, k], device_id=(*chip, c), device_id_type=MESH)
            for a in range(n) for k, chip in enumerate(chips)]

    return _Ride(csums, [jax.ShapeDtypeStruct((3,) + g.shape[1:], g.dtype) for g in csums],
                 [pltpu.SemaphoreType.DMA((n, 3)), pltpu.SemaphoreType.DMA((n, 3))], make)


def _join_halves(fulls, name):
    n = len(fulls)

    def body(*refs):
        outs = refs[n:2 * n]
        ssem, rsem = refs[2 * n:]
        x, y, c, _ = _place()
        cps = [pltpu.make_async_remote_copy(
            src_ref=_half(outs[a], c, True), dst_ref=_half(outs[a], c, True), send_sem=ssem.at[a],
            recv_sem=rsem.at[a], device_id=(x, y, 1 - c), device_id_type=MESH) for a in range(n)]
        for cp in cps:
            cp.start()
        for cp in cps:
            cp.wait()

    return pl.pallas_call(
        body, name=name, in_specs=[HBM] * n, out_specs=[HBM] * n,
        out_shape=[jax.ShapeDtypeStruct(f.shape, F32) for f in fulls],
        input_output_aliases={a: a for a in range(n)},
        scratch_shapes=[pltpu.SemaphoreType.DMA((n,))] * 2,
    )(*fulls)


def _pad_rows(a, rows, cols):
    return jnp.pad(a, ((0, rows - a.shape[0]), (0, cols - a.shape[1])))


def kernel(x, p, ffn1_norm, ffn1_w_gate, ffn1_w_up, ffn1_w_down, mix_norm, w_in, conv_w, conv_b, q_norm, k_norm, w_out, ffn2_norm, ffn2_w_gate, ffn2_w_up, ffn2_w_down, ple_norm, ple_w_gate, ple_w_proj, loss_target, m_ffn1_norm, m_ffn1_w_gate, m_ffn1_w_up, m_ffn1_w_down, m_mix_norm, m_w_in, m_conv_w, m_conv_b, m_q_norm, m_k_norm, m_w_out, m_ffn2_norm, m_ffn2_w_gate, m_ffn2_w_up, m_ffn2_w_down, m_ple_norm, m_ple_w_gate, m_ple_w_proj, v_ffn1_norm, v_ffn1_w_gate, v_ffn1_w_up, v_ffn1_w_down, v_mix_norm, v_w_in, v_conv_w, v_conv_b, v_q_norm, v_k_norm, v_w_out, v_ffn2_norm, v_ffn2_w_gate, v_ffn2_w_up, v_ffn2_w_down, v_ple_norm, v_ple_w_gate, v_ple_w_proj):
    big = dict(ffn1_w_gate=ffn1_w_gate, ffn1_w_up=ffn1_w_up, ffn1_w_down=ffn1_w_down, w_in=w_in, w_out=w_out,
               ffn2_w_gate=ffn2_w_gate, ffn2_w_up=ffn2_w_up, ffn2_w_down=ffn2_w_down,
               ple_w_gate=ple_w_gate, ple_w_proj=ple_w_proj)
    big_m = dict(ffn1_w_gate=m_ffn1_w_gate, ffn1_w_up=m_ffn1_w_up, ffn1_w_down=m_ffn1_w_down, w_in=m_w_in,
                 w_out=m_w_out, ffn2_w_gate=m_ffn2_w_gate, ffn2_w_up=m_ffn2_w_up, ffn2_w_down=m_ffn2_w_down,
                 ple_w_gate=m_ple_w_gate, ple_w_proj=m_ple_w_proj)
    big_v = dict(ffn1_w_gate=v_ffn1_w_gate, ffn1_w_up=v_ffn1_w_up, ffn1_w_down=v_ffn1_w_down, w_in=v_w_in,
                 w_out=v_w_out, ffn2_w_gate=v_ffn2_w_gate, ffn2_w_up=v_ffn2_w_up, ffn2_w_down=v_ffn2_w_down,
                 ple_w_gate=v_ple_w_gate, ple_w_proj=v_ple_w_proj)
    names = list(big)
    xs = x[0]
    ps = p[0, 0]
    tgt = loss_target[0]
    S, D = xs.shape
    nconv = conv_b.shape[1]
    nsb = D - nconv
    cwl = conv_w.shape[2]
    jchip = 2 * lax.axis_index("x") + lax.axis_index("y")
    core = lax.axis_index("c")

    early, late = names[:3], names[3:]
    assert all(k.startswith("ffn1") for k in early)
    shards = {k: big[k][0].astype(BF16) for k in names}
    gathered = _gather_weights([shards[k] for k in early], _pad_rows(conv_w[0], 8, LANES), "gather_weights")
    W = dict(zip(early, gathered[:-1]))
    cw_full = jnp.transpose(gathered[-1][:, :, :cwl], (1, 0, 2)).reshape(8, N_SHARDS * cwl)
    qg = jnp.tile(q_norm, (1, CONV_COLS // HEAD_DIM))
    kg = jnp.tile(k_norm, (1, CONV_COLS // HEAD_DIM))
    n_units = nconv // CONV_COLS

    x1, h1, a1, b1, *landed = _ffn_fwd(xs, ffn1_norm, W["ffn1_w_gate"], W["ffn1_w_up"], W["ffn1_w_down"], "ffn1_fwd",
                                        ride=_gather_ride([shards[k] for k in late]))
    W.update(zip(late, _forward_halves(landed, "gather_forward")))
    wout_full = W["w_out"].reshape(-1, D)
    wpg_full = W["ple_w_gate"].reshape(-1, D)
    proj, h2 = _norm_proj(x1, mix_norm, W["w_in"], "mix_in_proj")
    y_conv = _conv_fwd(proj, cw_full, conv_b, nconv, "conv_fwd")
    qs = _qk_norm(proj, qg, 3 * n_units, nsb, HEAD_DIM ** -0.5, "q_norm_fwd")
    kh = _qk_norm(proj, kg, 4 * n_units, nsb, 1.0, "k_norm_fwd")
    vb = proj[:, 3 * nconv + 2 * nsb:].astype(BF16)
    y_sb, carry = _attn_fwd(qs, kh, vb, "attn_fwd")
    ycat = jnp.concatenate([y_conv, y_sb], axis=1)
    x2 = _out_proj(ycat, wout_full, x1, "mix_out_proj")
    x3, h3, a3, b3 = _ffn_fwd(x2, ffn2_norm, W["ffn2_w_gate"], W["ffn2_w_up"], W["ffn2_w_down"], "ffn2_fwd")

    dx3, h4, du4, dpp, loss_blk, dg_ple = _ple(x3, ps, tgt, ple_norm, wpg_full, W["ple_w_proj"], "ple_loss")
    G = {}
    tk = _rows(S, 2048)
    nk = S // tk
    kd = wpg_full.shape[0] // N_SHARDS
    G["ple_w_gate"] = _tn_matmul(
        h4, du4, pl.BlockSpec((tk, kd), lambda m, k: (k, m)), pl.BlockSpec((tk, D), lambda m, k: (k, 0)),
        (N_SHARDS, kd, D), pl.BlockSpec((None, kd, D), lambda m, k: (m, 0, 0)), (N_SHARDS, nk), "ple_w_gate_grad")
    P = ps.shape[1]
    npp = D // N_SHARDS
    G["ple_w_proj"] = _tn_matmul(
        ps, dpp, pl.BlockSpec((tk, P), lambda m, k: (k, 0)), pl.BlockSpec((tk, npp), lambda m, k: (k, m)),
        (N_SHARDS, P, npp), pl.BlockSpec((None, P, npp), lambda m, k: (m, 0, 0)), (N_SHARDS, nk), "ple_w_proj_grad")

    def ffn_grads(pre, h, s, da, db, dy):
        fs = s.shape[2]
        tk, nk = _rows(S, 4096), S // _rows(S, 4096)
        hs = pl.BlockSpec((tk, D), lambda m, k: (k, 0))
        ss = pl.BlockSpec((None, tk, fs), lambda m, k: (m, k, 0))
        for leaf, lhs, rhs in (("_w_gate", da, h), ("_w_up", db, h), ("_w_down", s, dy)):
            G[pre + leaf] = _tn_matmul(lhs, rhs, ss, hs, (N_SHARDS, fs, D),
                                       pl.BlockSpec((None, fs, D), lambda m, k: (m, 0, 0)), (N_SHARDS, nk), pre + leaf + "_grad")

    s3, da3, db3, dy3 = _ffn_bwd_act(dx3, a3, b3, W["ffn2_w_down"], "ffn2_bwd_act")
    dx2, dg_ffn2 = _ffn_bwd_in(da3, db3, W["ffn2_w_gate"], W["ffn2_w_up"], dx3, x2, ffn2_norm, "ffn2_bwd_in")
    ffn_grads("ffn2", h3, s3, da3, db3, dy3)

    dycat = _out_proj_bwd(dx2, wout_full, "mix_out_proj_bwd")
    ko = wout_full.shape[0] // N_SHARDS
    G["w_out"] = _tn_matmul(
        ycat, dx2, pl.BlockSpec((tk, ko), lambda m, k: (k, m)), pl.BlockSpec((tk, D), lambda m, k: (k, 0)),
        (N_SHARDS, ko, D), pl.BlockSpec((None, ko, D), lambda m, k: (m, 0, 0)), (N_SHARDS, nk), "w_out_grad")
    db_, dc_, du_, dwb = _conv_bwd(proj, dycat, cw_full, conv_b, nconv, "conv_bwd")
    dqs, dkh, dv = _attn_bwd(qs, kh, vb, dycat, nconv // LANES, carry, "attn_bwd")
    dq, dg_q = _qk_norm_bwd(proj, dqs, qg, 3 * n_units, nsb, HEAD_DIM ** -0.5, "q_norm_bwd")
    dk, dg_k = _qk_norm_bwd(proj, dkh, kg, 4 * n_units, nsb, 1.0, "k_norm_bwd")
    dproj = jnp.concatenate([db_, dc_, du_, dq, dk, dv.astype(BF16)], axis=1)
    nin = W["w_in"].shape[2]
    G["w_in"] = _tn_matmul(
        h2, dproj, pl.BlockSpec((tk, D), lambda m, k: (k, 0)), pl.BlockSpec((tk, nin), lambda m, k: (k, m)),
        (N_SHARDS, D, nin), pl.BlockSpec((None, D, nin), lambda m, k: (m, 0, 0)), (N_SHARDS, nk), "w_in_grad")
    dx1, dg_mix, *recv_late = _norm_proj_bwd(dproj, W["w_in"], dx2, x1, mix_norm, "mix_in_proj_bwd",
                                             ride=_swap_ride([G[k] for k in late]))
    core_arr = jnp.reshape(core, (1,)).astype(jnp.int32)
    place = jnp.stack([jchip, core]).astype(jnp.int32)

    def chip_sums(group, recv):
        return zip(*[_chip_sum(G[k], r, core_arr, f"chip_sum_{k}") for k, r in zip(group, recv)])

    cs_late, csb_late = chip_sums(late, recv_late)
    s1, da1, db1, dy1, *got_late = _ffn_bwd_act(dx1, a1, b1, W["ffn1_w_down"], "ffn1_bwd_act",
                                                  ride=_scatter_ride(list(csb_late)))
    ffn_grads("ffn1", h1, s1, da1, db1, dy1)
    cs_early, csb_early = chip_sums(early, _swap_grad_halves([G[k] for k in early], "grad_swap_halves"))
    dx0, dg_ffn1, *got_early = _ffn_bwd_in(da1, db1, W["ffn1_w_gate"], W["ffn1_w_up"], dx1, xs, ffn1_norm,
                                           "ffn1_bwd_in", ride=_scatter_ride(list(csb_early)))

    assert D >= nconv and D % LANES == 0
    fold = lambda t: t[0].reshape(-1, HEAD_DIM).sum(axis=0)[None, :]
    small_rows = [dg_ffn1[0:1], dg_mix[0:1], dg_ffn2[0:1], dg_ple[0:1],
                  _pad_rows(dwb[3:4], 1, D), _pad_rows(dwb[0:3], 3, D),
                  _pad_rows(fold(dg_q), 1, D), _pad_rows(fold(dg_k), 1, D), _pad_rows(loss_blk[0:1, 0:1], 1, D)]
    small = _pad_rows(jnp.concatenate(small_rows, axis=0), SMALL_ROWS, D)

    small_sum = _sum_small(small, "small_sum")

    full = _join_halves([_shard_sum(cs, gt, place, f"shard_sum_{k}")
                         for k, cs, gt in zip(names, list(cs_early) + list(cs_late), got_early + got_late)],
                        "grad_join_halves")

    out_g, out_d, out_m, out_v = {}, {}, {}, {}
    for a, k in enumerate(names):
        shp = big[k].shape
        g2 = full[a]
        flipped = k.endswith(("_w_gate", "_w_up")) and k.startswith("ffn")
        view = (lambda t: jnp.transpose(t[0])) if flipped else (lambda t: t[0].reshape(g2.shape))
        back = (lambda t: jnp.transpose(t).reshape(shp)) if flipped else (lambda t: t.reshape(shp))
        d_, m_, v_ = _elementwise(_adamw, [view(big[k]), g2, view(big_m[k]), view(big_v[k])], 3, f"adamw_{k}")
        out_g[k], out_d[k], out_m[k], out_v[k] = (back(t) for t in (g2, d_, m_, v_))

    sm_names = ["ffn1_norm", "mix_norm", "ffn2_norm", "ple_norm", "conv_b", "conv_w", "q_norm", "k_norm"]
    sm_w = dict(ffn1_norm=ffn1_norm, mix_norm=mix_norm, ffn2_norm=ffn2_norm, ple_norm=ple_norm, conv_b=conv_b,
                conv_w=conv_w[0], q_norm=q_norm, k_norm=k_norm)
    sm_m = dict(ffn1_norm=m_ffn1_norm, mix_norm=m_mix_norm, ffn2_norm=m_ffn2_norm, ple_norm=m_ple_norm,
                conv_b=m_conv_b, conv_w=m_conv_w[0], q_norm=m_q_norm, k_norm=m_k_norm)
    sm_v = dict(ffn1_norm=v_ffn1_norm, mix_norm=v_mix_norm, ffn2_norm=v_ffn2_norm, ple_norm=v_ple_norm,
                conv_b=v_conv_b, conv_w=v_conv_w[0], q_norm=v_q_norm, k_norm=v_k_norm)
    sm_g = dict(ffn1_norm=small_sum[0:1], mix_norm=small_sum[1:2], ffn2_norm=small_sum[2:3], ple_norm=small_sum[3:4],
                conv_b=small_sum[4:5, :nconv],
                conv_w=lax.dynamic_slice_in_dim(small_sum[5:8, :nconv], jchip * cwl, cwl, axis=1),
                q_norm=small_sum[8:9, :HEAD_DIM], k_norm=small_sum[9:10, :HEAD_DIM])
    loss = small_sum[10, 0]
    pack = lambda d: _pad_rows(jnp.concatenate([_pad_rows(d[k], d[k].shape[0], D) for k in sm_names], axis=0), SMALL_ROWS, D)
    sd, smm, svv = _elementwise(_adamw, [pack(sm_w), pack(sm_g), pack(sm_m), pack(sm_v)], 3, "adamw_small")
    row = 0
    for k in sm_names:
        r_, c_ = sm_w[k].shape
        shp = (1, r_, c_) if k == "conv_w" else (r_, c_)
        out_g[k] = sm_g[k].reshape(shp)
        out_d[k], out_m[k], out_v[k] = (t[row:row + r_, :c_].reshape(shp) for t in (sd, smm, svv))
        row += r_

    order = ["ffn1_norm", "ffn1_w_gate", "ffn1_w_up", "ffn1_w_down", "mix_norm", "w_in", "conv_w", "conv_b",
             "q_norm", "k_norm", "w_out", "ffn2_norm", "ffn2_w_gate", "ffn2_w_up", "ffn2_w_down", "ple_norm",
             "ple_w_gate", "ple_w_proj"]
    return (loss, dx0[None], *[out_g[k] for k in order], *[out_d[k] for k in order],
            *[out_m[k] for k in order], *[out_v[k] for k in order])
```
